```python
import math
import jax, jax.numpy as jnp
from jax import lax
import numpy as np

D_MODEL = 1024
BATCH = 16
SEQ = 256
DEPTH = 4
DEC_BATCH = 4
DEC_SEQ = 1024
PAST_LEN = 256

GRID_W = 64
N_MIXERS = 2
MIX_WIDTH = D_MODEL
DA_HEADS = 8
DA_DH = 64
DA_DV = 2 * DA_DH
DA_IN = 2 * DA_HEADS * 2 * DA_DH + DA_HEADS * DA_DV + MIX_WIDTH
MLA_HEADS = 8
MLA_Q_LORA = 384
MLA_KV_LORA = 256
MLA_NOPE = 128
MLA_ROPE = 64
MLA_DV = 128
MLA_IN = MLA_Q_LORA + MLA_KV_LORA + MLA_ROPE + MIX_WIDTH
ROPE_THETA = 10000.0
NORM_EPS = 1e-6
Q_BLOCK = 128

kernel_name = 'diffmla_prefix_diffusion_step'

F32 = jnp.float32


def rmsnorm(x, g):
    xf = x.astype(F32)
    y = xf * lax.rsqrt(jnp.mean(xf * xf, axis=-1, keepdims=True) + NORM_EPS)
    return (y * g.astype(F32)).astype(x.dtype)


def axial_rope(n_tokens, dim):
    rows = n_tokens // GRID_W
    row = jnp.repeat(jnp.arange(rows), GRID_W).astype(F32)
    col = jnp.tile(jnp.arange(GRID_W), rows).astype(F32)
    n_freq = dim // 4
    inv = ROPE_THETA ** (-jnp.arange(n_freq, dtype=F32) / n_freq)
    ang = jnp.concatenate([row[:, None] * inv, col[:, None] * inv], axis=-1)
    return jnp.cos(ang), jnp.sin(ang)


def apply_rope(x, cos, sin):
    shape = (cos.shape[0],) + (1,) * (x.ndim - 3) + (cos.shape[-1],)
    cos = cos.reshape(shape)
    sin = sin.reshape(shape)
    xf = x.astype(F32)
    half = x.shape[-1] // 2
    x1, x2 = xf[..., :half], xf[..., half:]
    out = jnp.concatenate([x1 * cos - x2 * sin, x1 * sin + x2 * cos], axis=-1)
    return out.astype(x.dtype)


def ada_mod(cond, w, b):
    m = jax.nn.silu(cond) @ w + b
    if m.ndim == 2:
        m = m[:, None, :]
    shift, scale, gate = jnp.split(m, 3, axis=-1)
    return shift, scale, gate


def map_query_blocks(fn, q):
    b, s = q.shape[0], q.shape[1]
    if s % Q_BLOCK != 0 or s <= Q_BLOCK:
        return fn(q)
    nb = s // Q_BLOCK
    qb = jnp.swapaxes(q.reshape((b, nb, Q_BLOCK) + q.shape[2:]), 0, 1)
    out = jnp.swapaxes(lax.map(fn, qb), 0, 1)
    return out.reshape((b, s) + out.shape[3:])


def diff_project(h, w_in):
    b, s = h.shape[:2]
    proj = h @ w_in
    nq = DA_HEADS * 2 * DA_DH
    q, k, v, gate = jnp.split(proj, [nq, 2 * nq, 2 * nq + DA_HEADS * DA_DV], axis=-1)
    q = q.reshape(b, s, DA_HEADS, 2, DA_DH)
    k = k.reshape(b, s, DA_HEADS, 2, DA_DH)
    v = v.reshape(b, s, DA_HEADS, DA_DV)
    return q, k, v, gate


def diff_attend(q, k, v, gate, lam, lam_init, g_sub):
    scale = DA_DH ** -0.5

    def block(qb):
        s = jnp.einsum('bqhcd,bkhcd->bhcqk', qb, k, preferred_element_type=F32) * scale
        p = jax.nn.softmax(s, axis=-1)
        w = p[:, :, 0] - lam * p[:, :, 1]
        return jnp.einsum('bhqk,bkhe->bqhe', w.astype(v.dtype), v)

    o = map_query_blocks(block, q)
    o = rmsnorm(o, g_sub) * (1.0 - lam_init)
    b, sq = o.shape[:2]
    return o.reshape(b, sq, MIX_WIDTH) * jax.nn.silu(gate)


def mla_project(h, w_in, g_qa, w_qb, g_kva):
    b, s = h.shape[:2]
    proj = h @ w_in
    q_a, kv_a, k_pe, gate = jnp.split(
        proj, [MLA_Q_LORA, MLA_Q_LORA + MLA_KV_LORA, MLA_Q_LORA + MLA_KV_LORA + MLA_ROPE], axis=-1)
    q = (rmsnorm(q_a, g_qa) @ w_qb).reshape(b, s, MLA_HEADS, MLA_NOPE + MLA_ROPE)
    c_kv = rmsnorm(kv_a, g_kva)
    return q, c_kv, k_pe, gate


def mla_rope(q, k_pe, cos, sin):
    q_pe = apply_rope(q[..., MLA_NOPE:], cos, sin)
    q = jnp.concatenate([q[..., :MLA_NOPE], q_pe], axis=-1)
    return q, apply_rope(k_pe, cos, sin)


def mla_attend(q, c_kv, k_pe, gate, w_kvb):
    b, sk = c_kv.shape[:2]
    kv = (c_kv @ w_kvb).reshape(b, sk, MLA_HEADS, MLA_NOPE + MLA_DV)
    k_nope, v = kv[..., :MLA_NOPE], kv[..., MLA_NOPE:]
    k = jnp.concatenate(
        [k_nope, jnp.broadcast_to(k_pe[:, :, None, :], (b, sk, MLA_HEADS, MLA_ROPE))], axis=-1)
    scale = (MLA_NOPE + MLA_ROPE) ** -0.5

    def block(qb):
        s = jnp.einsum('bqhd,bkhd->bhqk', qb, k, preferred_element_type=F32) * scale
        p = jax.nn.softmax(s, axis=-1)
        return jnp.einsum('bhqk,bkhe->bqhe', p.astype(v.dtype), v)

    o = map_query_blocks(block, q)
    sq = o.shape[1]
    return o.reshape(b, sq, MIX_WIDTH) * jax.nn.silu(gate)


def setup_inputs(seed: int = 0) -> dict:
    key = jax.random.key(seed)
    ks = iter(jax.random.split(key, 32))
    n_diff = (DEPTH + 1) // 2
    n_mla = DEPTH // 2

    def nrm(shape, scale=1.0):
        return jax.random.normal(next(ks), shape, F32) * scale

    def gain(shape):
        return 1.0 + nrm(shape, 0.05)

    return {
        'x_prompt': nrm((BATCH, SEQ, D_MODEL)),
        'x_sample': nrm((DEC_BATCH, DEC_SEQ, D_MODEL)),
        'cache_diff_k': nrm((DEC_BATCH, n_diff, PAST_LEN, DA_HEADS, 2, DA_DH)),
        'cache_diff_v': nrm((DEC_BATCH, n_diff, PAST_LEN, DA_HEADS, DA_DV)),
        'cache_mla_ckv': nrm((DEC_BATCH, n_mla, PAST_LEN, MLA_KV_LORA)),
        'cache_mla_kpe': nrm((DEC_BATCH, n_mla, PAST_LEN, MLA_ROPE)),
        'c': nrm((DEC_BATCH, D_MODEL)),
        'c_ctx': nrm((D_MODEL,)),
        'w_ada': nrm((DEPTH, D_MODEL, 3 * D_MODEL), 0.5 * D_MODEL ** -0.5),
        'b_ada': nrm((DEPTH, 3 * D_MODEL), 0.01),
        'g_pre': gain((DEPTH, D_MODEL)),
        'g_post': gain((DEPTH, D_MODEL)),
        'w_out': nrm((DEPTH, MIX_WIDTH, D_MODEL), MIX_WIDTH ** -0.5),
        'da_w_in': nrm((n_diff, D_MODEL, DA_IN), D_MODEL ** -0.5),
        'da_lam_q1': nrm((n_diff, DA_DH), 0.1),
        'da_lam_k1': nrm((n_diff, DA_DH), 0.1),
        'da_lam_q2': nrm((n_diff, DA_DH), 0.1),
        'da_lam_k2': nrm((n_diff, DA_DH), 0.1),
        'da_g_sub': gain((n_diff, DA_DV)),
        'mla_w_in': nrm((n_mla, D_MODEL, MLA_IN), D_MODEL ** -0.5),
        'mla_g_qa': gain((n_mla, MLA_Q_LORA)),
        'mla_w_qb': nrm((n_mla, MLA_Q_LORA, MLA_HEADS * (MLA_NOPE + MLA_ROPE)), MLA_Q_LORA ** -0.5),
        'mla_g_kva': gain((n_mla, MLA_KV_LORA)),
        'mla_w_kvb': nrm((n_mla, MLA_KV_LORA, MLA_HEADS * (MLA_NOPE + MLA_DV)), MLA_KV_LORA ** -0.5),
    }


def reference(x_prompt, x_sample, cache_diff_k, cache_diff_v, cache_mla_ckv, cache_mla_kpe,
              c, c_ctx, w_ada, b_ada, g_pre, g_post, w_out,
              da_w_in, da_lam_q1, da_lam_k1, da_lam_q2, da_lam_k2, da_g_sub,
              mla_w_in, mla_g_qa, mla_w_qb, mla_g_kva, mla_w_kvb):
    x_ctx = x_prompt
    x_lat = x_sample
    s_lat = x_lat.shape[1]
    rope_da = axial_rope(s_lat, DA_DH)
    rope_mla = axial_rope(s_lat, MLA_ROPE)
    st_dk, st_dv, st_ckv, st_kpe = [], [], [], []

    for i in range(DEPTH):
        j = i // N_MIXERS
        kind = i % N_MIXERS
        sh_c, sc_c, gt_c = ada_mod(c_ctx, w_ada[i], b_ada[i])
        sh_l, sc_l, gt_l = ada_mod(c, w_ada[i], b_ada[i])
        hc = rmsnorm(x_ctx, g_pre[i]) * (1.0 + sc_c) + sh_c
        hl = rmsnorm(x_lat, g_pre[i]) * (1.0 + sc_l) + sh_l

        if kind == 0:
            lam_init = 0.8 - 0.6 * math.exp(-0.3 * i)
            lam = (jnp.exp(jnp.sum(da_lam_q1[j].astype(F32) * da_lam_k1[j].astype(F32)))
                   - jnp.exp(jnp.sum(da_lam_q2[j].astype(F32) * da_lam_k2[j].astype(F32)))
                   + lam_init)
            qc, kc, vc, gc = diff_project(hc, da_w_in[j])
            st_dk.append(kc)
            st_dv.append(vc)
            oc = diff_attend(qc, kc, vc, gc, lam, lam_init, da_g_sub[j])
            ql, kl, vl, gl = diff_project(hl, da_w_in[j])
            ql = apply_rope(ql, *rope_da)
            kl = apply_rope(kl, *rope_da)
            kl = jnp.concatenate([kl, cache_diff_k[:, j]], axis=1)
            vl = jnp.concatenate([vl, cache_diff_v[:, j]], axis=1)
            ol = diff_attend(ql, kl, vl, gl, lam, lam_init, da_g_sub[j])
        else:
            qc, ckv_c, kpe_c, gc = mla_project(hc, mla_w_in[j], mla_g_qa[j], mla_w_qb[j], mla_g_kva[j])
            st_ckv.append(ckv_c)
            st_kpe.append(kpe_c)
            oc = mla_attend(qc, ckv_c, kpe_c, gc, mla_w_kvb[j])
            ql, ckv_l, kpe_l, gl = mla_project(hl, mla_w_in[j], mla_g_qa[j], mla_w_qb[j], mla_g_kva[j])
            ql, kpe_l = mla_rope(ql, kpe_l, *rope_mla)
            ckv_all = jnp.concatenate([ckv_l, cache_mla_ckv[:, j]], axis=1)
            kpe_all = jnp.concatenate([kpe_l, cache_mla_kpe[:, j]], axis=1)
            ol = mla_attend(ql, ckv_all, kpe_all, gl, mla_w_kvb[j])

        x_ctx = x_ctx + gt_c * rmsnorm(oc @ w_out[i], g_post[i])
        x_lat = x_lat + gt_l * rmsnorm(ol @ w_out[i], g_post[i])

    state_diff_k = jnp.stack(st_dk, axis=1)
    state_diff_v = jnp.stack(st_dv, axis=1)
    state_mla_ckv = jnp.stack(st_ckv, axis=1)
    state_mla_kpe = jnp.stack(st_kpe, axis=1)
    return (x_ctx, x_lat, state_diff_k, state_diff_v, state_mla_ckv, state_mla_kpe)
```

```python
import functools
import math

import jax
import jax.numpy as jnp
from jax import lax
from jax.experimental import pallas as pl
from jax.experimental.pallas import tpu as pltpu

F32 = jnp.float32
BF16 = jnp.bfloat16

D_MODEL = 1024
DEPTH = 4
N_CTX_B = 16
CTX_S = 256
N_LAT_B = 4
LAT_S = 1024
PAST = 256
GRID_W = 64
HEADS = 8
DA_DH = 64
DA_DV = 128
MLA_Q_LORA = 384
MLA_KV_LORA = 256
MLA_NOPE = 128
MLA_ROPE = 64
MLA_DV = 128
MLA_QK_PAD = 256
MLA_IN_PAD = 1792
ROPE_THETA = 10000.0
NORM_EPS = 1e-6
COND_ROWS = 8

LANES = 128
TQ = 256
TM = 256
VMEM_LIMIT = 56 * 1024 * 1024

_TRANS_B = (((1,), (1,)), ((), ()))


def _silu(x):
    return x * (1.0 / (1.0 + jnp.exp(-x)))


def _rms(x, g):
    r = lax.rsqrt(jnp.mean(x * x, axis=-1, keepdims=True) + NORM_EPS)
    return (x * r) * g


def _rope_slab(x, cos, sin_signed):
    lane = lax.broadcasted_iota(jnp.int32, x.shape, 1)
    first_half = (lane % 64) < 32
    partner = jnp.where(first_half, pltpu.roll(x, 96, axis=1), pltpu.roll(x, 32, axis=1))
    return x * cos + partner * sin_signed


def _cond_row(mods_ref, row):
    m = mods_ref[pl.ds(row, 1), :]
    return m[:, 0:D_MODEL], m[:, D_MODEL:2 * D_MODEL], m[:, 2 * D_MODEL:3 * D_MODEL]


def _modulated(x_ref, mods_ref, g_ref, row):
    shift, scale, _ = _cond_row(mods_ref, row)
    h = _rms(x_ref[...], g_ref[...]) * (1.0 + scale) + shift
    return h.astype(BF16)


def _ada_kernel(cond_ref, w_ref, b_ref, o_ref):
    a = _silu(cond_ref[...]).astype(BF16)
    o_ref[...] = jnp.dot(a, w_ref[...].astype(BF16), preferred_element_type=F32) + b_ref[...]


def _ada_call(cond, w_ada, b_ada):
    tn = 1024
    return pl.pallas_call(
        _ada_kernel,
        grid=(DEPTH, 3 * D_MODEL // tn),
        in_specs=[
            pl.BlockSpec((COND_ROWS, D_MODEL), lambda i, n: (0, 0)),
            pl.BlockSpec((None, D_MODEL, tn), lambda i, n: (i, 0, n)),
            pl.BlockSpec((None, 1, tn), lambda i, n: (i, 0, n)),
        ],
        out_specs=pl.BlockSpec((None, COND_ROWS, tn), lambda i, n: (i, 0, n)),
        out_shape=jax.ShapeDtypeStruct((DEPTH, COND_ROWS, 3 * D_MODEL), F32),
        compiler_params=pltpu.CompilerParams(
            dimension_semantics=("arbitrary", "arbitrary"), vmem_limit_bytes=VMEM_LIMIT),
        name="ada_mod",
    )(cond, w_ada, b_ada.reshape(DEPTH, 1, 3 * D_MODEL))


def _pre_diff_ctx_kernel(x_ref, mods_ref, g_ref, w_ref, sk_in, sv_in,
                         q_ref, gate_ref, sk_ref, sv_ref):
    del sk_in, sv_in
    hb = _modulated(x_ref, mods_ref, g_ref, 0)
    n = D_MODEL
    q = jnp.dot(hb, w_ref[:, 0:n], preferred_element_type=F32)
    q_ref[...] = (q * (DA_DH ** -0.5)).astype(BF16)
    sk_ref[...] = jnp.dot(hb, w_ref[:, n:2 * n], preferred_element_type=F32)
    sv_ref[...] = jnp.dot(hb, w_ref[:, 2 * n:3 * n], preferred_element_type=F32)
    gate_ref[...] = _silu(jnp.dot(hb, w_ref[:, 3 * n:4 * n], preferred_element_type=F32))


def _pre_diff_ctx(x, mods, g_pre, w_in, st_k, st_v, layer, j):
    n_tok = N_CTX_B * CTX_S
    state_spec = pl.BlockSpec((None, None, CTX_S, D_MODEL), lambda t: (t, j, 0, 0))
    return pl.pallas_call(
        _pre_diff_ctx_kernel,
        grid=(N_CTX_B,),
        in_specs=[
            pl.BlockSpec((CTX_S, D_MODEL), lambda t: (t, 0)),
            pl.BlockSpec((None, COND_ROWS, 3 * D_MODEL), lambda t: (layer, 0, 0)),
            pl.BlockSpec((None, 1, D_MODEL), lambda t: (layer, 0, 0)),
            pl.BlockSpec((None, D_MODEL, 4 * D_MODEL), lambda t: (j, 0, 0)),
            pl.BlockSpec(memory_space=pl.ANY),
            pl.BlockSpec(memory_space=pl.ANY),
        ],
        out_specs=[
            pl.BlockSpec((CTX_S, D_MODEL), lambda t: (t, 0)),
            pl.BlockSpec((CTX_S, D_MODEL), lambda t: (t, 0)),
            state_spec,
            state_spec,
        ],
        out_shape=[
            jax.ShapeDtypeStruct((n_tok, D_MODEL), BF16),
            jax.ShapeDtypeStruct((n_tok, D_MODEL), F32),
            jax.ShapeDtypeStruct(st_k.shape, F32),
            jax.ShapeDtypeStruct(st_v.shape, F32),
        ],
        input_output_aliases={4: 2, 5: 3},
        compiler_params=pltpu.CompilerParams(
            dimension_semantics=("arbitrary",), vmem_limit_bytes=VMEM_LIMIT),
        name="pre_diff_ctx",
    )(x, mods, g_pre, w_in, st_k, st_v)


def _pre_diff_lat_kernel(x_ref, mods_ref, g_ref, w_ref, cos_ref, sin_ref,
                         q_ref, k_ref, v_ref, gate_ref):
    row = 1 + pl.program_id(0) // (LAT_S // TM)
    hb = _modulated(x_ref, mods_ref, g_ref, row)
    cos = cos_ref[...]
    sin = sin_ref[...]
    n = D_MODEL
    for h in range(HEADS):
        lo, hi = h * LANES, (h + 1) * LANES
        q = jnp.dot(hb, w_ref[:, lo:hi], preferred_element_type=F32)
        q_ref[:, lo:hi] = (_rope_slab(q, cos, sin) * (DA_DH ** -0.5)).astype(BF16)
        k = jnp.dot(hb, w_ref[:, n + lo:n + hi], preferred_element_type=F32)
        k_ref[:, lo:hi] = _rope_slab(k, cos, sin).astype(BF16)
    v_ref[...] = jnp.dot(hb, w_ref[:, 2 * n:3 * n], preferred_element_type=F32).astype(BF16)
    gate_ref[...] = _silu(jnp.dot(hb, w_ref[:, 3 * n:4 * n], preferred_element_type=F32))


def _pre_diff_lat(x, mods, g_pre, w_in, cos_t, sin_t, layer, j):
    n_tok = N_LAT_B * LAT_S
    tiles_per_req = LAT_S // TM
    tok_spec = pl.BlockSpec((TM, D_MODEL), lambda t: (t, 0))
    rope_spec = pl.BlockSpec((TM, LANES), lambda t: (t % tiles_per_req, 0))
    return pl.pallas_call(
        _pre_diff_lat_kernel,
        grid=(n_tok // TM,),
        in_specs=[
            tok_spec,
            pl.BlockSpec((None, COND_ROWS, 3 * D_MODEL), lambda t: (layer, 0, 0)),
            pl.BlockSpec((None, 1, D_MODEL), lambda t: (layer, 0, 0)),
            pl.BlockSpec((None, D_MODEL, 4 * D_MODEL), lambda t: (j, 0, 0)),
            rope_spec,
            rope_spec,
        ],
        out_specs=[tok_spec, tok_spec, tok_spec, tok_spec],
        out_shape=[
            jax.ShapeDtypeStruct((n_tok, D_MODEL), BF16),
            jax.ShapeDtypeStruct((n_tok, D_MODEL), BF16),
            jax.ShapeDtypeStruct((n_tok, D_MODEL), BF16),
            jax.ShapeDtypeStruct((n_tok, D_MODEL), F32),
        ],
        compiler_params=pltpu.CompilerParams(
            dimension_semantics=("arbitrary",), vmem_limit_bytes=VMEM_LIMIT),
        name="pre_diff_lat",
    )(x, mods, g_pre, w_in, cos_t, sin_t)


def _finish(og_ref, w_out_ref, x_ref, mods_ref, g_post_ref, row, o_ref):
    _, _, gate = _cond_row(mods_ref, row)
    y = jnp.dot(og_ref[...], w_out_ref[...], preferred_element_type=F32)
    o_ref[...] = x_ref[...] + gate * _rms(y, g_post_ref[...])


def _diff_lambda(lam_ref, lam_init):
    p = lam_ref[...]
    a = jnp.sum(p[0:1, :] * p[1:2, :], axis=-1, keepdims=True)
    b = jnp.sum(p[2:3, :] * p[3:4, :], axis=-1, keepdims=True)
    return jnp.exp(a) - jnp.exp(b) + lam_init


def _stack_diff_keys(k, k2_ref, sk):
    lane = lax.broadcasted_iota(jnp.int32, (sk, LANES), 1)
    zero = jnp.zeros((sk, LANES), BF16)
    for h in range(HEADS):
        kh = k[:, h * LANES:(h + 1) * LANES]
        k2_ref[h, 0:sk, :] = jnp.where(lane < DA_DH, kh, zero)
        k2_ref[h, sk:2 * sk, :] = jnp.where(lane >= DA_DH, kh, zero)


def _diff_heads(q_ref, k2_ref, v_ref, gate_ref, g_sub_ref, lam, lam_init, sk, og_ref):
    g_sub = g_sub_ref[...]
    for h in range(HEADS):
        lo, hi = h * LANES, (h + 1) * LANES
        s = lax.dot_general(q_ref[:, lo:hi], k2_ref[h], _TRANS_B, preferred_element_type=F32)
        s0 = s[:, 0:sk]
        s1 = s[:, sk:2 * sk]
        e0 = jnp.exp(s0 - jnp.max(s0, axis=-1, keepdims=True))
        e1 = jnp.exp(s1 - jnp.max(s1, axis=-1, keepdims=True))
        a0 = 1.0 / jnp.sum(e0, axis=-1, keepdims=True)
        a1 = lam / jnp.sum(e1, axis=-1, keepdims=True)
        w = (e0 * a0 - e1 * a1).astype(BF16)
        o = jnp.dot(w, v_ref[:, lo:hi], preferred_element_type=F32)
        o = _rms(o, g_sub) * (1.0 - lam_init)
        og_ref[:, lo:hi] = (o * gate_ref[:, lo:hi]).astype(BF16)


def _attn_diff_ctx_kernel(lam_init, q_ref, k_ref, v_ref, gate_ref, x_ref, mods_ref, g_post_ref,
                          w_out_ref, g_sub_ref, lam_ref, o_ref, k2_ref, vb_ref, og_ref):
    _stack_diff_keys(k_ref[...].astype(BF16), k2_ref, CTX_S)
    vb_ref[...] = v_ref[...].astype(BF16)
    lam = _diff_lambda(lam_ref, lam_init)
    _diff_heads(q_ref, k2_ref, vb_ref, gate_ref, g_sub_ref, lam, lam_init, CTX_S, og_ref)
    _finish(og_ref, w_out_ref, x_ref, mods_ref, g_post_ref, 0, o_ref)


def _attn_diff_ctx(q, st_k, st_v, gate, x, mods, g_post, w_out, g_sub, lam_p, layer, j):
    lam_init = 0.8 - 0.6 * math.exp(-0.3 * layer)
    tok_spec = pl.BlockSpec((CTX_S, D_MODEL), lambda b: (b, 0))
    state_spec = pl.BlockSpec((None, None, CTX_S, D_MODEL), lambda b: (b, j, 0, 0))
    return pl.pallas_call(
        functools.partial(_attn_diff_ctx_kernel, lam_init),
        grid=(N_CTX_B,),
        in_specs=[
            tok_spec, state_spec, state_spec, tok_spec, tok_spec,
            pl.BlockSpec((None, COND_ROWS, 3 * D_MODEL), lambda b: (layer, 0, 0)),
            pl.BlockSpec((None, 1, D_MODEL), lambda b: (layer, 0, 0)),
            pl.BlockSpec((None, D_MODEL, D_MODEL), lambda b: (layer, 0, 0)),
            pl.BlockSpec((None, 1, DA_DV), lambda b: (j, 0, 0)),
            pl.BlockSpec((None, 4, DA_DH), lambda b: (j, 0, 0)),
        ],
        out_specs=tok_spec,
        out_shape=jax.ShapeDtypeStruct(x.shape, F32),
        scratch_shapes=[
            pltpu.VMEM((HEADS, 2 * CTX_S, LANES), BF16),
            pltpu.VMEM((CTX_S, D_MODEL), BF16),
            pltpu.VMEM((CTX_S, D_MODEL), BF16),
        ],
        compiler_params=pltpu.CompilerParams(
            dimension_semantics=("arbitrary",), vmem_limit_bytes=VMEM_LIMIT),
        name="attn_diff_ctx",
    )(q, st_k, st_v, gate, x, mods, g_post, w_out, g_sub, lam_p)


def _attn_diff_lat_kernel(lam_init, q_ref, k_ref, v_ref, ck_ref, cv_ref, gate_ref, x_ref, mods_ref,
                          g_post_ref, w_out_ref, g_sub_ref, lam_ref, o_ref, k2_ref, vb_ref, og_ref):
    sk = LAT_S + PAST

    @pl.when(pl.program_id(1) == 0)
    def _():
        lane = lax.broadcasted_iota(jnp.int32, (PAST, LANES), 1)
        zero_l = jnp.zeros((LAT_S, LANES), BF16)
        zero_p = jnp.zeros((PAST, LANES), BF16)
        lane_l = lax.broadcasted_iota(jnp.int32, (LAT_S, LANES), 1)
        for h in range(HEADS):
            lo, hi = h * LANES, (h + 1) * LANES
            kh = k_ref[:, lo:hi]
            ch = ck_ref[:, lo:hi].astype(BF16)
            k2_ref[h, 0:LAT_S, :] = jnp.where(lane_l < DA_DH, kh, zero_l)
            k2_ref[h, LAT_S:sk, :] = jnp.where(lane < DA_DH, ch, zero_p)
            k2_ref[h, sk:sk + LAT_S, :] = jnp.where(lane_l >= DA_DH, kh, zero_l)
            k2_ref[h, sk + LAT_S:2 * sk, :] = jnp.where(lane >= DA_DH, ch, zero_p)
        vb_ref[0:LAT_S, :] = v_ref[...]
        vb_ref[LAT_S:sk, :] = cv_ref[...].astype(BF16)

    lam = _diff_lambda(lam_ref, lam_init)
    _diff_heads(q_ref, k2_ref, vb_ref, gate_ref, g_sub_ref, lam, lam_init, sk, og_ref)
    _finish(og_ref, w_out_ref, x_ref, mods_ref, g_post_ref, 1 + pl.program_id(0), o_ref)


def _attn_diff_lat(q, k, v, cache_k, cache_v, gate, x, mods, g_post, w_out, g_sub, lam_p, layer, j):
    lam_init = 0.8 - 0.6 * math.exp(-0.3 * layer)
    sk = LAT_S + PAST
    nq = LAT_S // TQ
    tok_spec = pl.BlockSpec((TQ, D_MODEL), lambda b, i: (b * nq + i, 0))
    kv_spec = pl.BlockSpec((LAT_S, D_MODEL), lambda b, i: (b, 0))
    cache_spec = pl.BlockSpec((None, None, PAST, D_MODEL), lambda b, i: (b, j, 0, 0))
    return pl.pallas_call(
        functools.partial(_attn_diff_lat_kernel, lam_init),
        grid=(N_LAT_B, nq),
        in_specs=[
            tok_spec, kv_spec, kv_spec, cache_spec, cache_spec, tok_spec, tok_spec,
            pl.BlockSpec((None, COND_ROWS, 3 * D_MODEL), lambda b, i: (layer, 0, 0)),
            pl.BlockSpec((None, 1, D_MODEL), lambda b, i: (layer, 0, 0)),
            pl.BlockSpec((None, D_MODEL, D_MODEL), lambda b, i: (layer, 0, 0)),
            pl.BlockSpec((None, 1, DA_DV), lambda b, i: (j, 0, 0)),
            pl.BlockSpec((None, 4, DA_DH), lambda b, i: (j, 0, 0)),
        ],
        out_specs=tok_spec,
        out_shape=jax.ShapeDtypeStruct(x.shape, F32),
        scratch_shapes=[
            pltpu.VMEM((HEADS, 2 * sk, LANES), BF16),
            pltpu.VMEM((sk, D_MODEL), BF16),
            pltpu.VMEM((TQ, D_MODEL), BF16),
        ],
        compiler_params=pltpu.CompilerParams(
            dimension_semantics=("arbitrary", "arbitrary"), vmem_limit_bytes=VMEM_LIMIT),
        name="attn_diff_lat",
    )(q, k, v, cache_k, cache_v, gate, x, mods, g_post, w_out, g_sub, lam_p)


_QA0, _QA1 = 0, MLA_Q_LORA
_KV0, _KV1 = MLA_Q_LORA, MLA_Q_LORA + MLA_KV_LORA
_GT0, _GT1 = _KV1, _KV1 + D_MODEL
_PE0, _PE1 = _GT1, _GT1 + LANES


def _mla_queries(hb, w_in_ref, g_qa_ref, w_qb_ref):
    q_a = jnp.dot(hb, w_in_ref[:, _QA0:_QA1], preferred_element_type=F32)
    return _rms(q_a, g_qa_ref[...]).astype(BF16)


def _pre_mla_ctx_kernel(x_ref, mods_ref, g_ref, w_in_ref, g_qa_ref, w_qb_ref, g_kva_ref, sc_in, sp_in,
                        q_ref, gate_ref, ckv_ref, kpe_ref):
    del sc_in, sp_in
    hb = _modulated(x_ref, mods_ref, g_ref, 0)
    qn = _mla_queries(hb, w_in_ref, g_qa_ref, w_qb_ref)
    q_ref[...] = jnp.dot(qn, w_qb_ref[...], preferred_element_type=F32).astype(BF16)
    kv_a = jnp.dot(hb, w_in_ref[:, _KV0:_KV1], preferred_element_type=F32)
    ckv_ref[...] = _rms(kv_a, g_kva_ref[...])
    gate_ref[...] = _silu(jnp.dot(hb, w_in_ref[:, _GT0:_GT1], preferred_element_type=F32))
    pe = jnp.dot(hb, w_in_ref[:, _PE0:_PE1], preferred_element_type=F32)
    kpe_ref[...] = pe[:, 0:MLA_ROPE]


def _pre_mla_ctx(x, mods, g_pre, w_in, g_qa, w_qb, g_kva, st_ckv, st_kpe, layer, j):
    n_tok = N_CTX_B * CTX_S
    tok = lambda w: pl.BlockSpec((CTX_S, w), lambda t: (t, 0))
    return pl.pallas_call(
        _pre_mla_ctx_kernel,
        grid=(N_CTX_B,),
        in_specs=[
            tok(D_MODEL),
            pl.BlockSpec((None, COND_ROWS, 3 * D_MODEL), lambda t: (layer, 0, 0)),
            pl.BlockSpec((None, 1, D_MODEL), lambda t: (layer, 0, 0)),
            pl.BlockSpec((None, D_MODEL, MLA_IN_PAD), lambda t: (j, 0, 0)),
            pl.BlockSpec((None, 1, MLA_Q_LORA), lambda t: (j, 0, 0)),
            pl.BlockSpec((None, MLA_Q_LORA, HEADS * MLA_QK_PAD), lambda t: (j, 0, 0)),
            pl.BlockSpec((None, 1, MLA_KV_LORA), lambda t: (j, 0, 0)),
            pl.BlockSpec(memory_space=pl.ANY),
            pl.BlockSpec(memory_space=pl.ANY),
        ],
        out_specs=[
            tok(HEADS * MLA_QK_PAD),
            tok(D_MODEL),
            pl.BlockSpec((None, None, CTX_S, MLA_KV_LORA), lambda t: (t, j, 0, 0)),
            pl.BlockSpec((None, None, CTX_S, MLA_ROPE), lambda t: (t, j, 0, 0)),
        ],
        out_shape=[
            jax.ShapeDtypeStruct((n_tok, HEADS * MLA_QK_PAD), BF16),
            jax.ShapeDtypeStruct((n_tok, D_MODEL), F32),
            jax.ShapeDtypeStruct(st_ckv.shape, F32),
            jax.ShapeDtypeStruct(st_kpe.shape, F32),
        ],
        input_output_aliases={7: 2, 8: 3},
        compiler_params=pltpu.CompilerParams(
            dimension_semantics=("arbitrary",), vmem_limit_bytes=VMEM_LIMIT),
        name="pre_mla_ctx",
    )(x, mods, g_pre, w_in, g_qa, w_qb, g_kva, st_ckv, st_kpe)


def _pre_mla_lat_kernel(x_ref, mods_ref, g_ref, w_in_ref, g_qa_ref, w_qb_ref, g_kva_ref, cos_ref, sin_ref,
                        q_ref, gate_ref, ckv_ref, kpe_ref):
    row = 1 + pl.program_id(0) // (LAT_S // TM)
    hb = _modulated(x_ref, mods_ref, g_ref, row)
    cos = cos_ref[...]
    sin = sin_ref[...]
    qn = _mla_queries(hb, w_in_ref, g_qa_ref, w_qb_ref)
    for h in range(HEADS):
        lo = h * MLA_QK_PAD
        q = jnp.dot(qn, w_qb_ref[:, lo:lo + MLA_QK_PAD], preferred_element_type=F32)
        q_ref[:, lo:lo + LANES] = q[:, 0:LANES].astype(BF16)
        q_ref[:, lo + LANES:lo + 2 * LANES] = _rope_slab(q[:, LANES:2 * LANES], cos, sin).astype(BF16)
    kv_a = jnp.dot(hb, w_in_ref[:, _KV0:_KV1], preferred_element_type=F32)
    ckv_ref[...] = _rms(kv_a, g_kva_ref[...]).astype(BF16)
    gate_ref[...] = _silu(jnp.dot(hb, w_in_ref[:, _GT0:_GT1], preferred_element_type=F32))
    pe = jnp.dot(hb, w_in_ref[:, _PE0:_PE1], preferred_element_type=F32)
    kpe_ref[...] = _rope_slab(pe, cos, sin)[:, 0:MLA_ROPE].astype(BF16)


def _pre_mla_lat(x, mods, g_pre, w_in, g_qa, w_qb, g_kva, cos_t, sin_t, layer, j):
    n_tok = N_LAT_B * LAT_S
    tiles_per_req = LAT_S // TM
    tok = lambda w: pl.BlockSpec((TM, w), lambda t: (t, 0))
    rope_spec = pl.BlockSpec((TM, LANES), lambda t: (t % tiles_per_req, 0))
    return pl.pallas_call(
        _pre_mla_lat_kernel,
        grid=(n_tok // TM,),
        in_specs=[
            tok(D_MODEL),
            pl.BlockSpec((None, COND_ROWS, 3 * D_MODEL), lambda t: (layer, 0, 0)),
            pl.BlockSpec((None, 1, D_MODEL), lambda t: (layer, 0, 0)),
            pl.BlockSpec((None, D_MODEL, MLA_IN_PAD), lambda t: (j, 0, 0)),
            pl.BlockSpec((None, 1, MLA_Q_LORA), lambda t: (j, 0, 0)),
            pl.BlockSpec((None, MLA_Q_LORA, HEADS * MLA_QK_PAD), lambda t: (j, 0, 0)),
            pl.BlockSpec((None, 1, MLA_KV_LORA), lambda t: (j, 0, 0)),
            rope_spec,
            rope_spec,
        ],
        out_specs=[tok(HEADS * MLA_QK_PAD), tok(D_MODEL), tok(MLA_KV_LORA), tok(MLA_ROPE)],
        out_shape=[
            jax.ShapeDtypeStruct((n_tok, HEADS * MLA_QK_PAD), BF16),
            jax.ShapeDtypeStruct((n_tok, D_MODEL), F32),
            jax.ShapeDtypeStruct((n_tok, MLA_KV_LORA), BF16),
            jax.ShapeDtypeStruct((n_tok, MLA_ROPE), BF16),
        ],
        compiler_params=pltpu.CompilerParams(
            dimension_semantics=("arbitrary",), vmem_limit_bytes=VMEM_LIMIT),
        name="pre_mla_lat",
    )(x, mods, g_pre, w_in, g_qa, w_qb, g_kva, cos_t, sin_t)


def _mla_expand(ckv, kpe, w_kvb_ref, kf_ref, vf_ref, r0, rows):
    zero = jnp.zeros((rows, LANES - MLA_ROPE), BF16)
    for h in range(HEADS):
        lo = h * (MLA_NOPE + MLA_DV)
        kv = jnp.dot(ckv, w_kvb_ref[:, lo:lo + MLA_NOPE + MLA_DV], preferred_element_type=F32)
        kf_ref[h, r0:r0 + rows, 0:MLA_NOPE] = kv[:, 0:MLA_NOPE].astype(BF16)
        kf_ref[h, r0:r0 + rows, MLA_NOPE:MLA_NOPE + MLA_ROPE] = kpe
        kf_ref[h, r0:r0 + rows, MLA_NOPE + MLA_ROPE:MLA_QK_PAD] = zero
        vf_ref[h, r0:r0 + rows, :] = kv[:, MLA_NOPE:MLA_NOPE + MLA_DV].astype(BF16)


def _mla_heads(q_ref, kf_ref, vf_ref, gate_ref, og_ref):
    scale = (MLA_NOPE + MLA_ROPE) ** -0.5
    for h in range(HEADS):
        qh = q_ref[:, h * MLA_QK_PAD:(h + 1) * MLA_QK_PAD]
        s = lax.dot_general(qh, kf_ref[h], _TRANS_B, preferred_element_type=F32) * scale
        e = jnp.exp(s - jnp.max(s, axis=-1, keepdims=True))
        inv = 1.0 / jnp.sum(e, axis=-1, keepdims=True)
        o = jnp.dot(e.astype(BF16), vf_ref[h], preferred_element_type=F32) * inv
        lo, hi = h * LANES, (h + 1) * LANES
        og_ref[:, lo:hi] = (o * gate_ref[:, lo:hi]).astype(BF16)


def _attn_mla_ctx_kernel(q_ref, ckv_ref, kpe_ref, gate_ref, x_ref, mods_ref, g_post_ref, w_out_ref,
                         w_kvb_ref, o_ref, kf_ref, vf_ref, og_ref):
    _mla_expand(ckv_ref[...].astype(BF16), kpe_ref[...].astype(BF16), w_kvb_ref, kf_ref, vf_ref, 0, CTX_S)
    _mla_heads(q_ref, kf_ref, vf_ref, gate_ref, og_ref)
    _finish(og_ref, w_out_ref, x_ref, mods_ref, g_post_ref, 0, o_ref)


def _attn_mla_ctx(q, st_ckv, st_kpe, gate, x, mods, g_post, w_out, w_kvb, layer, j):
    tok = lambda w: pl.BlockSpec((CTX_S, w), lambda b: (b, 0))
    return pl.pallas_call(
        _attn_mla_ctx_kernel,
        grid=(N_CTX_B,),
        in_specs=[
            tok(HEADS * MLA_QK_PAD),
            pl.BlockSpec((None, None, CTX_S, MLA_KV_LORA), lambda b: (b, j, 0, 0)),
            pl.BlockSpec((None, None, CTX_S, MLA_ROPE), lambda b: (b, j, 0, 0)),
            tok(D_MODEL), tok(D_MODEL),
            pl.BlockSpec((None, COND_ROWS, 3 * D_MODEL), lambda b: (layer, 0, 0)),
            pl.BlockSpec((None, 1, D_MODEL), lambda b: (layer, 0, 0)),
            pl.BlockSpec((None, D_MODEL, D_MODEL), lambda b: (layer, 0, 0)),
            pl.BlockSpec((None, MLA_KV_LORA, HEADS * (MLA_NOPE + MLA_DV)), lambda b: (j, 0, 0)),
        ],
        out_specs=tok(D_MODEL),
        out_shape=jax.ShapeDtypeStruct(x.shape, F32),
        scratch_shapes=[
            pltpu.VMEM((HEADS, CTX_S, MLA_QK_PAD), BF16),
            pltpu.VMEM((HEADS, CTX_S, MLA_DV), BF16),
            pltpu.VMEM((CTX_S, D_MODEL), BF16),
        ],
        compiler_params=pltpu.CompilerParams(
            dimension_semantics=("arbitrary",), vmem_limit_bytes=VMEM_LIMIT),
        name="attn_mla_ctx",
    )(q, st_ckv, st_kpe, gate, x, mods, g_post, w_out, w_kvb)


def _attn_mla_lat_kernel(q_ref, ckv_ref, kpe_ref, cckv_ref, ckpe_ref, gate_ref, x_ref, mods_ref,
                         g_post_ref, w_out_ref, w_kvb_ref, o_ref, kf_ref, vf_ref, og_ref):
    @pl.when(pl.program_id(1) == 0)
    def _():
        _mla_expand(ckv_ref[...], kpe_ref[...], w_kvb_ref, kf_ref, vf_ref, 0, LAT_S)
        _mla_expand(cckv_ref[...].astype(BF16), ckpe_ref[...].astype(BF16), w_kvb_ref, kf_ref, vf_ref,
                    LAT_S, PAST)

    _mla_heads(q_ref, kf_ref, vf_ref, gate_ref, og_ref)
    _finish(og_ref, w_out_ref, x_ref, mods_ref, g_post_ref, 1 + pl.program_id(0), o_ref)


def _attn_mla_lat(q, ckv, kpe, cache_ckv, cache_kpe, gate, x, mods, g_post, w_out, w_kvb, layer, j):
    sk = LAT_S + PAST
    nq = LAT_S // TQ
    tok = lambda w: pl.BlockSpec((TQ, w), lambda b, i: (b * nq + i, 0))
    req = lambda w: pl.BlockSpec((LAT_S, w), lambda b, i: (b, 0))
    return pl.pallas_call(
        _attn_mla_lat_kernel,
        grid=(N_LAT_B, nq),
        in_specs=[
            tok(HEADS * MLA_QK_PAD), req(MLA_KV_LORA), req(MLA_ROPE),
            pl.BlockSpec((None, None, PAST, MLA_KV_LORA), lambda b, i: (b, j, 0, 0)),
            pl.BlockSpec((None, None, PAST, MLA_ROPE), lambda b, i: (b, j, 0, 0)),
            tok(D_MODEL), tok(D_MODEL),
            pl.BlockSpec((None, COND_ROWS, 3 * D_MODEL), lambda b, i: (layer, 0, 0)),
            pl.BlockSpec((None, 1, D_MODEL), lambda b, i: (layer, 0, 0)),
            pl.BlockSpec((None, D_MODEL, D_MODEL), lambda b, i: (layer, 0, 0)),
            pl.BlockSpec((None, MLA_KV_LORA, HEADS * (MLA_NOPE + MLA_DV)), lambda b, i: (j, 0, 0)),
        ],
        out_specs=tok(D_MODEL),
        out_shape=jax.ShapeDtypeStruct(x.shape, F32),
        scratch_shapes=[
            pltpu.VMEM((HEADS, sk, MLA_QK_PAD), BF16),
            pltpu.VMEM((HEADS, sk, MLA_DV), BF16),
            pltpu.VMEM((TQ, D_MODEL), BF16),
        ],
        compiler_params=pltpu.CompilerParams(
            dimension_semantics=("arbitrary", "arbitrary"), vmem_limit_bytes=VMEM_LIMIT),
        name="attn_mla_lat",
    )(q, ckv, kpe, cache_ckv, cache_kpe, gate, x, mods, g_post, w_out, w_kvb)


def _rope_tables():
    rows = LAT_S // GRID_W
    row = jnp.repeat(jnp.arange(rows), GRID_W).astype(F32)
    col = jnp.tile(jnp.arange(GRID_W), rows).astype(F32)
    n_freq = DA_DH // 4
    inv = ROPE_THETA ** (-jnp.arange(n_freq, dtype=F32) / n_freq)
    ang = jnp.concatenate([row[:, None] * inv, col[:, None] * inv], axis=-1)
    cos, sin = jnp.cos(ang), jnp.sin(ang)
    return (jnp.concatenate([cos, cos, cos, cos], axis=-1),
            jnp.concatenate([-sin, sin, -sin, sin], axis=-1))


def kernel(x_prompt, x_sample, cache_diff_k, cache_diff_v, cache_mla_ckv, cache_mla_kpe,
           c, c_ctx, w_ada, b_ada, g_pre, g_post, w_out,
           da_w_in, da_lam_q1, da_lam_k1, da_lam_q2, da_lam_k2, da_g_sub,
           mla_w_in, mla_g_qa, mla_w_qb, mla_g_kva, mla_w_kvb):
    assert DA_DH == MLA_ROPE
    n_diff = (DEPTH + 1) // 2
    n_mla = DEPTH // 2

    cond = jnp.concatenate(
        [c_ctx[None, :], c, jnp.zeros((COND_ROWS - 1 - N_LAT_B, D_MODEL), F32)], axis=0)
    mods = _ada_call(cond, w_ada, b_ada)

    cos_t, sin_t = _rope_tables()
    g_pre3 = g_pre.reshape(DEPTH, 1, D_MODEL)
    g_post3 = g_post.reshape(DEPTH, 1, D_MODEL)
    w_out_b = w_out.astype(BF16)

    da_w_in_b = da_w_in.astype(BF16)
    g_sub3 = da_g_sub.reshape(n_diff, 1, DA_DV)
    lam_p = jnp.stack([da_lam_q1, da_lam_k1, da_lam_q2, da_lam_k2], axis=1)
    cache_dk = cache_diff_k.reshape(N_LAT_B, n_diff, PAST, D_MODEL)
    cache_dv = cache_diff_v.reshape(N_LAT_B, n_diff, PAST, D_MODEL)

    o1 = MLA_Q_LORA + MLA_KV_LORA
    o2 = o1 + MLA_ROPE
    mla_w_in_b = jnp.concatenate(
        [mla_w_in[:, :, :o1], mla_w_in[:, :, o2:], mla_w_in[:, :, o1:o2],
         jnp.zeros((n_mla, D_MODEL, MLA_IN_PAD - (o2 + D_MODEL)), F32)], axis=-1).astype(BF16)
    w_qb4 = mla_w_qb.reshape(n_mla, MLA_Q_LORA, HEADS, MLA_NOPE + MLA_ROPE)
    w_qb_b = jnp.pad(w_qb4, ((0, 0), (0, 0), (0, 0), (0, MLA_QK_PAD - MLA_NOPE - MLA_ROPE))).reshape(
        n_mla, MLA_Q_LORA, HEADS * MLA_QK_PAD).astype(BF16)
    w_kvb_b = mla_w_kvb.astype(BF16)
    g_qa3 = mla_g_qa.reshape(n_mla, 1, MLA_Q_LORA)
    g_kva3 = mla_g_kva.reshape(n_mla, 1, MLA_KV_LORA)

    x_ctx = x_prompt.reshape(N_CTX_B * CTX_S, D_MODEL)
    x_lat = x_sample.reshape(N_LAT_B * LAT_S, D_MODEL)
    st_dk = jnp.zeros((N_CTX_B, n_diff, CTX_S, D_MODEL), F32)
    st_dv = jnp.zeros((N_CTX_B, n_diff, CTX_S, D_MODEL), F32)
    st_ckv = jnp.zeros((N_CTX_B, n_mla, CTX_S, MLA_KV_LORA), F32)
    st_kpe = jnp.zeros((N_CTX_B, n_mla, CTX_S, MLA_ROPE), F32)

    for i in range(DEPTH):
        j = i // 2
        if i % 2 == 0:
            q_c, gate_c, st_dk, st_dv = _pre_diff_ctx(x_ctx, mods, g_pre3, da_w_in_b, st_dk, st_dv, i, j)
            q_l, k_l, v_l, gate_l = _pre_diff_lat(x_lat, mods, g_pre3, da_w_in_b, cos_t, sin_t, i, j)
            x_ctx = _attn_diff_ctx(q_c, st_dk, st_dv, gate_c, x_ctx, mods, g_post3, w_out_b,
                                   g_sub3, lam_p, i, j)
            x_lat = _attn_diff_lat(q_l, k_l, v_l, cache_dk, cache_dv, gate_l, x_lat, mods, g_post3,
                                   w_out_b, g_sub3, lam_p, i, j)
        else:
            q_c, gate_c, st_ckv, st_kpe = _pre_mla_ctx(
                x_ctx, mods, g_pre3, mla_w_in_b, g_qa3, w_qb_b, g_kva3, st_ckv, st_kpe, i, j)
            q_l, gate_l, ckv_l, kpe_l = _pre_mla_lat(
                x_lat, mods, g_pre3, mla_w_in_b, g_qa3, w_qb_b, g_kva3, cos_t, sin_t, i, j)
            x_ctx = _attn_mla_ctx(q_c, st_ckv, st_kpe, gate_c, x_ctx, mods, g_post3, w_out_b,
                                  w_kvb_b, i, j)
            x_lat = _attn_mla_lat(q_l, ckv_l, kpe_l, cache_mla_ckv, cache_mla_kpe, gate_l, x_lat,
                                  mods, g_post3, w_out_b, w_kvb_b, i, j)

    return (x_ctx.reshape(N_CTX_B, CTX_S, D_MODEL),
            x_lat.reshape(N_LAT_B, LAT_S, D_MODEL),
            st_dk.reshape(N_CTX_B, n_diff, CTX_S, HEADS, 2, DA_DH),
            st_dv.reshape(N_CTX_B, n_diff, CTX_S, HEADS, DA_DV),
            st_ckv,
            st_kpe)
```

```python
import functools
import math

import jax
import jax.numpy as jnp
from jax import lax
from jax.experimental import pallas as pl
from jax.experimental.pallas import tpu as pltpu

F32 = jnp.float32
BF16 = jnp.bfloat16

D_MODEL = 1024
DEPTH = 4
N_CTX_B = 16
CTX_S = 256
N_LAT_B = 4
LAT_S = 1024
PAST = 256
GRID_W = 64
HEADS = 8
DA_DH = 64
DA_DV = 128
MLA_Q_LORA = 384
MLA_KV_LORA = 256
MLA_NOPE = 128
MLA_ROPE = 64
MLA_DV = 128
MLA_QK_PAD = 256
MLA_IN_PAD = 1792
ROPE_THETA = 10000.0
NORM_EPS = 1e-6
COND_ROWS = 8

LANES = 128
TQ = 256
TM = 256
VMEM_LIMIT = 56 * 1024 * 1024

_TRANS_B = (((1,), (1,)), ((), ()))
SCORES_AHEAD = 2
DA_Q_SCALE = DA_DH ** -0.5 * math.log2(math.e)
MLA_Q_SCALE = (MLA_NOPE + MLA_ROPE) ** -0.5 * math.log2(math.e)


def _silu(x):
    return x * (1.0 / (1.0 + jnp.exp(-x)))


def _rms(x, g):
    r = lax.rsqrt(jnp.mean(x * x, axis=-1, keepdims=True) + NORM_EPS)
    return (x * r) * g


def _rope_slab(x, cos, sin_signed):
    lane = lax.broadcasted_iota(jnp.int32, x.shape, 1)
    first_half = (lane % 64) < 32
    partner = jnp.where(first_half, pltpu.roll(x, 96, axis=1), pltpu.roll(x, 32, axis=1))
    return x * cos + partner * sin_signed


def _cond_row(mods_ref, row):
    m = mods_ref[pl.ds(row, 1), :]
    return m[:, 0:D_MODEL], m[:, D_MODEL:2 * D_MODEL], m[:, 2 * D_MODEL:3 * D_MODEL]


def _modulated(x_ref, mods_ref, g_ref, row):
    shift, scale, _ = _cond_row(mods_ref, row)
    h = _rms(x_ref[...], g_ref[...]) * (1.0 + scale) + shift
    return h.astype(BF16)


def _ada_kernel(cond_ref, w_ref, b_ref, o_ref):
    a = _silu(cond_ref[...]).astype(BF16)
    o_ref[...] = jnp.dot(a, w_ref[...].astype(BF16), preferred_element_type=F32) + b_ref[...]


def _ada_call(cond, w_ada, b_ada):
    tn = 1024
    return pl.pallas_call(
        _ada_kernel,
        grid=(DEPTH, 3 * D_MODEL // tn),
        in_specs=[
            pl.BlockSpec((COND_ROWS, D_MODEL), lambda i, n: (0, 0)),
            pl.BlockSpec((None, D_MODEL, tn), lambda i, n: (i, 0, n)),
            pl.BlockSpec((None, 1, tn), lambda i, n: (i, 0, n)),
        ],
        out_specs=pl.BlockSpec((None, COND_ROWS, tn), lambda i, n: (i, 0, n)),
        out_shape=jax.ShapeDtypeStruct((DEPTH, COND_ROWS, 3 * D_MODEL), F32),
        compiler_params=pltpu.CompilerParams(
            dimension_semantics=("arbitrary", "arbitrary"), vmem_limit_bytes=VMEM_LIMIT),
        name="ada_mod",
    )(cond, w_ada, b_ada.reshape(DEPTH, 1, 3 * D_MODEL))


def _pre_diff_ctx_kernel(x_ref, mods_ref, g_ref, w_ref, sk_in, sv_in,
                         q_ref, gate_ref, sk_ref, sv_ref):
    del sk_in, sv_in
    hb = _modulated(x_ref, mods_ref, g_ref, 0)
    n = D_MODEL
    q = jnp.dot(hb, w_ref[:, 0:n], preferred_element_type=F32)
    q_ref[...] = (q * DA_Q_SCALE).astype(BF16)
    sk_ref[...] = jnp.dot(hb, w_ref[:, n:2 * n], preferred_element_type=F32)
    sv_ref[...] = jnp.dot(hb, w_ref[:, 2 * n:3 * n], preferred_element_type=F32)
    gate_ref[...] = _silu(jnp.dot(hb, w_ref[:, 3 * n:4 * n], preferred_element_type=F32))


def _pre_diff_ctx(x, mods, g_pre, w_in, st_k, st_v, layer, j):
    n_tok = N_CTX_B * CTX_S
    state_spec = pl.BlockSpec((None, None, CTX_S, D_MODEL), lambda t: (t, j, 0, 0))
    return pl.pallas_call(
        _pre_diff_ctx_kernel,
        grid=(N_CTX_B,),
        in_specs=[
            pl.BlockSpec((CTX_S, D_MODEL), lambda t: (t, 0)),
            pl.BlockSpec((None, COND_ROWS, 3 * D_MODEL), lambda t: (layer, 0, 0)),
            pl.BlockSpec((None, 1, D_MODEL), lambda t: (layer, 0, 0)),
            pl.BlockSpec((None, D_MODEL, 4 * D_MODEL), lambda t: (j, 0, 0)),
            pl.BlockSpec(memory_space=pl.ANY),
            pl.BlockSpec(memory_space=pl.ANY),
        ],
        out_specs=[
            pl.BlockSpec((CTX_S, D_MODEL), lambda t: (t, 0)),
            pl.BlockSpec((CTX_S, D_MODEL), lambda t: (t, 0)),
            state_spec,
            state_spec,
        ],
        out_shape=[
            jax.ShapeDtypeStruct((n_tok, D_MODEL), BF16),
            jax.ShapeDtypeStruct((n_tok, D_MODEL), F32),
            jax.ShapeDtypeStruct(st_k.shape, F32),
            jax.ShapeDtypeStruct(st_v.shape, F32),
        ],
        input_output_aliases={4: 2, 5: 3},
        compiler_params=pltpu.CompilerParams(
            dimension_semantics=("arbitrary",), vmem_limit_bytes=VMEM_LIMIT),
        name="pre_diff_ctx",
    )(x, mods, g_pre, w_in, st_k, st_v)


def _pre_diff_lat_kernel(x_ref, mods_ref, g_ref, w_ref, cos_ref, sin_ref,
                         q_ref, k_ref, v_ref, gate_ref):
    row = 1 + pl.program_id(0) // (LAT_S // TM)
    hb = _modulated(x_ref, mods_ref, g_ref, row)
    cos = cos_ref[...]
    sin = sin_ref[...]
    n = D_MODEL
    for h in range(HEADS):
        lo, hi = h * LANES, (h + 1) * LANES
        q = jnp.dot(hb, w_ref[:, lo:hi], preferred_element_type=F32)
        q_ref[:, lo:hi] = (_rope_slab(q, cos, sin) * DA_Q_SCALE).astype(BF16)
        k = jnp.dot(hb, w_ref[:, n + lo:n + hi], preferred_element_type=F32)
        k_ref[:, lo:hi] = _rope_slab(k, cos, sin).astype(BF16)
    v_ref[...] = jnp.dot(hb, w_ref[:, 2 * n:3 * n], preferred_element_type=F32).astype(BF16)
    gate_ref[...] = _silu(jnp.dot(hb, w_ref[:, 3 * n:4 * n], preferred_element_type=F32))


def _pre_diff_lat(x, mods, g_pre, w_in, cos_t, sin_t, layer, j):
    n_tok = N_LAT_B * LAT_S
    tiles_per_req = LAT_S // TM
    tok_spec = pl.BlockSpec((TM, D_MODEL), lambda t: (t, 0))
    rope_spec = pl.BlockSpec((TM, LANES), lambda t: (t % tiles_per_req, 0))
    return pl.pallas_call(
        _pre_diff_lat_kernel,
        grid=(n_tok // TM,),
        in_specs=[
            tok_spec,
            pl.BlockSpec((None, COND_ROWS, 3 * D_MODEL), lambda t: (layer, 0, 0)),
            pl.BlockSpec((None, 1, D_MODEL), lambda t: (layer, 0, 0)),
            pl.BlockSpec((None, D_MODEL, 4 * D_MODEL), lambda t: (j, 0, 0)),
            rope_spec,
            rope_spec,
        ],
        out_specs=[tok_spec, tok_spec, tok_spec, tok_spec],
        out_shape=[
            jax.ShapeDtypeStruct((n_tok, D_MODEL), BF16),
            jax.ShapeDtypeStruct((n_tok, D_MODEL), BF16),
            jax.ShapeDtypeStruct((n_tok, D_MODEL), BF16),
            jax.ShapeDtypeStruct((n_tok, D_MODEL), F32),
        ],
        compiler_params=pltpu.CompilerParams(
            dimension_semantics=("arbitrary",), vmem_limit_bytes=VMEM_LIMIT),
        name="pre_diff_lat",
    )(x, mods, g_pre, w_in, cos_t, sin_t)


def _finish(heads, w_out_ref, x_ref, mods_ref, g_post_ref, row, o_ref):
    _, _, gate = _cond_row(mods_ref, row)
    og = jnp.concatenate(heads, axis=1)
    y = jnp.dot(og, w_out_ref[...], preferred_element_type=F32)
    o_ref[...] = x_ref[...] + gate * _rms(y, g_post_ref[...])


def _diff_lambda(lam_ref, lam_init):
    p = lam_ref[...]
    a = jnp.sum(p[0:1, :] * p[1:2, :], axis=-1, keepdims=True)
    b = jnp.sum(p[2:3, :] * p[3:4, :], axis=-1, keepdims=True)
    return jnp.exp(a) - jnp.exp(b) + lam_init


def _stack_diff_keys(k, k2_ref, sk):
    lane = lax.broadcasted_iota(jnp.int32, (sk, LANES), 1)
    zero = jnp.zeros((sk, LANES), BF16)
    for h in range(HEADS):
        kh = k[:, h * LANES:(h + 1) * LANES]
        k2_ref[h, 0:sk, :] = jnp.where(lane < DA_DH, kh, zero)
        k2_ref[h, sk:2 * sk, :] = jnp.where(lane >= DA_DH, kh, zero)


def _diff_heads(q_ref, k2_ref, v_ref, gate_ref, g_sub_ref, lam, lam_init, sk):
    g_sub = g_sub_ref[...]

    def scores(h):
        return lax.dot_general(q_ref[:, h * LANES:(h + 1) * LANES], k2_ref[h], _TRANS_B,
                               preferred_element_type=F32)

    heads = []
    pending = [scores(h) for h in range(SCORES_AHEAD)]
    for h in range(HEADS):
        lo, hi = h * LANES, (h + 1) * LANES
        s = pending.pop(0)
        if h + SCORES_AHEAD < HEADS:
            pending.append(scores(h + SCORES_AHEAD))
        s0 = s[:, 0:sk]
        s1 = s[:, sk:2 * sk]
        e0 = jnp.exp2(s0 - jnp.max(s0, axis=-1, keepdims=True))
        e1 = jnp.exp2(s1 - jnp.max(s1, axis=-1, keepdims=True))
        a0 = 1.0 / jnp.sum(e0, axis=-1, keepdims=True)
        a1 = lam / jnp.sum(e1, axis=-1, keepdims=True)
        w = (e0 * a0 - e1 * a1).astype(BF16)
        o = jnp.dot(w, v_ref[:, lo:hi], preferred_element_type=F32)
        o = _rms(o, g_sub) * (1.0 - lam_init)
        heads.append((o * gate_ref[:, lo:hi]).astype(BF16))
    return heads


def _attn_diff_ctx_kernel(lam_init, q_ref, k_ref, v_ref, gate_ref, x_ref, mods_ref, g_post_ref,
                          w_out_ref, g_sub_ref, lam_ref, o_ref, k2_ref, vb_ref):
    _stack_diff_keys(k_ref[...].astype(BF16), k2_ref, CTX_S)
    vb_ref[...] = v_ref[...].astype(BF16)
    lam = _diff_lambda(lam_ref, lam_init)
    heads = _diff_heads(q_ref, k2_ref, vb_ref, gate_ref, g_sub_ref, lam, lam_init, CTX_S)
    _finish(heads, w_out_ref, x_ref, mods_ref, g_post_ref, 0, o_ref)


def _attn_diff_ctx(q, st_k, st_v, gate, x, mods, g_post, w_out, g_sub, lam_p, layer, j):
    lam_init = 0.8 - 0.6 * math.exp(-0.3 * layer)
    tok_spec = pl.BlockSpec((CTX_S, D_MODEL), lambda b: (b, 0))
    state_spec = pl.BlockSpec((None, None, CTX_S, D_MODEL), lambda b: (b, j, 0, 0))
    return pl.pallas_call(
        functools.partial(_attn_diff_ctx_kernel, lam_init),
        grid=(N_CTX_B,),
        in_specs=[
            tok_spec, state_spec, state_spec, tok_spec, tok_spec,
            pl.BlockSpec((None, COND_ROWS, 3 * D_MODEL), lambda b: (layer, 0, 0)),
            pl.BlockSpec((None, 1, D_MODEL), lambda b: (layer, 0, 0)),
            pl.BlockSpec((None, D_MODEL, D_MODEL), lambda b: (layer, 0, 0)),
            pl.BlockSpec((None, 1, DA_DV), lambda b: (j, 0, 0)),
            pl.BlockSpec((None, 4, DA_DH), lambda b: (j, 0, 0)),
        ],
        out_specs=tok_spec,
        out_shape=jax.ShapeDtypeStruct(x.shape, F32),
        scratch_shapes=[
            pltpu.VMEM((HEADS, 2 * CTX_S, LANES), BF16),
            pltpu.VMEM((CTX_S, D_MODEL), BF16),
        ],
        compiler_params=pltpu.CompilerParams(
            dimension_semantics=("arbitrary",), vmem_limit_bytes=VMEM_LIMIT),
        name="attn_diff_ctx",
    )(q, st_k, st_v, gate, x, mods, g_post, w_out, g_sub, lam_p)


def _attn_diff_lat_kernel(lam_init, q_ref, k_ref, v_ref, ck_ref, cv_ref, gate_ref, x_ref, mods_ref,
                          g_post_ref, w_out_ref, g_sub_ref, lam_ref, o_ref, k2_ref, vb_ref):
    sk = LAT_S + PAST

    @pl.when(pl.program_id(1) == 0)
    def _():
        lane = lax.broadcasted_iota(jnp.int32, (PAST, LANES), 1)
        zero_l = jnp.zeros((LAT_S, LANES), BF16)
        zero_p = jnp.zeros((PAST, LANES), BF16)
        lane_l = lax.broadcasted_iota(jnp.int32, (LAT_S, LANES), 1)
        for h in range(HEADS):
            lo, hi = h * LANES, (h + 1) * LANES
            kh = k_ref[:, lo:hi]
            ch = ck_ref[:, lo:hi].astype(BF16)
            k2_ref[h, 0:LAT_S, :] = jnp.where(lane_l < DA_DH, kh, zero_l)
            k2_ref[h, LAT_S:sk, :] = jnp.where(lane < DA_DH, ch, zero_p)
            k2_ref[h, sk:sk + LAT_S, :] = jnp.where(lane_l >= DA_DH, kh, zero_l)
            k2_ref[h, sk + LAT_S:2 * sk, :] = jnp.where(lane >= DA_DH, ch, zero_p)
        vb_ref[0:LAT_S, :] = v_ref[...]
        vb_ref[LAT_S:sk, :] = cv_ref[...].astype(BF16)

    lam = _diff_lambda(lam_ref, lam_init)
    heads = _diff_heads(q_ref, k2_ref, vb_ref, gate_ref, g_sub_ref, lam, lam_init, sk)
    _finish(heads, w_out_ref, x_ref, mods_ref, g_post_ref, 1 + pl.program_id(0), o_ref)


def _attn_diff_lat(q, k, v, cache_k, cache_v, gate, x, mods, g_post, w_out, g_sub, lam_p, layer, j):
    lam_init = 0.8 - 0.6 * math.exp(-0.3 * layer)
    sk = LAT_S + PAST
    nq = LAT_S // TQ
    tok_spec = pl.BlockSpec((TQ, D_MODEL), lambda b, i: (b * nq + i, 0))
    kv_spec = pl.BlockSpec((LAT_S, D_MODEL), lambda b, i: (b, 0))
    cache_spec = pl.BlockSpec((None, None, PAST, D_MODEL), lambda b, i: (b, j, 0, 0))
    return pl.pallas_call(
        functools.partial(_attn_diff_lat_kernel, lam_init),
        grid=(N_LAT_B, nq),
        in_specs=[
            tok_spec, kv_spec, kv_spec, cache_spec, cache_spec, tok_spec, tok_spec,
            pl.BlockSpec((None, COND_ROWS, 3 * D_MODEL), lambda b, i: (layer, 0, 0)),
            pl.BlockSpec((None, 1, D_MODEL), lambda b, i: (layer, 0, 0)),
            pl.BlockSpec((None, D_MODEL, D_MODEL), lambda b, i: (layer, 0, 0)),
            pl.BlockSpec((None, 1, DA_DV), lambda b, i: (j, 0, 0)),
            pl.BlockSpec((None, 4, DA_DH), lambda b, i: (j, 0, 0)),
        ],
        out_specs=tok_spec,
        out_shape=jax.ShapeDtypeStruct(x.shape, F32),
        scratch_shapes=[
            pltpu.VMEM((HEADS, 2 * sk, LANES), BF16),
            pltpu.VMEM((sk, D_MODEL), BF16),
        ],
        compiler_params=pltpu.CompilerParams(
            dimension_semantics=("arbitrary", "arbitrary"), vmem_limit_bytes=VMEM_LIMIT),
        name="attn_diff_lat",
    )(q, k, v, cache_k, cache_v, gate, x, mods, g_post, w_out, g_sub, lam_p)


_QA0, _QA1 = 0, MLA_Q_LORA
_KV0, _KV1 = MLA_Q_LORA, MLA_Q_LORA + MLA_KV_LORA
_GT0, _GT1 = _KV1, _KV1 + D_MODEL
_PE0, _PE1 = _GT1, _GT1 + LANES


def _mla_queries(hb, w_in_ref, g_qa_ref):
    q_a = jnp.dot(hb, w_in_ref[:, _QA0:_QA1], preferred_element_type=F32)
    return _rms(q_a, g_qa_ref[...]).astype(BF16)


def _pre_mla_ctx_kernel(x_ref, mods_ref, g_ref, w_in_ref, g_qa_ref, w_qb_ref, g_kva_ref, sc_in, sp_in,
                        q_ref, gate_ref, ckv_ref, kpe_ref):
    del sc_in, sp_in
    hb = _modulated(x_ref, mods_ref, g_ref, 0)
    qn = _mla_queries(hb, w_in_ref, g_qa_ref)
    q_ref[...] = (jnp.dot(qn, w_qb_ref[...], preferred_element_type=F32) * MLA_Q_SCALE).astype(BF16)
    kv_a = jnp.dot(hb, w_in_ref[:, _KV0:_KV1], preferred_element_type=F32)
    ckv_ref[...] = _rms(kv_a, g_kva_ref[...])
    gate_ref[...] = _silu(jnp.dot(hb, w_in_ref[:, _GT0:_GT1], preferred_element_type=F32))
    pe = jnp.dot(hb, w_in_ref[:, _PE0:_PE1], preferred_element_type=F32)
    kpe_ref[...] = pe[:, 0:MLA_ROPE]


def _pre_mla_ctx(x, mods, g_pre, w_in, g_qa, w_qb, g_kva, st_ckv, st_kpe, layer, j):
    n_tok = N_CTX_B * CTX_S
    tok = lambda w: pl.BlockSpec((CTX_S, w), lambda t: (t, 0))
    return pl.pallas_call(
        _pre_mla_ctx_kernel,
        grid=(N_CTX_B,),
        in_specs=[
            tok(D_MODEL),
            pl.BlockSpec((None, COND_ROWS, 3 * D_MODEL), lambda t: (layer, 0, 0)),
            pl.BlockSpec((None, 1, D_MODEL), lambda t: (layer, 0, 0)),
            pl.BlockSpec((None, D_MODEL, MLA_IN_PAD), lambda t: (j, 0, 0)),
            pl.BlockSpec((None, 1, MLA_Q_LORA), lambda t: (j, 0, 0)),
            pl.BlockSpec((None, MLA_Q_LORA, HEADS * MLA_QK_PAD), lambda t: (j, 0, 0)),
            pl.BlockSpec((None, 1, MLA_KV_LORA), lambda t: (j, 0, 0)),
            pl.BlockSpec(memory_space=pl.ANY),
            pl.BlockSpec(memory_space=pl.ANY),
        ],
        out_specs=[
            tok(HEADS * MLA_QK_PAD),
            tok(D_MODEL),
            pl.BlockSpec((None, None, CTX_S, MLA_KV_LORA), lambda t: (t, j, 0, 0)),
            pl.BlockSpec((None, None, CTX_S, MLA_ROPE), lambda t: (t, j, 0, 0)),
        ],
        out_shape=[
            jax.ShapeDtypeStruct((n_tok, HEADS * MLA_QK_PAD), BF16),
            jax.ShapeDtypeStruct((n_tok, D_MODEL), F32),
            jax.ShapeDtypeStruct(st_ckv.shape, F32),
            jax.ShapeDtypeStruct(st_kpe.shape, F32),
        ],
        input_output_aliases={7: 2, 8: 3},
        compiler_params=pltpu.CompilerParams(
            dimension_semantics=("arbitrary",), vmem_limit_bytes=VMEM_LIMIT),
        name="pre_mla_ctx",
    )(x, mods, g_pre, w_in, g_qa, w_qb, g_kva, st_ckv, st_kpe)


def _pre_mla_lat_kernel(x_ref, mods_ref, g_ref, w_in_ref, g_qa_ref, w_qb_ref, g_kva_ref, cos_ref, sin_ref,
                        q_ref, gate_ref, ckv_ref, kpe_ref):
    row = 1 + pl.program_id(0) // (LAT_S // TM)
    hb = _modulated(x_ref, mods_ref, g_ref, row)
    cos = cos_ref[...]
    sin = sin_ref[...]
    qn = _mla_queries(hb, w_in_ref, g_qa_ref)
    for h in range(HEADS):
        lo = h * MLA_QK_PAD
        q = jnp.dot(qn, w_qb_ref[:, lo:lo + MLA_QK_PAD], preferred_element_type=F32)
        q = q * MLA_Q_SCALE
        q_ref[:, lo:lo + LANES] = q[:, 0:LANES].astype(BF16)
        q_ref[:, lo + LANES:lo + 2 * LANES] = _rope_slab(q[:, LANES:2 * LANES], cos, sin).astype(BF16)
    kv_a = jnp.dot(hb, w_in_ref[:, _KV0:_KV1], preferred_element_type=F32)
    ckv_ref[...] = _rms(kv_a, g_kva_ref[...]).astype(BF16)
    gate_ref[...] = _silu(jnp.dot(hb, w_in_ref[:, _GT0:_GT1], preferred_element_type=F32))
    pe = jnp.dot(hb, w_in_ref[:, _PE0:_PE1], preferred_element_type=F32)
    kpe_ref[...] = _rope_slab(pe, cos, sin)[:, 0:MLA_ROPE].astype(BF16)


def _pre_mla_lat(x, mods, g_pre, w_in, g_qa, w_qb, g_kva, cos_t, sin_t, layer, j):
    n_tok = N_LAT_B * LAT_S
    tiles_per_req = LAT_S // TM
    tok = lambda w: pl.BlockSpec((TM, w), lambda t: (t, 0))
    rope_spec = pl.BlockSpec((TM, LANES), lambda t: (t % tiles_per_req, 0))
    return pl.pallas_call(
        _pre_mla_lat_kernel,
        grid=(n_tok // TM,),
        in_specs=[
            tok(D_MODEL),
            pl.BlockSpec((None, COND_ROWS, 3 * D_MODEL), lambda t: (layer, 0, 0)),
            pl.BlockSpec((None, 1, D_MODEL), lambda t: (layer, 0, 0)),
            pl.BlockSpec((None, D_MODEL, MLA_IN_PAD), lambda t: (j, 0, 0)),
            pl.BlockSpec((None, 1, MLA_Q_LORA), lambda t: (j, 0, 0)),
            pl.BlockSpec((None, MLA_Q_LORA, HEADS * MLA_QK_PAD), lambda t: (j, 0, 0)),
            pl.BlockSpec((None, 1, MLA_KV_LORA), lambda t: (j, 0, 0)),
            rope_spec,
            rope_spec,
        ],
        out_specs=[tok(HEADS * MLA_QK_PAD), tok(D_MODEL), tok(MLA_KV_LORA), tok(MLA_ROPE)],
        out_shape=[
            jax.ShapeDtypeStruct((n_tok, HEADS * MLA_QK_PAD), BF16),
            jax.ShapeDtypeStruct((n_tok, D_MODEL), F32),
            jax.ShapeDtypeStruct((n_tok, MLA_KV_LORA), BF16),
            jax.ShapeDtypeStruct((n_tok, MLA_ROPE), BF16),
        ],
        compiler_params=pltpu.CompilerParams(
            dimension_semantics=("arbitrary",), vmem_limit_bytes=VMEM_LIMIT),
        name="pre_mla_lat",
    )(x, mods, g_pre, w_in, g_qa, w_qb, g_kva, cos_t, sin_t)


def _mla_expand(ckv, kpe, w_kvb_ref, kf_ref, vf_ref, r0, rows):
    zero = jnp.zeros((rows, LANES - MLA_ROPE), BF16)
    for h in range(HEADS):
        lo = h * (MLA_NOPE + MLA_DV)
        kv = jnp.dot(ckv, w_kvb_ref[:, lo:lo + MLA_NOPE + MLA_DV], preferred_element_type=F32)
        kf_ref[h, r0:r0 + rows, 0:MLA_NOPE] = kv[:, 0:MLA_NOPE].astype(BF16)
        kf_ref[h, r0:r0 + rows, MLA_NOPE:MLA_NOPE + MLA_ROPE] = kpe
        kf_ref[h, r0:r0 + rows, MLA_NOPE + MLA_ROPE:MLA_QK_PAD] = zero
        vf_ref[h, r0:r0 + rows, :] = kv[:, MLA_NOPE:MLA_NOPE + MLA_DV].astype(BF16)


def _mla_heads(q_ref, kf_ref, vf_ref, gate_ref):
    def scores(h):
        return lax.dot_general(q_ref[:, h * MLA_QK_PAD:(h + 1) * MLA_QK_PAD], kf_ref[h], _TRANS_B,
                               preferred_element_type=F32)

    heads = []
    pending = [scores(h) for h in range(SCORES_AHEAD)]
    for h in range(HEADS):
        s = pending.pop(0)
        if h + SCORES_AHEAD < HEADS:
            pending.append(scores(h + SCORES_AHEAD))
        e = jnp.exp2(s - jnp.max(s, axis=-1, keepdims=True))
        inv = 1.0 / jnp.sum(e, axis=-1, keepdims=True)
        o = jnp.dot(e.astype(BF16), vf_ref[h], preferred_element_type=F32) * inv
        lo, hi = h * LANES, (h + 1) * LANES
        heads.append((o * gate_ref[:, lo:hi]).astype(BF16))
    return heads


def _attn_mla_ctx_kernel(q_ref, ckv_ref, kpe_ref, gate_ref, x_ref, mods_ref, g_post_ref, w_out_ref,
                         w_kvb_ref, o_ref, kf_ref, vf_ref):
    _mla_expand(ckv_ref[...].astype(BF16), kpe_ref[...].astype(BF16), w_kvb_ref, kf_ref, vf_ref, 0, CTX_S)
    heads = _mla_heads(q_ref, kf_ref, vf_ref, gate_ref)
    _finish(heads, w_out_ref, x_ref, mods_ref, g_post_ref, 0, o_ref)


def _attn_mla_ctx(q, st_ckv, st_kpe, gate, x, mods, g_post, w_out, w_kvb, layer, j):
    tok = lambda w: pl.BlockSpec((CTX_S, w), lambda b: (b, 0))
    return pl.pallas_call(
        _attn_mla_ctx_kernel,
        grid=(N_CTX_B,),
        in_specs=[
            tok(HEADS * MLA_QK_PAD),
            pl.BlockSpec((None, None, CTX_S, MLA_KV_LORA), lambda b: (b, j, 0, 0)),
            pl.BlockSpec((None, None, CTX_S, MLA_ROPE), lambda b: (b, j, 0, 0)),
            tok(D_MODEL), tok(D_MODEL),
            pl.BlockSpec((None, COND_ROWS, 3 * D_MODEL), lambda b: (layer, 0, 0)),
            pl.BlockSpec((None, 1, D_MODEL), lambda b: (layer, 0, 0)),
            pl.BlockSpec((None, D_MODEL, D_MODEL), lambda b: (layer, 0, 0)),
            pl.BlockSpec((None, MLA_KV_LORA, HEADS * (MLA_NOPE + MLA_DV)), lambda b: (j, 0, 0)),
        ],
        out_specs=tok(D_MODEL),
        out_shape=jax.ShapeDtypeStruct(x.shape, F32),
        scratch_shapes=[
            pltpu.VMEM((HEADS, CTX_S, MLA_QK_PAD), BF16),
            pltpu.VMEM((HEADS, CTX_S, MLA_DV), BF16),
        ],
        compiler_params=pltpu.CompilerParams(
            dimension_semantics=("arbitrary",), vmem_limit_bytes=VMEM_LIMIT),
        name="attn_mla_ctx",
    )(q, st_ckv, st_kpe, gate, x, mods, g_post, w_out, w_kvb)


def _attn_mla_lat_kernel(q_ref, ckv_ref, kpe_ref, cckv_ref, ckpe_ref, gate_ref, x_ref, mods_ref,
                         g_post_ref, w_out_ref, w_kvb_ref, o_ref, kf_ref, vf_ref):
    @pl.when(pl.program_id(1) == 0)
    def _():
        _mla_expand(ckv_ref[...], kpe_ref[...], w_kvb_ref, kf_ref, vf_ref, 0, LAT_S)
        _mla_expand(cckv_ref[...].astype(BF16), ckpe_ref[...].astype(BF16), w_kvb_ref, kf_ref, vf_ref,
                    LAT_S, PAST)

    heads = _mla_heads(q_ref, kf_ref, vf_ref, gate_ref)
    _finish(heads, w_out_ref, x_ref, mods_ref, g_post_ref, 1 + pl.program_id(0), o_ref)


def _attn_mla_lat(q, ckv, kpe, cache_ckv, cache_kpe, gate, x, mods, g_post, w_out, w_kvb, layer, j):
    sk = LAT_S + PAST
    nq = LAT_S // TQ
    tok = lambda w: pl.BlockSpec((TQ, w), lambda b, i: (b * nq + i, 0))
    req = lambda w: pl.BlockSpec((LAT_S, w), lambda b, i: (b, 0))
    return pl.pallas_call(
        _attn_mla_lat_kernel,
        grid=(N_LAT_B, nq),
        in_specs=[
            tok(HEADS * MLA_QK_PAD), req(MLA_KV_LORA), req(MLA_ROPE),
            pl.BlockSpec((None, None, PAST, MLA_KV_LORA), lambda b, i: (b, j, 0, 0)),
            pl.BlockSpec((None, None, PAST, MLA_ROPE), lambda b, i: (b, j, 0, 0)),
            tok(D_MODEL), tok(D_MODEL),
            pl.BlockSpec((None, COND_ROWS, 3 * D_MODEL), lambda b, i: (layer, 0, 0)),
            pl.BlockSpec((None, 1, D_MODEL), lambda b, i: (layer, 0, 0)),
            pl.BlockSpec((None, D_MODEL, D_MODEL), lambda b, i: (layer, 0, 0)),
            pl.BlockSpec((None, MLA_KV_LORA, HEADS * (MLA_NOPE + MLA_DV)), lambda b, i: (j, 0, 0)),
        ],
        out_specs=tok(D_MODEL),
        out_shape=jax.ShapeDtypeStruct(x.shape, F32),
        scratch_shapes=[
            pltpu.VMEM((HEADS, sk, MLA_QK_PAD), BF16),
            pltpu.VMEM((HEADS, sk, MLA_DV), BF16),
        ],
        compiler_params=pltpu.CompilerParams(
            dimension_semantics=("arbitrary", "arbitrary"), vmem_limit_bytes=VMEM_LIMIT),
        name="attn_mla_lat",
    )(q, ckv, kpe, cache_ckv, cache_kpe, gate, x, mods, g_post, w_out, w_kvb)


def _rope_tables():
    rows = LAT_S // GRID_W
    row = jnp.repeat(jnp.arange(rows), GRID_W).astype(F32)
    col = jnp.tile(jnp.arange(GRID_W), rows).astype(F32)
    n_freq = DA_DH // 4
    inv = ROPE_THETA ** (-jnp.arange(n_freq, dtype=F32) / n_freq)
    ang = jnp.concatenate([row[:, None] * inv, col[:, None] * inv], axis=-1)
    cos, sin = jnp.cos(ang), jnp.sin(ang)
    return (jnp.concatenate([cos, cos, cos, cos], axis=-1),
            jnp.concatenate([-sin, sin, -sin, sin], axis=-1))


def kernel(x_prompt, x_sample, cache_diff_k, cache_diff_v, cache_mla_ckv, cache_mla_kpe,
           c, c_ctx, w_ada, b_ada, g_pre, g_post, w_out,
           da_w_in, da_lam_q1, da_lam_k1, da_lam_q2, da_lam_k2, da_g_sub,
           mla_w_in, mla_g_qa, mla_w_qb, mla_g_kva, mla_w_kvb):
    assert DA_DH == MLA_ROPE
    n_diff = (DEPTH + 1) // 2
    n_mla = DEPTH // 2

    cond = jnp.concatenate(
        [c_ctx[None, :], c, jnp.zeros((COND_ROWS - 1 - N_LAT_B, D_MODEL), F32)], axis=0)
    mods = _ada_call(cond, w_ada, b_ada)

    cos_t, sin_t = _rope_tables()
    g_pre3 = g_pre.reshape(DEPTH, 1, D_MODEL)
    g_post3 = g_post.reshape(DEPTH, 1, D_MODEL)
    w_out_b = w_out.astype(BF16)

    da_w_in_b = da_w_in.astype(BF16)
    g_sub3 = da_g_sub.reshape(n_diff, 1, DA_DV)
    lam_p = jnp.stack([da_lam_q1, da_lam_k1, da_lam_q2, da_lam_k2], axis=1)
    cache_dk = cache_diff_k.reshape(N_LAT_B, n_diff, PAST, D_MODEL)
    cache_dv = cache_diff_v.reshape(N_LAT_B, n_diff, PAST, D_MODEL)

    o1 = MLA_Q_LORA + MLA_KV_LORA
    o2 = o1 + MLA_ROPE
    mla_w_in_b = jnp.concatenate(
        [mla_w_in[:, :, :o1], mla_w_in[:, :, o2:], mla_w_in[:, :, o1:o2],
         jnp.zeros((n_mla, D_MODEL, MLA_IN_PAD - (o2 + D_MODEL)), F32)], axis=-1).astype(BF16)
    w_qb4 = mla_w_qb.reshape(n_mla, MLA_Q_LORA, HEADS, MLA_NOPE + MLA_ROPE)
    w_qb_b = jnp.pad(w_qb4, ((0, 0), (0, 0), (0, 0), (0, MLA_QK_PAD - MLA_NOPE - MLA_ROPE))).reshape(
        n_mla, MLA_Q_LORA, HEADS * MLA_QK_PAD).astype(BF16)
    w_kvb_b = mla_w_kvb.astype(BF16)
    g_qa3 = mla_g_qa.reshape(n_mla, 1, MLA_Q_LORA)
    g_kva3 = mla_g_kva.reshape(n_mla, 1, MLA_KV_LORA)

    x_ctx = x_prompt.reshape(N_CTX_B * CTX_S, D_MODEL)
    x_lat = x_sample.reshape(N_LAT_B * LAT_S, D_MODEL)
    st_dk = jnp.zeros((N_CTX_B, n_diff, CTX_S, D_MODEL), F32)
    st_dv = jnp.zeros((N_CTX_B, n_diff, CTX_S, D_MODEL), F32)
    st_ckv = jnp.zeros((N_CTX_B, n_mla, CTX_S, MLA_KV_LORA), F32)
    st_kpe = jnp.zeros((N_CTX_B, n_mla, CTX_S, MLA_ROPE), F32)

    for i in range(DEPTH):
        j = i // 2
        if i % 2 == 0:
            q_c, gate_c, st_dk, st_dv = _pre_diff_ctx(x_ctx, mods, g_pre3, da_w_in_b, st_dk, st_dv, i, j)
            q_l, k_l, v_l, gate_l = _pre_diff_lat(x_lat, mods, g_pre3, da_w_in_b, cos_t, sin_t, i, j)
            x_ctx = _attn_diff_ctx(q_c, st_dk, st_dv, gate_c, x_ctx, mods, g_post3, w_out_b,
                                   g_sub3, lam_p, i, j)
            x_lat = _attn_diff_lat(q_l, k_l, v_l, cache_dk, cache_dv, gate_l, x_lat, mods, g_post3,
                                   w_out_b, g_sub3, lam_p, i, j)
        else:
            q_c, gate_c, st_ckv, st_kpe = _pre_mla_ctx(
                x_ctx, mods, g_pre3, mla_w_in_b, g_qa3, w_qb_b, g_kva3, st_ckv, st_kpe, i, j)
            q_l, gate_l, ckv_l, kpe_l = _pre_mla_lat(
                x_lat, mods, g_pre3, mla_w_in_b, g_qa3, w_qb_b, g_kva3, cos_t, sin_t, i, j)
            x_ctx = _attn_mla_ctx(q_c, st_ckv, st_kpe, gate_c, x_ctx, mods, g_post3, w_out_b,
                                  w_kvb_b, i, j)
            x_lat = _attn_mla_lat(q_l, ckv_l, kpe_l, cache_mla_ckv, cache_mla_kpe, gate_l, x_lat,
                                  mods, g_post3, w_out_b, w_kvb_b, i, j)

    return (x_ctx.reshape(N_CTX_B, CTX_S, D_MODEL),
            x_lat.reshape(N_LAT_B, LAT_S, D_MODEL),
            st_dk.reshape(N_CTX_B, n_diff, CTX_S, HEADS, 2, DA_DH),
            st_dv.reshape(N_CTX_B, n_diff, CTX_S, HEADS, DA_DV),
            st_ckv,
            st_kpe)
```

```python
import functools
import math

import jax
import jax.numpy as jnp
from jax import lax
from jax.experimental import pallas as pl
from jax.experimental.pallas import tpu as pltpu

F32 = jnp.float32
BF16 = jnp.bfloat16

D_MODEL = 1024
DEPTH = 4
N_CTX_B = 16
CTX_S = 256
N_LAT_B = 4
LAT_S = 1024
PAST = 256
GRID_W = 64
HEADS = 8
DA_DH = 64
DA_DV = 128
MLA_Q_LORA = 384
MLA_KV_LORA = 256
MLA_NOPE = 128
MLA_ROPE = 64
MLA_DV = 128
MLA_QK_PAD = 256
MLA_IN_PAD = 1792
ROPE_THETA = 10000.0
NORM_EPS = 1e-6
COND_ROWS = 8

LANES = 128
TQ = 256
TM = 256
VMEM_LIMIT = 56 * 1024 * 1024

_TRANS_B = (((1,), (1,)), ((), ()))
SCORES_AHEAD = 2
DA_Q_SCALE = DA_DH ** -0.5 * math.log2(math.e)
MLA_Q_SCALE = (MLA_NOPE + MLA_ROPE) ** -0.5 * math.log2(math.e)


def _silu(x):
    return x * (1.0 / (1.0 + jnp.exp(-x)))


def _rms(x, g):
    r = lax.rsqrt(jnp.mean(x * x, axis=-1, keepdims=True) + NORM_EPS)
    return (x * r) * g


def _rope_slab(x, cos, sin_signed):
    lane = lax.broadcasted_iota(jnp.int32, x.shape, 1)
    first_half = (lane % 64) < 32
    partner = jnp.where(first_half, pltpu.roll(x, 96, axis=1), pltpu.roll(x, 32, axis=1))
    return x * cos + partner * sin_signed


def _ones_column(rows):
    lane = lax.broadcasted_iota(jnp.int32, (rows, LANES), 1)
    return jnp.where(lane == 0, 1.0, 0.0).astype(BF16)


def _cond_row(mods_ref, row):
    m = mods_ref[pl.ds(row, 1), :]
    return m[:, 0:D_MODEL], m[:, D_MODEL:2 * D_MODEL], m[:, 2 * D_MODEL:3 * D_MODEL]


def _modulated(x_ref, mods_ref, g_ref, row):
    shift, scale, _ = _cond_row(mods_ref, row)
    h = _rms(x_ref[...], g_ref[...]) * (1.0 + scale) + shift
    return h.astype(BF16)


def _ada_kernel(cond_ref, w_ref, b_ref, o_ref):
    a = _silu(cond_ref[...]).astype(BF16)
    o_ref[...] = jnp.dot(a, w_ref[...].astype(BF16), preferred_element_type=F32) + b_ref[...]


def _ada_call(cond, w_ada, b_ada):
    tn = 1024
    return pl.pallas_call(
        _ada_kernel,
        grid=(DEPTH, 3 * D_MODEL // tn),
        in_specs=[
            pl.BlockSpec((COND_ROWS, D_MODEL), lambda i, n: (0, 0)),
            pl.BlockSpec((None, D_MODEL, tn), lambda i, n: (i, 0, n)),
            pl.BlockSpec((None, 1, tn), lambda i, n: (i, 0, n)),
        ],
        out_specs=pl.BlockSpec((None, COND_ROWS, tn), lambda i, n: (i, 0, n)),
        out_shape=jax.ShapeDtypeStruct((DEPTH, COND_ROWS, 3 * D_MODEL), F32),
        compiler_params=pltpu.CompilerParams(
            dimension_semantics=("arbitrary", "arbitrary"), vmem_limit_bytes=VMEM_LIMIT),
        name="ada_mod",
    )(cond, w_ada, b_ada.reshape(DEPTH, 1, 3 * D_MODEL))


def _pre_diff_ctx_kernel(x_ref, mods_ref, g_ref, w_ref, sk_in, sv_in,
                         q_ref, gate_ref, sk_ref, sv_ref):
    del sk_in, sv_in
    hb = _modulated(x_ref, mods_ref, g_ref, 0)
    n = D_MODEL
    q = jnp.dot(hb, w_ref[:, 0:n], preferred_element_type=F32)
    q_ref[...] = (q * DA_Q_SCALE).astype(BF16)
    sk_ref[...] = jnp.dot(hb, w_ref[:, n:2 * n], preferred_element_type=F32)
    sv_ref[...] = jnp.dot(hb, w_ref[:, 2 * n:3 * n], preferred_element_type=F32)
    gate_ref[...] = _silu(jnp.dot(hb, w_ref[:, 3 * n:4 * n], preferred_element_type=F32)).astype(BF16)


def _pre_diff_ctx(x, mods, g_pre, w_in, st_k, st_v, layer, j):
    n_tok = N_CTX_B * CTX_S
    state_spec = pl.BlockSpec((None, None, CTX_S, D_MODEL), lambda t: (t, j, 0, 0))
    return pl.pallas_call(
        _pre_diff_ctx_kernel,
        grid=(N_CTX_B,),
        in_specs=[
            pl.BlockSpec((CTX_S, D_MODEL), lambda t: (t, 0)),
            pl.BlockSpec((None, COND_ROWS, 3 * D_MODEL), lambda t: (layer, 0, 0)),
            pl.BlockSpec((None, 1, D_MODEL), lambda t: (layer, 0, 0)),
            pl.BlockSpec((None, D_MODEL, 4 * D_MODEL), lambda t: (j, 0, 0)),
            pl.BlockSpec(memory_space=pl.ANY),
            pl.BlockSpec(memory_space=pl.ANY),
        ],
        out_specs=[
            pl.BlockSpec((CTX_S, D_MODEL), lambda t: (t, 0)),
            pl.BlockSpec((CTX_S, D_MODEL), lambda t: (t, 0)),
            state_spec,
            state_spec,
        ],
        out_shape=[
            jax.ShapeDtypeStruct((n_tok, D_MODEL), BF16),
            jax.ShapeDtypeStruct((n_tok, D_MODEL), BF16),
            jax.ShapeDtypeStruct(st_k.shape, F32),
            jax.ShapeDtypeStruct(st_v.shape, F32),
        ],
        input_output_aliases={4: 2, 5: 3},
        compiler_params=pltpu.CompilerParams(
            dimension_semantics=("arbitrary",), vmem_limit_bytes=VMEM_LIMIT),
        name="pre_diff_ctx",
    )(x, mods, g_pre, w_in, st_k, st_v)


def _pre_diff_lat_kernel(x_ref, mods_ref, g_ref, w_ref, cos_ref, sin_ref,
                         q_ref, k_ref, v_ref, gate_ref):
    row = 1 + pl.program_id(0) // (LAT_S // TM)
    hb = _modulated(x_ref, mods_ref, g_ref, row)
    cos = cos_ref[...]
    sin = sin_ref[...]
    n = D_MODEL
    for h in range(HEADS):
        lo, hi = h * LANES, (h + 1) * LANES
        q = jnp.dot(hb, w_ref[:, lo:hi], preferred_element_type=F32)
        q_ref[:, lo:hi] = (_rope_slab(q, cos, sin) * DA_Q_SCALE).astype(BF16)
        k = jnp.dot(hb, w_ref[:, n + lo:n + hi], preferred_element_type=F32)
        k_ref[:, lo:hi] = _rope_slab(k, cos, sin).astype(BF16)
    v_ref[...] = jnp.dot(hb, w_ref[:, 2 * n:3 * n], preferred_element_type=F32).astype(BF16)
    gate_ref[...] = _silu(jnp.dot(hb, w_ref[:, 3 * n:4 * n], preferred_element_type=F32)).astype(BF16)


def _pre_diff_lat(x, mods, g_pre, w_in, cos_t, sin_t, layer, j):
    n_tok = N_LAT_B * LAT_S
    tiles_per_req = LAT_S // TM
    tok_spec = pl.BlockSpec((TM, D_MODEL), lambda t: (t, 0))
    rope_spec = pl.BlockSpec((TM, LANES), lambda t: (t % tiles_per_req, 0))
    return pl.pallas_call(
        _pre_diff_lat_kernel,
        grid=(n_tok // TM,),
        in_specs=[
            tok_spec,
            pl.BlockSpec((None, COND_ROWS, 3 * D_MODEL), lambda t: (layer, 0, 0)),
            pl.BlockSpec((None, 1, D_MODEL), lambda t: (layer, 0, 0)),
            pl.BlockSpec((None, D_MODEL, 4 * D_MODEL), lambda t: (j, 0, 0)),
            rope_spec,
            rope_spec,
        ],
        out_specs=[tok_spec, tok_spec, tok_spec, tok_spec],
        out_shape=[
            jax.ShapeDtypeStruct((n_tok, D_MODEL), BF16),
            jax.ShapeDtypeStruct((n_tok, D_MODEL), BF16),
            jax.ShapeDtypeStruct((n_tok, D_MODEL), BF16),
            jax.ShapeDtypeStruct((n_tok, D_MODEL), BF16),
        ],
        compiler_params=pltpu.CompilerParams(
            dimension_semantics=("arbitrary",), vmem_limit_bytes=VMEM_LIMIT),
        name="pre_diff_lat",
    )(x, mods, g_pre, w_in, cos_t, sin_t)


def _finish(heads, w_out_ref, x_ref, mods_ref, g_post_ref, row, o_ref):
    _, _, gate = _cond_row(mods_ref, row)
    og = jnp.concatenate(heads, axis=1)
    y = jnp.dot(og, w_out_ref[...], preferred_element_type=F32)
    o_ref[...] = x_ref[...] + gate * _rms(y, g_post_ref[...])


def _diff_lambda(lam_ref, lam_init):
    p = lam_ref[...]
    a = jnp.sum(p[0:1, :] * p[1:2, :], axis=-1, keepdims=True)
    b = jnp.sum(p[2:3, :] * p[3:4, :], axis=-1, keepdims=True)
    return jnp.exp(a) - jnp.exp(b) + lam_init


def _stack_diff_keys(k, k2_ref, sk):
    lane = lax.broadcasted_iota(jnp.int32, (sk, LANES), 1)
    zero = jnp.zeros((sk, LANES), BF16)
    for h in range(HEADS):
        kh = k[:, h * LANES:(h + 1) * LANES]
        k2_ref[h, 0:sk, :] = jnp.where(lane < DA_DH, kh, zero)
        k2_ref[h, sk:2 * sk, :] = jnp.where(lane >= DA_DH, kh, zero)


def _stack_diff_values(v, vb_ref, r0, rows):
    ones = _ones_column(rows)
    for h in range(HEADS):
        vb_ref[h, r0:r0 + rows, 0:DA_DV] = v[:, h * LANES:(h + 1) * LANES]
        vb_ref[h, r0:r0 + rows, DA_DV:2 * DA_DV] = ones


def _diff_heads(q_ref, k2_ref, v_ref, gate_ref, g_sub_ref, lam, lam_init, sk):
    g_sub = g_sub_ref[...]

    def scores(h):
        return lax.dot_general(q_ref[:, h * LANES:(h + 1) * LANES], k2_ref[h], _TRANS_B,
                               preferred_element_type=F32)

    heads = []
    pending = [scores(h) for h in range(SCORES_AHEAD)]
    for h in range(HEADS):
        lo, hi = h * LANES, (h + 1) * LANES
        s = pending.pop(0)
        if h + SCORES_AHEAD < HEADS:
            pending.append(scores(h + SCORES_AHEAD))
        s0 = s[:, 0:sk]
        s1 = s[:, sk:2 * sk]
        e0 = jnp.exp2(s0 - jnp.max(s0, axis=-1, keepdims=True)).astype(BF16)
        e1 = jnp.exp2(s1 - jnp.max(s1, axis=-1, keepdims=True)).astype(BF16)
        pv0 = jnp.dot(e0, v_ref[h], preferred_element_type=F32)
        pv1 = jnp.dot(e1, v_ref[h], preferred_element_type=F32)
        a0 = 1.0 / pv0[:, DA_DV:DA_DV + 1]
        a1 = lam / pv1[:, DA_DV:DA_DV + 1]
        o = pv0[:, 0:DA_DV] * a0 - pv1[:, 0:DA_DV] * a1
        o = _rms(o, g_sub) * (1.0 - lam_init)
        heads.append((o * gate_ref[:, lo:hi]).astype(BF16))
    return heads


def _attn_diff_ctx_kernel(lam_init, q_ref, k_ref, v_ref, gate_ref, x_ref, mods_ref, g_post_ref,
                          w_out_ref, g_sub_ref, lam_ref, o_ref, k2_ref, vb_ref):
    _stack_diff_keys(k_ref[...].astype(BF16), k2_ref, CTX_S)
    _stack_diff_values(v_ref[...].astype(BF16), vb_ref, 0, CTX_S)
    lam = _diff_lambda(lam_ref, lam_init)
    heads = _diff_heads(q_ref, k2_ref, vb_ref, gate_ref, g_sub_ref, lam, lam_init, CTX_S)
    _finish(heads, w_out_ref, x_ref, mods_ref, g_post_ref, 0, o_ref)


def _attn_diff_ctx(q, st_k, st_v, gate, x, mods, g_post, w_out, g_sub, lam_p, layer, j):
    lam_init = 0.8 - 0.6 * math.exp(-0.3 * layer)
    tok_spec = pl.BlockSpec((CTX_S, D_MODEL), lambda b: (b, 0))
    state_spec = pl.BlockSpec((None, None, CTX_S, D_MODEL), lambda b: (b, j, 0, 0))
    return pl.pallas_call(
        functools.partial(_attn_diff_ctx_kernel, lam_init),
        grid=(N_CTX_B,),
        in_specs=[
            tok_spec, state_spec, state_spec, tok_spec, tok_spec,
            pl.BlockSpec((None, COND_ROWS, 3 * D_MODEL), lambda b: (layer, 0, 0)),
            pl.BlockSpec((None, 1, D_MODEL), lambda b: (layer, 0, 0)),
            pl.BlockSpec((None, D_MODEL, D_MODEL), lambda b: (layer, 0, 0)),
            pl.BlockSpec((None, 1, DA_DV), lambda b: (j, 0, 0)),
            pl.BlockSpec((None, 4, DA_DH), lambda b: (j, 0, 0)),
        ],
        out_specs=tok_spec,
        out_shape=jax.ShapeDtypeStruct(x.shape, F32),
        scratch_shapes=[
            pltpu.VMEM((HEADS, 2 * CTX_S, LANES), BF16),
            pltpu.VMEM((HEADS, CTX_S, 2 * DA_DV), BF16),
        ],
        compiler_params=pltpu.CompilerParams(
            dimension_semantics=("arbitrary",), vmem_limit_bytes=VMEM_LIMIT),
        name="attn_diff_ctx",
    )(q, st_k, st_v, gate, x, mods, g_post, w_out, g_sub, lam_p)


def _attn_diff_lat_kernel(lam_init, q_ref, k_ref, v_ref, ck_ref, cv_ref, gate_ref, x_ref, mods_ref,
                          g_post_ref, w_out_ref, g_sub_ref, lam_ref, o_ref, k2_ref, vb_ref):
    sk = LAT_S + PAST

    @pl.when(pl.program_id(1) == 0)
    def _():
        lane = lax.broadcasted_iota(jnp.int32, (PAST, LANES), 1)
        zero_l = jnp.zeros((LAT_S, LANES), BF16)
        zero_p = jnp.zeros((PAST, LANES), BF16)
        lane_l = lax.broadcasted_iota(jnp.int32, (LAT_S, LANES), 1)
        for h in range(HEADS):
            lo, hi = h * LANES, (h + 1) * LANES
            kh = k_ref[:, lo:hi]
            ch = ck_ref[:, lo:hi].astype(BF16)
            k2_ref[h, 0:LAT_S, :] = jnp.where(lane_l < DA_DH, kh, zero_l)
            k2_ref[h, LAT_S:sk, :] = jnp.where(lane < DA_DH, ch, zero_p)
            k2_ref[h, sk:sk + LAT_S, :] = jnp.where(lane_l >= DA_DH, kh, zero_l)
            k2_ref[h, sk + LAT_S:2 * sk, :] = jnp.where(lane >= DA_DH, ch, zero_p)
        _stack_diff_values(v_ref[...], vb_ref, 0, LAT_S)
        _stack_diff_values(cv_ref[...].astype(BF16), vb_ref, LAT_S, PAST)

    lam = _diff_lambda(lam_ref, lam_init)
    heads = _diff_heads(q_ref, k2_ref, vb_ref, gate_ref, g_sub_ref, lam, lam_init, sk)
    _finish(heads, w_out_ref, x_ref, mods_ref, g_post_ref, 1 + pl.program_id(0), o_ref)


def _attn_diff_lat(q, k, v, cache_k, cache_v, gate, x, mods, g_post, w_out, g_sub, lam_p, layer, j):
    lam_init = 0.8 - 0.6 * math.exp(-0.3 * layer)
    sk = LAT_S + PAST
    nq = LAT_S // TQ
    tok_spec = pl.BlockSpec((TQ, D_MODEL), lambda b, i: (b * nq + i, 0))
    kv_spec = pl.BlockSpec((LAT_S, D_MODEL), lambda b, i: (b, 0))
    cache_spec = pl.BlockSpec((None, None, PAST, D_MODEL), lambda b, i: (b, j, 0, 0))
    return pl.pallas_call(
        functools.partial(_attn_diff_lat_kernel, lam_init),
        grid=(N_LAT_B, nq),
        in_specs=[
            tok_spec, kv_spec, kv_spec, cache_spec, cache_spec, tok_spec, tok_spec,
            pl.BlockSpec((None, COND_ROWS, 3 * D_MODEL), lambda b, i: (layer, 0, 0)),
            pl.BlockSpec((None, 1, D_MODEL), lambda b, i: (layer, 0, 0)),
            pl.BlockSpec((None, D_MODEL, D_MODEL), lambda b, i: (layer, 0, 0)),
            pl.BlockSpec((None, 1, DA_DV), lambda b, i: (j, 0, 0)),
            pl.BlockSpec((None, 4, DA_DH), lambda b, i: (j, 0, 0)),
        ],
        out_specs=tok_spec,
        out_shape=jax.ShapeDtypeStruct(x.shape, F32),
        scratch_shapes=[
            pltpu.VMEM((HEADS, 2 * sk, LANES), BF16),
            pltpu.VMEM((HEADS, sk, 2 * DA_DV), BF16),
        ],
        compiler_params=pltpu.CompilerParams(
            dimension_semantics=("arbitrary", "arbitrary"), vmem_limit_bytes=VMEM_LIMIT),
        name="attn_diff_lat",
    )(q, k, v, cache_k, cache_v, gate, x, mods, g_post, w_out, g_sub, lam_p)


_QA0, _QA1 = 0, MLA_Q_LORA
_KV0, _KV1 = MLA_Q_LORA, MLA_Q_LORA + MLA_KV_LORA
_GT0, _GT1 = _KV1, _KV1 + D_MODEL
_PE0, _PE1 = _GT1, _GT1 + LANES


def _mla_queries(hb, w_in_ref, g_qa_ref):
    q_a = jnp.dot(hb, w_in_ref[:, _QA0:_QA1], preferred_element_type=F32)
    return _rms(q_a, g_qa_ref[...]).astype(BF16)


def _pre_mla_ctx_kernel(x_ref, mods_ref, g_ref, w_in_ref, g_qa_ref, w_qb_ref, g_kva_ref, sc_in, sp_in,
                        q_ref, gate_ref, ckv_ref, kpe_ref):
    del sc_in, sp_in
    hb = _modulated(x_ref, mods_ref, g_ref, 0)
    qn = _mla_queries(hb, w_in_ref, g_qa_ref)
    q_ref[...] = (jnp.dot(qn, w_qb_ref[...], preferred_element_type=F32) * MLA_Q_SCALE).astype(BF16)
    kv_a = jnp.dot(hb, w_in_ref[:, _KV0:_KV1], preferred_element_type=F32)
    ckv_ref[...] = _rms(kv_a, g_kva_ref[...])
    gate_ref[...] = _silu(jnp.dot(hb, w_in_ref[:, _GT0:_GT1], preferred_element_type=F32)).astype(BF16)
    pe = jnp.dot(hb, w_in_ref[:, _PE0:_PE1], preferred_element_type=F32)
    kpe_ref[...] = pe[:, 0:MLA_ROPE]


def _pre_mla_ctx(x, mods, g_pre, w_in, g_qa, w_qb, g_kva, st_ckv, st_kpe, layer, j):
    n_tok = N_CTX_B * CTX_S
    tok = lambda w: pl.BlockSpec((CTX_S, w), lambda t: (t, 0))
    return pl.pallas_call(
        _pre_mla_ctx_kernel,
        grid=(N_CTX_B,),
        in_specs=[
            tok(D_MODEL),
            pl.BlockSpec((None, COND_ROWS, 3 * D_MODEL), lambda t: (layer, 0, 0)),
            pl.BlockSpec((None, 1, D_MODEL), lambda t: (layer, 0, 0)),
            pl.BlockSpec((None, D_MODEL, MLA_IN_PAD), lambda t: (j, 0, 0)),
            pl.BlockSpec((None, 1, MLA_Q_LORA), lambda t: (j, 0, 0)),
            pl.BlockSpec((None, MLA_Q_LORA, HEADS * MLA_QK_PAD), lambda t: (j, 0, 0)),
            pl.BlockSpec((None, 1, MLA_KV_LORA), lambda t: (j, 0, 0)),
            pl.BlockSpec(memory_space=pl.ANY),
            pl.BlockSpec(memory_space=pl.ANY),
        ],
        out_specs=[
            tok(HEADS * MLA_QK_PAD),
            tok(D_MODEL),
            pl.BlockSpec((None, None, CTX_S, MLA_KV_LORA), lambda t: (t, j, 0, 0)),
            pl.BlockSpec((None, None, CTX_S, MLA_ROPE), lambda t: (t, j, 0, 0)),
        ],
        out_shape=[
            jax.ShapeDtypeStruct((n_tok, HEADS * MLA_QK_PAD), BF16),
            jax.ShapeDtypeStruct((n_tok, D_MODEL), BF16),
            jax.ShapeDtypeStruct(st_ckv.shape, F32),
            jax.ShapeDtypeStruct(st_kpe.shape, F32),
        ],
        input_output_aliases={7: 2, 8: 3},
        compiler_params=pltpu.CompilerParams(
            dimension_semantics=("arbitrary",), vmem_limit_bytes=VMEM_LIMIT),
        name="pre_mla_ctx",
    )(x, mods, g_pre, w_in, g_qa, w_qb, g_kva, st_ckv, st_kpe)


def _pre_mla_lat_kernel(x_ref, mods_ref, g_ref, w_in_ref, g_qa_ref, w_qb_ref, g_kva_ref, cos_ref, sin_ref,
                        q_ref, gate_ref, ckv_ref, kpe_ref):
    row = 1 + pl.program_id(0) // (LAT_S // TM)
    hb = _modulated(x_ref, mods_ref, g_ref, row)
    cos = cos_ref[...]
    sin = sin_ref[...]
    qn = _mla_queries(hb, w_in_ref, g_qa_ref)
    for h in range(HEADS):
        lo = h * MLA_QK_PAD
        q = jnp.dot(qn, w_qb_ref[:, lo:lo + MLA_QK_PAD], preferred_element_type=F32)
        q = q * MLA_Q_SCALE
        q_ref[:, lo:lo + LANES] = q[:, 0:LANES].astype(BF16)
        q_ref[:, lo + LANES:lo + 2 * LANES] = _rope_slab(q[:, LANES:2 * LANES], cos, sin).astype(BF16)
    kv_a = jnp.dot(hb, w_in_ref[:, _KV0:_KV1], preferred_element_type=F32)
    ckv_ref[...] = _rms(kv_a, g_kva_ref[...]).astype(BF16)
    gate_ref[...] = _silu(jnp.dot(hb, w_in_ref[:, _GT0:_GT1], preferred_element_type=F32)).astype(BF16)
    pe = jnp.dot(hb, w_in_ref[:, _PE0:_PE1], preferred_element_type=F32)
    kpe_ref[...] = _rope_slab(pe, cos, sin)[:, 0:MLA_ROPE].astype(BF16)


def _pre_mla_lat(x, mods, g_pre, w_in, g_qa, w_qb, g_kva, cos_t, sin_t, layer, j):
    n_tok = N_LAT_B * LAT_S
    tiles_per_req = LAT_S // TM
    tok = lambda w: pl.BlockSpec((TM, w), lambda t: (t, 0))
    rope_spec = pl.BlockSpec((TM, LANES), lambda t: (t % tiles_per_req, 0))
    return pl.pallas_call(
        _pre_mla_lat_kernel,
        grid=(n_tok // TM,),
        in_specs=[
            tok(D_MODEL),
            pl.BlockSpec((None, COND_ROWS, 3 * D_MODEL), lambda t: (layer, 0, 0)),
            pl.BlockSpec((None, 1, D_MODEL), lambda t: (layer, 0, 0)),
            pl.BlockSpec((None, D_MODEL, MLA_IN_PAD), lambda t: (j, 0, 0)),
            pl.BlockSpec((None, 1, MLA_Q_LORA), lambda t: (j, 0, 0)),
            pl.BlockSpec((None, MLA_Q_LORA, HEADS * MLA_QK_PAD), lambda t: (j, 0, 0)),
            pl.BlockSpec((None, 1, MLA_KV_LORA), lambda t: (j, 0, 0)),
            rope_spec,
            rope_spec,
        ],
        out_specs=[tok(HEADS * MLA_QK_PAD), tok(D_MODEL), tok(MLA_KV_LORA), tok(MLA_ROPE)],
        out_shape=[
            jax.ShapeDtypeStruct((n_tok, HEADS * MLA_QK_PAD), BF16),
            jax.ShapeDtypeStruct((n_tok, D_MODEL), BF16),
            jax.ShapeDtypeStruct((n_tok, MLA_KV_LORA), BF16),
            jax.ShapeDtypeStruct((n_tok, MLA_ROPE), BF16),
        ],
        compiler_params=pltpu.CompilerParams(
            dimension_semantics=("arbitrary",), vmem_limit_bytes=VMEM_LIMIT),
        name="pre_mla_lat",
    )(x, mods, g_pre, w_in, g_qa, w_qb, g_kva, cos_t, sin_t)


def _mla_expand(ckv, kpe, w_kvb_ref, kf_ref, vf_ref, r0, rows):
    zero = jnp.zeros((rows, LANES - MLA_ROPE), BF16)
    for h in range(HEADS):
        lo = h * (MLA_NOPE + MLA_DV)
        kv = jnp.dot(ckv, w_kvb_ref[:, lo:lo + MLA_NOPE + MLA_DV], preferred_element_type=F32)
        kf_ref[h, r0:r0 + rows, 0:MLA_NOPE] = kv[:, 0:MLA_NOPE].astype(BF16)
        kf_ref[h, r0:r0 + rows, MLA_NOPE:MLA_NOPE + MLA_ROPE] = kpe
        kf_ref[h, r0:r0 + rows, MLA_NOPE + MLA_ROPE:MLA_QK_PAD] = zero
        vf_ref[h, r0:r0 + rows, 0:MLA_DV] = kv[:, MLA_NOPE:MLA_NOPE + MLA_DV].astype(BF16)
        vf_ref[h, r0:r0 + rows, MLA_DV:2 * MLA_DV] = _ones_column(rows)


def _mla_heads(q_ref, kf_ref, vf_ref, gate_ref):
    def scores(h):
        return lax.dot_general(q_ref[:, h * MLA_QK_PAD:(h + 1) * MLA_QK_PAD], kf_ref[h], _TRANS_B,
                               preferred_element_type=F32)

    heads = []
    pending = [scores(h) for h in range(SCORES_AHEAD)]
    for h in range(HEADS):
        s = pending.pop(0)
        if h + SCORES_AHEAD < HEADS:
            pending.append(scores(h + SCORES_AHEAD))
        e = jnp.exp2(s - jnp.max(s, axis=-1, keepdims=True)).astype(BF16)
        pv = jnp.dot(e, vf_ref[h], preferred_element_type=F32)
        o = pv[:, 0:MLA_DV] * (1.0 / pv[:, MLA_DV:MLA_DV + 1])
        lo, hi = h * LANES, (h + 1) * LANES
        heads.append((o * gate_ref[:, lo:hi]).astype(BF16))
    return heads


def _attn_mla_ctx_kernel(q_ref, ckv_ref, kpe_ref, gate_ref, x_ref, mods_ref, g_post_ref, w_out_ref,
                         w_kvb_ref, o_ref, kf_ref, vf_ref):
    _mla_expand(ckv_ref[...].astype(BF16), kpe_ref[...].astype(BF16), w_kvb_ref, kf_ref, vf_ref, 0, CTX_S)
    heads = _mla_heads(q_ref, kf_ref, vf_ref, gate_ref)
    _finish(heads, w_out_ref, x_ref, mods_ref, g_post_ref, 0, o_ref)


def _attn_mla_ctx(q, st_ckv, st_kpe, gate, x, mods, g_post, w_out, w_kvb, layer, j):
    tok = lambda w: pl.BlockSpec((CTX_S, w), lambda b: (b, 0))
    return pl.pallas_call(
        _attn_mla_ctx_kernel,
        grid=(N_CTX_B,),
        in_specs=[
            tok(HEADS * MLA_QK_PAD),
            pl.BlockSpec((None, None, CTX_S, MLA_KV_LORA), lambda b: (b, j, 0, 0)),
            pl.BlockSpec((None, None, CTX_S, MLA_ROPE), lambda b: (b, j, 0, 0)),
            tok(D_MODEL), tok(D_MODEL),
            pl.BlockSpec((None, COND_ROWS, 3 * D_MODEL), lambda b: (layer, 0, 0)),
            pl.BlockSpec((None, 1, D_MODEL), lambda b: (layer, 0, 0)),
            pl.BlockSpec((None, D_MODEL, D_MODEL), lambda b: (layer, 0, 0)),
            pl.BlockSpec((None, MLA_KV_LORA, HEADS * (MLA_NOPE + MLA_DV)), lambda b: (j, 0, 0)),
        ],
        out_specs=tok(D_MODEL),
        out_shape=jax.ShapeDtypeStruct(x.shape, F32),
        scratch_shapes=[
            pltpu.VMEM((HEADS, CTX_S, MLA_QK_PAD), BF16),
            pltpu.VMEM((HEADS, CTX_S, 2 * MLA_DV), BF16),
        ],
        compiler_params=pltpu.CompilerParams(
            dimension_semantics=("arbitrary",), vmem_limit_bytes=VMEM_LIMIT),
        name="attn_mla_ctx",
    )(q, st_ckv, st_kpe, gate, x, mods, g_post, w_out, w_kvb)


def _attn_mla_lat_kernel(q_ref, ckv_ref, kpe_ref, cckv_ref, ckpe_ref, gate_ref, x_ref, mods_ref,
                         g_post_ref, w_out_ref, w_kvb_ref, o_ref, kf_ref, vf_ref):
    @pl.when(pl.program_id(1) == 0)
    def _():
        _mla_expand(ckv_ref[...], kpe_ref[...], w_kvb_ref, kf_ref, vf_ref, 0, LAT_S)
        _mla_expand(cckv_ref[...].astype(BF16), ckpe_ref[...].astype(BF16), w_kvb_ref, kf_ref, vf_ref,
                    LAT_S, PAST)

    heads = _mla_heads(q_ref, kf_ref, vf_ref, gate_ref)
    _finish(heads, w_out_ref, x_ref, mods_ref, g_post_ref, 1 + pl.program_id(0), o_ref)


def _attn_mla_lat(q, ckv, kpe, cache_ckv, cache_kpe, gate, x, mods, g_post, w_out, w_kvb, layer, j):
    sk = LAT_S + PAST
    nq = LAT_S // TQ
    tok = lambda w: pl.BlockSpec((TQ, w), lambda b, i: (b * nq + i, 0))
    req = lambda w: pl.BlockSpec((LAT_S, w), lambda b, i: (b, 0))
    return pl.pallas_call(
        _attn_mla_lat_kernel,
        grid=(N_LAT_B, nq),
        in_specs=[
            tok(HEADS * MLA_QK_PAD), req(MLA_KV_LORA), req(MLA_ROPE),
            pl.BlockSpec((None, None, PAST, MLA_KV_LORA), lambda b, i: (b, j, 0, 0)),
            pl.BlockSpec((None, None, PAST, MLA_ROPE), lambda b, i: (b, j, 0, 0)),
            tok(D_MODEL), tok(D_MODEL),
            pl.BlockSpec((None, COND_ROWS, 3 * D_MODEL), lambda b, i: (layer, 0, 0)),
            pl.BlockSpec((None, 1, D_MODEL), lambda b, i: (layer, 0, 0)),
            pl.BlockSpec((None, D_MODEL, D_MODEL), lambda b, i: (layer, 0, 0)),
            pl.BlockSpec((None, MLA_KV_LORA, HEADS * (MLA_NOPE + MLA_DV)), lambda b, i: (j, 0, 0)),
        ],
        out_specs=tok(D_MODEL),
        out_shape=jax.ShapeDtypeStruct(x.shape, F32),
        scratch_shapes=[
            pltpu.VMEM((HEADS, sk, MLA_QK_PAD), BF16),
            pltpu.VMEM((HEADS, sk, 2 * MLA_DV), BF16),
        ],
        compiler_params=pltpu.CompilerParams(
            dimension_semantics=("arbitrary", "arbitrary"), vmem_limit_bytes=VMEM_LIMIT),
        name="attn_mla_lat",
    )(q, ckv, kpe, cache_ckv, cache_kpe, gate, x, mods, g_post, w_out, w_kvb)


def _rope_tables():
    rows = LAT_S // GRID_W
    row = jnp.repeat(jnp.arange(rows), GRID_W).astype(F32)
    col = jnp.tile(jnp.arange(GRID_W), rows).astype(F32)
    n_freq = DA_DH // 4
    inv = ROPE_THETA ** (-jnp.arange(n_freq, dtype=F32) / n_freq)
    ang = jnp.concatenate([row[:, None] * inv, col[:, None] * inv], axis=-1)
    cos, sin = jnp.cos(ang), jnp.sin(ang)
    return (jnp.concatenate([cos, cos, cos, cos], axis=-1),
            jnp.concatenate([-sin, sin, -sin, sin], axis=-1))


def kernel(x_prompt, x_sample, cache_diff_k, cache_diff_v, cache_mla_ckv, cache_mla_kpe,
           c, c_ctx, w_ada, b_ada, g_pre, g_post, w_out,
           da_w_in, da_lam_q1, da_lam_k1, da_lam_q2, da_lam_k2, da_g_sub,
           mla_w_in, mla_g_qa, mla_w_qb, mla_g_kva, mla_w_kvb):
    assert DA_DH == MLA_ROPE
    n_diff = (DEPTH + 1) // 2
    n_mla = DEPTH // 2

    cond = jnp.concatenate(
        [c_ctx[None, :], c, jnp.zeros((COND_ROWS - 1 - N_LAT_B, D_MODEL), F32)], axis=0)
    mods = _ada_call(cond, w_ada, b_ada)

    cos_t, sin_t = _rope_tables()
    g_pre3 = g_pre.reshape(DEPTH, 1, D_MODEL)
    g_post3 = g_post.reshape(DEPTH, 1, D_MODEL)
    w_out_b = w_out.astype(BF16)

    da_w_in_b = da_w_in.astype(BF16)
    g_sub3 = da_g_sub.reshape(n_diff, 1, DA_DV)
    lam_p = jnp.stack([da_lam_q1, da_lam_k1, da_lam_q2, da_lam_k2], axis=1)
    cache_dk = cache_diff_k.reshape(N_LAT_B, n_diff, PAST, D_MODEL)
    cache_dv = cache_diff_v.reshape(N_LAT_B, n_diff, PAST, D_MODEL)

    o1 = MLA_Q_LORA + MLA_KV_LORA
    o2 = o1 + MLA_ROPE
    mla_w_in_b = jnp.concatenate(
        [mla_w_in[:, :, :o1], mla_w_in[:, :, o2:], mla_w_in[:, :, o1:o2],
         jnp.zeros((n_mla, D_MODEL, MLA_IN_PAD - (o2 + D_MODEL)), F32)], axis=-1).astype(BF16)
    w_qb4 = mla_w_qb.reshape(n_mla, MLA_Q_LORA, HEADS, MLA_NOPE + MLA_ROPE)
    w_qb_b = jnp.pad(w_qb4, ((0, 0), (0, 0), (0, 0), (0, MLA_QK_PAD - MLA_NOPE - MLA_ROPE))).reshape(
        n_mla, MLA_Q_LORA, HEADS * MLA_QK_PAD).astype(BF16)
    w_kvb_b = mla_w_kvb.astype(BF16)
    g_qa3 = mla_g_qa.reshape(n_mla, 1, MLA_Q_LORA)
    g_kva3 = mla_g_kva.reshape(n_mla, 1, MLA_KV_LORA)

    x_ctx = x_prompt.reshape(N_CTX_B * CTX_S, D_MODEL)
    x_lat = x_sample.reshape(N_LAT_B * LAT_S, D_MODEL)
    st_dk = jnp.zeros((N_CTX_B, n_diff, CTX_S, D_MODEL), F32)
    st_dv = jnp.zeros((N_CTX_B, n_diff, CTX_S, D_MODEL), F32)
    st_ckv = jnp.zeros((N_CTX_B, n_mla, CTX_S, MLA_KV_LORA), F32)
    st_kpe = jnp.zeros((N_CTX_B, n_mla, CTX_S, MLA_ROPE), F32)

    for i in range(DEPTH):
        j = i // 2
        if i % 2 == 0:
            q_c, gate_c, st_dk, st_dv = _pre_diff_ctx(x_ctx, mods, g_pre3, da_w_in_b, st_dk, st_dv, i, j)
            q_l, k_l, v_l, gate_l = _pre_diff_lat(x_lat, mods, g_pre3, da_w_in_b, cos_t, sin_t, i, j)
            x_ctx = _attn_diff_ctx(q_c, st_dk, st_dv, gate_c, x_ctx, mods, g_post3, w_out_b,
                                   g_sub3, lam_p, i, j)
            x_lat = _attn_diff_lat(q_l, k_l, v_l, cache_dk, cache_dv, gate_l, x_lat, mods, g_post3,
                                   w_out_b, g_sub3, lam_p, i, j)
        else:
            q_c, gate_c, st_ckv, st_kpe = _pre_mla_ctx(
                x_ctx, mods, g_pre3, mla_w_in_b, g_qa3, w_qb_b, g_kva3, st_ckv, st_kpe, i, j)
            q_l, gate_l, ckv_l, kpe_l = _pre_mla_lat(
                x_lat, mods, g_pre3, mla_w_in_b, g_qa3, w_qb_b, g_kva3, cos_t, sin_t, i, j)
            x_ctx = _attn_mla_ctx(q_c, st_ckv, st_kpe, gate_c, x_ctx, mods, g_post3, w_out_b,
                                  w_kvb_b, i, j)
            x_lat = _attn_mla_lat(q_l, ckv_l, kpe_l, cache_mla_ckv, cache_mla_kpe, gate_l, x_lat,
                                  mods, g_post3, w_out_b, w_kvb_b, i, j)

    return (x_ctx.reshape(N_CTX_B, CTX_S, D_MODEL),
            x_lat.reshape(N_LAT_B, LAT_S, D_MODEL),
            st_dk.reshape(N_CTX_B, n_diff, CTX_S, HEADS, 2, DA_DH),
            st_dv.reshape(N_CTX_B, n_diff, CTX_S, HEADS, DA_DV),
            st_ckv,
            st_kpe)
```

```python
import functools
import math

import jax
import jax.numpy as jnp
from jax import lax
from jax.experimental import pallas as pl
from jax.experimental.pallas import tpu as pltpu

F32 = jnp.float32
BF16 = jnp.bfloat16

D_MODEL = 1024
DEPTH = 4
N_CTX_B = 16
CTX_S = 256
N_LAT_B = 4
LAT_S = 1024
PAST = 256
GRID_W = 64
HEADS = 8
DA_DH = 64
DA_DV = 128
MLA_Q_LORA = 384
MLA_KV_LORA = 256
MLA_NOPE = 128
MLA_ROPE = 64
MLA_DV = 128
MLA_QK_PAD = 256
ROPE_THETA = 10000.0
NORM_EPS = 1e-6
COND_ROWS = 8

LANES = 128
TQ = 256
TM = 256
VMEM_LIMIT = 56 * 1024 * 1024

_TRANS_B = (((1,), (1,)), ((), ()))
SCORES_AHEAD = 2
DA_Q_SCALE = DA_DH ** -0.5 * math.log2(math.e)
MLA_Q_SCALE = (MLA_NOPE + MLA_ROPE) ** -0.5 * math.log2(math.e)


def _silu(x):
    return x * (1.0 / (1.0 + jnp.exp(-x)))


def _rms(x, g):
    r = lax.rsqrt(jnp.mean(x * x, axis=-1, keepdims=True) + NORM_EPS)
    return (x * r) * g


def _rope_slab(x, cos, sin_signed):
    lane = lax.broadcasted_iota(jnp.int32, x.shape, 1)
    first_half = (lane % 64) < 32
    partner = jnp.where(first_half, pltpu.roll(x, 96, axis=1), pltpu.roll(x, 32, axis=1))
    return x * cos + partner * sin_signed


def _ones_column(rows):
    lane = lax.broadcasted_iota(jnp.int32, (rows, LANES), 1)
    return jnp.where(lane == 0, 1.0, 0.0).astype(BF16)


def _cond_row(mods_ref, row):
    m = mods_ref[pl.ds(row, 1), :]
    return m[:, 0:D_MODEL], m[:, D_MODEL:2 * D_MODEL], m[:, 2 * D_MODEL:3 * D_MODEL]


def _modulated(x_ref, mods_ref, g_ref, row):
    shift, scale, _ = _cond_row(mods_ref, row)
    h = _rms(x_ref[...], g_ref[...]) * (1.0 + scale) + shift
    return h.astype(BF16)


def _ada_kernel(cond_ref, w_ref, b_ref, o_ref):
    a = _silu(cond_ref[...]).astype(BF16)
    o_ref[...] = jnp.dot(a, w_ref[...].astype(BF16), preferred_element_type=F32) + b_ref[...]


def _ada_call(cond, w_ada, b_ada):
    tn = 1024
    return pl.pallas_call(
        _ada_kernel,
        grid=(DEPTH, 3 * D_MODEL // tn),
        in_specs=[
            pl.BlockSpec((COND_ROWS, D_MODEL), lambda i, n: (0, 0)),
            pl.BlockSpec((None, D_MODEL, tn), lambda i, n: (i, 0, n)),
            pl.BlockSpec((None, 1, tn), lambda i, n: (i, 0, n)),
        ],
        out_specs=pl.BlockSpec((None, COND_ROWS, tn), lambda i, n: (i, 0, n)),
        out_shape=jax.ShapeDtypeStruct((DEPTH, COND_ROWS, 3 * D_MODEL), F32),
        compiler_params=pltpu.CompilerParams(
            dimension_semantics=("arbitrary", "arbitrary"), vmem_limit_bytes=VMEM_LIMIT),
        name="ada_mod",
    )(cond, w_ada, b_ada.reshape(DEPTH, 1, 3 * D_MODEL))


def _write_state(ref, j, value):
    if j is None:
        ref[...] = value
    else:
        for jj in range(ref.shape[0]):
            ref[jj] = value if jj == j else jnp.zeros_like(value)


def _pre_diff_ctx_kernel(create_j, x_ref, mods_ref, g_ref, w_ref, *refs):
    q_ref, gate_ref, sk_ref, sv_ref = refs[-4:]
    hb = _modulated(x_ref, mods_ref, g_ref, 0)
    n = D_MODEL
    q = jnp.dot(hb, w_ref[:, 0:n], preferred_element_type=F32)
    q_ref[...] = (q * DA_Q_SCALE).astype(BF16)
    _write_state(sk_ref, create_j, jnp.dot(hb, w_ref[:, n:2 * n], preferred_element_type=F32))
    _write_state(sv_ref, create_j, jnp.dot(hb, w_ref[:, 2 * n:3 * n], preferred_element_type=F32))
    gate_ref[...] = _silu(jnp.dot(hb, w_ref[:, 3 * n:4 * n], preferred_element_type=F32)).astype(BF16)


def _state_plumbing(states, shapes, j, n_inputs, n_outputs):
    if states is None:
        specs = [pl.BlockSpec((None,) + shp[1:], lambda t: (t, 0, 0, 0)) for shp in shapes]
        return [], specs, {}, ()
    specs = [pl.BlockSpec((None, None) + shp[2:], lambda t: (t, j, 0, 0)) for shp in shapes]
    in_specs = [pl.BlockSpec(memory_space=pl.ANY) for _ in shapes]
    aliases = {n_inputs + i: n_outputs + i for i in range(len(shapes))}
    return in_specs, specs, aliases, tuple(states)


def _pre_diff_ctx(x, mods, g_pre, w_in, states, layer, j, n_layers):
    n_tok = N_CTX_B * CTX_S
    shapes = [(N_CTX_B, n_layers, CTX_S, D_MODEL)] * 2
    st_in_specs, st_out_specs, aliases, st_args = _state_plumbing(states, shapes, j, 4, 2)
    return pl.pallas_call(
        functools.partial(_pre_diff_ctx_kernel, j if states is None else None),
        grid=(N_CTX_B,),
        in_specs=[
            pl.BlockSpec((CTX_S, D_MODEL), lambda t: (t, 0)),
            pl.BlockSpec((None, COND_ROWS, 3 * D_MODEL), lambda t: (layer, 0, 0)),
            pl.BlockSpec((None, 1, D_MODEL), lambda t: (layer, 0, 0)),
            pl.BlockSpec((None, D_MODEL, 4 * D_MODEL), lambda t: (j, 0, 0)),
        ] + st_in_specs,
        out_specs=[
            pl.BlockSpec((CTX_S, D_MODEL), lambda t: (t, 0)),
            pl.BlockSpec((CTX_S, D_MODEL), lambda t: (t, 0)),
        ] + st_out_specs,
        out_shape=[
            jax.ShapeDtypeStruct((n_tok, D_MODEL), BF16),
            jax.ShapeDtypeStruct((n_tok, D_MODEL), BF16),
        ] + [jax.ShapeDtypeStruct(shp, F32) for shp in shapes],
        input_output_aliases=aliases,
        compiler_params=pltpu.CompilerParams(
            dimension_semantics=("arbitrary",), vmem_limit_bytes=VMEM_LIMIT),
        name="pre_diff_ctx",
    )(x, mods, g_pre, w_in, *st_args)


def _pre_diff_lat_kernel(x_ref, mods_ref, g_ref, w_ref, cos_ref, sin_ref,
                         q_ref, k_ref, v_ref, gate_ref):
    row = 1 + pl.program_id(0) // (LAT_S // TM)
    hb = _modulated(x_ref, mods_ref, g_ref, row)
    cos = cos_ref[...]
    sin = sin_ref[...]
    n = D_MODEL
    q = jnp.dot(hb, w_ref[:, 0:n], preferred_element_type=F32)
    k = jnp.dot(hb, w_ref[:, n:2 * n], preferred_element_type=F32)
    cos_q = cos * DA_Q_SCALE
    sin_q = sin * DA_Q_SCALE
    slabs = [slice(h * LANES, (h + 1) * LANES) for h in range(HEADS)]
    q_ref[...] = jnp.concatenate(
        [_rope_slab(q[:, sl], cos_q, sin_q).astype(BF16) for sl in slabs], axis=1)
    k_ref[...] = jnp.concatenate(
        [_rope_slab(k[:, sl], cos, sin).astype(BF16) for sl in slabs], axis=1)
    v_ref[...] = jnp.dot(hb, w_ref[:, 2 * n:3 * n], preferred_element_type=F32).astype(BF16)
    gate_ref[...] = _silu(jnp.dot(hb, w_ref[:, 3 * n:4 * n], preferred_element_type=F32)).astype(BF16)


def _pre_diff_lat(x, mods, g_pre, w_in, cos_t, sin_t, layer, j):
    n_tok = N_LAT_B * LAT_S
    tiles_per_req = LAT_S // TM
    tok_spec = pl.BlockSpec((TM, D_MODEL), lambda t: (t, 0))
    rope_spec = pl.BlockSpec((TM, LANES), lambda t: (t % tiles_per_req, 0))
    return pl.pallas_call(
        _pre_diff_lat_kernel,
        grid=(n_tok // TM,),
        in_specs=[
            tok_spec,
            pl.BlockSpec((None, COND_ROWS, 3 * D_MODEL), lambda t: (layer, 0, 0)),
            pl.BlockSpec((None, 1, D_MODEL), lambda t: (layer, 0, 0)),
            pl.BlockSpec((None, D_MODEL, 4 * D_MODEL), lambda t: (j, 0, 0)),
            rope_spec,
            rope_spec,
        ],
        out_specs=[tok_spec, tok_spec, tok_spec, tok_spec],
        out_shape=[
            jax.ShapeDtypeStruct((n_tok, D_MODEL), BF16),
            jax.ShapeDtypeStruct((n_tok, D_MODEL), BF16),
            jax.ShapeDtypeStruct((n_tok, D_MODEL), BF16),
            jax.ShapeDtypeStruct((n_tok, D_MODEL), BF16),
        ],
        compiler_params=pltpu.CompilerParams(
            dimension_semantics=("arbitrary",), vmem_limit_bytes=VMEM_LIMIT),
        name="pre_diff_lat",
    )(x, mods, g_pre, w_in, cos_t, sin_t)


def _finish(heads, w_out_ref, x_ref, mods_ref, g_post_ref, row, o_ref):
    _, _, gate = _cond_row(mods_ref, row)
    og = jnp.concatenate(heads, axis=1)
    y = jnp.dot(og, w_out_ref[...], preferred_element_type=F32)
    o_ref[...] = x_ref[...] + gate * _rms(y, g_post_ref[...])


def _diff_lambda(lam_ref, lam_init):
    p = lam_ref[...]
    a = jnp.sum(p[0:1, :] * p[1:2, :], axis=-1, keepdims=True)
    b = jnp.sum(p[2:3, :] * p[3:4, :], axis=-1, keepdims=True)
    return jnp.exp(a) - jnp.exp(b) + lam_init


def _stack_diff_keys(k, k2_ref, sk):
    lane = lax.broadcasted_iota(jnp.int32, (sk, LANES), 1)
    zero = jnp.zeros((sk, LANES), BF16)
    for h in range(HEADS):
        kh = k[:, h * LANES:(h + 1) * LANES]
        k2_ref[h, 0:sk, :] = jnp.where(lane < DA_DH, kh, zero)
        k2_ref[h, sk:2 * sk, :] = jnp.where(lane >= DA_DH, kh, zero)


def _stack_diff_values(v, vb_ref, r0, rows):
    ones = _ones_column(rows)
    for h in range(HEADS):
        vb_ref[h, r0:r0 + rows, 0:DA_DV] = v[:, h * LANES:(h + 1) * LANES]
        vb_ref[h, r0:r0 + rows, DA_DV:2 * DA_DV] = ones


def _diff_heads(q_ref, k2_ref, v_ref, gate_ref, g_sub_ref, lam, lam_init, sk):
    g_sub = g_sub_ref[...]

    def scores(h):
        return lax.dot_general(q_ref[:, h * LANES:(h + 1) * LANES], k2_ref[h], _TRANS_B,
                               preferred_element_type=F32)

    heads = []
    pending = [scores(h) for h in range(SCORES_AHEAD)]
    for h in range(HEADS):
        lo, hi = h * LANES, (h + 1) * LANES
        s = pending.pop(0)
        if h + SCORES_AHEAD < HEADS:
            pending.append(scores(h + SCORES_AHEAD))
        s0 = s[:, 0:sk]
        s1 = s[:, sk:2 * sk]
        e0 = jnp.exp2(s0 - jnp.max(s0, axis=-1, keepdims=True)).astype(BF16)
        e1 = jnp.exp2(s1 - jnp.max(s1, axis=-1, keepdims=True)).astype(BF16)
        pv0 = jnp.dot(e0, v_ref[h], preferred_element_type=F32)
        pv1 = jnp.dot(e1, v_ref[h], preferred_element_type=F32)
        a0 = 1.0 / pv0[:, DA_DV:DA_DV + 1]
        a1 = lam / pv1[:, DA_DV:DA_DV + 1]
        o = pv0[:, 0:DA_DV] * a0 - pv1[:, 0:DA_DV] * a1
        o = _rms(o, g_sub) * (1.0 - lam_init)
        heads.append((o * gate_ref[:, lo:hi]).astype(BF16))
    return heads


def _attn_diff_ctx_kernel(lam_init, q_ref, k_ref, v_ref, gate_ref, x_ref, mods_ref, g_post_ref,
                          w_out_ref, g_sub_ref, lam_ref, o_ref, k2_ref, vb_ref):
    _stack_diff_keys(k_ref[...].astype(BF16), k2_ref, CTX_S)
    _stack_diff_values(v_ref[...].astype(BF16), vb_ref, 0, CTX_S)
    lam = _diff_lambda(lam_ref, lam_init)
    heads = _diff_heads(q_ref, k2_ref, vb_ref, gate_ref, g_sub_ref, lam, lam_init, CTX_S)
    _finish(heads, w_out_ref, x_ref, mods_ref, g_post_ref, 0, o_ref)


def _attn_diff_ctx(q, st_k, st_v, gate, x, mods, g_post, w_out, g_sub, lam_p, layer, j):
    lam_init = 0.8 - 0.6 * math.exp(-0.3 * layer)
    tok_spec = pl.BlockSpec((CTX_S, D_MODEL), lambda b: (b, 0))
    state_spec = pl.BlockSpec((None, None, CTX_S, D_MODEL), lambda b: (b, j, 0, 0))
    return pl.pallas_call(
        functools.partial(_attn_diff_ctx_kernel, lam_init),
        grid=(N_CTX_B,),
        in_specs=[
            tok_spec, state_spec, state_spec, tok_spec, tok_spec,
            pl.BlockSpec((None, COND_ROWS, 3 * D_MODEL), lambda b: (layer, 0, 0)),
            pl.BlockSpec((None, 1, D_MODEL), lambda b: (layer, 0, 0)),
            pl.BlockSpec((None, D_MODEL, D_MODEL), lambda b: (layer, 0, 0)),
            pl.BlockSpec((None, 1, DA_DV), lambda b: (j, 0, 0)),
            pl.BlockSpec((None, 4, DA_DH), lambda b: (j, 0, 0)),
        ],
        out_specs=tok_spec,
        out_shape=jax.ShapeDtypeStruct(x.shape, F32),
        scratch_shapes=[
            pltpu.VMEM((HEADS, 2 * CTX_S, LANES), BF16),
            pltpu.VMEM((HEADS, CTX_S, 2 * DA_DV), BF16),
        ],
        compiler_params=pltpu.CompilerParams(
            dimension_semantics=("arbitrary",), vmem_limit_bytes=VMEM_LIMIT),
        name="attn_diff_ctx",
    )(q, st_k, st_v, gate, x, mods, g_post, w_out, g_sub, lam_p)


def _attn_diff_lat_kernel(lam_init, q_ref, k_ref, v_ref, ck_ref, cv_ref, gate_ref, x_ref, mods_ref,
                          g_post_ref, w_out_ref, g_sub_ref, lam_ref, o_ref, k2_ref, vb_ref):
    sk = LAT_S + PAST

    @pl.when(pl.program_id(1) == 0)
    def _():
        lane = lax.broadcasted_iota(jnp.int32, (PAST, LANES), 1)
        zero_l = jnp.zeros((LAT_S, LANES), BF16)
        zero_p = jnp.zeros((PAST, LANES), BF16)
        lane_l = lax.broadcasted_iota(jnp.int32, (LAT_S, LANES), 1)
        for h in range(HEADS):
            lo, hi = h * LANES, (h + 1) * LANES
            kh = k_ref[:, lo:hi]
            ch = ck_ref[:, lo:hi].astype(BF16)
            k2_ref[h, 0:LAT_S, :] = jnp.where(lane_l < DA_DH, kh, zero_l)
            k2_ref[h, LAT_S:sk, :] = jnp.where(lane < DA_DH, ch, zero_p)
            k2_ref[h, sk:sk + LAT_S, :] = jnp.where(lane_l >= DA_DH, kh, zero_l)
            k2_ref[h, sk + LAT_S:2 * sk, :] = jnp.where(lane >= DA_DH, ch, zero_p)
        _stack_diff_values(v_ref[...], vb_ref, 0, LAT_S)
        _stack_diff_values(cv_ref[...].astype(BF16), vb_ref, LAT_S, PAST)

    lam = _diff_lambda(lam_ref, lam_init)
    heads = _diff_heads(q_ref, k2_ref, vb_ref, gate_ref, g_sub_ref, lam, lam_init, sk)
    _finish(heads, w_out_ref, x_ref, mods_ref, g_post_ref, 1 + pl.program_id(0), o_ref)


def _attn_diff_lat(q, k, v, cache_k, cache_v, gate, x, mods, g_post, w_out, g_sub, lam_p, layer, j):
    lam_init = 0.8 - 0.6 * math.exp(-0.3 * layer)
    sk = LAT_S + PAST
    nq = LAT_S // TQ
    tok_spec = pl.BlockSpec((TQ, D_MODEL), lambda b, i: (b * nq + i, 0))
    kv_spec = pl.BlockSpec((LAT_S, D_MODEL), lambda b, i: (b, 0))
    cache_spec = pl.BlockSpec((None, None, PAST, D_MODEL), lambda b, i: (b, j, 0, 0))
    return pl.pallas_call(
        functools.partial(_attn_diff_lat_kernel, lam_init),
        grid=(N_LAT_B, nq),
        in_specs=[
            tok_spec, kv_spec, kv_spec, cache_spec, cache_spec, tok_spec, tok_spec,
            pl.BlockSpec((None, COND_ROWS, 3 * D_MODEL), lambda b, i: (layer, 0, 0)),
            pl.BlockSpec((None, 1, D_MODEL), lambda b, i: (layer, 0, 0)),
            pl.BlockSpec((None, D_MODEL, D_MODEL), lambda b, i: (layer, 0, 0)),
            pl.BlockSpec((None, 1, DA_DV), lambda b, i: (j, 0, 0)),
            pl.BlockSpec((None, 4, DA_DH), lambda b, i: (j, 0, 0)),
        ],
        out_specs=tok_spec,
        out_shape=jax.ShapeDtypeStruct(x.shape, F32),
        scratch_shapes=[
            pltpu.VMEM((HEADS, 2 * sk, LANES), BF16),
            pltpu.VMEM((HEADS, sk, 2 * DA_DV), BF16),
        ],
        compiler_params=pltpu.CompilerParams(
            dimension_semantics=("arbitrary", "arbitrary"), vmem_limit_bytes=VMEM_LIMIT),
        name="attn_diff_lat",
    )(q, k, v, cache_k, cache_v, gate, x, mods, g_post, w_out, g_sub, lam_p)


MLA_IN = MLA_Q_LORA + MLA_KV_LORA + MLA_ROPE + D_MODEL
_QA = slice(0, MLA_Q_LORA)
_KV = slice(MLA_Q_LORA, MLA_Q_LORA + MLA_KV_LORA)
_PE = slice(_KV.stop, _KV.stop + LANES)
_GT = slice(_KV.stop + MLA_ROPE, MLA_IN)


def _proj_t(hb, w_t_ref, rows):
    return lax.dot_general(hb, w_t_ref[rows, :], _TRANS_B, preferred_element_type=F32)


def _mla_queries(hb, w_in_ref, g_qa_ref):
    q_a = _proj_t(hb, w_in_ref, _QA)
    return _rms(q_a, g_qa_ref[...]).astype(BF16)


def _pre_mla_ctx_kernel(create_j, x_ref, mods_ref, g_ref, w_in_ref, g_qa_ref, w_qb_ref, g_kva_ref, *refs):
    q_ref, gate_ref, ckv_ref, kpe_ref = refs[-4:]
    hb = _modulated(x_ref, mods_ref, g_ref, 0)
    qn = _mla_queries(hb, w_in_ref, g_qa_ref)
    q_ref[...] = (jnp.dot(qn, w_qb_ref[...], preferred_element_type=F32) * MLA_Q_SCALE).astype(BF16)
    kv_a = _proj_t(hb, w_in_ref, _KV)
    _write_state(ckv_ref, create_j, _rms(kv_a, g_kva_ref[...]))
    gate_ref[...] = _silu(_proj_t(hb, w_in_ref, _GT)).astype(BF16)
    pe = _proj_t(hb, w_in_ref, _PE)
    _write_state(kpe_ref, create_j, pe[:, 0:MLA_ROPE])


def _pre_mla_ctx(x, mods, g_pre, w_in, g_qa, w_qb, g_kva, states, layer, j, n_layers):
    n_tok = N_CTX_B * CTX_S
    tok = lambda w: pl.BlockSpec((CTX_S, w), lambda t: (t, 0))
    shapes = [(N_CTX_B, n_layers, CTX_S, MLA_KV_LORA), (N_CTX_B, n_layers, CTX_S, MLA_ROPE)]
    st_in_specs, st_out_specs, aliases, st_args = _state_plumbing(states, shapes, j, 7, 2)
    return pl.pallas_call(
        functools.partial(_pre_mla_ctx_kernel, j if states is None else None),
        grid=(N_CTX_B,),
        in_specs=[
            tok(D_MODEL),
            pl.BlockSpec((None, COND_ROWS, 3 * D_MODEL), lambda t: (layer, 0, 0)),
            pl.BlockSpec((None, 1, D_MODEL), lambda t: (layer, 0, 0)),
            pl.BlockSpec((None, MLA_IN, D_MODEL), lambda t: (j, 0, 0)),
            pl.BlockSpec((None, 1, MLA_Q_LORA), lambda t: (j, 0, 0)),
            pl.BlockSpec((None, MLA_Q_LORA, HEADS * MLA_QK_PAD), lambda t: (j, 0, 0)),
            pl.BlockSpec((None, 1, MLA_KV_LORA), lambda t: (j, 0, 0)),
        ] + st_in_specs,
        out_specs=[tok(HEADS * MLA_QK_PAD), tok(D_MODEL)] + st_out_specs,
        out_shape=[
            jax.ShapeDtypeStruct((n_tok, HEADS * MLA_QK_PAD), BF16),
            jax.ShapeDtypeStruct((n_tok, D_MODEL), BF16),
        ] + [jax.ShapeDtypeStruct(shp, F32) for shp in shapes],
        input_output_aliases=aliases,
        compiler_params=pltpu.CompilerParams(
            dimension_semantics=("arbitrary",), vmem_limit_bytes=VMEM_LIMIT),
        name="pre_mla_ctx",
    )(x, mods, g_pre, w_in, g_qa, w_qb, g_kva, *st_args)


def _pre_mla_lat_kernel(x_ref, mods_ref, g_ref, w_in_ref, g_qa_ref, w_qb_ref, g_kva_ref, cos_ref, sin_ref,
                        q_ref, gate_ref, ckv_ref, kpe_ref):
    row = 1 + pl.program_id(0) // (LAT_S // TM)
    hb = _modulated(x_ref, mods_ref, g_ref, row)
    cos = cos_ref[...]
    sin = sin_ref[...]
    qn = _mla_queries(hb, w_in_ref, g_qa_ref)
    for h in range(HEADS):
        lo = h * MLA_QK_PAD
        q = jnp.dot(qn, w_qb_ref[:, lo:lo + MLA_QK_PAD], preferred_element_type=F32)
        q = q * MLA_Q_SCALE
        q_ref[:, lo:lo + LANES] = q[:, 0:LANES].astype(BF16)
        q_ref[:, lo + LANES:lo + 2 * LANES] = _rope_slab(q[:, LANES:2 * LANES], cos, sin).astype(BF16)
    kv_a = _proj_t(hb, w_in_ref, _KV)
    ckv_ref[...] = _rms(kv_a, g_kva_ref[...]).astype(BF16)
    gate_ref[...] = _silu(_proj_t(hb, w_in_ref, _GT)).astype(BF16)
    pe = _proj_t(hb, w_in_ref, _PE)
    kpe_ref[...] = _rope_slab(pe, cos, sin)[:, 0:MLA_ROPE].astype(BF16)


def _pre_mla_lat(x, mods, g_pre, w_in, g_qa, w_qb, g_kva, cos_t, sin_t, layer, j):
    n_tok = N_LAT_B * LAT_S
    tiles_per_req = LAT_S // TM
    tok = lambda w: pl.BlockSpec((TM, w), lambda t: (t, 0))
    rope_spec = pl.BlockSpec((TM, LANES), lambda t: (t % tiles_per_req, 0))
    return pl.pallas_call(
        _pre_mla_lat_kernel,
        grid=(n_tok // TM,),
        in_specs=[
            tok(D_MODEL),
            pl.BlockSpec((None, COND_ROWS, 3 * D_MODEL), lambda t: (layer, 0, 0)),
            pl.BlockSpec((None, 1, D_MODEL), lambda t: (layer, 0, 0)),
            pl.BlockSpec((None, MLA_IN, D_MODEL), lambda t: (j, 0, 0)),
            pl.BlockSpec((None, 1, MLA_Q_LORA), lambda t: (j, 0, 0)),
            pl.BlockSpec((None, MLA_Q_LORA, HEADS * MLA_QK_PAD), lambda t: (j, 0, 0)),
            pl.BlockSpec((None, 1, MLA_KV_LORA), lambda t: (j, 0, 0)),
            rope_spec,
            rope_spec,
        ],
        out_specs=[tok(HEADS * MLA_QK_PAD), tok(D_MODEL), tok(MLA_KV_LORA), tok(MLA_ROPE)],
        out_shape=[
            jax.ShapeDtypeStruct((n_tok, HEADS * MLA_QK_PAD), BF16),
            jax.ShapeDtypeStruct((n_tok, D_MODEL), BF16),
            jax.ShapeDtypeStruct((n_tok, MLA_KV_LORA), BF16),
            jax.ShapeDtypeStruct((n_tok, MLA_ROPE), BF16),
        ],
        compiler_params=pltpu.CompilerParams(
            dimension_semantics=("arbitrary",), vmem_limit_bytes=VMEM_LIMIT),
        name="pre_mla_lat",
    )(x, mods, g_pre, w_in, g_qa, w_qb, g_kva, cos_t, sin_t)


def _mla_expand(ckv, kpe, w_kvb_ref, kf_ref, vf_ref, r0, rows):
    zero = jnp.zeros((rows, LANES - MLA_ROPE), BF16)
    for h in range(HEADS):
        lo = h * (MLA_NOPE + MLA_DV)
        kv = jnp.dot(ckv, w_kvb_ref[:, lo:lo + MLA_NOPE + MLA_DV], preferred_element_type=F32)
        kf_ref[h, r0:r0 + rows, 0:MLA_NOPE] = kv[:, 0:MLA_NOPE].astype(BF16)
        kf_ref[h, r0:r0 + rows, MLA_NOPE:MLA_NOPE + MLA_ROPE] = kpe
        kf_ref[h, r0:r0 + rows, MLA_NOPE + MLA_ROPE:MLA_QK_PAD] = zero
        vf_ref[h, r0:r0 + rows, :] = kv[:, MLA_NOPE:MLA_NOPE + MLA_DV].astype(BF16)


def _mla_heads(q_ref, kf_ref, vf_ref, gate_ref):
    def scores(h):
        return lax.dot_general(q_ref[:, h * MLA_QK_PAD:(h + 1) * MLA_QK_PAD], kf_ref[h], _TRANS_B,
                               preferred_element_type=F32)

    heads = []
    pending = [scores(h) for h in range(SCORES_AHEAD)]
    for h in range(HEADS):
        s = pending.pop(0)
        if h + SCORES_AHEAD < HEADS:
            pending.append(scores(h + SCORES_AHEAD))
        e = jnp.exp2(s - jnp.max(s, axis=-1, keepdims=True))
        inv = 1.0 / jnp.sum(e, axis=-1, keepdims=True)
        o = jnp.dot(e.astype(BF16), vf_ref[h], preferred_element_type=F32) * inv
        lo, hi = h * LANES, (h + 1) * LANES
        heads.append((o * gate_ref[:, lo:hi]).astype(BF16))
    return heads


def _attn_mla_ctx_kernel(q_ref, ckv_ref, kpe_ref, gate_ref, x_ref, mods_ref, g_post_ref, w_out_ref,
                         w_kvb_ref, o_ref, kf_ref, vf_ref):
    _mla_expand(ckv_ref[...].astype(BF16), kpe_ref[...].astype(BF16), w_kvb_ref, kf_ref, vf_ref, 0, CTX_S)
    heads = _mla_heads(q_ref, kf_ref, vf_ref, gate_ref)
    _finish(heads, w_out_ref, x_ref, mods_ref, g_post_ref, 0, o_ref)


def _attn_mla_ctx(q, st_ckv, st_kpe, gate, x, mods, g_post, w_out, w_kvb, layer, j):
    tok = lambda w: pl.BlockSpec((CTX_S, w), lambda b: (b, 0))
    return pl.pallas_call(
        _attn_mla_ctx_kernel,
        grid=(N_CTX_B,),
        in_specs=[
            tok(HEADS * MLA_QK_PAD),
            pl.BlockSpec((None, None, CTX_S, MLA_KV_LORA), lambda b: (b, j, 0, 0)),
            pl.BlockSpec((None, None, CTX_S, MLA_ROPE), lambda b: (b, j, 0, 0)),
            tok(D_MODEL), tok(D_MODEL),
            pl.BlockSpec((None, COND_ROWS, 3 * D_MODEL), lambda b: (layer, 0, 0)),
            pl.BlockSpec((None, 1, D_MODEL), lambda b: (layer, 0, 0)),
            pl.BlockSpec((None, D_MODEL, D_MODEL), lambda b: (layer, 0, 0)),
            pl.BlockSpec((None, MLA_KV_LORA, HEADS * (MLA_NOPE + MLA_DV)), lambda b: (j, 0, 0)),
        ],
        out_specs=tok(D_MODEL),
        out_shape=jax.ShapeDtypeStruct(x.shape, F32),
        scratch_shapes=[
            pltpu.VMEM((HEADS, CTX_S, MLA_QK_PAD), BF16),
            pltpu.VMEM((HEADS, CTX_S, MLA_DV), BF16),
        ],
        compiler_params=pltpu.CompilerParams(
            dimension_semantics=("arbitrary",), vmem_limit_bytes=VMEM_LIMIT),
        name="attn_mla_ctx",
    )(q, st_ckv, st_kpe, gate, x, mods, g_post, w_out, w_kvb)


def _attn_mla_lat_kernel(q_ref, ckv_ref, kpe_ref, cckv_ref, ckpe_ref, gate_ref, x_ref, mods_ref,
                         g_post_ref, w_out_ref, w_kvb_ref, o_ref, kf_ref, vf_ref):
    @pl.when(pl.program_id(1) == 0)
    def _():
        _mla_expand(ckv_ref[...], kpe_ref[...], w_kvb_ref, kf_ref, vf_ref, 0, LAT_S)
        _mla_expand(cckv_ref[...].astype(BF16), ckpe_ref[...].astype(BF16), w_kvb_ref, kf_ref, vf_ref,
                    LAT_S, PAST)

    heads = _mla_heads(q_ref, kf_ref, vf_ref, gate_ref)
    _finish(heads, w_out_ref, x_ref, mods_ref, g_post_ref, 1 + pl.program_id(0), o_ref)


def _attn_mla_lat(q, ckv, kpe, cache_ckv, cache_kpe, gate, x, mods, g_post, w_out, w_kvb, layer, j):
    sk = LAT_S + PAST
    nq = LAT_S // TQ
    tok = lambda w: pl.BlockSpec((TQ, w), lambda b, i: (b * nq + i, 0))
    req = lambda w: pl.BlockSpec((LAT_S, w), lambda b, i: (b, 0))
    return pl.pallas_call(
        _attn_mla_lat_kernel,
        grid=(N_LAT_B, nq),
        in_specs=[
            tok(HEADS * MLA_QK_PAD), req(MLA_KV_LORA), req(MLA_ROPE),
            pl.BlockSpec((None, None, PAST, MLA_KV_LORA), lambda b, i: (b, j, 0, 0)),
            pl.BlockSpec((None, None, PAST, MLA_ROPE), lambda b, i: (b, j, 0, 0)),
            tok(D_MODEL), tok(D_MODEL),
            pl.BlockSpec((None, COND_ROWS, 3 * D_MODEL), lambda b, i: (layer, 0, 0)),
            pl.BlockSpec((None, 1, D_MODEL), lambda b, i: (layer, 0, 0)),
            pl.BlockSpec((None, D_MODEL, D_MODEL), lambda b, i: (layer, 0, 0)),
            pl.BlockSpec((None, MLA_KV_LORA, HEADS * (MLA_NOPE + MLA_DV)), lambda b, i: (j, 0, 0)),
        ],
        out_specs=tok(D_MODEL),
        out_shape=jax.ShapeDtypeStruct(x.shape, F32),
        scratch_shapes=[
            pltpu.VMEM((HEADS, sk, MLA_QK_PAD), BF16),
            pltpu.VMEM((HEADS, sk, MLA_DV), BF16),
        ],
        compiler_params=pltpu.CompilerParams(
            dimension_semantics=("arbitrary", "arbitrary"), vmem_limit_bytes=VMEM_LIMIT),
        name="attn_mla_lat",
    )(q, ckv, kpe, cache_ckv, cache_kpe, gate, x, mods, g_post, w_out, w_kvb)


def _rope_tables():
    rows = LAT_S // GRID_W
    row = jnp.repeat(jnp.arange(rows), GRID_W).astype(F32)
    col = jnp.tile(jnp.arange(GRID_W), rows).astype(F32)
    n_freq = DA_DH // 4
    inv = ROPE_THETA ** (-jnp.arange(n_freq, dtype=F32) / n_freq)
    ang = jnp.concatenate([row[:, None] * inv, col[:, None] * inv], axis=-1)
    cos, sin = jnp.cos(ang), jnp.sin(ang)
    return (jnp.concatenate([cos, cos, cos, cos], axis=-1),
            jnp.concatenate([-sin, sin, -sin, sin], axis=-1))


def kernel(x_prompt, x_sample, cache_diff_k, cache_diff_v, cache_mla_ckv, cache_mla_kpe,
           c, c_ctx, w_ada, b_ada, g_pre, g_post, w_out,
           da_w_in, da_lam_q1, da_lam_k1, da_lam_q2, da_lam_k2, da_g_sub,
           mla_w_in, mla_g_qa, mla_w_qb, mla_g_kva, mla_w_kvb):
    assert DA_DH == MLA_ROPE
    n_diff = (DEPTH + 1) // 2
    n_mla = DEPTH // 2

    cond = jnp.concatenate(
        [c_ctx[None, :], c, jnp.zeros((COND_ROWS - 1 - N_LAT_B, D_MODEL), F32)], axis=0)
    mods = _ada_call(cond, w_ada, b_ada)

    cos_t, sin_t = _rope_tables()
    g_pre3 = g_pre.reshape(DEPTH, 1, D_MODEL)
    g_post3 = g_post.reshape(DEPTH, 1, D_MODEL)
    w_out_b = w_out.astype(BF16)

    da_w_in_b = da_w_in.astype(BF16)
    g_sub3 = da_g_sub.reshape(n_diff, 1, DA_DV)
    lam_p = jnp.stack([da_lam_q1, da_lam_k1, da_lam_q2, da_lam_k2], axis=1)
    cache_dk = cache_diff_k.reshape(N_LAT_B, n_diff, PAST, D_MODEL)
    cache_dv = cache_diff_v.reshape(N_LAT_B, n_diff, PAST, D_MODEL)

    mla_w_in_b = jnp.swapaxes(mla_w_in, 1, 2).astype(BF16)
    w_qb4 = mla_w_qb.reshape(n_mla, MLA_Q_LORA, HEADS, MLA_NOPE + MLA_ROPE)
    w_qb_b = jnp.pad(w_qb4, ((0, 0), (0, 0), (0, 0), (0, MLA_QK_PAD - MLA_NOPE - MLA_ROPE))).reshape(
        n_mla, MLA_Q_LORA, HEADS * MLA_QK_PAD).astype(BF16)
    w_kvb_b = mla_w_kvb.astype(BF16)
    g_qa3 = mla_g_qa.reshape(n_mla, 1, MLA_Q_LORA)
    g_kva3 = mla_g_kva.reshape(n_mla, 1, MLA_KV_LORA)

    x_ctx = x_prompt.reshape(N_CTX_B * CTX_S, D_MODEL)
    x_lat = x_sample.reshape(N_LAT_B * LAT_S, D_MODEL)
    st_diff = None
    st_mla = None

    for i in range(DEPTH):
        j = i // 2
        if i % 2 == 0:
            q_c, gate_c, st_dk, st_dv = _pre_diff_ctx(x_ctx, mods, g_pre3, da_w_in_b, st_diff, i, j, n_diff)
            st_diff = (st_dk, st_dv)
            q_l, k_l, v_l, gate_l = _pre_diff_lat(x_lat, mods, g_pre3, da_w_in_b, cos_t, sin_t, i, j)
            x_ctx = _attn_diff_ctx(q_c, st_dk, st_dv, gate_c, x_ctx, mods, g_post3, w_out_b,
                                   g_sub3, lam_p, i, j)
            x_lat = _attn_diff_lat(q_l, k_l, v_l, cache_dk, cache_dv, gate_l, x_lat, mods, g_post3,
                                   w_out_b, g_sub3, lam_p, i, j)
        else:
            q_c, gate_c, st_ckv, st_kpe = _pre_mla_ctx(
                x_ctx, mods, g_pre3, mla_w_in_b, g_qa3, w_qb_b, g_kva3, st_mla, i, j, n_mla)
            st_mla = (st_ckv, st_kpe)
            q_l, gate_l, ckv_l, kpe_l = _pre_mla_lat(
                x_lat, mods, g_pre3, mla_w_in_b, g_qa3, w_qb_b, g_kva3, cos_t, sin_t, i, j)
            x_ctx = _attn_mla_ctx(q_c, st_ckv, st_kpe, gate_c, x_ctx, mods, g_post3, w_out_b,
                                  w_kvb_b, i, j)
            x_lat = _attn_mla_lat(q_l, ckv_l, kpe_l, cache_mla_ckv, cache_mla_kpe, gate_l, x_lat,
                                  mods, g_post3, w_out_b, w_kvb_b, i, j)

    return (x_ctx.reshape(N_CTX_B, CTX_S, D_MODEL),
            x_lat.reshape(N_LAT_B, LAT_S, D_MODEL),
            st_dk.reshape(N_CTX_B, n_diff, CTX_S, HEADS, 2, DA_DH),
            st_dv.reshape(N_CTX_B, n_diff, CTX_S, HEADS, DA_DV),
            st_ckv,
            st_kpe)
```

```python
import functools
import math

import jax
import jax.numpy as jnp
from jax import lax
from jax.experimental import pallas as pl
from jax.experimental.pallas import tpu as pltpu

F32 = jnp.float32
BF16 = jnp.bfloat16

D_MODEL = 1024
DEPTH = 4
N_CTX_B = 16
CTX_S = 256
N_LAT_B = 4
LAT_S = 1024
PAST = 256
GRID_W = 64
HEADS = 8
DA_DH = 64
DA_DV = 128
MLA_Q_LORA = 384
MLA_KV_LORA = 256
MLA_NOPE = 128
MLA_ROPE = 64
MLA_DV = 128
MLA_QK_PAD = 256
ROPE_THETA = 10000.0
NORM_EPS = 1e-6
COND_ROWS = 8

LANES = 128
V_PAD = 16
TQ = 256
TM = 256
VMEM_LIMIT = 56 * 1024 * 1024

_TRANS_B = (((1,), (1,)), ((), ()))
SCORES_AHEAD = 2
DA_Q_SCALE = DA_DH ** -0.5 * math.log2(math.e)
MLA_Q_SCALE = (MLA_NOPE + MLA_ROPE) ** -0.5 * math.log2(math.e)


def _silu(x):
    return x * (1.0 / (1.0 + jnp.exp(-x)))


def _rms(x, g):
    r = lax.rsqrt(jnp.mean(x * x, axis=-1, keepdims=True) + NORM_EPS)
    return (x * r) * g


def _rope_slab(x, cos, sin_signed):
    lane = lax.broadcasted_iota(jnp.int32, x.shape, 1)
    first_half = (lane % 64) < 32
    partner = jnp.where(first_half, pltpu.roll(x, 96, axis=1), pltpu.roll(x, 32, axis=1))
    return x * cos + partner * sin_signed


def _ones_rows(cols):
    row = lax.broadcasted_iota(jnp.int32, (V_PAD, cols), 0)
    return jnp.where(row == 0, 1.0, 0.0).astype(BF16)


def _cond_row(mods_ref, row):
    m = mods_ref[pl.ds(row, 1), :]
    return m[:, 0:D_MODEL], m[:, D_MODEL:2 * D_MODEL], m[:, 2 * D_MODEL:3 * D_MODEL]


def _modulated(x_ref, mods_ref, g_ref, row):
    shift, scale, _ = _cond_row(mods_ref, row)
    h = _rms(x_ref[...], g_ref[...]) * (1.0 + scale) + shift
    return h.astype(BF16)


def _ada_kernel(cond_ref, w_ref, b_ref, o_ref):
    a = _silu(cond_ref[...]).astype(BF16)
    o_ref[...] = jnp.dot(a, w_ref[...].astype(BF16), preferred_element_type=F32) + b_ref[...]


def _ada_call(cond, w_ada, b_ada):
    tn = 1024
    return pl.pallas_call(
        _ada_kernel,
        grid=(DEPTH, 3 * D_MODEL // tn),
        in_specs=[
            pl.BlockSpec((COND_ROWS, D_MODEL), lambda i, n: (0, 0)),
            pl.BlockSpec((None, D_MODEL, tn), lambda i, n: (i, 0, n)),
            pl.BlockSpec((None, 1, tn), lambda i, n: (i, 0, n)),
        ],
        out_specs=pl.BlockSpec((None, COND_ROWS, tn), lambda i, n: (i, 0, n)),
        out_shape=jax.ShapeDtypeStruct((DEPTH, COND_ROWS, 3 * D_MODEL), F32),
        compiler_params=pltpu.CompilerParams(
            dimension_semantics=("arbitrary", "arbitrary"), vmem_limit_bytes=VMEM_LIMIT),
        name="ada_mod",
    )(cond, w_ada, b_ada.reshape(DEPTH, 1, 3 * D_MODEL))


def _write_state(ref, j, value):
    if j is None:
        ref[...] = value
    else:
        for jj in range(ref.shape[0]):
            ref[jj] = value if jj == j else jnp.zeros_like(value)


def _pre_diff_ctx_kernel(create_j, x_ref, mods_ref, g_ref, w_ref, *refs):
    q_ref, k_ref, vt_ref, gate_ref, skt_ref, sv_ref = refs[-6:]
    hb = _modulated(x_ref, mods_ref, g_ref, 0)
    n = D_MODEL
    q = jnp.dot(hb, w_ref[:, 0:n], preferred_element_type=F32)
    q_ref[...] = (q * DA_Q_SCALE).astype(BF16)
    k = jnp.dot(hb, w_ref[:, n:2 * n], preferred_element_type=F32)
    v = jnp.dot(hb, w_ref[:, 2 * n:3 * n], preferred_element_type=F32)
    vt = v.T
    _write_state(skt_ref, create_j, k.T)
    _write_state(sv_ref, create_j, v.reshape(CTX_S, HEADS, DA_DV))
    k_ref[...] = k.astype(BF16)
    vt_ref[...] = vt.astype(BF16)
    gate_ref[...] = _silu(jnp.dot(hb, w_ref[:, 3 * n:4 * n], preferred_element_type=F32)).astype(BF16)


def _state_plumbing(states, shapes, j, n_inputs, n_outputs):
    def index_map(shp, layer_index):
        return lambda t: (t, layer_index) + (0,) * (len(shp) - 2)

    if states is None:
        specs = [pl.BlockSpec((None,) + shp[1:], index_map(shp, 0)) for shp in shapes]
        return [], specs, {}, ()
    specs = [pl.BlockSpec((None, None) + shp[2:], index_map(shp, j)) for shp in shapes]
    in_specs = [pl.BlockSpec(memory_space=pl.ANY) for _ in shapes]
    aliases = {n_inputs + i: n_outputs + i for i in range(len(shapes))}
    return in_specs, specs, aliases, tuple(states)


def _pre_diff_ctx(x, mods, g_pre, w_in, states, layer, j, n_layers):
    n_tok = N_CTX_B * CTX_S
    shapes = [(N_CTX_B, n_layers, D_MODEL, CTX_S), (N_CTX_B, n_layers, CTX_S, HEADS, DA_DV)]
    st_in_specs, st_out_specs, aliases, st_args = _state_plumbing(states, shapes, j, 4, 4)
    return pl.pallas_call(
        functools.partial(_pre_diff_ctx_kernel, j if states is None else None),
        grid=(N_CTX_B,),
        in_specs=[
            pl.BlockSpec((CTX_S, D_MODEL), lambda t: (t, 0)),
            pl.BlockSpec((None, COND_ROWS, 3 * D_MODEL), lambda t: (layer, 0, 0)),
            pl.BlockSpec((None, 1, D_MODEL), lambda t: (layer, 0, 0)),
            pl.BlockSpec((None, D_MODEL, 4 * D_MODEL), lambda t: (j, 0, 0)),
        ] + st_in_specs,
        out_specs=[
            pl.BlockSpec((CTX_S, D_MODEL), lambda t: (t, 0)),
            pl.BlockSpec((CTX_S, D_MODEL), lambda t: (t, 0)),
            pl.BlockSpec((None, D_MODEL, CTX_S), lambda t: (t, 0, 0)),
            pl.BlockSpec((CTX_S, D_MODEL), lambda t: (t, 0)),
        ] + st_out_specs,
        out_shape=[
            jax.ShapeDtypeStruct((n_tok, D_MODEL), BF16),
            jax.ShapeDtypeStruct((n_tok, D_MODEL), BF16),
            jax.ShapeDtypeStruct((N_CTX_B, D_MODEL, CTX_S), BF16),
            jax.ShapeDtypeStruct((n_tok, D_MODEL), BF16),
        ] + [jax.ShapeDtypeStruct(shp, F32) for shp in shapes],
        input_output_aliases=aliases,
        compiler_params=pltpu.CompilerParams(
            dimension_semantics=("arbitrary",), vmem_limit_bytes=VMEM_LIMIT),
        name="pre_diff_ctx",
    )(x, mods, g_pre, w_in, *st_args)


def _pre_diff_lat_kernel(x_ref, mods_ref, g_ref, w_ref, cos_ref, sin_ref,
                         q_ref, k_ref, vt_ref, gate_ref):
    row = 1 + pl.program_id(0) // (LAT_S // TM)
    hb = _modulated(x_ref, mods_ref, g_ref, row)
    cos = cos_ref[...]
    sin = sin_ref[...]
    n = D_MODEL
    q = jnp.dot(hb, w_ref[:, 0:n], preferred_element_type=F32)
    k = jnp.dot(hb, w_ref[:, n:2 * n], preferred_element_type=F32)
    cos_q = cos * DA_Q_SCALE
    sin_q = sin * DA_Q_SCALE
    slabs = [slice(h * LANES, (h + 1) * LANES) for h in range(HEADS)]
    q_ref[...] = jnp.concatenate(
        [_rope_slab(q[:, sl], cos_q, sin_q).astype(BF16) for sl in slabs], axis=1)
    k_ref[...] = jnp.concatenate(
        [_rope_slab(k[:, sl], cos, sin).astype(BF16) for sl in slabs], axis=1)
    vt_ref[...] = jnp.dot(hb, w_ref[:, 2 * n:3 * n], preferred_element_type=F32).T.astype(BF16)
    gate_ref[...] = _silu(jnp.dot(hb, w_ref[:, 3 * n:4 * n], preferred_element_type=F32)).astype(BF16)


def _pre_diff_lat(x, mods, g_pre, w_in, cos_t, sin_t, layer, j):
    n_tok = N_LAT_B * LAT_S
    tiles_per_req = LAT_S // TM
    tok_spec = pl.BlockSpec((TM, D_MODEL), lambda t: (t, 0))
    rope_spec = pl.BlockSpec((TM, LANES), lambda t: (t % tiles_per_req, 0))
    return pl.pallas_call(
        _pre_diff_lat_kernel,
        grid=(n_tok // TM,),
        in_specs=[
            tok_spec,
            pl.BlockSpec((None, COND_ROWS, 3 * D_MODEL), lambda t: (layer, 0, 0)),
            pl.BlockSpec((None, 1, D_MODEL), lambda t: (layer, 0, 0)),
            pl.BlockSpec((None, D_MODEL, 4 * D_MODEL), lambda t: (j, 0, 0)),
            rope_spec,
            rope_spec,
        ],
        out_specs=[
            tok_spec, tok_spec,
            pl.BlockSpec((None, D_MODEL, TM), lambda t: (t // tiles_per_req, 0, t % tiles_per_req)),
            tok_spec,
        ],
        out_shape=[
            jax.ShapeDtypeStruct((n_tok, D_MODEL), BF16),
            jax.ShapeDtypeStruct((n_tok, D_MODEL), BF16),
            jax.ShapeDtypeStruct((N_LAT_B, D_MODEL, LAT_S), BF16),
            jax.ShapeDtypeStruct((n_tok, D_MODEL), BF16),
        ],
        compiler_params=pltpu.CompilerParams(
            dimension_semantics=("arbitrary",), vmem_limit_bytes=VMEM_LIMIT),
        name="pre_diff_lat",
    )(x, mods, g_pre, w_in, cos_t, sin_t)


def _finish(heads, w_out_ref, x_ref, mods_ref, g_post_ref, row, o_ref):
    _, _, gate = _cond_row(mods_ref, row)
    og = jnp.concatenate(heads, axis=1)
    y = jnp.dot(og, w_out_ref[...], preferred_element_type=F32)
    o_ref[...] = x_ref[...] + gate * _rms(y, g_post_ref[...])


def _diff_lambda(lam_ref, lam_init):
    p = lam_ref[...]
    a = jnp.sum(p[0:1, :] * p[1:2, :], axis=-1, keepdims=True)
    b = jnp.sum(p[2:3, :] * p[3:4, :], axis=-1, keepdims=True)
    return jnp.exp(a) - jnp.exp(b) + lam_init


def _stack_diff_keys(k, k2_ref, sk):
    lane = lax.broadcasted_iota(jnp.int32, (sk, LANES), 1)
    zero = jnp.zeros((sk, LANES), BF16)
    for h in range(HEADS):
        kh = k[:, h * LANES:(h + 1) * LANES]
        k2_ref[h, 0:sk, :] = jnp.where(lane < DA_DH, kh, zero)
        k2_ref[h, sk:2 * sk, :] = jnp.where(lane >= DA_DH, kh, zero)


def _stack_values_t(vt, vt_ref, c0, cols):
    ones = _ones_rows(cols)
    for h in range(HEADS):
        vt_ref[h, 0:DA_DV, c0:c0 + cols] = vt[h * DA_DV:(h + 1) * DA_DV, :]
        vt_ref[h, DA_DV:DA_DV + V_PAD, c0:c0 + cols] = ones


def _diff_heads(q_ref, k2_ref, vt_ref, gate_ref, g_sub_ref, lam, lam_init, sk):
    g_sub = g_sub_ref[...]

    def scores(h):
        return lax.dot_general(k2_ref[h], q_ref[:, h * LANES:(h + 1) * LANES], _TRANS_B,
                               preferred_element_type=F32)

    heads = []
    pending = [scores(h) for h in range(SCORES_AHEAD)]
    for h in range(HEADS):
        lo, hi = h * LANES, (h + 1) * LANES
        s = pending.pop(0)
        if h + SCORES_AHEAD < HEADS:
            pending.append(scores(h + SCORES_AHEAD))
        s0 = s[0:sk, :]
        s1 = s[sk:2 * sk, :]
        e0 = jnp.exp2(s0 - jnp.max(s0, axis=0, keepdims=True)).astype(BF16)
        e1 = jnp.exp2(s1 - jnp.max(s1, axis=0, keepdims=True)).astype(BF16)
        pv0 = jnp.dot(vt_ref[h], e0, preferred_element_type=F32)
        pv1 = jnp.dot(vt_ref[h], e1, preferred_element_type=F32)
        a0 = 1.0 / pv0[DA_DV:DA_DV + 1, :]
        a1 = lam / pv1[DA_DV:DA_DV + 1, :]
        ot = pv0[0:DA_DV, :] * a0 - pv1[0:DA_DV, :] * a1
        ot = ot * lax.rsqrt(jnp.mean(ot * ot, axis=0, keepdims=True) + NORM_EPS)
        o = (ot.T * g_sub) * (1.0 - lam_init)
        heads.append((o * gate_ref[:, lo:hi]).astype(BF16))
    return heads


def _attn_diff_ctx_kernel(lam_init, q_ref, k_ref, vt_ref, gate_ref, x_ref, mods_ref, g_post_ref,
                          w_out_ref, g_sub_ref, lam_ref, o_ref, k2_ref, vb_ref):
    _stack_diff_keys(k_ref[...], k2_ref, CTX_S)
    _stack_values_t(vt_ref[...], vb_ref, 0, CTX_S)
    lam = _diff_lambda(lam_ref, lam_init)
    heads = _diff_heads(q_ref, k2_ref, vb_ref, gate_ref, g_sub_ref, lam, lam_init, CTX_S)
    _finish(heads, w_out_ref, x_ref, mods_ref, g_post_ref, 0, o_ref)


def _attn_diff_ctx(q, k, vt, gate, x, mods, g_post, w_out, g_sub, lam_p, layer, j):
    lam_init = 0.8 - 0.6 * math.exp(-0.3 * layer)
    tok_spec = pl.BlockSpec((CTX_S, D_MODEL), lambda b: (b, 0))
    return pl.pallas_call(
        functools.partial(_attn_diff_ctx_kernel, lam_init),
        grid=(N_CTX_B,),
        in_specs=[
            tok_spec, tok_spec, pl.BlockSpec((None, D_MODEL, CTX_S), lambda b: (b, 0, 0)), tok_spec, tok_spec,
            pl.BlockSpec((None, COND_ROWS, 3 * D_MODEL), lambda b: (layer, 0, 0)),
            pl.BlockSpec((None, 1, D_MODEL), lambda b: (layer, 0, 0)),
            pl.BlockSpec((None, D_MODEL, D_MODEL), lambda b: (layer, 0, 0)),
            pl.BlockSpec((None, 1, DA_DV), lambda b: (j, 0, 0)),
            pl.BlockSpec((None, 4, DA_DH), lambda b: (j, 0, 0)),
        ],
        out_specs=tok_spec,
        out_shape=jax.ShapeDtypeStruct(x.shape, F32),
        scratch_shapes=[
            pltpu.VMEM((HEADS, 2 * CTX_S, LANES), BF16),
            pltpu.VMEM((HEADS, DA_DV + V_PAD, CTX_S), BF16),
        ],
        compiler_params=pltpu.CompilerParams(
            dimension_semantics=("arbitrary",), vmem_limit_bytes=VMEM_LIMIT),
        name="attn_diff_ctx",
    )(q, k, vt, gate, x, mods, g_post, w_out, g_sub, lam_p)


def _attn_diff_lat_kernel(lam_init, q_ref, k_ref, vt_ref, ckt_ref, cv_ref, gate_ref, x_ref, mods_ref,
                          g_post_ref, w_out_ref, g_sub_ref, lam_ref, o_ref, k2_ref, vb_ref):
    sk = LAT_S + PAST

    @pl.when(pl.program_id(1) == 0)
    def _():
        lane = lax.broadcasted_iota(jnp.int32, (PAST, LANES), 1)
        zero_l = jnp.zeros((LAT_S, LANES), BF16)
        zero_p = jnp.zeros((PAST, LANES), BF16)
        lane_l = lax.broadcasted_iota(jnp.int32, (LAT_S, LANES), 1)
        for h in range(HEADS):
            lo, hi = h * LANES, (h + 1) * LANES
            kh = k_ref[:, lo:hi]
            ch = ckt_ref[lo:hi, :].T.astype(BF16)
            k2_ref[h, 0:LAT_S, :] = jnp.where(lane_l < DA_DH, kh, zero_l)
            k2_ref[h, LAT_S:sk, :] = jnp.where(lane < DA_DH, ch, zero_p)
            k2_ref[h, sk:sk + LAT_S, :] = jnp.where(lane_l >= DA_DH, kh, zero_l)
            k2_ref[h, sk + LAT_S:2 * sk, :] = jnp.where(lane >= DA_DH, ch, zero_p)
            vb_ref[h, 0:DA_DV, LAT_S:sk] = cv_ref[:, h, :].T.astype(BF16)
        _stack_values_t(vt_ref[...], vb_ref, 0, LAT_S)
        ones = _ones_rows(PAST)
        for h in range(HEADS):
            vb_ref[h, DA_DV:DA_DV + V_PAD, LAT_S:sk] = ones

    lam = _diff_lambda(lam_ref, lam_init)
    heads = _diff_heads(q_ref, k2_ref, vb_ref, gate_ref, g_sub_ref, lam, lam_init, sk)
    _finish(heads, w_out_ref, x_ref, mods_ref, g_post_ref, 1 + pl.program_id(0), o_ref)


def _attn_diff_lat(q, k, vt, cache_kt, cache_v, gate, x, mods, g_post, w_out, g_sub, lam_p, layer, j):
    lam_init = 0.8 - 0.6 * math.exp(-0.3 * layer)
    sk = LAT_S + PAST
    nq = LAT_S // TQ
    tok_spec = pl.BlockSpec((TQ, D_MODEL), lambda b, i: (b * nq + i, 0))
    kv_spec = pl.BlockSpec((LAT_S, D_MODEL), lambda b, i: (b, 0))
    cache_kt_spec = pl.BlockSpec((None, None, D_MODEL, PAST), lambda b, i: (b, j, 0, 0))
    cache_v_spec = pl.BlockSpec((None, None, PAST, HEADS, DA_DV), lambda b, i: (b, j, 0, 0, 0))
    return pl.pallas_call(
        functools.partial(_attn_diff_lat_kernel, lam_init),
        grid=(N_LAT_B, nq),
        in_specs=[
            tok_spec, kv_spec, pl.BlockSpec((None, D_MODEL, LAT_S), lambda b, i: (b, 0, 0)),
            cache_kt_spec, cache_v_spec, tok_spec, tok_spec,
            pl.BlockSpec((None, COND_ROWS, 3 * D_MODEL), lambda b, i: (layer, 0, 0)),
            pl.BlockSpec((None, 1, D_MODEL), lambda b, i: (layer, 0, 0)),
            pl.BlockSpec((None, D_MODEL, D_MODEL), lambda b, i: (layer, 0, 0)),
            pl.BlockSpec((None, 1, DA_DV), lambda b, i: (j, 0, 0)),
            pl.BlockSpec((None, 4, DA_DH), lambda b, i: (j, 0, 0)),
        ],
        out_specs=tok_spec,
        out_shape=jax.ShapeDtypeStruct(x.shape, F32),
        scratch_shapes=[
            pltpu.VMEM((HEADS, 2 * sk, LANES), BF16),
            pltpu.VMEM((HEADS, DA_DV + V_PAD, sk), BF16),
        ],
        compiler_params=pltpu.CompilerParams(
            dimension_semantics=("arbitrary", "arbitrary"), vmem_limit_bytes=VMEM_LIMIT),
        name="attn_diff_lat",
    )(q, k, vt, cache_kt, cache_v, gate, x, mods, g_post, w_out, g_sub, lam_p)


MLA_IN = MLA_Q_LORA + MLA_KV_LORA + MLA_ROPE + D_MODEL
_QA = slice(0, MLA_Q_LORA)
_KV = slice(MLA_Q_LORA, MLA_Q_LORA + MLA_KV_LORA)
_PE = slice(_KV.stop, _KV.stop + LANES)
_GT = slice(_KV.stop + MLA_ROPE, MLA_IN)


def _proj_t(hb, w_t_ref, rows):
    return lax.dot_general(hb, w_t_ref[rows, :], _TRANS_B, preferred_element_type=F32)


def _mla_queries(hb, w_in_ref, g_qa_ref):
    q_a = _proj_t(hb, w_in_ref, _QA)
    return _rms(q_a, g_qa_ref[...]).astype(BF16)


def _pre_mla_ctx_kernel(create_j, x_ref, mods_ref, g_ref, w_in_ref, g_qa_ref, w_qb_ref, g_kva_ref, *refs):
    q_ref, gate_ref, ckv_ref, kpe_ref = refs[-4:]
    hb = _modulated(x_ref, mods_ref, g_ref, 0)
    qn = _mla_queries(hb, w_in_ref, g_qa_ref)
    q_ref[...] = (jnp.dot(qn, w_qb_ref[...], preferred_element_type=F32) * MLA_Q_SCALE).astype(BF16)
    kv_a = _proj_t(hb, w_in_ref, _KV)
    _write_state(ckv_ref, create_j, _rms(kv_a, g_kva_ref[...]))
    gate_ref[...] = _silu(_proj_t(hb, w_in_ref, _GT)).astype(BF16)
    pe = _proj_t(hb, w_in_ref, _PE)
    _write_state(kpe_ref, create_j, pe[:, 0:MLA_ROPE])


def _pre_mla_ctx(x, mods, g_pre, w_in, g_qa, w_qb, g_kva, states, layer, j, n_layers):
    n_tok = N_CTX_B * CTX_S
    tok = lambda w: pl.BlockSpec((CTX_S, w), lambda t: (t, 0))
    shapes = [(N_CTX_B, n_layers, CTX_S, MLA_KV_LORA), (N_CTX_B, n_layers, CTX_S, MLA_ROPE)]
    st_in_specs, st_out_specs, aliases, st_args = _state_plumbing(states, shapes, j, 7, 2)
    return pl.pallas_call(
        functools.partial(_pre_mla_ctx_kernel, j if states is None else None),
        grid=(N_CTX_B,),
        in_specs=[
            tok(D_MODEL),
            pl.BlockSpec((None, COND_ROWS, 3 * D_MODEL), lambda t: (layer, 0, 0)),
            pl.BlockSpec((None, 1, D_MODEL), lambda t: (layer, 0, 0)),
            pl.BlockSpec((None, MLA_IN, D_MODEL), lambda t: (j, 0, 0)),
            pl.BlockSpec((None, 1, MLA_Q_LORA), lambda t: (j, 0, 0)),
            pl.BlockSpec((None, MLA_Q_LORA, HEADS * MLA_QK_PAD), lambda t: (j, 0, 0)),
            pl.BlockSpec((None, 1, MLA_KV_LORA), lambda t: (j, 0, 0)),
        ] + st_in_specs,
        out_specs=[tok(HEADS * MLA_QK_PAD), tok(D_MODEL)] + st_out_specs,
        out_shape=[
            jax.ShapeDtypeStruct((n_tok, HEADS * MLA_QK_PAD), BF16),
            jax.ShapeDtypeStruct((n_tok, D_MODEL), BF16),
        ] + [jax.ShapeDtypeStruct(shp, F32) for shp in shapes],
        input_output_aliases=aliases,
        compiler_params=pltpu.CompilerParams(
            dimension_semantics=("arbitrary",), vmem_limit_bytes=VMEM_LIMIT),
        name="pre_mla_ctx",
    )(x, mods, g_pre, w_in, g_qa, w_qb, g_kva, *st_args)


def _pre_mla_lat_kernel(x_ref, mods_ref, g_ref, w_in_ref, g_qa_ref, w_qb_ref, g_kva_ref, cos_ref, sin_ref,
                        q_ref, gate_ref, ckv_ref, kpe_ref):
    row = 1 + pl.program_id(0) // (LAT_S // TM)
    hb = _modulated(x_ref, mods_ref, g_ref, row)
    cos = cos_ref[...]
    sin = sin_ref[...]
    qn = _mla_queries(hb, w_in_ref, g_qa_ref)
    for h in range(HEADS):
        lo = h * MLA_QK_PAD
        q = jnp.dot(qn, w_qb_ref[:, lo:lo + MLA_QK_PAD], preferred_element_type=F32)
        q = q * MLA_Q_SCALE
        q_ref[:, lo:lo + LANES] = q[:, 0:LANES].astype(BF16)
        q_ref[:, lo + LANES:lo + 2 * LANES] = _rope_slab(q[:, LANES:2 * LANES], cos, sin).astype(BF16)
    kv_a = _proj_t(hb, w_in_ref, _KV)
    ckv_ref[...] = _rms(kv_a, g_kva_ref[...]).astype(BF16)
    gate_ref[...] = _silu(_proj_t(hb, w_in_ref, _GT)).astype(BF16)
    pe = _proj_t(hb, w_in_ref, _PE)
    kpe_ref[...] = _rope_slab(pe, cos, sin)[:, 0:MLA_ROPE].astype(BF16)


def _pre_mla_lat(x, mods, g_pre, w_in, g_qa, w_qb, g_kva, cos_t, sin_t, layer, j):
    n_tok = N_LAT_B * LAT_S
    tiles_per_req = LAT_S // TM
    tok = lambda w: pl.BlockSpec((TM, w), lambda t: (t, 0))
    rope_spec = pl.BlockSpec((TM, LANES), lambda t: (t % tiles_per_req, 0))
    return pl.pallas_call(
        _pre_mla_lat_kernel,
        grid=(n_tok // TM,),
        in_specs=[
            tok(D_MODEL),
            pl.BlockSpec((None, COND_ROWS, 3 * D_MODEL), lambda t: (layer, 0, 0)),
            pl.BlockSpec((None, 1, D_MODEL), lambda t: (layer, 0, 0)),
            pl.BlockSpec((None, MLA_IN, D_MODEL), lambda t: (j, 0, 0)),
            pl.BlockSpec((None, 1, MLA_Q_LORA), lambda t: (j, 0, 0)),
            pl.BlockSpec((None, MLA_Q_LORA, HEADS * MLA_QK_PAD), lambda t: (j, 0, 0)),
            pl.BlockSpec((None, 1, MLA_KV_LORA), lambda t: (j, 0, 0)),
            rope_spec,
            rope_spec,
        ],
        out_specs=[tok(HEADS * MLA_QK_PAD), tok(D_MODEL), tok(MLA_KV_LORA), tok(MLA_ROPE)],
        out_shape=[
            jax.ShapeDtypeStruct((n_tok, HEADS * MLA_QK_PAD), BF16),
            jax.ShapeDtypeStruct((n_tok, D_MODEL), BF16),
            jax.ShapeDtypeStruct((n_tok, MLA_KV_LORA), BF16),
            jax.ShapeDtypeStruct((n_tok, MLA_ROPE), BF16),
        ],
        compiler_params=pltpu.CompilerParams(
            dimension_semantics=("arbitrary",), vmem_limit_bytes=VMEM_LIMIT),
        name="pre_mla_lat",
    )(x, mods, g_pre, w_in, g_qa, w_qb, g_kva, cos_t, sin_t)


def _mla_expand(ckv, kpe, w_kvb_ref, kf_ref, vf_ref, r0, rows):
    zero = jnp.zeros((rows, LANES - MLA_ROPE), BF16)
    for h in range(HEADS):
        lo = h * (MLA_NOPE + MLA_DV)
        kv = jnp.dot(ckv, w_kvb_ref[:, lo:lo + MLA_NOPE + MLA_DV], preferred_element_type=F32)
        kf_ref[h, r0:r0 + rows, 0:MLA_NOPE] = kv[:, 0:MLA_NOPE].astype(BF16)
        kf_ref[h, r0:r0 + rows, MLA_NOPE:MLA_NOPE + MLA_ROPE] = kpe
        kf_ref[h, r0:r0 + rows, MLA_NOPE + MLA_ROPE:MLA_QK_PAD] = zero
        vf_ref[h, r0:r0 + rows, :] = kv[:, MLA_NOPE:MLA_NOPE + MLA_DV].astype(BF16)


def _mla_heads(q_ref, kf_ref, vf_ref, gate_ref):
    def scores(h):
        return lax.dot_general(q_ref[:, h * MLA_QK_PAD:(h + 1) * MLA_QK_PAD], kf_ref[h], _TRANS_B,
                               preferred_element_type=F32)

    heads = []
    pending = [scores(h) for h in range(SCORES_AHEAD)]
    for h in range(HEADS):
        s = pending.pop(0)
        if h + SCORES_AHEAD < HEADS:
            pending.append(scores(h + SCORES_AHEAD))
        e = jnp.exp2(s - jnp.max(s, axis=-1, keepdims=True))
        inv = 1.0 / jnp.sum(e, axis=-1, keepdims=True)
        o = jnp.dot(e.astype(BF16), vf_ref[h], preferred_element_type=F32) * inv
        lo, hi = h * LANES, (h + 1) * LANES
        heads.append((o * gate_ref[:, lo:hi]).astype(BF16))
    return heads


def _attn_mla_ctx_kernel(q_ref, ckv_ref, kpe_ref, gate_ref, x_ref, mods_ref, g_post_ref, w_out_ref,
                         w_kvb_ref, o_ref, kf_ref, vf_ref):
    _mla_expand(ckv_ref[...].astype(BF16), kpe_ref[...].astype(BF16), w_kvb_ref, kf_ref, vf_ref, 0, CTX_S)
    heads = _mla_heads(q_ref, kf_ref, vf_ref, gate_ref)
    _finish(heads, w_out_ref, x_ref, mods_ref, g_post_ref, 0, o_ref)


def _attn_mla_ctx(q, st_ckv, st_kpe, gate, x, mods, g_post, w_out, w_kvb, layer, j):
    tok = lambda w: pl.BlockSpec((CTX_S, w), lambda b: (b, 0))
    return pl.pallas_call(
        _attn_mla_ctx_kernel,
        grid=(N_CTX_B,),
        in_specs=[
            tok(HEADS * MLA_QK_PAD),
            pl.BlockSpec((None, None, CTX_S, MLA_KV_LORA), lambda b: (b, j, 0, 0)),
            pl.BlockSpec((None, None, CTX_S, MLA_ROPE), lambda b: (b, j, 0, 0)),
            tok(D_MODEL), tok(D_MODEL),
            pl.BlockSpec((None, COND_ROWS, 3 * D_MODEL), lambda b: (layer, 0, 0)),
            pl.BlockSpec((None, 1, D_MODEL), lambda b: (layer, 0, 0)),
            pl.BlockSpec((None, D_MODEL, D_MODEL), lambda b: (layer, 0, 0)),
            pl.BlockSpec((None, MLA_KV_LORA, HEADS * (MLA_NOPE + MLA_DV)), lambda b: (j, 0, 0)),
        ],
        out_specs=tok(D_MODEL),
        out_shape=jax.ShapeDtypeStruct(x.shape, F32),
        scratch_shapes=[
            pltpu.VMEM((HEADS, CTX_S, MLA_QK_PAD), BF16),
            pltpu.VMEM((HEADS, CTX_S, MLA_DV), BF16),
        ],
        compiler_params=pltpu.CompilerParams(
            dimension_semantics=("arbitrary",), vmem_limit_bytes=VMEM_LIMIT),
        name="attn_mla_ctx",
    )(q, st_ckv, st_kpe, gate, x, mods, g_post, w_out, w_kvb)


def _attn_mla_lat_kernel(q_ref, ckv_ref, kpe_ref, cckv_ref, ckpe_ref, gate_ref, x_ref, mods_ref,
                         g_post_ref, w_out_ref, w_kvb_ref, o_ref, kf_ref, vf_ref):
    @pl.when(pl.program_id(1) == 0)
    def _():
        _mla_expand(ckv_ref[...], kpe_ref[...], w_kvb_ref, kf_ref, vf_ref, 0, LAT_S)
        _mla_expand(cckv_ref[...].astype(BF16), ckpe_ref[...].astype(BF16), w_kvb_ref, kf_ref, vf_ref,
                    LAT_S, PAST)

    heads = _mla_heads(q_ref, kf_ref, vf_ref, gate_ref)
    _finish(heads, w_out_ref, x_ref, mods_ref, g_post_ref, 1 + pl.program_id(0), o_ref)


def _attn_mla_lat(q, ckv, kpe, cache_ckv, cache_kpe, gate, x, mods, g_post, w_out, w_kvb, layer, j):
    sk = LAT_S + PAST
    nq = LAT_S // TQ
    tok = lambda w: pl.BlockSpec((TQ, w), lambda b, i: (b * nq + i, 0))
    req = lambda w: pl.BlockSpec((LAT_S, w), lambda b, i: (b, 0))
    return pl.pallas_call(
        _attn_mla_lat_kernel,
        grid=(N_LAT_B, nq),
        in_specs=[
            tok(HEADS * MLA_QK_PAD), req(MLA_KV_LORA), req(MLA_ROPE),
            pl.BlockSpec((None, None, PAST, MLA_KV_LORA), lambda b, i: (b, j, 0, 0)),
            pl.BlockSpec((None, None, PAST, MLA_ROPE), lambda b, i: (b, j, 0, 0)),
            tok(D_MODEL), tok(D_MODEL),
            pl.BlockSpec((None, COND_ROWS, 3 * D_MODEL), lambda b, i: (layer, 0, 0)),
            pl.BlockSpec((None, 1, D_MODEL), lambda b, i: (layer, 0, 0)),
            pl.BlockSpec((None, D_MODEL, D_MODEL), lambda b, i: (layer, 0, 0)),
            pl.BlockSpec((None, MLA_KV_LORA, HEADS * (MLA_NOPE + MLA_DV)), lambda b, i: (j, 0, 0)),
        ],
        out_specs=tok(D_MODEL),
        out_shape=jax.ShapeDtypeStruct(x.shape, F32),
        scratch_shapes=[
            pltpu.VMEM((HEADS, sk, MLA_QK_PAD), BF16),
            pltpu.VMEM((HEADS, sk, MLA_DV), BF16),
        ],
        compiler_params=pltpu.CompilerParams(
            dimension_semantics=("arbitrary", "arbitrary"), vmem_limit_bytes=VMEM_LIMIT),
        name="attn_mla_lat",
    )(q, ckv, kpe, cache_ckv, cache_kpe, gate, x, mods, g_post, w_out, w_kvb)


def _rope_tables():
    rows = LAT_S // GRID_W
    row = jnp.repeat(jnp.arange(rows), GRID_W).astype(F32)
    col = jnp.tile(jnp.arange(GRID_W), rows).astype(F32)
    n_freq = DA_DH // 4
    inv = ROPE_THETA ** (-jnp.arange(n_freq, dtype=F32) / n_freq)
    ang = jnp.concatenate([row[:, None] * inv, col[:, None] * inv], axis=-1)
    cos, sin = jnp.cos(ang), jnp.sin(ang)
    return (jnp.concatenate([cos, cos, cos, cos], axis=-1),
            jnp.concatenate([-sin, sin, -sin, sin], axis=-1))


def kernel(x_prompt, x_sample, cache_diff_k, cache_diff_v, cache_mla_ckv, cache_mla_kpe,
           c, c_ctx, w_ada, b_ada, g_pre, g_post, w_out,
           da_w_in, da_lam_q1, da_lam_k1, da_lam_q2, da_lam_k2, da_g_sub,
           mla_w_in, mla_g_qa, mla_w_qb, mla_g_kva, mla_w_kvb):
    assert DA_DH == MLA_ROPE
    n_diff = (DEPTH + 1) // 2
    n_mla = DEPTH // 2

    cond = jnp.concatenate(
        [c_ctx[None, :], c, jnp.zeros((COND_ROWS - 1 - N_LAT_B, D_MODEL), F32)], axis=0)
    mods = _ada_call(cond, w_ada, b_ada)

    cos_t, sin_t = _rope_tables()
    g_pre3 = g_pre.reshape(DEPTH, 1, D_MODEL)
    g_post3 = g_post.reshape(DEPTH, 1, D_MODEL)
    w_out_b = w_out.astype(BF16)

    da_w_in_b = da_w_in.astype(BF16)
    g_sub3 = da_g_sub.reshape(n_diff, 1, DA_DV)
    lam_p = jnp.stack([da_lam_q1, da_lam_k1, da_lam_q2, da_lam_k2], axis=1)
    cache_dkt = jnp.transpose(cache_diff_k, (0, 1, 3, 4, 5, 2)).reshape(N_LAT_B, n_diff, D_MODEL, PAST)

    mla_w_in_b = jnp.swapaxes(mla_w_in, 1, 2).astype(BF16)
    w_qb4 = mla_w_qb.reshape(n_mla, MLA_Q_LORA, HEADS, MLA_NOPE + MLA_ROPE)
    w_qb_b = jnp.pad(w_qb4, ((0, 0), (0, 0), (0, 0), (0, MLA_QK_PAD - MLA_NOPE - MLA_ROPE))).reshape(
        n_mla, MLA_Q_LORA, HEADS * MLA_QK_PAD).astype(BF16)
    w_kvb_b = mla_w_kvb.astype(BF16)
    g_qa3 = mla_g_qa.reshape(n_mla, 1, MLA_Q_LORA)
    g_kva3 = mla_g_kva.reshape(n_mla, 1, MLA_KV_LORA)

    x_ctx = x_prompt.reshape(N_CTX_B * CTX_S, D_MODEL)
    x_lat = x_sample.reshape(N_LAT_B * LAT_S, D_MODEL)
    st_diff = None
    st_mla = None

    for i in range(DEPTH):
        j = i // 2
        if i % 2 == 0:
            q_c, k_c, vt_c, gate_c, st_dkt, st_dv = _pre_diff_ctx(
                x_ctx, mods, g_pre3, da_w_in_b, st_diff, i, j, n_diff)
            st_diff = (st_dkt, st_dv)
            q_l, k_l, vt_l, gate_l = _pre_diff_lat(x_lat, mods, g_pre3, da_w_in_b, cos_t, sin_t, i, j)
            x_ctx = _attn_diff_ctx(q_c, k_c, vt_c, gate_c, x_ctx, mods, g_post3, w_out_b,
                                   g_sub3, lam_p, i, j)
            x_lat = _attn_diff_lat(q_l, k_l, vt_l, cache_dkt, cache_diff_v, gate_l, x_lat, mods, g_post3,
                                   w_out_b, g_sub3, lam_p, i, j)
        else:
            q_c, gate_c, st_ckv, st_kpe = _pre_mla_ctx(
                x_ctx, mods, g_pre3, mla_w_in_b, g_qa3, w_qb_b, g_kva3, st_mla, i, j, n_mla)
            st_mla = (st_ckv, st_kpe)
            q_l, gate_l, ckv_l, kpe_l = _pre_mla_lat(
                x_lat, mods, g_pre3, mla_w_in_b, g_qa3, w_qb_b, g_kva3, cos_t, sin_t, i, j)
            x_ctx = _attn_mla_ctx(q_c, st_ckv, st_kpe, gate_c, x_ctx, mods, g_post3, w_out_b,
                                  w_kvb_b, i, j)
            x_lat = _attn_mla_lat(q_l, ckv_l, kpe_l, cache_mla_ckv, cache_mla_kpe, gate_l, x_lat,
                                  mods, g_post3, w_out_b, w_kvb_b, i, j)

    return (x_ctx.reshape(N_CTX_B, CTX_S, D_MODEL),
            x_lat.reshape(N_LAT_B, LAT_S, D_MODEL),
            jnp.transpose(st_dkt.reshape(N_CTX_B, n_diff, HEADS, 2, DA_DH, CTX_S), (0, 1, 5, 2, 3, 4)),
            st_dv,
            st_ckv,
            st_kpe)
```

```python
import functools
import math

import jax
import jax.numpy as jnp
from jax import lax
from jax.experimental import pallas as pl
from jax.experimental.pallas import tpu as pltpu

F32 = jnp.float32
BF16 = jnp.bfloat16

D_MODEL = 1024
DEPTH = 4
N_CTX_B = 16
CTX_S = 256
N_LAT_B = 4
LAT_S = 1024
PAST = 256
GRID_W = 64
HEADS = 8
DA_DH = 64
DA_DV = 128
MLA_Q_LORA = 384
MLA_KV_LORA = 256
MLA_NOPE = 128
MLA_ROPE = 64
MLA_DV = 128
MLA_QK_PAD = 256
ROPE_THETA = 10000.0
NORM_EPS = 1e-6
COND_ROWS = 8

LANES = 128
V_PAD = 16
TQ = 256
TM = 512
VMEM_LIMIT = 56 * 1024 * 1024

_TRANS_B = (((1,), (1,)), ((), ()))
SCORES_AHEAD = 2
CTX_PER_STEP = 2
DA_Q_SCALE = DA_DH ** -0.5 * math.log2(math.e)
MLA_Q_SCALE = (MLA_NOPE + MLA_ROPE) ** -0.5 * math.log2(math.e)


def _silu(x):
    return x * (1.0 / (1.0 + jnp.exp(-x)))


def _rms(x, g):
    r = lax.rsqrt(jnp.mean(x * x, axis=-1, keepdims=True) + NORM_EPS)
    return (x * r) * g


def _rope_slab(x, cos, sin_signed):
    lane = lax.broadcasted_iota(jnp.int32, x.shape, 1)
    first_half = (lane % 64) < 32
    partner = jnp.where(first_half, pltpu.roll(x, 96, axis=1), pltpu.roll(x, 32, axis=1))
    return x * cos + partner * sin_signed


def _ones_rows(cols):
    row = lax.broadcasted_iota(jnp.int32, (V_PAD, cols), 0)
    return jnp.where(row == 0, 1.0, 0.0).astype(BF16)


def _cond_row(mods_ref, row):
    m = mods_ref[pl.ds(row, 1), :]
    return m[:, 0:D_MODEL], m[:, D_MODEL:2 * D_MODEL], m[:, 2 * D_MODEL:3 * D_MODEL]


def _modulated(x_ref, mods_ref, g_ref, row):
    shift, scale, _ = _cond_row(mods_ref, row)
    h = _rms(x_ref[...], g_ref[...]) * (1.0 + scale) + shift
    return h.astype(BF16)


def _ada_kernel(cond_ref, w_ref, b_ref, o_ref):
    a = _silu(cond_ref[...]).astype(BF16)
    o_ref[...] = jnp.dot(a, w_ref[...].astype(BF16), preferred_element_type=F32) + b_ref[...]


def _ada_call(cond, w_ada, b_ada):
    tn = 3 * D_MODEL
    return pl.pallas_call(
        _ada_kernel,
        grid=(DEPTH, 3 * D_MODEL // tn),
        in_specs=[
            pl.BlockSpec((COND_ROWS, D_MODEL), lambda i, n: (0, 0)),
            pl.BlockSpec((None, D_MODEL, tn), lambda i, n: (i, 0, n)),
            pl.BlockSpec((None, 1, tn), lambda i, n: (i, 0, n)),
        ],
        out_specs=pl.BlockSpec((None, COND_ROWS, tn), lambda i, n: (i, 0, n)),
        out_shape=jax.ShapeDtypeStruct((DEPTH, COND_ROWS, 3 * D_MODEL), F32),
        compiler_params=pltpu.CompilerParams(
            dimension_semantics=("arbitrary", "arbitrary"), vmem_limit_bytes=VMEM_LIMIT),
        name="ada_mod",
    )(cond, w_ada, b_ada.reshape(DEPTH, 1, 3 * D_MODEL))


def _write_state(ref, j, value):
    if j is None:
        ref[...] = value
    else:
        for jj in range(ref.shape[0]):
            ref[jj] = value if jj == j else jnp.zeros_like(value)


def _pre_diff_ctx_kernel(create_j, x_ref, mods_ref, g_ref, w_ref, *refs):
    q_ref, k_ref, vt_ref, gate_ref, skt_ref, sv_ref = refs[-6:]
    hb = _modulated(x_ref, mods_ref, g_ref, 0)
    n = D_MODEL
    q = jnp.dot(hb, w_ref[:, 0:n], preferred_element_type=F32)
    q_ref[...] = (q * DA_Q_SCALE).astype(BF16)
    k = jnp.dot(hb, w_ref[:, n:2 * n], preferred_element_type=F32)
    v = jnp.dot(hb, w_ref[:, 2 * n:3 * n], preferred_element_type=F32)
    vt = v.T
    _write_state(skt_ref, create_j, k.T)
    _write_state(sv_ref, create_j, v.reshape(CTX_S, HEADS, DA_DV))
    k_ref[...] = k.astype(BF16)
    vt_ref[...] = vt.astype(BF16)
    gate_ref[...] = _silu(jnp.dot(hb, w_ref[:, 3 * n:4 * n], preferred_element_type=F32)).astype(BF16)


def _state_plumbing(states, shapes, j, n_inputs, n_outputs):
    def index_map(shp, layer_index):
        return lambda t: (t, layer_index) + (0,) * (len(shp) - 2)

    if states is None:
        specs = [pl.BlockSpec((None,) + shp[1:], index_map(shp, 0)) for shp in shapes]
        return [], specs, {}, ()
    specs = [pl.BlockSpec((None, None) + shp[2:], index_map(shp, j)) for shp in shapes]
    in_specs = [pl.BlockSpec(memory_space=pl.ANY) for _ in shapes]
    aliases = {n_inputs + i: n_outputs + i for i in range(len(shapes))}
    return in_specs, specs, aliases, tuple(states)


def _pre_diff_ctx(x, mods, g_pre, w_in, states, layer, j, n_layers):
    n_tok = N_CTX_B * CTX_S
    shapes = [(N_CTX_B, n_layers, D_MODEL, CTX_S), (N_CTX_B, n_layers, CTX_S, HEADS, DA_DV)]
    st_in_specs, st_out_specs, aliases, st_args = _state_plumbing(states, shapes, j, 4, 4)
    return pl.pallas_call(
        functools.partial(_pre_diff_ctx_kernel, j if states is None else None),
        grid=(N_CTX_B,),
        in_specs=[
            pl.BlockSpec((CTX_S, D_MODEL), lambda t: (t, 0)),
            pl.BlockSpec((None, COND_ROWS, 3 * D_MODEL), lambda t: (layer, 0, 0)),
            pl.BlockSpec((None, 1, D_MODEL), lambda t: (layer, 0, 0)),
            pl.BlockSpec((None, D_MODEL, 4 * D_MODEL), lambda t: (j, 0, 0)),
        ] + st_in_specs,
        out_specs=[
            pl.BlockSpec((CTX_S, D_MODEL), lambda t: (t, 0)),
            pl.BlockSpec((CTX_S, D_MODEL), lambda t: (t, 0)),
            pl.BlockSpec((None, D_MODEL, CTX_S), lambda t: (t, 0, 0)),
            pl.BlockSpec((CTX_S, D_MODEL), lambda t: (t, 0)),
        ] + st_out_specs,
        out_shape=[
            jax.ShapeDtypeStruct((n_tok, D_MODEL), BF16),
            jax.ShapeDtypeStruct((n_tok, D_MODEL), BF16),
            jax.ShapeDtypeStruct((N_CTX_B, D_MODEL, CTX_S), BF16),
            jax.ShapeDtypeStruct((n_tok, D_MODEL), BF16),
        ] + [jax.ShapeDtypeStruct(shp, F32) for shp in shapes],
        input_output_aliases=aliases,
        compiler_params=pltpu.CompilerParams(
            dimension_semantics=("arbitrary",), vmem_limit_bytes=VMEM_LIMIT),
        name="pre_diff_ctx",
    )(x, mods, g_pre, w_in, *st_args)


def _pre_diff_lat_kernel(x_ref, mods_ref, g_ref, w_ref, cos_ref, sin_ref,
                         q_ref, k_ref, vt_ref, gate_ref):
    row = 1 + pl.program_id(0) // (LAT_S // TM)
    hb = _modulated(x_ref, mods_ref, g_ref, row)
    cos = cos_ref[...]
    sin = sin_ref[...]
    n = D_MODEL
    q = jnp.dot(hb, w_ref[:, 0:n], preferred_element_type=F32)
    k = jnp.dot(hb, w_ref[:, n:2 * n], preferred_element_type=F32)
    cos_q = cos * DA_Q_SCALE
    sin_q = sin * DA_Q_SCALE
    slabs = [slice(h * LANES, (h + 1) * LANES) for h in range(HEADS)]
    q_ref[...] = jnp.concatenate(
        [_rope_slab(q[:, sl], cos_q, sin_q).astype(BF16) for sl in slabs], axis=1)
    k_ref[...] = jnp.concatenate(
        [_rope_slab(k[:, sl], cos, sin).astype(BF16) for sl in slabs], axis=1)
    vt_ref[...] = jnp.dot(hb, w_ref[:, 2 * n:3 * n], preferred_element_type=F32).T.astype(BF16)
    gate_ref[...] = _silu(jnp.dot(hb, w_ref[:, 3 * n:4 * n], preferred_element_type=F32)).astype(BF16)


def _pre_diff_lat(x, mods, g_pre, w_in, cos_t, sin_t, layer, j):
    n_tok = N_LAT_B * LAT_S
    tiles_per_req = LAT_S // TM
    tok_spec = pl.BlockSpec((TM, D_MODEL), lambda t: (t, 0))
    rope_spec = pl.BlockSpec((TM, LANES), lambda t: (t % tiles_per_req, 0))
    return pl.pallas_call(
        _pre_diff_lat_kernel,
        grid=(n_tok // TM,),
        in_specs=[
            tok_spec,
            pl.BlockSpec((None, COND_ROWS, 3 * D_MODEL), lambda t: (layer, 0, 0)),
            pl.BlockSpec((None, 1, D_MODEL), lambda t: (layer, 0, 0)),
            pl.BlockSpec((None, D_MODEL, 4 * D_MODEL), lambda t: (j, 0, 0)),
            rope_spec,
            rope_spec,
        ],
        out_specs=[
            tok_spec, tok_spec,
            pl.BlockSpec((None, D_MODEL, TM), lambda t: (t // tiles_per_req, 0, t % tiles_per_req)),
            tok_spec,
        ],
        out_shape=[
            jax.ShapeDtypeStruct((n_tok, D_MODEL), BF16),
            jax.ShapeDtypeStruct((n_tok, D_MODEL), BF16),
            jax.ShapeDtypeStruct((N_LAT_B, D_MODEL, LAT_S), BF16),
            jax.ShapeDtypeStruct((n_tok, D_MODEL), BF16),
        ],
        compiler_params=pltpu.CompilerParams(
            dimension_semantics=("arbitrary",), vmem_limit_bytes=VMEM_LIMIT),
        name="pre_diff_lat",
    )(x, mods, g_pre, w_in, cos_t, sin_t)


def _finish(heads, w_out_ref, x_ref, mods_ref, g_post_ref, row, o_ref, rows=slice(None)):
    _, _, gate = _cond_row(mods_ref, row)
    og = jnp.concatenate(heads, axis=1)
    y = jnp.dot(og, w_out_ref[...], preferred_element_type=F32)
    o_ref[rows, :] = x_ref[rows, :] + gate * _rms(y, g_post_ref[...])


def _diff_lambda(lam_ref, lam_init):
    p = lam_ref[...]
    a = jnp.sum(p[0:1, :] * p[1:2, :], axis=-1, keepdims=True)
    b = jnp.sum(p[2:3, :] * p[3:4, :], axis=-1, keepdims=True)
    return jnp.exp(a) - jnp.exp(b) + lam_init


def _stack_diff_keys(k, k2_ref, sk):
    lane = lax.broadcasted_iota(jnp.int32, (sk, LANES), 1)
    zero = jnp.zeros((sk, LANES), BF16)
    for h in range(HEADS):
        kh = k[:, h * LANES:(h + 1) * LANES]
        k2_ref[h, 0:sk, :] = jnp.where(lane < DA_DH, kh, zero)
        k2_ref[h, sk:2 * sk, :] = jnp.where(lane >= DA_DH, kh, zero)


def _stack_values_t(vt, vt_ref, c0, cols):
    ones = _ones_rows(cols)
    for h in range(HEADS):
        vt_ref[h, 0:DA_DV, c0:c0 + cols] = vt[h * DA_DV:(h + 1) * DA_DV, :]
        vt_ref[h, DA_DV:DA_DV + V_PAD, c0:c0 + cols] = ones


def _diff_heads(q_ref, k2_ref, vt_ref, gate_ref, g_sub_ref, lam, lam_init, sk, rows=slice(None)):
    g_sub = g_sub_ref[...]

    def scores(h):
        return lax.dot_general(k2_ref[h], q_ref[rows, h * LANES:(h + 1) * LANES], _TRANS_B,
                               preferred_element_type=F32)

    heads = []
    pending = [scores(h) for h in range(SCORES_AHEAD)]
    for h in range(HEADS):
        lo, hi = h * LANES, (h + 1) * LANES
        s = pending.pop(0)
        if h + SCORES_AHEAD < HEADS:
            pending.append(scores(h + SCORES_AHEAD))
        s0 = s[0:sk, :]
        s1 = s[sk:2 * sk, :]
        e0 = jnp.exp2(s0 - jnp.max(s0, axis=0, keepdims=True)).astype(BF16)
        e1 = jnp.exp2(s1 - jnp.max(s1, axis=0, keepdims=True)).astype(BF16)
        pv0 = jnp.dot(vt_ref[h], e0, preferred_element_type=F32)
        pv1 = jnp.dot(vt_ref[h], e1, preferred_element_type=F32)
        a0 = 1.0 / pv0[DA_DV:DA_DV + 1, :]
        a1 = lam / pv1[DA_DV:DA_DV + 1, :]
        ot = pv0[0:DA_DV, :] * a0 - pv1[0:DA_DV, :] * a1
        ot = ot * lax.rsqrt(jnp.mean(ot * ot, axis=0, keepdims=True) + NORM_EPS)
        o = (ot.T * g_sub) * (1.0 - lam_init)
        heads.append((o * gate_ref[rows, lo:hi]).astype(BF16))
    return heads


def _attn_diff_ctx_kernel(lam_init, q_ref, k_ref, vt_ref, gate_ref, x_ref, mods_ref, g_post_ref,
                          w_out_ref, g_sub_ref, lam_ref, o_ref, k2_ref, vb_ref):
    lam = _diff_lambda(lam_ref, lam_init)
    for r in range(CTX_PER_STEP):
        rows = slice(r * CTX_S, (r + 1) * CTX_S)
        _stack_diff_keys(k_ref[rows, :], k2_ref.at[r], CTX_S)
        _stack_values_t(vt_ref[r], vb_ref.at[r], 0, CTX_S)
        heads = _diff_heads(q_ref, k2_ref.at[r], vb_ref.at[r], gate_ref, g_sub_ref, lam, lam_init,
                            CTX_S, rows)
        _finish(heads, w_out_ref, x_ref, mods_ref, g_post_ref, 0, o_ref, rows)


def _attn_diff_ctx(q, k, vt, gate, x, mods, g_post, w_out, g_sub, lam_p, layer, j):
    lam_init = 0.8 - 0.6 * math.exp(-0.3 * layer)
    tok_spec = pl.BlockSpec((CTX_PER_STEP * CTX_S, D_MODEL), lambda b: (b, 0))
    return pl.pallas_call(
        functools.partial(_attn_diff_ctx_kernel, lam_init),
        grid=(N_CTX_B // CTX_PER_STEP,),
        in_specs=[
            tok_spec, tok_spec, pl.BlockSpec((CTX_PER_STEP, D_MODEL, CTX_S), lambda b: (b, 0, 0)),
            tok_spec, tok_spec,
            pl.BlockSpec((None, COND_ROWS, 3 * D_MODEL), lambda b: (layer, 0, 0)),
            pl.BlockSpec((None, 1, D_MODEL), lambda b: (layer, 0, 0)),
            pl.BlockSpec((None, D_MODEL, D_MODEL), lambda b: (layer, 0, 0)),
            pl.BlockSpec((None, 1, DA_DV), lambda b: (j, 0, 0)),
            pl.BlockSpec((None, 4, DA_DH), lambda b: (j, 0, 0)),
        ],
        out_specs=tok_spec,
        out_shape=jax.ShapeDtypeStruct(x.shape, F32),
        scratch_shapes=[
            pltpu.VMEM((CTX_PER_STEP, HEADS, 2 * CTX_S, LANES), BF16),
            pltpu.VMEM((CTX_PER_STEP, HEADS, DA_DV + V_PAD, CTX_S), BF16),
        ],
        compiler_params=pltpu.CompilerParams(
            dimension_semantics=("arbitrary",), vmem_limit_bytes=VMEM_LIMIT),
        name="attn_diff_ctx",
    )(q, k, vt, gate, x, mods, g_post, w_out, g_sub, lam_p)


def _attn_diff_lat_kernel(lam_init, q_ref, k_ref, vt_ref, ckt_ref, cv_ref, gate_ref, x_ref, mods_ref,
                          g_post_ref, w_out_ref, g_sub_ref, lam_ref, o_ref, k2_ref, vb_ref):
    sk = LAT_S + PAST

    @pl.when(pl.program_id(1) == 0)
    def _():
        lane = lax.broadcasted_iota(jnp.int32, (PAST, LANES), 1)
        zero_l = jnp.zeros((LAT_S, LANES), BF16)
        zero_p = jnp.zeros((PAST, LANES), BF16)
        lane_l = lax.broadcasted_iota(jnp.int32, (LAT_S, LANES), 1)
        for h in range(HEADS):
            lo, hi = h * LANES, (h + 1) * LANES
            kh = k_ref[:, lo:hi]
            ch = ckt_ref[lo:hi, :].T.astype(BF16)
            k2_ref[h, 0:LAT_S, :] = jnp.where(lane_l < DA_DH, kh, zero_l)
            k2_ref[h, LAT_S:sk, :] = jnp.where(lane < DA_DH, ch, zero_p)
            k2_ref[h, sk:sk + LAT_S, :] = jnp.where(lane_l >= DA_DH, kh, zero_l)
            k2_ref[h, sk + LAT_S:2 * sk, :] = jnp.where(lane >= DA_DH, ch, zero_p)
            vb_ref[h, 0:DA_DV, LAT_S:sk] = cv_ref[:, h, :].T.astype(BF16)
        _stack_values_t(vt_ref[...], vb_ref, 0, LAT_S)
        ones = _ones_rows(PAST)
        for h in range(HEADS):
            vb_ref[h, DA_DV:DA_DV + V_PAD, LAT_S:sk] = ones

    lam = _diff_lambda(lam_ref, lam_init)
    heads = _diff_heads(q_ref, k2_ref, vb_ref, gate_ref, g_sub_ref, lam, lam_init, sk)
    _finish(heads, w_out_ref, x_ref, mods_ref, g_post_ref, 1 + pl.program_id(0), o_ref)


def _attn_diff_lat(q, k, vt, cache_kt, cache_v, gate, x, mods, g_post, w_out, g_sub, lam_p, layer, j):
    lam_init = 0.8 - 0.6 * math.exp(-0.3 * layer)
    sk = LAT_S + PAST
    nq = LAT_S // TQ
    tok_spec = pl.BlockSpec((TQ, D_MODEL), lambda b, i: (b * nq + i, 0))
    kv_spec = pl.BlockSpec((LAT_S, D_MODEL), lambda b, i: (b, 0))
    cache_kt_spec = pl.BlockSpec((None, None, D_MODEL, PAST), lambda b, i: (b, j, 0, 0))
    cache_v_spec = pl.BlockSpec((None, None, PAST, HEADS, DA_DV), lambda b, i: (b, j, 0, 0, 0))
    return pl.pallas_call(
        functools.partial(_attn_diff_lat_kernel, lam_init),
        grid=(N_LAT_B, nq),
        in_specs=[
            tok_spec, kv_spec, pl.BlockSpec((None, D_MODEL, LAT_S), lambda b, i: (b, 0, 0)),
            cache_kt_spec, cache_v_spec, tok_spec, tok_spec,
            pl.BlockSpec((None, COND_ROWS, 3 * D_MODEL), lambda b, i: (layer, 0, 0)),
            pl.BlockSpec((None, 1, D_MODEL), lambda b, i: (layer, 0, 0)),
            pl.BlockSpec((None, D_MODEL, D_MODEL), lambda b, i: (layer, 0, 0)),
            pl.BlockSpec((None, 1, DA_DV), lambda b, i: (j, 0, 0)),
            pl.BlockSpec((None, 4, DA_DH), lambda b, i: (j, 0, 0)),
        ],
        out_specs=tok_spec,
        out_shape=jax.ShapeDtypeStruct(x.shape, F32),
        scratch_shapes=[
            pltpu.VMEM((HEADS, 2 * sk, LANES), BF16),
            pltpu.VMEM((HEADS, DA_DV + V_PAD, sk), BF16),
        ],
        compiler_params=pltpu.CompilerParams(
            dimension_semantics=("arbitrary", "arbitrary"), vmem_limit_bytes=VMEM_LIMIT),
        name="attn_diff_lat",
    )(q, k, vt, cache_kt, cache_v, gate, x, mods, g_post, w_out, g_sub, lam_p)


MLA_IN = MLA_Q_LORA + MLA_KV_LORA + MLA_ROPE + D_MODEL
_QA = slice(0, MLA_Q_LORA)
_KV = slice(MLA_Q_LORA, MLA_Q_LORA + MLA_KV_LORA)
_PE = slice(_KV.stop, _KV.stop + LANES)
_GT = slice(_KV.stop + MLA_ROPE, MLA_IN)


def _proj_t(hb, w_t_ref, rows):
    return lax.dot_general(hb, w_t_ref[rows, :], _TRANS_B, preferred_element_type=F32)


def _mla_queries(hb, w_in_ref, g_qa_ref):
    q_a = _proj_t(hb, w_in_ref, _QA)
    return _rms(q_a, g_qa_ref[...]).astype(BF16)


def _pre_mla_ctx_kernel(create_j, x_ref, mods_ref, g_ref, w_in_ref, g_qa_ref, w_qb_ref, g_kva_ref, *refs):
    q_ref, gate_ref, ckv_ref, kpe_ref = refs[-4:]
    hb = _modulated(x_ref, mods_ref, g_ref, 0)
    qn = _mla_queries(hb, w_in_ref, g_qa_ref)
    q_ref[...] = (jnp.dot(qn, w_qb_ref[...], preferred_element_type=F32) * MLA_Q_SCALE).astype(BF16)
    kv_a = _proj_t(hb, w_in_ref, _KV)
    _write_state(ckv_ref, create_j, _rms(kv_a, g_kva_ref[...]))
    gate_ref[...] = _silu(_proj_t(hb, w_in_ref, _GT)).astype(BF16)
    pe = _proj_t(hb, w_in_ref, _PE)
    _write_state(kpe_ref, create_j, pe[:, 0:MLA_ROPE])


def _pre_mla_ctx(x, mods, g_pre, w_in, g_qa, w_qb, g_kva, states, layer, j, n_layers):
    n_tok = N_CTX_B * CTX_S
    tok = lambda w: pl.BlockSpec((CTX_S, w), lambda t: (t, 0))
    shapes = [(N_CTX_B, n_layers, CTX_S, MLA_KV_LORA), (N_CTX_B, n_layers, CTX_S, MLA_ROPE)]
    st_in_specs, st_out_specs, aliases, st_args = _state_plumbing(states, shapes, j, 7, 2)
    return pl.pallas_call(
        functools.partial(_pre_mla_ctx_kernel, j if states is None else None),
        grid=(N_CTX_B,),
        in_specs=[
            tok(D_MODEL),
            pl.BlockSpec((None, COND_ROWS, 3 * D_MODEL), lambda t: (layer, 0, 0)),
            pl.BlockSpec((None, 1, D_MODEL), lambda t: (layer, 0, 0)),
            pl.BlockSpec((None, MLA_IN, D_MODEL), lambda t: (j, 0, 0)),
            pl.BlockSpec((None, 1, MLA_Q_LORA), lambda t: (j, 0, 0)),
            pl.BlockSpec((None, MLA_Q_LORA, HEADS * MLA_QK_PAD), lambda t: (j, 0, 0)),
            pl.BlockSpec((None, 1, MLA_KV_LORA), lambda t: (j, 0, 0)),
        ] + st_in_specs,
        out_specs=[tok(HEADS * MLA_QK_PAD), tok(D_MODEL)] + st_out_specs,
        out_shape=[
            jax.ShapeDtypeStruct((n_tok, HEADS * MLA_QK_PAD), BF16),
            jax.ShapeDtypeStruct((n_tok, D_MODEL), BF16),
        ] + [jax.ShapeDtypeStruct(shp, F32) for shp in shapes],
        input_output_aliases=aliases,
        compiler_params=pltpu.CompilerParams(
            dimension_semantics=("arbitrary",), vmem_limit_bytes=VMEM_LIMIT),
        name="pre_mla_ctx",
    )(x, mods, g_pre, w_in, g_qa, w_qb, g_kva, *st_args)


def _pre_mla_lat_kernel(x_ref, mods_ref, g_ref, w_in_ref, g_qa_ref, w_qb_ref, g_kva_ref, cos_ref, sin_ref,
                        q_ref, gate_ref, ckv_ref, kpe_ref):
    row = 1 + pl.program_id(0) // (LAT_S // TM)
    hb = _modulated(x_ref, mods_ref, g_ref, row)
    cos = cos_ref[...]
    sin = sin_ref[...]
    qn = _mla_queries(hb, w_in_ref, g_qa_ref)
    for h in range(HEADS):
        lo = h * MLA_QK_PAD
        q = jnp.dot(qn, w_qb_ref[:, lo:lo + MLA_QK_PAD], preferred_element_type=F32)
        q = q * MLA_Q_SCALE
        q_ref[:, lo:lo + LANES] = q[:, 0:LANES].astype(BF16)
        q_ref[:, lo + LANES:lo + 2 * LANES] = _rope_slab(q[:, LANES:2 * LANES], cos, sin).astype(BF16)
    kv_a = _proj_t(hb, w_in_ref, _KV)
    ckv_ref[...] = _rms(kv_a, g_kva_ref[...]).astype(BF16)
    gate_ref[...] = _silu(_proj_t(hb, w_in_ref, _GT)).astype(BF16)
    pe = _proj_t(hb, w_in_ref, _PE)
    kpe_ref[...] = _rope_slab(pe, cos, sin)[:, 0:MLA_ROPE].astype(BF16)


def _pre_mla_lat(x, mods, g_pre, w_in, g_qa, w_qb, g_kva, cos_t, sin_t, layer, j):
    n_tok = N_LAT_B * LAT_S
    tiles_per_req = LAT_S // TM
    tok = lambda w: pl.BlockSpec((TM, w), lambda t: (t, 0))
    rope_spec = pl.BlockSpec((TM, LANES), lambda t: (t % tiles_per_req, 0))
    return pl.pallas_call(
        _pre_mla_lat_kernel,
        grid=(n_tok // TM,),
        in_specs=[
            tok(D_MODEL),
            pl.BlockSpec((None, COND_ROWS, 3 * D_MODEL), lambda t: (layer, 0, 0)),
            pl.BlockSpec((None, 1, D_MODEL), lambda t: (layer, 0, 0)),
            pl.BlockSpec((None, MLA_IN, D_MODEL), lambda t: (j, 0, 0)),
            pl.BlockSpec((None, 1, MLA_Q_LORA), lambda t: (j, 0, 0)),
            pl.BlockSpec((None, MLA_Q_LORA, HEADS * MLA_QK_PAD), lambda t: (j, 0, 0)),
            pl.BlockSpec((None, 1, MLA_KV_LORA), lambda t: (j, 0, 0)),
            rope_spec,
            rope_spec,
        ],
        out_specs=[tok(HEADS * MLA_QK_PAD), tok(D_MODEL), tok(MLA_KV_LORA), tok(MLA_ROPE)],
        out_shape=[
            jax.ShapeDtypeStruct((n_tok, HEADS * MLA_QK_PAD), BF16),
            jax.ShapeDtypeStruct((n_tok, D_MODEL), BF16),
            jax.ShapeDtypeStruct((n_tok, MLA_KV_LORA), BF16),
            jax.ShapeDtypeStruct((n_tok, MLA_ROPE), BF16),
        ],
        compiler_params=pltpu.CompilerParams(
            dimension_semantics=("arbitrary",), vmem_limit_bytes=VMEM_LIMIT),
        name="pre_mla_lat",
    )(x, mods, g_pre, w_in, g_qa, w_qb, g_kva, cos_t, sin_t)


def _mla_expand(ckv, kpe, w_kvb_ref, kf_ref, vf_ref, r0, rows):
    zero = jnp.zeros((rows, LANES - MLA_ROPE), BF16)
    for h in range(HEADS):
        lo = h * (MLA_NOPE + MLA_DV)
        kv = jnp.dot(ckv, w_kvb_ref[:, lo:lo + MLA_NOPE + MLA_DV], preferred_element_type=F32)
        kf_ref[h, r0:r0 + rows, 0:MLA_NOPE] = kv[:, 0:MLA_NOPE].astype(BF16)
        kf_ref[h, r0:r0 + rows, MLA_NOPE:MLA_NOPE + MLA_ROPE] = kpe
        kf_ref[h, r0:r0 + rows, MLA_NOPE + MLA_ROPE:MLA_QK_PAD] = zero
        vf_ref[h, r0:r0 + rows, :] = kv[:, MLA_NOPE:MLA_NOPE + MLA_DV].astype(BF16)


def _mla_heads(q_ref, kf_ref, vf_ref, gate_ref, rows=slice(None)):
    def scores(h):
        return lax.dot_general(q_ref[rows, h * MLA_QK_PAD:(h + 1) * MLA_QK_PAD], kf_ref[h], _TRANS_B,
                               preferred_element_type=F32)

    heads = []
    pending = [scores(h) for h in range(SCORES_AHEAD)]
    for h in range(HEADS):
        s = pending.pop(0)
        if h + SCORES_AHEAD < HEADS:
            pending.append(scores(h + SCORES_AHEAD))
        e = jnp.exp2(s - jnp.max(s, axis=-1, keepdims=True))
        inv = 1.0 / jnp.sum(e, axis=-1, keepdims=True)
        o = jnp.dot(e.astype(BF16), vf_ref[h], preferred_element_type=F32) * inv
        lo, hi = h * LANES, (h + 1) * LANES
        heads.append((o * gate_ref[rows, lo:hi]).astype(BF16))
    return heads


def _attn_mla_ctx_kernel(q_ref, ckv_ref, kpe_ref, gate_ref, x_ref, mods_ref, g_post_ref, w_out_ref,
                         w_kvb_ref, o_ref, kf_ref, vf_ref):
    for r in range(CTX_PER_STEP):
        rows = slice(r * CTX_S, (r + 1) * CTX_S)
        _mla_expand(ckv_ref[r].astype(BF16), kpe_ref[r].astype(BF16), w_kvb_ref, kf_ref.at[r], vf_ref.at[r],
                    0, CTX_S)
        heads = _mla_heads(q_ref, kf_ref.at[r], vf_ref.at[r], gate_ref, rows)
        _finish(heads, w_out_ref, x_ref, mods_ref, g_post_ref, 0, o_ref, rows)


def _attn_mla_ctx(q, st_ckv, st_kpe, gate, x, mods, g_post, w_out, w_kvb, layer, j):
    tok = lambda w: pl.BlockSpec((CTX_PER_STEP * CTX_S, w), lambda b: (b, 0))
    return pl.pallas_call(
        _attn_mla_ctx_kernel,
        grid=(N_CTX_B // CTX_PER_STEP,),
        in_specs=[
            tok(HEADS * MLA_QK_PAD),
            pl.BlockSpec((CTX_PER_STEP, None, CTX_S, MLA_KV_LORA), lambda b: (b, j, 0, 0)),
            pl.BlockSpec((CTX_PER_STEP, None, CTX_S, MLA_ROPE), lambda b: (b, j, 0, 0)),
            tok(D_MODEL), tok(D_MODEL),
            pl.BlockSpec((None, COND_ROWS, 3 * D_MODEL), lambda b: (layer, 0, 0)),
            pl.BlockSpec((None, 1, D_MODEL), lambda b: (layer, 0, 0)),
            pl.BlockSpec((None, D_MODEL, D_MODEL), lambda b: (layer, 0, 0)),
            pl.BlockSpec((None, MLA_KV_LORA, HEADS * (MLA_NOPE + MLA_DV)), lambda b: (j, 0, 0)),
        ],
        out_specs=tok(D_MODEL),
        out_shape=jax.ShapeDtypeStruct(x.shape, F32),
        scratch_shapes=[
            pltpu.VMEM((CTX_PER_STEP, HEADS, CTX_S, MLA_QK_PAD), BF16),
            pltpu.VMEM((CTX_PER_STEP, HEADS, CTX_S, MLA_DV), BF16),
        ],
        compiler_params=pltpu.CompilerParams(
            dimension_semantics=("arbitrary",), vmem_limit_bytes=VMEM_LIMIT),
        name="attn_mla_ctx",
    )(q, st_ckv, st_kpe, gate, x, mods, g_post, w_out, w_kvb)


def _attn_mla_lat_kernel(q_ref, ckv_ref, kpe_ref, cckv_ref, ckpe_ref, gate_ref, x_ref, mods_ref,
                         g_post_ref, w_out_ref, w_kvb_ref, o_ref, kf_ref, vf_ref):
    @pl.when(pl.program_id(1) == 0)
    def _():
        _mla_expand(ckv_ref[...], kpe_ref[...], w_kvb_ref, kf_ref, vf_ref, 0, LAT_S)
        _mla_expand(cckv_ref[...].astype(BF16), ckpe_ref[...].astype(BF16), w_kvb_ref, kf_ref, vf_ref,
                    LAT_S, PAST)

    heads = _mla_heads(q_ref, kf_ref, vf_ref, gate_ref)
    _finish(heads, w_out_ref, x_ref, mods_ref, g_post_ref, 1 + pl.program_id(0), o_ref)


def _attn_mla_lat(q, ckv, kpe, cache_ckv, cache_kpe, gate, x, mods, g_post, w_out, w_kvb, layer, j):
    sk = LAT_S + PAST
    nq = LAT_S // TQ
    tok = lambda w: pl.BlockSpec((TQ, w), lambda b, i: (b * nq + i, 0))
    req = lambda w: pl.BlockSpec((LAT_S, w), lambda b, i: (b, 0))
    return pl.pallas_call(
        _attn_mla_lat_kernel,
        grid=(N_LAT_B, nq),
        in_specs=[
            tok(HEADS * MLA_QK_PAD), req(MLA_KV_LORA), req(MLA_ROPE),
            pl.BlockSpec((None, None, PAST, MLA_KV_LORA), lambda b, i: (b, j, 0, 0)),
            pl.BlockSpec((None, None, PAST, MLA_ROPE), lambda b, i: (b, j, 0, 0)),
            tok(D_MODEL), tok(D_MODEL),
            pl.BlockSpec((None, COND_ROWS, 3 * D_MODEL), lambda b, i: (layer, 0, 0)),
            pl.BlockSpec((None, 1, D_MODEL), lambda b, i: (layer, 0, 0)),
            pl.BlockSpec((None, D_MODEL, D_MODEL), lambda b, i: (layer, 0, 0)),
            pl.BlockSpec((None, MLA_KV_LORA, HEADS * (MLA_NOPE + MLA_DV)), lambda b, i: (j, 0, 0)),
        ],
        out_specs=tok(D_MODEL),
        out_shape=jax.ShapeDtypeStruct(x.shape, F32),
        scratch_shapes=[
            pltpu.VMEM((HEADS, sk, MLA_QK_PAD), BF16),
            pltpu.VMEM((HEADS, sk, MLA_DV), BF16),
        ],
        compiler_params=pltpu.CompilerParams(
            dimension_semantics=("arbitrary", "arbitrary"), vmem_limit_bytes=VMEM_LIMIT),
        name="attn_mla_lat",
    )(q, ckv, kpe, cache_ckv, cache_kpe, gate, x, mods, g_post, w_out, w_kvb)


def _rope_tables():
    rows = LAT_S // GRID_W
    row = jnp.repeat(jnp.arange(rows), GRID_W).astype(F32)
    col = jnp.tile(jnp.arange(GRID_W), rows).astype(F32)
    n_freq = DA_DH // 4
    inv = ROPE_THETA ** (-jnp.arange(n_freq, dtype=F32) / n_freq)
    ang = jnp.concatenate([row[:, None] * inv, col[:, None] * inv], axis=-1)
    cos, sin = jnp.cos(ang), jnp.sin(ang)
    return (jnp.concatenate([cos, cos, cos, cos], axis=-1),
            jnp.concatenate([-sin, sin, -sin, sin], axis=-1))


def kernel(x_prompt, x_sample, cache_diff_k, cache_diff_v, cache_mla_ckv, cache_mla_kpe,
           c, c_ctx, w_ada, b_ada, g_pre, g_post, w_out,
           da_w_in, da_lam_q1, da_lam_k1, da_lam_q2, da_lam_k2, da_g_sub,
           mla_w_in, mla_g_qa, mla_w_qb, mla_g_kva, mla_w_kvb):
    assert DA_DH == MLA_ROPE
    n_diff = (DEPTH + 1) // 2
    n_mla = DEPTH // 2

    cond = jnp.concatenate(
        [c_ctx[None, :], c, jnp.zeros((COND_ROWS - 1 - N_LAT_B, D_MODEL), F32)], axis=0)
    mods = _ada_call(cond, w_ada, b_ada)

    cos_t, sin_t = _rope_tables()
    g_pre3 = g_pre.reshape(DEPTH, 1, D_MODEL)
    g_post3 = g_post.reshape(DEPTH, 1, D_MODEL)
    w_out_b = w_out.astype(BF16)

    da_w_in_b = da_w_in.astype(BF16)
    g_sub3 = da_g_sub.reshape(n_diff, 1, DA_DV)
    lam_p = jnp.stack([da_lam_q1, da_lam_k1, da_lam_q2, da_lam_k2], axis=1)
    cache_dkt = jnp.transpose(cache_diff_k, (0, 1, 3, 4, 5, 2)).reshape(N_LAT_B, n_diff, D_MODEL, PAST)

    mla_w_in_b = jnp.swapaxes(mla_w_in, 1, 2).astype(BF16)
    w_qb4 = mla_w_qb.reshape(n_mla, MLA_Q_LORA, HEADS, MLA_NOPE + MLA_ROPE)
    w_qb_b = jnp.pad(w_qb4, ((0, 0), (0, 0), (0, 0), (0, MLA_QK_PAD - MLA_NOPE - MLA_ROPE))).reshape(
        n_mla, MLA_Q_LORA, HEADS * MLA_QK_PAD).astype(BF16)
    w_kvb_b = mla_w_kvb.astype(BF16)
    g_qa3 = mla_g_qa.reshape(n_mla, 1, MLA_Q_LORA)
    g_kva3 = mla_g_kva.reshape(n_mla, 1, MLA_KV_LORA)

    x_ctx = x_prompt.reshape(N_CTX_B * CTX_S, D_MODEL)
    x_lat = x_sample.reshape(N_LAT_B * LAT_S, D_MODEL)
    st_diff = None
    st_mla = None

    for i in range(DEPTH):
        j = i // 2
        if i % 2 == 0:
            q_c, k_c, vt_c, gate_c, st_dkt, st_dv = _pre_diff_ctx(
                x_ctx, mods, g_pre3, da_w_in_b, st_diff, i, j, n_diff)
            st_diff = (st_dkt, st_dv)
            q_l, k_l, vt_l, gate_l = _pre_diff_lat(x_lat, mods, g_pre3, da_w_in_b, cos_t, sin_t, i, j)
            x_ctx = _attn_diff_ctx(q_c, k_c, vt_c, gate_c, x_ctx, mods, g_post3, w_out_b,
                                   g_sub3, lam_p, i, j)
            x_lat = _attn_diff_lat(q_l, k_l, vt_l, cache_dkt, cache_diff_v, gate_l, x_lat, mods, g_post3,
                                   w_out_b, g_sub3, lam_p, i, j)
        else:
            q_c, gate_c, st_ckv, st_kpe = _pre_mla_ctx(
                x_ctx, mods, g_pre3, mla_w_in_b, g_qa3, w_qb_b, g_kva3, st_mla, i, j, n_mla)
            st_mla = (st_ckv, st_kpe)
            q_l, gate_l, ckv_l, kpe_l = _pre_mla_lat(
                x_lat, mods, g_pre3, mla_w_in_b, g_qa3, w_qb_b, g_kva3, cos_t, sin_t, i, j)
            x_ctx = _attn_mla_ctx(q_c, st_ckv, st_kpe, gate_c, x_ctx, mods, g_post3, w_out_b,
                                  w_kvb_b, i, j)
            x_lat = _attn_mla_lat(q_l, ckv_l, kpe_l, cache_mla_ckv, cache_mla_kpe, gate_l, x_lat,
                                  mods, g_post3, w_out_b, w_kvb_b, i, j)

    return (x_ctx.reshape(N_CTX_B, CTX_S, D_MODEL),
            x_lat.reshape(N_LAT_B, LAT_S, D_MODEL),
            jnp.transpose(st_dkt.reshape(N_CTX_B, n_diff, HEADS, 2, DA_DH, CTX_S), (0, 1, 5, 2, 3, 4)),
            st_dv,
            st_ckv,
            st_kpe)
```

```python
import functools
import math

import jax
import jax.numpy as jnp
from jax import lax
from jax.experimental import pallas as pl
from jax.experimental.pallas import tpu as pltpu

F32 = jnp.float32
BF16 = jnp.bfloat16

D_MODEL = 1024
DEPTH = 4
N_CTX_B = 16
CTX_S = 256
N_LAT_B = 4
LAT_S = 1024
PAST = 256
GRID_W = 64
HEADS = 8
DA_DH = 64
DA_DV = 128
MLA_Q_LORA = 384
MLA_KV_LORA = 256
MLA_NOPE = 128
MLA_ROPE = 64
MLA_DV = 128
MLA_QK_PAD = 256
ROPE_THETA = 10000.0
NORM_EPS = 1e-6
COND_ROWS = 8

LANES = 128
V_PAD = 16
TQ = 256
TM = 512
VMEM_LIMIT = 56 * 1024 * 1024

_TRANS_B = (((1,), (1,)), ((), ()))
SCORES_AHEAD = 2
CTX_PER_STEP = 2
DA_Q_SCALE = DA_DH ** -0.5 * math.log2(math.e)
MLA_Q_SCALE = (MLA_NOPE + MLA_ROPE) ** -0.5 * math.log2(math.e)


def _silu(x):
    return x * (1.0 / (1.0 + jnp.exp(-x)))


def _rms(x, g):
    r = lax.rsqrt(jnp.mean(x * x, axis=-1, keepdims=True) + NORM_EPS)
    return (x * r) * g


def _rope_slab(x, cos, sin_signed):
    lane = lax.broadcasted_iota(jnp.int32, x.shape, 1)
    first_half = (lane % 64) < 32
    partner = jnp.where(first_half, pltpu.roll(x, 96, axis=1), pltpu.roll(x, 32, axis=1))
    return x * cos + partner * sin_signed


def _ones_rows(cols):
    row = lax.broadcasted_iota(jnp.int32, (V_PAD, cols), 0)
    return jnp.where(row == 0, 1.0, 0.0).astype(BF16)


def _cond_row(mods_ref, row):
    m = mods_ref[pl.ds(row, 1), :]
    return m[:, 0:D_MODEL], m[:, D_MODEL:2 * D_MODEL], m[:, 2 * D_MODEL:3 * D_MODEL]


def _modulated(x_ref, mods_ref, g_ref, row):
    shift, scale, _ = _cond_row(mods_ref, row)
    h = _rms(x_ref[...], g_ref[...]) * (1.0 + scale) + shift
    return h.astype(BF16)


def _ada_kernel(cond_ref, w_ref, b_ref, o_ref):
    a = _silu(cond_ref[...]).astype(BF16)
    o_ref[...] = jnp.dot(a, w_ref[...].astype(BF16), preferred_element_type=F32) + b_ref[...]


def _ada_call(cond, w_ada, b_ada):
    tn = 3 * D_MODEL
    return pl.pallas_call(
        _ada_kernel,
        grid=(DEPTH, 3 * D_MODEL // tn),
        in_specs=[
            pl.BlockSpec((COND_ROWS, D_MODEL), lambda i, n: (0, 0)),
            pl.BlockSpec((None, D_MODEL, tn), lambda i, n: (i, 0, n)),
            pl.BlockSpec((None, 1, tn), lambda i, n: (i, 0, n)),
        ],
        out_specs=pl.BlockSpec((None, COND_ROWS, tn), lambda i, n: (i, 0, n)),
        out_shape=jax.ShapeDtypeStruct((DEPTH, COND_ROWS, 3 * D_MODEL), F32),
        compiler_params=pltpu.CompilerParams(
            dimension_semantics=("arbitrary", "arbitrary"), vmem_limit_bytes=VMEM_LIMIT),
        name="ada_mod",
    )(cond, w_ada, b_ada.reshape(DEPTH, 1, 3 * D_MODEL))


def _cast_job_specs(jobs):
    in_specs = [pl.BlockSpec(blk, src_map) for _, blk, src_map, _, _ in jobs]
    out_specs = [pl.BlockSpec(blk, dst_map) for _, blk, _, _, dst_map in jobs]
    out_shape = [jax.ShapeDtypeStruct(shape, BF16) for _, _, _, shape, _ in jobs]
    return in_specs, out_specs, out_shape, [a for a, _, _, _, _ in jobs]


def _run_cast_jobs(src_refs, dst_refs):
    for src, dst in zip(src_refs, dst_refs):
        dst[...] = src[...].astype(BF16)


def _write_state(ref, r, j, value):
    if j is None:
        ref[r] = value
    else:
        for jj in range(ref.shape[1]):
            ref[r, jj] = value if jj == j else jnp.zeros_like(value)


def _pre_diff_ctx_kernel(create_j, x_ref, mods_ref, g_ref, w_ref, *refs):
    q_ref, k_ref, vt_ref, gate_ref, skt_ref, sv_ref = refs[-6:]
    hb = _modulated(x_ref, mods_ref, g_ref, 0)
    n = D_MODEL
    q = jnp.dot(hb, w_ref[:, 0:n], preferred_element_type=F32)
    q_ref[...] = (q * DA_Q_SCALE).astype(BF16)
    k = jnp.dot(hb, w_ref[:, n:2 * n], preferred_element_type=F32)
    v = jnp.dot(hb, w_ref[:, 2 * n:3 * n], preferred_element_type=F32)
    k_ref[...] = k.astype(BF16)
    for r in range(CTX_PER_STEP):
        rows = slice(r * CTX_S, (r + 1) * CTX_S)
        vt = v[rows, :].T
        _write_state(skt_ref, r, create_j, k[rows, :].T)
        _write_state(sv_ref, r, create_j, v[rows, :].reshape(CTX_S, HEADS, DA_DV))
        vt_ref[r] = vt.astype(BF16)
    gate_ref[...] = _silu(jnp.dot(hb, w_ref[:, 3 * n:4 * n], preferred_element_type=F32)).astype(BF16)


def _state_plumbing(states, shapes, j, n_inputs, n_outputs):
    def index_map(shp, layer_index):
        return lambda t: (t, layer_index) + (0,) * (len(shp) - 2)

    if states is None:
        specs = [pl.BlockSpec((CTX_PER_STEP,) + shp[1:], index_map(shp, 0)) for shp in shapes]
        return [], specs, {}, ()
    specs = [pl.BlockSpec((CTX_PER_STEP, None) + shp[2:], index_map(shp, j)) for shp in shapes]
    in_specs = [pl.BlockSpec(memory_space=pl.ANY) for _ in shapes]
    aliases = {n_inputs + i: n_outputs + i for i in range(len(shapes))}
    return in_specs, specs, aliases, tuple(states)


def _pre_diff_ctx(x, mods, g_pre, w_in, wj, states, layer, j, n_layers):
    n_tok = N_CTX_B * CTX_S
    shapes = [(N_CTX_B, n_layers, D_MODEL, CTX_S), (N_CTX_B, n_layers, CTX_S, HEADS, DA_DV)]
    st_in_specs, st_out_specs, aliases, st_args = _state_plumbing(states, shapes, j, 4, 4)
    tok_spec = pl.BlockSpec((CTX_PER_STEP * CTX_S, D_MODEL), lambda t: (t, 0))
    return pl.pallas_call(
        functools.partial(_pre_diff_ctx_kernel, j if states is None else None),
        grid=(N_CTX_B // CTX_PER_STEP,),
        in_specs=[
            tok_spec,
            pl.BlockSpec((None, COND_ROWS, 3 * D_MODEL), lambda t: (layer, 0, 0)),
            pl.BlockSpec((None, 1, D_MODEL), lambda t: (layer, 0, 0)),
            pl.BlockSpec((None, D_MODEL, 4 * D_MODEL), lambda t: (wj, 0, 0)),
        ] + st_in_specs,
        out_specs=[
            tok_spec,
            tok_spec,
            pl.BlockSpec((CTX_PER_STEP, D_MODEL, CTX_S), lambda t: (t, 0, 0)),
            tok_spec,
        ] + st_out_specs,
        out_shape=[
            jax.ShapeDtypeStruct((n_tok, D_MODEL), BF16),
            jax.ShapeDtypeStruct((n_tok, D_MODEL), BF16),
            jax.ShapeDtypeStruct((N_CTX_B, D_MODEL, CTX_S), BF16),
            jax.ShapeDtypeStruct((n_tok, D_MODEL), BF16),
        ] + [jax.ShapeDtypeStruct(shp, F32) for shp in shapes],
        input_output_aliases=aliases,
        compiler_params=pltpu.CompilerParams(
            dimension_semantics=("arbitrary",), vmem_limit_bytes=VMEM_LIMIT),
        name="pre_diff_ctx",
    )(x, mods, g_pre, w_in, *st_args)


def _pre_diff_lat_kernel(x_ref, mods_ref, g_ref, w_ref, cos_ref, sin_ref,
                         q_ref, k_ref, vt_ref, gate_ref):
    row = 1 + pl.program_id(0) // (LAT_S // TM)
    hb = _modulated(x_ref, mods_ref, g_ref, row)
    cos = cos_ref[...]
    sin = sin_ref[...]
    n = D_MODEL
    q = jnp.dot(hb, w_ref[:, 0:n], preferred_element_type=F32)
    k = jnp.dot(hb, w_ref[:, n:2 * n], preferred_element_type=F32)
    cos_q = cos * DA_Q_SCALE
    sin_q = sin * DA_Q_SCALE
    slabs = [slice(h * LANES, (h + 1) * LANES) for h in range(HEADS)]
    q_ref[...] = jnp.concatenate(
        [_rope_slab(q[:, sl], cos_q, sin_q).astype(BF16) for sl in slabs], axis=1)
    k_ref[...] = jnp.concatenate(
        [_rope_slab(k[:, sl], cos, sin).astype(BF16) for sl in slabs], axis=1)
    vt_ref[...] = jnp.dot(hb, w_ref[:, 2 * n:3 * n], preferred_element_type=F32).T.astype(BF16)
    gate_ref[...] = _silu(jnp.dot(hb, w_ref[:, 3 * n:4 * n], preferred_element_type=F32)).astype(BF16)


def _pre_diff_lat(x, mods, g_pre, w_in, wj, cos_t, sin_t, layer, j):
    n_tok = N_LAT_B * LAT_S
    tiles_per_req = LAT_S // TM
    tok_spec = pl.BlockSpec((TM, D_MODEL), lambda t: (t, 0))
    rope_spec = pl.BlockSpec((TM, LANES), lambda t: (t % tiles_per_req, 0))
    return pl.pallas_call(
        _pre_diff_lat_kernel,
        grid=(n_tok // TM,),
        in_specs=[
            tok_spec,
            pl.BlockSpec((None, COND_ROWS, 3 * D_MODEL), lambda t: (layer, 0, 0)),
            pl.BlockSpec((None, 1, D_MODEL), lambda t: (layer, 0, 0)),
            pl.BlockSpec((None, D_MODEL, 4 * D_MODEL), lambda t: (wj, 0, 0)),
            rope_spec,
            rope_spec,
        ],
        out_specs=[
            tok_spec, tok_spec,
            pl.BlockSpec((None, D_MODEL, TM), lambda t: (t // tiles_per_req, 0, t % tiles_per_req)),
            tok_spec,
        ],
        out_shape=[
            jax.ShapeDtypeStruct((n_tok, D_MODEL), BF16),
            jax.ShapeDtypeStruct((n_tok, D_MODEL), BF16),
            jax.ShapeDtypeStruct((N_LAT_B, D_MODEL, LAT_S), BF16),
            jax.ShapeDtypeStruct((n_tok, D_MODEL), BF16),
        ],
        compiler_params=pltpu.CompilerParams(
            dimension_semantics=("arbitrary",), vmem_limit_bytes=VMEM_LIMIT),
        name="pre_diff_lat",
    )(x, mods, g_pre, w_in, cos_t, sin_t)


def _finish(heads, w_out_ref, x_ref, mods_ref, g_post_ref, row, o_ref, rows=slice(None)):
    _, _, gate = _cond_row(mods_ref, row)
    og = jnp.concatenate(heads, axis=1)
    y = jnp.dot(og, w_out_ref[...], preferred_element_type=F32)
    o_ref[rows, :] = x_ref[rows, :] + gate * _rms(y, g_post_ref[...])


def _diff_lambda(lam_ref, lam_init):
    p = lam_ref[...]
    a = jnp.sum(p[0:1, :] * p[1:2, :], axis=-1, keepdims=True)
    b = jnp.sum(p[2:3, :] * p[3:4, :], axis=-1, keepdims=True)
    return jnp.exp(a) - jnp.exp(b) + lam_init


def _stack_diff_keys(k, k2_ref, sk):
    lane = lax.broadcasted_iota(jnp.int32, (sk, LANES), 1)
    zero = jnp.zeros((sk, LANES), BF16)
    for h in range(HEADS):
        kh = k[:, h * LANES:(h + 1) * LANES]
        k2_ref[h, 0:sk, :] = jnp.where(lane < DA_DH, kh, zero)
        k2_ref[h, sk:2 * sk, :] = jnp.where(lane >= DA_DH, kh, zero)


def _stack_values_t(vt, vt_ref, c0, cols):
    ones = _ones_rows(cols)
    for h in range(HEADS):
        vt_ref[h, 0:DA_DV, c0:c0 + cols] = vt[h * DA_DV:(h + 1) * DA_DV, :]
        vt_ref[h, DA_DV:DA_DV + V_PAD, c0:c0 + cols] = ones


def _diff_heads(q_ref, k2_ref, vt_ref, gate_ref, g_sub_ref, lam, lam_init, sk, rows=slice(None)):
    g_sub = g_sub_ref[...]

    def scores(h):
        return lax.dot_general(k2_ref[h], q_ref[rows, h * LANES:(h + 1) * LANES], _TRANS_B,
                               preferred_element_type=F32)

    heads = []
    pending = [scores(h) for h in range(SCORES_AHEAD)]
    for h in range(HEADS):
        lo, hi = h * LANES, (h + 1) * LANES
        s = pending.pop(0)
        if h + SCORES_AHEAD < HEADS:
            pending.append(scores(h + SCORES_AHEAD))
        s0 = s[0:sk, :]
        s1 = s[sk:2 * sk, :]
        e0 = jnp.exp2(s0 - jnp.max(s0, axis=0, keepdims=True)).astype(BF16)
        e1 = jnp.exp2(s1 - jnp.max(s1, axis=0, keepdims=True)).astype(BF16)
        pv0 = jnp.dot(vt_ref[h], e0, preferred_element_type=F32)
        pv1 = jnp.dot(vt_ref[h], e1, preferred_element_type=F32)
        a0 = 1.0 / pv0[DA_DV:DA_DV + 1, :]
        a1 = lam / pv1[DA_DV:DA_DV + 1, :]
        ot = pv0[0:DA_DV, :] * a0 - pv1[0:DA_DV, :] * a1
        ot = ot * lax.rsqrt(jnp.mean(ot * ot, axis=0, keepdims=True) + NORM_EPS)
        o = (ot.T * g_sub) * (1.0 - lam_init)
        heads.append((o * gate_ref[rows, lo:hi]).astype(BF16))
    return heads


def _attn_diff_ctx_kernel(lam_init, n_cast, q_ref, k_ref, vt_ref, gate_ref, x_ref, mods_ref, g_post_ref,
                          w_out_ref, g_sub_ref, lam_ref, *refs):
    o_ref = refs[n_cast]
    k2_ref, vb_ref = refs[-2:]
    _run_cast_jobs(refs[0:n_cast], refs[n_cast + 1:2 * n_cast + 1])
    lam = _diff_lambda(lam_ref, lam_init)
    for r in range(CTX_PER_STEP):
        rows = slice(r * CTX_S, (r + 1) * CTX_S)
        _stack_diff_keys(k_ref[rows, :], k2_ref.at[r], CTX_S)
        _stack_values_t(vt_ref[r], vb_ref.at[r], 0, CTX_S)
        heads = _diff_heads(q_ref, k2_ref.at[r], vb_ref.at[r], gate_ref, g_sub_ref, lam, lam_init,
                            CTX_S, rows)
        _finish(heads, w_out_ref, x_ref, mods_ref, g_post_ref, 0, o_ref, rows)


def _attn_diff_ctx(q, k, vt, gate, x, mods, g_post, w_out, wo, g_sub, lam_p, layer, j, cast_jobs=()):
    lam_init = 0.8 - 0.6 * math.exp(-0.3 * layer)
    tok_spec = pl.BlockSpec((CTX_PER_STEP * CTX_S, D_MODEL), lambda b: (b, 0))
    cj_in, cj_out, cj_shapes, cj_args = _cast_job_specs(cast_jobs)
    return pl.pallas_call(
        functools.partial(_attn_diff_ctx_kernel, lam_init, len(cast_jobs)),
        grid=(N_CTX_B // CTX_PER_STEP,),
        in_specs=[
            tok_spec, tok_spec, pl.BlockSpec((CTX_PER_STEP, D_MODEL, CTX_S), lambda b: (b, 0, 0)),
            tok_spec, tok_spec,
            pl.BlockSpec((None, COND_ROWS, 3 * D_MODEL), lambda b: (layer, 0, 0)),
            pl.BlockSpec((None, 1, D_MODEL), lambda b: (layer, 0, 0)),
            pl.BlockSpec((None, D_MODEL, D_MODEL), lambda b: (wo, 0, 0)),
            pl.BlockSpec((None, 1, DA_DV), lambda b: (j, 0, 0)),
            pl.BlockSpec((None, 4, DA_DH), lambda b: (j, 0, 0)),
        ] + cj_in,
        out_specs=[tok_spec] + cj_out,
        out_shape=[jax.ShapeDtypeStruct(x.shape, F32)] + cj_shapes,
        scratch_shapes=[
            pltpu.VMEM((CTX_PER_STEP, HEADS, 2 * CTX_S, LANES), BF16),
            pltpu.VMEM((CTX_PER_STEP, HEADS, DA_DV + V_PAD, CTX_S), BF16),
        ],
        compiler_params=pltpu.CompilerParams(
            dimension_semantics=("arbitrary",), vmem_limit_bytes=VMEM_LIMIT),
        name="attn_diff_ctx",
    )(q, k, vt, gate, x, mods, g_post, w_out, g_sub, lam_p, *cj_args)


def _attn_diff_lat_kernel(lam_init, q_ref, k_ref, vt_ref, ckt_ref, cv_ref, gate_ref, x_ref, mods_ref,
                          g_post_ref, w_out_ref, g_sub_ref, lam_ref, o_ref, k2_ref, vb_ref):
    sk = LAT_S + PAST

    @pl.when(pl.program_id(1) == 0)
    def _():
        lane = lax.broadcasted_iota(jnp.int32, (PAST, LANES), 1)
        zero_l = jnp.zeros((LAT_S, LANES), BF16)
        zero_p = jnp.zeros((PAST, LANES), BF16)
        lane_l = lax.broadcasted_iota(jnp.int32, (LAT_S, LANES), 1)
        for h in range(HEADS):
            lo, hi = h * LANES, (h + 1) * LANES
            kh = k_ref[:, lo:hi]
            ch = ckt_ref[lo:hi, :].T.astype(BF16)
            k2_ref[h, 0:LAT_S, :] = jnp.where(lane_l < DA_DH, kh, zero_l)
            k2_ref[h, LAT_S:sk, :] = jnp.where(lane < DA_DH, ch, zero_p)
            k2_ref[h, sk:sk + LAT_S, :] = jnp.where(lane_l >= DA_DH, kh, zero_l)
            k2_ref[h, sk + LAT_S:2 * sk, :] = jnp.where(lane >= DA_DH, ch, zero_p)
            vb_ref[h, 0:DA_DV, LAT_S:sk] = cv_ref[:, h, :].T.astype(BF16)
        _stack_values_t(vt_ref[...], vb_ref, 0, LAT_S)
        ones = _ones_rows(PAST)
        for h in range(HEADS):
            vb_ref[h, DA_DV:DA_DV + V_PAD, LAT_S:sk] = ones

    lam = _diff_lambda(lam_ref, lam_init)
    heads = _diff_heads(q_ref, k2_ref, vb_ref, gate_ref, g_sub_ref, lam, lam_init, sk)
    _finish(heads, w_out_ref, x_ref, mods_ref, g_post_ref, 1 + pl.program_id(0), o_ref)


def _attn_diff_lat(q, k, vt, cache_kt, cache_v, gate, x, mods, g_post, w_out, wo, g_sub, lam_p, layer, j):
    lam_init = 0.8 - 0.6 * math.exp(-0.3 * layer)
    sk = LAT_S + PAST
    nq = LAT_S // TQ
    tok_spec = pl.BlockSpec((TQ, D_MODEL), lambda b, i: (b * nq + i, 0))
    kv_spec = pl.BlockSpec((LAT_S, D_MODEL), lambda b, i: (b, 0))
    cache_kt_spec = pl.BlockSpec((None, None, D_MODEL, PAST), lambda b, i: (b, j, 0, 0))
    cache_v_spec = pl.BlockSpec((None, None, PAST, HEADS, DA_DV), lambda b, i: (b, j, 0, 0, 0))
    return pl.pallas_call(
        functools.partial(_attn_diff_lat_kernel, lam_init),
        grid=(N_LAT_B, nq),
        in_specs=[
            tok_spec, kv_spec, pl.BlockSpec((None, D_MODEL, LAT_S), lambda b, i: (b, 0, 0)),
            cache_kt_spec, cache_v_spec, tok_spec, tok_spec,
            pl.BlockSpec((None, COND_ROWS, 3 * D_MODEL), lambda b, i: (layer, 0, 0)),
            pl.BlockSpec((None, 1, D_MODEL), lambda b, i: (layer, 0, 0)),
            pl.BlockSpec((None, D_MODEL, D_MODEL), lambda b, i: (wo, 0, 0)),
            pl.BlockSpec((None, 1, DA_DV), lambda b, i: (j, 0, 0)),
            pl.BlockSpec((None, 4, DA_DH), lambda b, i: (j, 0, 0)),
        ],
        out_specs=tok_spec,
        out_shape=jax.ShapeDtypeStruct(x.shape, F32),
        scratch_shapes=[
            pltpu.VMEM((HEADS, 2 * sk, LANES), BF16),
            pltpu.VMEM((HEADS, DA_DV + V_PAD, sk), BF16),
        ],
        compiler_params=pltpu.CompilerParams(
            dimension_semantics=("arbitrary", "arbitrary"), vmem_limit_bytes=VMEM_LIMIT),
        name="attn_diff_lat",
    )(q, k, vt, cache_kt, cache_v, gate, x, mods, g_post, w_out, g_sub, lam_p)


MLA_IN = MLA_Q_LORA + MLA_KV_LORA + MLA_ROPE + D_MODEL
_QA = slice(0, MLA_Q_LORA)
_KV = slice(MLA_Q_LORA, MLA_Q_LORA + MLA_KV_LORA)
_PE = slice(_KV.stop, _KV.stop + LANES)
_GT = slice(_KV.stop + MLA_ROPE, MLA_IN)


def _proj_t(hb, w_t_ref, rows):
    return lax.dot_general(hb, w_t_ref[rows, :], _TRANS_B, preferred_element_type=F32)


def _mla_queries(hb, w_in_ref, g_qa_ref):
    q_a = _proj_t(hb, w_in_ref, _QA)
    return _rms(q_a, g_qa_ref[...]).astype(BF16)


def _pre_mla_ctx_kernel(create_j, n_cast, x_ref, mods_ref, g_ref, w_in_ref, g_qa_ref, w_qb_ref, g_kva_ref,
                        *refs):
    n_out = 4 + n_cast
    q_ref, gate_ref, ckv_ref, kpe_ref = refs[len(refs) - n_out:len(refs) - n_cast]
    _run_cast_jobs(refs[len(refs) - n_out - n_cast:len(refs) - n_out], refs[len(refs) - n_cast:])
    hb = _modulated(x_ref, mods_ref, g_ref, 0)
    qn = _mla_queries(hb, w_in_ref, g_qa_ref)
    q_ref[...] = (jnp.dot(qn, w_qb_ref[...], preferred_element_type=F32) * MLA_Q_SCALE).astype(BF16)
    kv_a = _proj_t(hb, w_in_ref, _KV)
    ckv = _rms(kv_a, g_kva_ref[...])
    gate_ref[...] = _silu(_proj_t(hb, w_in_ref, _GT)).astype(BF16)
    pe = _proj_t(hb, w_in_ref, _PE)
    for r in range(CTX_PER_STEP):
        rows = slice(r * CTX_S, (r + 1) * CTX_S)
        _write_state(ckv_ref, r, create_j, ckv[rows, :])
        _write_state(kpe_ref, r, create_j, pe[rows, 0:MLA_ROPE])


def _pre_mla_ctx(x, mods, g_pre, w_in, wj, g_qa, w_qb, g_kva, states, layer, j, n_layers, cast_jobs=()):
    n_tok = N_CTX_B * CTX_S
    tok = lambda w: pl.BlockSpec((CTX_PER_STEP * CTX_S, w), lambda t: (t, 0))
    shapes = [(N_CTX_B, n_layers, CTX_S, MLA_KV_LORA), (N_CTX_B, n_layers, CTX_S, MLA_ROPE)]
    st_in_specs, st_out_specs, aliases, st_args = _state_plumbing(states, shapes, j, 7, 2)
    cj_in, cj_out, cj_shapes, cj_args = _cast_job_specs(cast_jobs)
    return pl.pallas_call(
        functools.partial(_pre_mla_ctx_kernel, j if states is None else None, len(cast_jobs)),
        grid=(N_CTX_B // CTX_PER_STEP,),
        in_specs=[
            tok(D_MODEL),
            pl.BlockSpec((None, COND_ROWS, 3 * D_MODEL), lambda t: (layer, 0, 0)),
            pl.BlockSpec((None, 1, D_MODEL), lambda t: (layer, 0, 0)),
            pl.BlockSpec((None, MLA_IN, D_MODEL), lambda t: (wj, 0, 0)),
            pl.BlockSpec((None, 1, MLA_Q_LORA), lambda t: (j, 0, 0)),
            pl.BlockSpec((None, MLA_Q_LORA, HEADS * MLA_QK_PAD), lambda t: (j, 0, 0)),
            pl.BlockSpec((None, 1, MLA_KV_LORA), lambda t: (j, 0, 0)),
        ] + st_in_specs + cj_in,
        out_specs=[tok(HEADS * MLA_QK_PAD), tok(D_MODEL)] + st_out_specs + cj_out,
        out_shape=[
            jax.ShapeDtypeStruct((n_tok, HEADS * MLA_QK_PAD), BF16),
            jax.ShapeDtypeStruct((n_tok, D_MODEL), BF16),
        ] + [jax.ShapeDtypeStruct(shp, F32) for shp in shapes] + cj_shapes,
        input_output_aliases=aliases,
        compiler_params=pltpu.CompilerParams(
            dimension_semantics=("arbitrary",), vmem_limit_bytes=VMEM_LIMIT),
        name="pre_mla_ctx",
    )(x, mods, g_pre, w_in, g_qa, w_qb, g_kva, *st_args, *cj_args)


def _pre_mla_lat_kernel(x_ref, mods_ref, g_ref, w_in_ref, g_qa_ref, w_qb_ref, g_kva_ref, cos_ref, sin_ref,
                        q_ref, gate_ref, ckv_ref, kpe_ref):
    row = 1 + pl.program_id(0) // (LAT_S // TM)
    hb = _modulated(x_ref, mods_ref, g_ref, row)
    cos = cos_ref[...]
    sin = sin_ref[...]
    qn = _mla_queries(hb, w_in_ref, g_qa_ref)
    for h in range(HEADS):
        lo = h * MLA_QK_PAD
        q = jnp.dot(qn, w_qb_ref[:, lo:lo + MLA_QK_PAD], preferred_element_type=F32)
        q = q * MLA_Q_SCALE
        q_ref[:, lo:lo + LANES] = q[:, 0:LANES].astype(BF16)
        q_ref[:, lo + LANES:lo + 2 * LANES] = _rope_slab(q[:, LANES:2 * LANES], cos, sin).astype(BF16)
    kv_a = _proj_t(hb, w_in_ref, _KV)
    ckv_ref[...] = _rms(kv_a, g_kva_ref[...]).astype(BF16)
    gate_ref[...] = _silu(_proj_t(hb, w_in_ref, _GT)).astype(BF16)
    pe = _proj_t(hb, w_in_ref, _PE)
    kpe_ref[...] = _rope_slab(pe, cos, sin)[:, 0:MLA_ROPE].astype(BF16)


def _pre_mla_lat(x, mods, g_pre, w_in, wj, g_qa, w_qb, g_kva, cos_t, sin_t, layer, j):
    n_tok = N_LAT_B * LAT_S
    tiles_per_req = LAT_S // TM
    tok = lambda w: pl.BlockSpec((TM, w), lambda t: (t, 0))
    rope_spec = pl.BlockSpec((TM, LANES), lambda t: (t % tiles_per_req, 0))
    return pl.pallas_call(
        _pre_mla_lat_kernel,
        grid=(n_tok // TM,),
        in_specs=[
            tok(D_MODEL),
            pl.BlockSpec((None, COND_ROWS, 3 * D_MODEL), lambda t: (layer, 0, 0)),
            pl.BlockSpec((None, 1, D_MODEL), lambda t: (layer, 0, 0)),
            pl.BlockSpec((None, MLA_IN, D_MODEL), lambda t: (wj, 0, 0)),
            pl.BlockSpec((None, 1, MLA_Q_LORA), lambda t: (j, 0, 0)),
            pl.BlockSpec((None, MLA_Q_LORA, HEADS * MLA_QK_PAD), lambda t: (j, 0, 0)),
            pl.BlockSpec((None, 1, MLA_KV_LORA), lambda t: (j, 0, 0)),
            rope_spec,
            rope_spec,
        ],
        out_specs=[tok(HEADS * MLA_QK_PAD), tok(D_MODEL), tok(MLA_KV_LORA), tok(MLA_ROPE)],
        out_shape=[
            jax.ShapeDtypeStruct((n_tok, HEADS * MLA_QK_PAD), BF16),
            jax.ShapeDtypeStruct((n_tok, D_MODEL), BF16),
            jax.ShapeDtypeStruct((n_tok, MLA_KV_LORA), BF16),
            jax.ShapeDtypeStruct((n_tok, MLA_ROPE), BF16),
        ],
        compiler_params=pltpu.CompilerParams(
            dimension_semantics=("arbitrary",), vmem_limit_bytes=VMEM_LIMIT),
        name="pre_mla_lat",
    )(x, mods, g_pre, w_in, g_qa, w_qb, g_kva, cos_t, sin_t)


def _mla_expand(ckv, kpe, w_kvb_ref, kf_ref, vf_ref, r0, rows):
    zero = jnp.zeros((rows, LANES - MLA_ROPE), BF16)
    for h in range(HEADS):
        lo = h * (MLA_NOPE + MLA_DV)
        kv = jnp.dot(ckv, w_kvb_ref[:, lo:lo + MLA_NOPE + MLA_DV], preferred_element_type=F32)
        kf_ref[h, r0:r0 + rows, 0:MLA_NOPE] = kv[:, 0:MLA_NOPE].astype(BF16)
        kf_ref[h, r0:r0 + rows, MLA_NOPE:MLA_NOPE + MLA_ROPE] = kpe
        kf_ref[h, r0:r0 + rows, MLA_NOPE + MLA_ROPE:MLA_QK_PAD] = zero
        vf_ref[h, r0:r0 + rows, :] = kv[:, MLA_NOPE:MLA_NOPE + MLA_DV].astype(BF16)


def _mla_heads(q_ref, kf_ref, vf_ref, gate_ref, rows=slice(None)):
    def scores(h):
        return lax.dot_general(q_ref[rows, h * MLA_QK_PAD:(h + 1) * MLA_QK_PAD], kf_ref[h], _TRANS_B,
                               preferred_element_type=F32)

    heads = []
    pending = [scores(h) for h in range(SCORES_AHEAD)]
    for h in range(HEADS):
        s = pending.pop(0)
        if h + SCORES_AHEAD < HEADS:
            pending.append(scores(h + SCORES_AHEAD))
        e = jnp.exp2(s - jnp.max(s, axis=-1, keepdims=True))
        inv = 1.0 / jnp.sum(e, axis=-1, keepdims=True)
        o = jnp.dot(e.astype(BF16), vf_ref[h], preferred_element_type=F32) * inv
        lo, hi = h * LANES, (h + 1) * LANES
        heads.append((o * gate_ref[rows, lo:hi]).astype(BF16))
    return heads


def _attn_mla_ctx_kernel(q_ref, ckv_ref, kpe_ref, gate_ref, x_ref, mods_ref, g_post_ref, w_out_ref,
                         w_kvb_ref, o_ref, kf_ref, vf_ref):
    for r in range(CTX_PER_STEP):
        rows = slice(r * CTX_S, (r + 1) * CTX_S)
        _mla_expand(ckv_ref[r].astype(BF16), kpe_ref[r].astype(BF16), w_kvb_ref, kf_ref.at[r], vf_ref.at[r],
                    0, CTX_S)
        heads = _mla_heads(q_ref, kf_ref.at[r], vf_ref.at[r], gate_ref, rows)
        _finish(heads, w_out_ref, x_ref, mods_ref, g_post_ref, 0, o_ref, rows)


def _attn_mla_ctx(q, st_ckv, st_kpe, gate, x, mods, g_post, w_out, wo, w_kvb, layer, j):
    tok = lambda w: pl.BlockSpec((CTX_PER_STEP * CTX_S, w), lambda b: (b, 0))
    return pl.pallas_call(
        _attn_mla_ctx_kernel,
        grid=(N_CTX_B // CTX_PER_STEP,),
        in_specs=[
            tok(HEADS * MLA_QK_PAD),
            pl.BlockSpec((CTX_PER_STEP, None, CTX_S, MLA_KV_LORA), lambda b: (b, j, 0, 0)),
            pl.BlockSpec((CTX_PER_STEP, None, CTX_S, MLA_ROPE), lambda b: (b, j, 0, 0)),
            tok(D_MODEL), tok(D_MODEL),
            pl.BlockSpec((None, COND_ROWS, 3 * D_MODEL), lambda b: (layer, 0, 0)),
            pl.BlockSpec((None, 1, D_MODEL), lambda b: (layer, 0, 0)),
            pl.BlockSpec((None, D_MODEL, D_MODEL), lambda b: (wo, 0, 0)),
            pl.BlockSpec((None, MLA_KV_LORA, HEADS * (MLA_NOPE + MLA_DV)), lambda b: (j, 0, 0)),
        ],
        out_specs=tok(D_MODEL),
        out_shape=jax.ShapeDtypeStruct(x.shape, F32),
        scratch_shapes=[
            pltpu.VMEM((CTX_PER_STEP, HEADS, CTX_S, MLA_QK_PAD), BF16),
            pltpu.VMEM((CTX_PER_STEP, HEADS, CTX_S, MLA_DV), BF16),
        ],
        compiler_params=pltpu.CompilerParams(
            dimension_semantics=("arbitrary",), vmem_limit_bytes=VMEM_LIMIT),
        name="attn_mla_ctx",
    )(q, st_ckv, st_kpe, gate, x, mods, g_post, w_out, w_kvb)


def _attn_mla_lat_kernel(q_ref, ckv_ref, kpe_ref, cckv_ref, ckpe_ref, gate_ref, x_ref, mods_ref,
                         g_post_ref, w_out_ref, w_kvb_ref, o_ref, kf_ref, vf_ref):
    @pl.when(pl.program_id(1) == 0)
    def _():
        _mla_expand(ckv_ref[...], kpe_ref[...], w_kvb_ref, kf_ref, vf_ref, 0, LAT_S)
        _mla_expand(cckv_ref[...].astype(BF16), ckpe_ref[...].astype(BF16), w_kvb_ref, kf_ref, vf_ref,
                    LAT_S, PAST)

    heads = _mla_heads(q_ref, kf_ref, vf_ref, gate_ref)
    _finish(heads, w_out_ref, x_ref, mods_ref, g_post_ref, 1 + pl.program_id(0), o_ref)


def _attn_mla_lat(q, ckv, kpe, cache_ckv, cache_kpe, gate, x, mods, g_post, w_out, wo, w_kvb, layer, j):
    sk = LAT_S + PAST
    nq = LAT_S // TQ
    tok = lambda w: pl.BlockSpec((TQ, w), lambda b, i: (b * nq + i, 0))
    req = lambda w: pl.BlockSpec((LAT_S, w), lambda b, i: (b, 0))
    return pl.pallas_call(
        _attn_mla_lat_kernel,
        grid=(N_LAT_B, nq),
        in_specs=[
            tok(HEADS * MLA_QK_PAD), req(MLA_KV_LORA), req(MLA_ROPE),
            pl.BlockSpec((None, None, PAST, MLA_KV_LORA), lambda b, i: (b, j, 0, 0)),
            pl.BlockSpec((None, None, PAST, MLA_ROPE), lambda b, i: (b, j, 0, 0)),
            tok(D_MODEL), tok(D_MODEL),
            pl.BlockSpec((None, COND_ROWS, 3 * D_MODEL), lambda b, i: (layer, 0, 0)),
            pl.BlockSpec((None, 1, D_MODEL), lambda b, i: (layer, 0, 0)),
            pl.BlockSpec((None, D_MODEL, D_MODEL), lambda b, i: (wo, 0, 0)),
            pl.BlockSpec((None, MLA_KV_LORA, HEADS * (MLA_NOPE + MLA_DV)), lambda b, i: (j, 0, 0)),
        ],
        out_specs=tok(D_MODEL),
        out_shape=jax.ShapeDtypeStruct(x.shape, F32),
        scratch_shapes=[
            pltpu.VMEM((HEADS, sk, MLA_QK_PAD), BF16),
            pltpu.VMEM((HEADS, sk, MLA_DV), BF16),
        ],
        compiler_params=pltpu.CompilerParams(
            dimension_semantics=("arbitrary", "arbitrary"), vmem_limit_bytes=VMEM_LIMIT),
        name="attn_mla_lat",
    )(q, ckv, kpe, cache_ckv, cache_kpe, gate, x, mods, g_post, w_out, w_kvb)


def _rope_tables():
    rows = LAT_S // GRID_W
    row = jnp.repeat(jnp.arange(rows), GRID_W).astype(F32)
    col = jnp.tile(jnp.arange(GRID_W), rows).astype(F32)
    n_freq = DA_DH // 4
    inv = ROPE_THETA ** (-jnp.arange(n_freq, dtype=F32) / n_freq)
    ang = jnp.concatenate([row[:, None] * inv, col[:, None] * inv], axis=-1)
    cos, sin = jnp.cos(ang), jnp.sin(ang)
    return (jnp.concatenate([cos, cos, cos, cos], axis=-1),
            jnp.concatenate([-sin, sin, -sin, sin], axis=-1))


def kernel(x_prompt, x_sample, cache_diff_k, cache_diff_v, cache_mla_ckv, cache_mla_kpe,
           c, c_ctx, w_ada, b_ada, g_pre, g_post, w_out,
           da_w_in, da_lam_q1, da_lam_k1, da_lam_q2, da_lam_k2, da_g_sub,
           mla_w_in, mla_g_qa, mla_w_qb, mla_g_kva, mla_w_kvb):
    assert DA_DH == MLA_ROPE
    n_diff = (DEPTH + 1) // 2
    n_mla = DEPTH // 2

    cond = jnp.concatenate(
        [c_ctx[None, :], c, jnp.zeros((COND_ROWS - 1 - N_LAT_B, D_MODEL), F32)], axis=0)
    mods = _ada_call(cond, w_ada, b_ada)

    cos_t, sin_t = _rope_tables()
    g_pre3 = g_pre.reshape(DEPTH, 1, D_MODEL)
    g_post3 = g_post.reshape(DEPTH, 1, D_MODEL)
    steps = N_CTX_B // CTX_PER_STEP
    w_out_first = w_out[0:2].astype(BF16)
    da_w_in_first = da_w_in[0:1].astype(BF16)
    g_sub3 = da_g_sub.reshape(n_diff, 1, DA_DV)
    lam_p = jnp.stack([da_lam_q1, da_lam_k1, da_lam_q2, da_lam_k2], axis=1)
    cache_dkt = jnp.transpose(cache_diff_k, (0, 1, 3, 4, 5, 2)).reshape(N_LAT_B, n_diff, D_MODEL, PAST)

    mla_w_in_t = jnp.swapaxes(mla_w_in, 1, 2)
    mla_w_in_first = mla_w_in_t[0:1].astype(BF16)
    w_qb4 = mla_w_qb.reshape(n_mla, MLA_Q_LORA, HEADS, MLA_NOPE + MLA_ROPE)
    w_qb_b = jnp.pad(w_qb4, ((0, 0), (0, 0), (0, 0), (0, MLA_QK_PAD - MLA_NOPE - MLA_ROPE))).reshape(
        n_mla, MLA_Q_LORA, HEADS * MLA_QK_PAD).astype(BF16)
    w_kvb_b = mla_w_kvb.astype(BF16)
    g_qa3 = mla_g_qa.reshape(n_mla, 1, MLA_Q_LORA)
    g_kva3 = mla_g_kva.reshape(n_mla, 1, MLA_KV_LORA)

    x_ctx = x_prompt.reshape(N_CTX_B * CTX_S, D_MODEL)
    x_lat = x_sample.reshape(N_LAT_B * LAT_S, D_MODEL)
    st_diff = None
    st_mla = None

    assert DEPTH == 4
    da_w = [(da_w_in_first, 0), None]
    mla_w = [(mla_w_in_first, 0), None]
    w_o = [(w_out_first, 0), (w_out_first, 1), None, None]
    wide = D_MODEL * 4 // steps
    da_cast = [(da_w_in, (None, D_MODEL, wide), lambda b: (1, 0, b),
                (1, D_MODEL, 4 * D_MODEL), lambda b: (0, 0, b))]
    cols = D_MODEL // steps
    per_layer = steps // 2
    late_casts = [(mla_w_in_t, (None, MLA_IN, cols), lambda t: (1, 0, t),
                   (1, MLA_IN, D_MODEL), lambda t: (0, 0, t)),
                  (w_out, (None, D_MODEL // per_layer, D_MODEL), lambda t: (2 + t // per_layer, t % per_layer, 0),
                   (2, D_MODEL, D_MODEL), lambda t: (t // per_layer, t % per_layer, 0))]

    for i in range(DEPTH):
        j = i // 2
        if i % 2 == 0:
            q_c, k_c, vt_c, gate_c, st_dkt, st_dv = _pre_diff_ctx(
                x_ctx, mods, g_pre3, *da_w[j], st_diff, i, j, n_diff)
            st_diff = (st_dkt, st_dv)
            q_l, k_l, vt_l, gate_l = _pre_diff_lat(x_lat, mods, g_pre3, *da_w[j], cos_t, sin_t, i, j)
            outs = _attn_diff_ctx(q_c, k_c, vt_c, gate_c, x_ctx, mods, g_post3, *w_o[i],
                                  g_sub3, lam_p, i, j, cast_jobs=da_cast if i == 0 else ())
            x_ctx = outs[0]
            if i == 0:
                da_w[1] = (outs[1], 0)
            x_lat = _attn_diff_lat(q_l, k_l, vt_l, cache_dkt, cache_diff_v, gate_l, x_lat, mods, g_post3,
                                   *w_o[i], g_sub3, lam_p, i, j)
        else:
            outs = _pre_mla_ctx(x_ctx, mods, g_pre3, *mla_w[j], g_qa3, w_qb_b, g_kva3, st_mla, i, j, n_mla,
                                cast_jobs=late_casts if i == 1 else ())
            q_c, gate_c, st_ckv, st_kpe = outs[0:4]
            if i == 1:
                mla_w[1] = (outs[4], 0)
                w_o[2], w_o[3] = (outs[5], 0), (outs[5], 1)
            st_mla = (st_ckv, st_kpe)
            q_l, gate_l, ckv_l, kpe_l = _pre_mla_lat(
                x_lat, mods, g_pre3, *mla_w[j], g_qa3, w_qb_b, g_kva3, cos_t, sin_t, i, j)
            x_ctx = _attn_mla_ctx(q_c, st_ckv, st_kpe, gate_c, x_ctx, mods, g_post3, *w_o[i],
                                  w_kvb_b, i, j)
            x_lat = _attn_mla_lat(q_l, ckv_l, kpe_l, cache_mla_ckv, cache_mla_kpe, gate_l, x_lat,
                                  mods, g_post3, *w_o[i], w_kvb_b, i, j)

    return (x_ctx.reshape(N_CTX_B, CTX_S, D_MODEL),
            x_lat.reshape(N_LAT_B, LAT_S, D_MODEL),
            jnp.transpose(st_dkt.reshape(N_CTX_B, n_diff, HEADS, 2, DA_DH, CTX_S), (0, 1, 5, 2, 3, 4)),
            st_dv,
            st_ckv,
            st_kpe)
```

```python
import functools
import math

import jax
import jax.numpy as jnp
from jax import lax
from jax.experimental import pallas as pl
from jax.experimental.pallas import tpu as pltpu

F32 = jnp.float32
BF16 = jnp.bfloat16

D_MODEL = 1024
DEPTH = 4
N_CTX_B = 16
CTX_S = 256
N_LAT_B = 4
LAT_S = 1024
PAST = 256
GRID_W = 64
HEADS = 8
DA_DH = 64
DA_DV = 128
MLA_Q_LORA = 384
MLA_KV_LORA = 256
MLA_NOPE = 128
MLA_ROPE = 64
MLA_DV = 128
MLA_QK_PAD = 256
ROPE_THETA = 10000.0
NORM_EPS = 1e-6
COND_ROWS = 8

LANES = 128
V_PAD = 16
TQ = 256
TM = 512
MIB = 1024 * 1024
VMEM_MIB = {
    "ada_mod": 28, "pre_diff_ctx": 42, "pre_diff_lat": 28, "attn_diff_ctx": 30, "attn_diff_lat": 38,
    "pre_mla_ctx": 30, "pre_mla_lat": 24, "attn_mla_ctx": 28, "attn_mla_lat": 32,
}

_TRANS_B = (((1,), (1,)), ((), ()))
SCORES_AHEAD = 2
CTX_PER_STEP = 2
DA_Q_SCALE = DA_DH ** -0.5 * math.log2(math.e)
MLA_Q_SCALE = (MLA_NOPE + MLA_ROPE) ** -0.5 * math.log2(math.e)


def _silu(x):
    return x * (1.0 / (1.0 + jnp.exp(-x)))


def _rms(x, g):
    r = lax.rsqrt(jnp.mean(x * x, axis=-1, keepdims=True) + NORM_EPS)
    return (x * r) * g


def _rope_slab(x, cos, sin_signed):
    lane = lax.broadcasted_iota(jnp.int32, x.shape, 1)
    first_half = (lane % 64) < 32
    partner = jnp.where(first_half, pltpu.roll(x, 96, axis=1), pltpu.roll(x, 32, axis=1))
    return x * cos + partner * sin_signed


def _ones_rows(cols):
    row = lax.broadcasted_iota(jnp.int32, (V_PAD, cols), 0)
    return jnp.where(row == 0, 1.0, 0.0).astype(BF16)


def _cond_row(mods_ref, row):
    m = mods_ref[pl.ds(row, 1), :]
    return m[:, 0:D_MODEL], m[:, D_MODEL:2 * D_MODEL], m[:, 2 * D_MODEL:3 * D_MODEL]


def _modulated(x_ref, mods_ref, g_ref, row):
    shift, scale, _ = _cond_row(mods_ref, row)
    h = _rms(x_ref[...], g_ref[...]) * (1.0 + scale) + shift
    return h.astype(BF16)


def _ada_kernel(cond_ref, w_ref, b_ref, o_ref):
    a = _silu(cond_ref[...]).astype(BF16)
    o_ref[...] = jnp.dot(a, w_ref[...].astype(BF16), preferred_element_type=F32) + b_ref[...]


def _ada_call(cond, w_ada, b_ada):
    tn = 3 * D_MODEL
    return pl.pallas_call(
        _ada_kernel,
        grid=(DEPTH, 3 * D_MODEL // tn),
        in_specs=[
            pl.BlockSpec((COND_ROWS, D_MODEL), lambda i, n: (0, 0)),
            pl.BlockSpec((None, D_MODEL, tn), lambda i, n: (i, 0, n)),
            pl.BlockSpec((None, 1, tn), lambda i, n: (i, 0, n)),
        ],
        out_specs=pl.BlockSpec((None, COND_ROWS, tn), lambda i, n: (i, 0, n)),
        out_shape=jax.ShapeDtypeStruct((DEPTH, COND_ROWS, 3 * D_MODEL), F32),
        compiler_params=pltpu.CompilerParams(
            dimension_semantics=("arbitrary", "arbitrary"), vmem_limit_bytes=VMEM_MIB["ada_mod"] * MIB),
        name="ada_mod",
    )(cond, w_ada, b_ada.reshape(DEPTH, 1, 3 * D_MODEL))


def _cast_job_specs(jobs):
    in_specs = [pl.BlockSpec(blk, src_map) for _, blk, src_map, _, _ in jobs]
    out_specs = [pl.BlockSpec(blk, dst_map) for _, blk, _, _, dst_map in jobs]
    out_shape = [jax.ShapeDtypeStruct(shape, BF16) for _, _, _, shape, _ in jobs]
    return in_specs, out_specs, out_shape, [a for a, _, _, _, _ in jobs]


def _run_cast_jobs(src_refs, dst_refs):
    for src, dst in zip(src_refs, dst_refs):
        dst[...] = src[...].astype(BF16)


def _write_state(ref, r, j, value):
    if j is None:
        ref[r] = value
    else:
        for jj in range(ref.shape[1]):
            ref[r, jj] = value if jj == j else jnp.zeros_like(value)


def _pre_diff_ctx_kernel(create_j, x_ref, mods_ref, g_ref, w_ref, *refs):
    q_ref, k_ref, vt_ref, gate_ref, skt_ref, sv_ref = refs[-6:]
    hb = _modulated(x_ref, mods_ref, g_ref, 0)
    n = D_MODEL
    q = jnp.dot(hb, w_ref[:, 0:n], preferred_element_type=F32)
    q_ref[...] = (q * DA_Q_SCALE).astype(BF16)
    k = jnp.dot(hb, w_ref[:, n:2 * n], preferred_element_type=F32)
    v = jnp.dot(hb, w_ref[:, 2 * n:3 * n], preferred_element_type=F32)
    k_ref[...] = k.astype(BF16)
    for r in range(CTX_PER_STEP):
        rows = slice(r * CTX_S, (r + 1) * CTX_S)
        vt = v[rows, :].T
        _write_state(skt_ref, r, create_j, k[rows, :].T)
        _write_state(sv_ref, r, create_j, v[rows, :].reshape(CTX_S, HEADS, DA_DV))
        vt_ref[r] = vt.astype(BF16)
    gate_ref[...] = _silu(jnp.dot(hb, w_ref[:, 3 * n:4 * n], preferred_element_type=F32)).astype(BF16)


def _state_plumbing(states, shapes, j, n_inputs, n_outputs):
    def index_map(shp, layer_index):
        return lambda t: (t, layer_index) + (0,) * (len(shp) - 2)

    if states is None:
        specs = [pl.BlockSpec((CTX_PER_STEP,) + shp[1:], index_map(shp, 0)) for shp in shapes]
        return [], specs, {}, ()
    specs = [pl.BlockSpec((CTX_PER_STEP, None) + shp[2:], index_map(shp, j)) for shp in shapes]
    in_specs = [pl.BlockSpec(memory_space=pl.ANY) for _ in shapes]
    aliases = {n_inputs + i: n_outputs + i for i in range(len(shapes))}
    return in_specs, specs, aliases, tuple(states)


def _pre_diff_ctx(x, mods, g_pre, w_in, wj, states, layer, j, n_layers):
    n_tok = N_CTX_B * CTX_S
    shapes = [(N_CTX_B, n_layers, D_MODEL, CTX_S), (N_CTX_B, n_layers, CTX_S, HEADS, DA_DV)]
    st_in_specs, st_out_specs, aliases, st_args = _state_plumbing(states, shapes, j, 4, 4)
    tok_spec = pl.BlockSpec((CTX_PER_STEP * CTX_S, D_MODEL), lambda t: (t, 0))
    return pl.pallas_call(
        functools.partial(_pre_diff_ctx_kernel, j if states is None else None),
        grid=(N_CTX_B // CTX_PER_STEP,),
        in_specs=[
            tok_spec,
            pl.BlockSpec((None, COND_ROWS, 3 * D_MODEL), lambda t: (layer, 0, 0)),
            pl.BlockSpec((None, 1, D_MODEL), lambda t: (layer, 0, 0)),
            pl.BlockSpec((None, D_MODEL, 4 * D_MODEL), lambda t: (wj, 0, 0)),
        ] + st_in_specs,
        out_specs=[
            tok_spec,
            tok_spec,
            pl.BlockSpec((CTX_PER_STEP, D_MODEL, CTX_S), lambda t: (t, 0, 0)),
            tok_spec,
        ] + st_out_specs,
        out_shape=[
            jax.ShapeDtypeStruct((n_tok, D_MODEL), BF16),
            jax.ShapeDtypeStruct((n_tok, D_MODEL), BF16),
            jax.ShapeDtypeStruct((N_CTX_B, D_MODEL, CTX_S), BF16),
            jax.ShapeDtypeStruct((n_tok, D_MODEL), BF16),
        ] + [jax.ShapeDtypeStruct(shp, F32) for shp in shapes],
        input_output_aliases=aliases,
        compiler_params=pltpu.CompilerParams(
            dimension_semantics=("arbitrary",), vmem_limit_bytes=VMEM_MIB["pre_diff_ctx"] * MIB),
        name="pre_diff_ctx",
    )(x, mods, g_pre, w_in, *st_args)


def _pre_diff_lat_kernel(x_ref, mods_ref, g_ref, w_ref, cos_ref, sin_ref,
                         q_ref, k_ref, vt_ref, gate_ref):
    row = 1 + pl.program_id(0) // (LAT_S // TM)
    hb = _modulated(x_ref, mods_ref, g_ref, row)
    cos = cos_ref[...]
    sin = sin_ref[...]
    n = D_MODEL
    q = jnp.dot(hb, w_ref[:, 0:n], preferred_element_type=F32)
    k = jnp.dot(hb, w_ref[:, n:2 * n], preferred_element_type=F32)
    cos_q = cos * DA_Q_SCALE
    sin_q = sin * DA_Q_SCALE
    slabs = [slice(h * LANES, (h + 1) * LANES) for h in range(HEADS)]
    q_ref[...] = jnp.concatenate(
        [_rope_slab(q[:, sl], cos_q, sin_q).astype(BF16) for sl in slabs], axis=1)
    k_ref[...] = jnp.concatenate(
        [_rope_slab(k[:, sl], cos, sin).astype(BF16) for sl in slabs], axis=1)
    vt_ref[...] = jnp.dot(hb, w_ref[:, 2 * n:3 * n], preferred_element_type=F32).T.astype(BF16)
    gate_ref[...] = _silu(jnp.dot(hb, w_ref[:, 3 * n:4 * n], preferred_element_type=F32)).astype(BF16)


def _pre_diff_lat(x, mods, g_pre, w_in, wj, cos_t, sin_t, layer, j):
    n_tok = N_LAT_B * LAT_S
    tiles_per_req = LAT_S // TM
    tok_spec = pl.BlockSpec((TM, D_MODEL), lambda t: (t, 0))
    rope_spec = pl.BlockSpec((TM, LANES), lambda t: (t % tiles_per_req, 0))
    return pl.pallas_call(
        _pre_diff_lat_kernel,
        grid=(n_tok // TM,),
        in_specs=[
            tok_spec,
            pl.BlockSpec((None, COND_ROWS, 3 * D_MODEL), lambda t: (layer, 0, 0)),
            pl.BlockSpec((None, 1, D_MODEL), lambda t: (layer, 0, 0)),
            pl.BlockSpec((None, D_MODEL, 4 * D_MODEL), lambda t: (wj, 0, 0)),
            rope_spec,
            rope_spec,
        ],
        out_specs=[
            tok_spec, tok_spec,
            pl.BlockSpec((None, D_MODEL, TM), lambda t: (t // tiles_per_req, 0, t % tiles_per_req)),
            tok_spec,
        ],
        out_shape=[
            jax.ShapeDtypeStruct((n_tok, D_MODEL), BF16),
            jax.ShapeDtypeStruct((n_tok, D_MODEL), BF16),
            jax.ShapeDtypeStruct((N_LAT_B, D_MODEL, LAT_S), BF16),
            jax.ShapeDtypeStruct((n_tok, D_MODEL), BF16),
        ],
        compiler_params=pltpu.CompilerParams(
            dimension_semantics=("arbitrary",), vmem_limit_bytes=VMEM_MIB["pre_diff_lat"] * MIB),
        name="pre_diff_lat",
    )(x, mods, g_pre, w_in, cos_t, sin_t)


def _finish(heads, w_out_ref, x_ref, mods_ref, g_post_ref, row, o_ref, rows=slice(None)):
    _, _, gate = _cond_row(mods_ref, row)
    og = jnp.concatenate(heads, axis=1)
    y = jnp.dot(og, w_out_ref[...], preferred_element_type=F32)
    o_ref[rows, :] = x_ref[rows, :] + gate * _rms(y, g_post_ref[...])


def _diff_lambda(lam_ref, lam_init):
    p = lam_ref[...]
    a = jnp.sum(p[0:1, :] * p[1:2, :], axis=-1, keepdims=True)
    b = jnp.sum(p[2:3, :] * p[3:4, :], axis=-1, keepdims=True)
    return jnp.exp(a) - jnp.exp(b) + lam_init


def _stack_diff_keys(k, k2_ref, sk):
    lane = lax.broadcasted_iota(jnp.int32, (sk, LANES), 1)
    zero = jnp.zeros((sk, LANES), BF16)
    for h in range(HEADS):
        kh = k[:, h * LANES:(h + 1) * LANES]
        k2_ref[h, 0:sk, :] = jnp.where(lane < DA_DH, kh, zero)
        k2_ref[h, sk:2 * sk, :] = jnp.where(lane >= DA_DH, kh, zero)


def _stack_values_t(vt, vt_ref, c0, cols):
    ones = _ones_rows(cols)
    for h in range(HEADS):
        vt_ref[h, 0:DA_DV, c0:c0 + cols] = vt[h * DA_DV:(h + 1) * DA_DV, :]
        vt_ref[h, DA_DV:DA_DV + V_PAD, c0:c0 + cols] = ones


def _diff_heads(q_ref, k2_ref, vt_ref, gate_ref, g_sub_ref, lam, lam_init, sk, rows=slice(None)):
    g_sub = g_sub_ref[...]

    def scores(h):
        return lax.dot_general(k2_ref[h], q_ref[rows, h * LANES:(h + 1) * LANES], _TRANS_B,
                               preferred_element_type=F32)

    heads = []
    pending = [scores(h) for h in range(SCORES_AHEAD)]
    for h in range(HEADS):
        lo, hi = h * LANES, (h + 1) * LANES
        s = pending.pop(0)
        if h + SCORES_AHEAD < HEADS:
            pending.append(scores(h + SCORES_AHEAD))
        s0 = s[0:sk, :]
        s1 = s[sk:2 * sk, :]
        e0 = jnp.exp2(s0 - jnp.max(s0, axis=0, keepdims=True)).astype(BF16)
        e1 = jnp.exp2(s1 - jnp.max(s1, axis=0, keepdims=True)).astype(BF16)
        pv0 = jnp.dot(vt_ref[h], e0, preferred_element_type=F32)
        pv1 = jnp.dot(vt_ref[h], e1, preferred_element_type=F32)
        a0 = 1.0 / pv0[DA_DV:DA_DV + 1, :]
        a1 = lam / pv1[DA_DV:DA_DV + 1, :]
        ot = pv0[0:DA_DV, :] * a0 - pv1[0:DA_DV, :] * a1
        ot = ot * lax.rsqrt(jnp.mean(ot * ot, axis=0, keepdims=True) + NORM_EPS)
        o = (ot.T * g_sub) * (1.0 - lam_init)
        heads.append((o * gate_ref[rows, lo:hi]).astype(BF16))
    return heads


def _attn_diff_ctx_kernel(lam_init, n_cast, q_ref, k_ref, vt_ref, gate_ref, x_ref, mods_ref, g_post_ref,
                          w_out_ref, g_sub_ref, lam_ref, *refs):
    o_ref = refs[n_cast]
    k2_ref, vb_ref = refs[-2:]
    _run_cast_jobs(refs[0:n_cast], refs[n_cast + 1:2 * n_cast + 1])
    lam = _diff_lambda(lam_ref, lam_init)
    for r in range(CTX_PER_STEP):
        rows = slice(r * CTX_S, (r + 1) * CTX_S)
        _stack_diff_keys(k_ref[rows, :], k2_ref.at[r], CTX_S)
        _stack_values_t(vt_ref[r], vb_ref.at[r], 0, CTX_S)
        heads = _diff_heads(q_ref, k2_ref.at[r], vb_ref.at[r], gate_ref, g_sub_ref, lam, lam_init,
                            CTX_S, rows)
        _finish(heads, w_out_ref, x_ref, mods_ref, g_post_ref, 0, o_ref, rows)


def _attn_diff_ctx(q, k, vt, gate, x, mods, g_post, w_out, wo, g_sub, lam_p, layer, j, cast_jobs=()):
    lam_init = 0.8 - 0.6 * math.exp(-0.3 * layer)
    tok_spec = pl.BlockSpec((CTX_PER_STEP * CTX_S, D_MODEL), lambda b: (b, 0))
    cj_in, cj_out, cj_shapes, cj_args = _cast_job_specs(cast_jobs)
    return pl.pallas_call(
        functools.partial(_attn_diff_ctx_kernel, lam_init, len(cast_jobs)),
        grid=(N_CTX_B // CTX_PER_STEP,),
        in_specs=[
            tok_spec, tok_spec, pl.BlockSpec((CTX_PER_STEP, D_MODEL, CTX_S), lambda b: (b, 0, 0)),
            tok_spec, tok_spec,
            pl.BlockSpec((None, COND_ROWS, 3 * D_MODEL), lambda b: (layer, 0, 0)),
            pl.BlockSpec((None, 1, D_MODEL), lambda b: (layer, 0, 0)),
            pl.BlockSpec((None, D_MODEL, D_MODEL), lambda b: (wo, 0, 0)),
            pl.BlockSpec((None, 1, DA_DV), lambda b: (j, 0, 0)),
            pl.BlockSpec((None, 4, DA_DH), lambda b: (j, 0, 0)),
        ] + cj_in,
        out_specs=[tok_spec] + cj_out,
        out_shape=[jax.ShapeDtypeStruct(x.shape, F32)] + cj_shapes,
        scratch_shapes=[
            pltpu.VMEM((CTX_PER_STEP, HEADS, 2 * CTX_S, LANES), BF16),
            pltpu.VMEM((CTX_PER_STEP, HEADS, DA_DV + V_PAD, CTX_S), BF16),
        ],
        compiler_params=pltpu.CompilerParams(
            dimension_semantics=("arbitrary",), vmem_limit_bytes=VMEM_MIB["attn_diff_ctx"] * MIB),
        name="attn_diff_ctx",
    )(q, k, vt, gate, x, mods, g_post, w_out, g_sub, lam_p, *cj_args)


def _attn_diff_lat_kernel(lam_init, q_ref, k_ref, vt_ref, ckt_ref, cv_ref, gate_ref, x_ref, mods_ref,
                          g_post_ref, w_out_ref, g_sub_ref, lam_ref, o_ref, k2_ref, vb_ref):
    sk = LAT_S + PAST

    @pl.when(pl.program_id(1) == 0)
    def _():
        lane = lax.broadcasted_iota(jnp.int32, (PAST, LANES), 1)
        zero_l = jnp.zeros((LAT_S, LANES), BF16)
        zero_p = jnp.zeros((PAST, LANES), BF16)
        lane_l = lax.broadcasted_iota(jnp.int32, (LAT_S, LANES), 1)
        for h in range(HEADS):
            lo, hi = h * LANES, (h + 1) * LANES
            kh = k_ref[:, lo:hi]
            ch = ckt_ref[lo:hi, :].T.astype(BF16)
            k2_ref[h, 0:LAT_S, :] = jnp.where(lane_l < DA_DH, kh, zero_l)
            k2_ref[h, LAT_S:sk, :] = jnp.where(lane < DA_DH, ch, zero_p)
            k2_ref[h, sk:sk + LAT_S, :] = jnp.where(lane_l >= DA_DH, kh, zero_l)
            k2_ref[h, sk + LAT_S:2 * sk, :] = jnp.where(lane >= DA_DH, ch, zero_p)
            vb_ref[h, 0:DA_DV, LAT_S:sk] = cv_ref[:, h, :].T.astype(BF16)
        _stack_values_t(vt_ref[...], vb_ref, 0, LAT_S)
        ones = _ones_rows(PAST)
        for h in range(HEADS):
            vb_ref[h, DA_DV:DA_DV + V_PAD, LAT_S:sk] = ones

    lam = _diff_lambda(lam_ref, lam_init)
    heads = _diff_heads(q_ref, k2_ref, vb_ref, gate_ref, g_sub_ref, lam, lam_init, sk)
    _finish(heads, w_out_ref, x_ref, mods_ref, g_post_ref, 1 + pl.program_id(0), o_ref)


def _attn_diff_lat(q, k, vt, cache_kt, cache_v, gate, x, mods, g_post, w_out, wo, g_sub, lam_p, layer, j):
    lam_init = 0.8 - 0.6 * math.exp(-0.3 * layer)
    sk = LAT_S + PAST
    nq = LAT_S // TQ
    tok_spec = pl.BlockSpec((TQ, D_MODEL), lambda b, i: (b * nq + i, 0))
    kv_spec = pl.BlockSpec((LAT_S, D_MODEL), lambda b, i: (b, 0))
    cache_kt_spec = pl.BlockSpec((None, None, D_MODEL, PAST), lambda b, i: (b, j, 0, 0))
    cache_v_spec = pl.BlockSpec((None, None, PAST, HEADS, DA_DV), lambda b, i: (b, j, 0, 0, 0))
    return pl.pallas_call(
        functools.partial(_attn_diff_lat_kernel, lam_init),
        grid=(N_LAT_B, nq),
        in_specs=[
            tok_spec, kv_spec, pl.BlockSpec((None, D_MODEL, LAT_S), lambda b, i: (b, 0, 0)),
            cache_kt_spec, cache_v_spec, tok_spec, tok_spec,
            pl.BlockSpec((None, COND_ROWS, 3 * D_MODEL), lambda b, i: (layer, 0, 0)),
            pl.BlockSpec((None, 1, D_MODEL), lambda b, i: (layer, 0, 0)),
            pl.BlockSpec((None, D_MODEL, D_MODEL), lambda b, i: (wo, 0, 0)),
            pl.BlockSpec((None, 1, DA_DV), lambda b, i: (j, 0, 0)),
            pl.BlockSpec((None, 4, DA_DH), lambda b, i: (j, 0, 0)),
        ],
        out_specs=tok_spec,
        out_shape=jax.ShapeDtypeStruct(x.shape, F32),
        scratch_shapes=[
            pltpu.VMEM((HEADS, 2 * sk, LANES), BF16),
            pltpu.VMEM((HEADS, DA_DV + V_PAD, sk), BF16),
        ],
        compiler_params=pltpu.CompilerParams(
            dimension_semantics=("arbitrary", "arbitrary"), vmem_limit_bytes=VMEM_MIB["attn_diff_lat"] * MIB),
        name="attn_diff_lat",
    )(q, k, vt, cache_kt, cache_v, gate, x, mods, g_post, w_out, g_sub, lam_p)


MLA_IN = MLA_Q_LORA + MLA_KV_LORA + MLA_ROPE + D_MODEL
_QA = slice(0, MLA_Q_LORA)
_KV = slice(MLA_Q_LORA, MLA_Q_LORA + MLA_KV_LORA)
_PE = slice(_KV.stop, _KV.stop + LANES)
_GT = slice(_KV.stop + MLA_ROPE, MLA_IN)


def _proj_t(hb, w_t_ref, rows):
    return lax.dot_general(hb, w_t_ref[rows, :], _TRANS_B, preferred_element_type=F32)


def _mla_queries(hb, w_in_ref, g_qa_ref):
    q_a = _proj_t(hb, w_in_ref, _QA)
    return _rms(q_a, g_qa_ref[...]).astype(BF16)


def _pre_mla_ctx_kernel(create_j, n_cast, x_ref, mods_ref, g_ref, w_in_ref, g_qa_ref, w_qb_ref, g_kva_ref,
                        *refs):
    n_out = 4 + n_cast
    q_ref, gate_ref, ckv_ref, kpe_ref = refs[len(refs) - n_out:len(refs) - n_cast]
    _run_cast_jobs(refs[len(refs) - n_out - n_cast:len(refs) - n_out], refs[len(refs) - n_cast:])
    hb = _modulated(x_ref, mods_ref, g_ref, 0)
    qn = _mla_queries(hb, w_in_ref, g_qa_ref)
    q_ref[...] = (jnp.dot(qn, w_qb_ref[...], preferred_element_type=F32) * MLA_Q_SCALE).astype(BF16)
    kv_a = _proj_t(hb, w_in_ref, _KV)
    ckv = _rms(kv_a, g_kva_ref[...])
    gate_ref[...] = _silu(_proj_t(hb, w_in_ref, _GT)).astype(BF16)
    pe = _proj_t(hb, w_in_ref, _PE)
    for r in range(CTX_PER_STEP):
        rows = slice(r * CTX_S, (r + 1) * CTX_S)
        _write_state(ckv_ref, r, create_j, ckv[rows, :])
        _write_state(kpe_ref, r, create_j, pe[rows, 0:MLA_ROPE])


def _pre_mla_ctx(x, mods, g_pre, w_in, wj, g_qa, w_qb, g_kva, states, layer, j, n_layers, cast_jobs=()):
    n_tok = N_CTX_B * CTX_S
    tok = lambda w: pl.BlockSpec((CTX_PER_STEP * CTX_S, w), lambda t: (t, 0))
    shapes = [(N_CTX_B, n_layers, CTX_S, MLA_KV_LORA), (N_CTX_B, n_layers, CTX_S, MLA_ROPE)]
    st_in_specs, st_out_specs, aliases, st_args = _state_plumbing(states, shapes, j, 7, 2)
    cj_in, cj_out, cj_shapes, cj_args = _cast_job_specs(cast_jobs)
    return pl.pallas_call(
        functools.partial(_pre_mla_ctx_kernel, j if states is None else None, len(cast_jobs)),
        grid=(N_CTX_B // CTX_PER_STEP,),
        in_specs=[
            tok(D_MODEL),
            pl.BlockSpec((None, COND_ROWS, 3 * D_MODEL), lambda t: (layer, 0, 0)),
            pl.BlockSpec((None, 1, D_MODEL), lambda t: (layer, 0, 0)),
            pl.BlockSpec((None, MLA_IN, D_MODEL), lambda t: (wj, 0, 0)),
            pl.BlockSpec((None, 1, MLA_Q_LORA), lambda t: (j, 0, 0)),
            pl.BlockSpec((None, MLA_Q_LORA, HEADS * MLA_QK_PAD), lambda t: (j, 0, 0)),
            pl.BlockSpec((None, 1, MLA_KV_LORA), lambda t: (j, 0, 0)),
        ] + st_in_specs + cj_in,
        out_specs=[tok(HEADS * MLA_QK_PAD), tok(D_MODEL)] + st_out_specs + cj_out,
        out_shape=[
            jax.ShapeDtypeStruct((n_tok, HEADS * MLA_QK_PAD), BF16),
            jax.ShapeDtypeStruct((n_tok, D_MODEL), BF16),
        ] + [jax.ShapeDtypeStruct(shp, F32) for shp in shapes] + cj_shapes,
        input_output_aliases=aliases,
        compiler_params=pltpu.CompilerParams(
            dimension_semantics=("arbitrary",), vmem_limit_bytes=VMEM_MIB["pre_mla_ctx"] * MIB),
        name="pre_mla_ctx",
    )(x, mods, g_pre, w_in, g_qa, w_qb, g_kva, *st_args, *cj_args)


def _pre_mla_lat_kernel(x_ref, mods_ref, g_ref, w_in_ref, g_qa_ref, w_qb_ref, g_kva_ref, cos_ref, sin_ref,
                        q_ref, gate_ref, ckv_ref, kpe_ref):
    row = 1 + pl.program_id(0) // (LAT_S // TM)
    hb = _modulated(x_ref, mods_ref, g_ref, row)
    cos = cos_ref[...]
    sin = sin_ref[...]
    qn = _mla_queries(hb, w_in_ref, g_qa_ref)
    for h in range(HEADS):
        lo = h * MLA_QK_PAD
        q = jnp.dot(qn, w_qb_ref[:, lo:lo + MLA_QK_PAD], preferred_element_type=F32)
        q = q * MLA_Q_SCALE
        q_ref[:, lo:lo + LANES] = q[:, 0:LANES].astype(BF16)
        q_ref[:, lo + LANES:lo + 2 * LANES] = _rope_slab(q[:, LANES:2 * LANES], cos, sin).astype(BF16)
    kv_a = _proj_t(hb, w_in_ref, _KV)
    ckv_ref[...] = _rms(kv_a, g_kva_ref[...]).astype(BF16)
    gate_ref[...] = _silu(_proj_t(hb, w_in_ref, _GT)).astype(BF16)
    pe = _proj_t(hb, w_in_ref, _PE)
    kpe_ref[...] = _rope_slab(pe, cos, sin)[:, 0:MLA_ROPE].astype(BF16)


def _pre_mla_lat(x, mods, g_pre, w_in, wj, g_qa, w_qb, g_kva, cos_t, sin_t, layer, j):
    n_tok = N_LAT_B * LAT_S
    tiles_per_req = LAT_S // TM
    tok = lambda w: pl.BlockSpec((TM, w), lambda t: (t, 0))
    rope_spec = pl.BlockSpec((TM, LANES), lambda t: (t % tiles_per_req, 0))
    return pl.pallas_call(
        _pre_mla_lat_kernel,
        grid=(n_tok // TM,),
        in_specs=[
            tok(D_MODEL),
            pl.BlockSpec((None, COND_ROWS, 3 * D_MODEL), lambda t: (layer, 0, 0)),
            pl.BlockSpec((None, 1, D_MODEL), lambda t: (layer, 0, 0)),
            pl.BlockSpec((None, MLA_IN, D_MODEL), lambda t: (wj, 0, 0)),
            pl.BlockSpec((None, 1, MLA_Q_LORA), lambda t: (j, 0, 0)),
            pl.BlockSpec((None, MLA_Q_LORA, HEADS * MLA_QK_PAD), lambda t: (j, 0, 0)),
            pl.BlockSpec((None, 1, MLA_KV_LORA), lambda t: (j, 0, 0)),
            rope_spec,
            rope_spec,
        ],
        out_specs=[tok(HEADS * MLA_QK_PAD), tok(D_MODEL), tok(MLA_KV_LORA), tok(MLA_ROPE)],
        out_shape=[
            jax.ShapeDtypeStruct((n_tok, HEADS * MLA_QK_PAD), BF16),
            jax.ShapeDtypeStruct((n_tok, D_MODEL), BF16),
            jax.ShapeDtypeStruct((n_tok, MLA_KV_LORA), BF16),
            jax.ShapeDtypeStruct((n_tok, MLA_ROPE), BF16),
        ],
        compiler_params=pltpu.CompilerParams(
            dimension_semantics=("arbitrary",), vmem_limit_bytes=VMEM_MIB["pre_mla_lat"] * MIB),
        name="pre_mla_lat",
    )(x, mods, g_pre, w_in, g_qa, w_qb, g_kva, cos_t, sin_t)


def _mla_expand(ckv, kpe, w_kvb_ref, kf_ref, vf_ref, r0, rows):
    zero = jnp.zeros((rows, LANES - MLA_ROPE), BF16)
    for h in range(HEADS):
        lo = h * (MLA_NOPE + MLA_DV)
        kv = jnp.dot(ckv, w_kvb_ref[:, lo:lo + MLA_NOPE + MLA_DV], preferred_element_type=F32)
        kf_ref[h, r0:r0 + rows, 0:MLA_NOPE] = kv[:, 0:MLA_NOPE].astype(BF16)
        kf_ref[h, r0:r0 + rows, MLA_NOPE:MLA_NOPE + MLA_ROPE] = kpe
        kf_ref[h, r0:r0 + rows, MLA_NOPE + MLA_ROPE:MLA_QK_PAD] = zero
        vf_ref[h, r0:r0 + rows, :] = kv[:, MLA_NOPE:MLA_NOPE + MLA_DV].astype(BF16)


def _mla_heads(q_ref, kf_ref, vf_ref, gate_ref, rows=slice(None)):
    def scores(h):
        return lax.dot_general(q_ref[rows, h * MLA_QK_PAD:(h + 1) * MLA_QK_PAD], kf_ref[h], _TRANS_B,
                               preferred_element_type=F32)

    heads = []
    pending = [scores(h) for h in range(SCORES_AHEAD)]
    for h in range(HEADS):
        s = pending.pop(0)
        if h + SCORES_AHEAD < HEADS:
            pending.append(scores(h + SCORES_AHEAD))
        e = jnp.exp2(s - jnp.max(s, axis=-1, keepdims=True))
        inv = 1.0 / jnp.sum(e, axis=-1, keepdims=True)
        o = jnp.dot(e.astype(BF16), vf_ref[h], preferred_element_type=F32) * inv
        lo, hi = h * LANES, (h + 1) * LANES
        heads.append((o * gate_ref[rows, lo:hi]).astype(BF16))
    return heads


def _attn_mla_ctx_kernel(q_ref, ckv_ref, kpe_ref, gate_ref, x_ref, mods_ref, g_post_ref, w_out_ref,
                         w_kvb_ref, o_ref, kf_ref, vf_ref):
    for r in range(CTX_PER_STEP):
        rows = slice(r * CTX_S, (r + 1) * CTX_S)
        _mla_expand(ckv_ref[r].astype(BF16), kpe_ref[r].astype(BF16), w_kvb_ref, kf_ref.at[r], vf_ref.at[r],
                    0, CTX_S)
        heads = _mla_heads(q_ref, kf_ref.at[r], vf_ref.at[r], gate_ref, rows)
        _finish(heads, w_out_ref, x_ref, mods_ref, g_post_ref, 0, o_ref, rows)


def _attn_mla_ctx(q, st_ckv, st_kpe, gate, x, mods, g_post, w_out, wo, w_kvb, layer, j):
    tok = lambda w: pl.BlockSpec((CTX_PER_STEP * CTX_S, w), lambda b: (b, 0))
    return pl.pallas_call(
        _attn_mla_ctx_kernel,
        grid=(N_CTX_B // CTX_PER_STEP,),
        in_specs=[
            tok(HEADS * MLA_QK_PAD),
            pl.BlockSpec((CTX_PER_STEP, None, CTX_S, MLA_KV_LORA), lambda b: (b, j, 0, 0)),
            pl.BlockSpec((CTX_PER_STEP, None, CTX_S, MLA_ROPE), lambda b: (b, j, 0, 0)),
            tok(D_MODEL), tok(D_MODEL),
            pl.BlockSpec((None, COND_ROWS, 3 * D_MODEL), lambda b: (layer, 0, 0)),
            pl.BlockSpec((None, 1, D_MODEL), lambda b: (layer, 0, 0)),
            pl.BlockSpec((None, D_MODEL, D_MODEL), lambda b: (wo, 0, 0)),
            pl.BlockSpec((None, MLA_KV_LORA, HEADS * (MLA_NOPE + MLA_DV)), lambda b: (j, 0, 0)),
        ],
        out_specs=tok(D_MODEL),
        out_shape=jax.ShapeDtypeStruct(x.shape, F32),
        scratch_shapes=[
            pltpu.VMEM((CTX_PER_STEP, HEADS, CTX_S, MLA_QK_PAD), BF16),
            pltpu.VMEM((CTX_PER_STEP, HEADS, CTX_S, MLA_DV), BF16),
        ],
        compiler_params=pltpu.CompilerParams(
            dimension_semantics=("arbitrary",), vmem_limit_bytes=VMEM_MIB["attn_mla_ctx"] * MIB),
        name="attn_mla_ctx",
    )(q, st_ckv, st_kpe, gate, x, mods, g_post, w_out, w_kvb)


def _attn_mla_lat_kernel(q_ref, ckv_ref, kpe_ref, cckv_ref, ckpe_ref, gate_ref, x_ref, mods_ref,
                         g_post_ref, w_out_ref, w_kvb_ref, o_ref, kf_ref, vf_ref):
    @pl.when(pl.program_id(1) == 0)
    def _():
        _mla_expand(ckv_ref[...], kpe_ref[...], w_kvb_ref, kf_ref, vf_ref, 0, LAT_S)
        _mla_expand(cckv_ref[...].astype(BF16), ckpe_ref[...].astype(BF16), w_kvb_ref, kf_ref, vf_ref,
                    LAT_S, PAST)

    heads = _mla_heads(q_ref, kf_ref, vf_ref, gate_ref)
    _finish(heads, w_out_ref, x_ref, mods_ref, g_post_ref, 1 + pl.program_id(0), o_ref)


def _attn_mla_lat(q, ckv, kpe, cache_ckv, cache_kpe, gate, x, mods, g_post, w_out, wo, w_kvb, layer, j):
    sk = LAT_S + PAST
    nq = LAT_S // TQ
    tok = lambda w: pl.BlockSpec((TQ, w), lambda b, i: (b * nq + i, 0))
    req = lambda w: pl.BlockSpec((LAT_S, w), lambda b, i: (b, 0))
    return pl.pallas_call(
        _attn_mla_lat_kernel,
        grid=(N_LAT_B, nq),
        in_specs=[
            tok(HEADS * MLA_QK_PAD), req(MLA_KV_LORA), req(MLA_ROPE),
            pl.BlockSpec((None, None, PAST, MLA_KV_LORA), lambda b, i: (b, j, 0, 0)),
            pl.BlockSpec((None, None, PAST, MLA_ROPE), lambda b, i: (b, j, 0, 0)),
            tok(D_MODEL), tok(D_MODEL),
            pl.BlockSpec((None, COND_ROWS, 3 * D_MODEL), lambda b, i: (layer, 0, 0)),
            pl.BlockSpec((None, 1, D_MODEL), lambda b, i: (layer, 0, 0)),
            pl.BlockSpec((None, D_MODEL, D_MODEL), lambda b, i: (wo, 0, 0)),
            pl.BlockSpec((None, MLA_KV_LORA, HEADS * (MLA_NOPE + MLA_DV)), lambda b, i: (j, 0, 0)),
        ],
        out_specs=tok(D_MODEL),
        out_shape=jax.ShapeDtypeStruct(x.shape, F32),
        scratch_shapes=[
            pltpu.VMEM((HEADS, sk, MLA_QK_PAD), BF16),
            pltpu.VMEM((HEADS, sk, MLA_DV), BF16),
        ],
        compiler_params=pltpu.CompilerParams(
            dimension_semantics=("arbitrary", "arbitrary"), vmem_limit_bytes=VMEM_MIB["attn_mla_lat"] * MIB),
        name="attn_mla_lat",
    )(q, ckv, kpe, cache_ckv, cache_kpe, gate, x, mods, g_post, w_out, w_kvb)


def _rope_tables():
    rows = LAT_S // GRID_W
    row = jnp.repeat(jnp.arange(rows), GRID_W).astype(F32)
    col = jnp.tile(jnp.arange(GRID_W), rows).astype(F32)
    n_freq = DA_DH // 4
    inv = ROPE_THETA ** (-jnp.arange(n_freq, dtype=F32) / n_freq)
    ang = jnp.concatenate([row[:, None] * inv, col[:, None] * inv], axis=-1)
    cos, sin = jnp.cos(ang), jnp.sin(ang)
    return (jnp.concatenate([cos, cos, cos, cos], axis=-1),
            jnp.concatenate([-sin, sin, -sin, sin], axis=-1))


def kernel(x_prompt, x_sample, cache_diff_k, cache_diff_v, cache_mla_ckv, cache_mla_kpe,
           c, c_ctx, w_ada, b_ada, g_pre, g_post, w_out,
           da_w_in, da_lam_q1, da_lam_k1, da_lam_q2, da_lam_k2, da_g_sub,
           mla_w_in, mla_g_qa, mla_w_qb, mla_g_kva, mla_w_kvb):
    assert DA_DH == MLA_ROPE
    n_diff = (DEPTH + 1) // 2
    n_mla = DEPTH // 2

    cond = jnp.concatenate(
        [c_ctx[None, :], c, jnp.zeros((COND_ROWS - 1 - N_LAT_B, D_MODEL), F32)], axis=0)
    mods = _ada_call(cond, w_ada, b_ada)

    cos_t, sin_t = _rope_tables()
    g_pre3 = g_pre.reshape(DEPTH, 1, D_MODEL)
    g_post3 = g_post.reshape(DEPTH, 1, D_MODEL)
    steps = N_CTX_B // CTX_PER_STEP
    w_out_first = w_out[0:2].astype(BF16)
    da_w_in_first = da_w_in[0:1].astype(BF16)
    g_sub3 = da_g_sub.reshape(n_diff, 1, DA_DV)
    lam_p = jnp.stack([da_lam_q1, da_lam_k1, da_lam_q2, da_lam_k2], axis=1)
    cache_dkt = jnp.transpose(cache_diff_k, (0, 1, 3, 4, 5, 2)).reshape(N_LAT_B, n_diff, D_MODEL, PAST)

    mla_w_in_t = jnp.swapaxes(mla_w_in, 1, 2)
    mla_w_in_first = mla_w_in_t[0:1].astype(BF16)
    w_qb4 = mla_w_qb.reshape(n_mla, MLA_Q_LORA, HEADS, MLA_NOPE + MLA_ROPE)
    w_qb_b = jnp.pad(w_qb4, ((0, 0), (0, 0), (0, 0), (0, MLA_QK_PAD - MLA_NOPE - MLA_ROPE))).reshape(
        n_mla, MLA_Q_LORA, HEADS * MLA_QK_PAD).astype(BF16)
    w_kvb_b = mla_w_kvb.astype(BF16)
    g_qa3 = mla_g_qa.reshape(n_mla, 1, MLA_Q_LORA)
    g_kva3 = mla_g_kva.reshape(n_mla, 1, MLA_KV_LORA)

    x_ctx = x_prompt.reshape(N_CTX_B * CTX_S, D_MODEL)
    x_lat = x_sample.reshape(N_LAT_B * LAT_S, D_MODEL)
    st_diff = None
    st_mla = None

    assert DEPTH == 4
    da_w = [(da_w_in_first, 0), None]
    mla_w = [(mla_w_in_first, 0), None]
    w_o = [(w_out_first, 0), (w_out_first, 1), None, None]
    wide = D_MODEL * 4 // steps
    da_cast = [(da_w_in, (None, D_MODEL, wide), lambda b: (1, 0, b),
                (1, D_MODEL, 4 * D_MODEL), lambda b: (0, 0, b))]
    cols = D_MODEL // steps
    per_layer = steps // 2
    late_casts = [(mla_w_in_t, (None, MLA_IN, cols), lambda t: (1, 0, t),
                   (1, MLA_IN, D_MODEL), lambda t: (0, 0, t)),
                  (w_out, (None, D_MODEL // per_layer, D_MODEL), lambda t: (2 + t // per_layer, t % per_layer, 0),
                   (2, D_MODEL, D_MODEL), lambda t: (t // per_layer, t % per_layer, 0))]

    for i in range(DEPTH):
        j = i // 2
        if i % 2 == 0:
            q_c, k_c, vt_c, gate_c, st_dkt, st_dv = _pre_diff_ctx(
                x_ctx, mods, g_pre3, *da_w[j], st_diff, i, j, n_diff)
            st_diff = (st_dkt, st_dv)
            q_l, k_l, vt_l, gate_l = _pre_diff_lat(x_lat, mods, g_pre3, *da_w[j], cos_t, sin_t, i, j)
            outs = _attn_diff_ctx(q_c, k_c, vt_c, gate_c, x_ctx, mods, g_post3, *w_o[i],
                                  g_sub3, lam_p, i, j, cast_jobs=da_cast if i == 0 else ())
            x_ctx = outs[0]
            if i == 0:
                da_w[1] = (outs[1], 0)
            x_lat = _attn_diff_lat(q_l, k_l, vt_l, cache_dkt, cache_diff_v, gate_l, x_lat, mods, g_post3,
                                   *w_o[i], g_sub3, lam_p, i, j)
        else:
            outs = _pre_mla_ctx(x_ctx, mods, g_pre3, *mla_w[j], g_qa3, w_qb_b, g_kva3, st_mla, i, j, n_mla,
                                cast_jobs=late_casts if i == 1 else ())
            q_c, gate_c, st_ckv, st_kpe = outs[0:4]
            if i == 1:
                mla_w[1] = (outs[4], 0)
                w_o[2], w_o[3] = (outs[5], 0), (outs[5], 1)
            st_mla = (st_ckv, st_kpe)
            q_l, gate_l, ckv_l, kpe_l = _pre_mla_lat(
                x_lat, mods, g_pre3, *mla_w[j], g_qa3, w_qb_b, g_kva3, cos_t, sin_t, i, j)
            x_ctx = _attn_mla_ctx(q_c, st_ckv, st_kpe, gate_c, x_ctx, mods, g_post3, *w_o[i],
                                  w_kvb_b, i, j)
            x_lat = _attn_mla_lat(q_l, ckv_l, kpe_l, cache_mla_ckv, cache_mla_kpe, gate_l, x_lat,
                                  mods, g_post3, *w_o[i], w_kvb_b, i, j)

    return (x_ctx.reshape(N_CTX_B, CTX_S, D_MODEL),
            x_lat.reshape(N_LAT_B, LAT_S, D_MODEL),
            jnp.transpose(st_dkt.reshape(N_CTX_B, n_diff, HEADS, 2, DA_DH, CTX_S), (0, 1, 5, 2, 3, 4)),
            st_dv,
            st_ckv,
            st_kpe)
```

```python
import functools
import math

import jax
import jax.numpy as jnp
from jax import lax
from jax.experimental import pallas as pl
from jax.experimental.pallas import tpu as pltpu

F32 = jnp.float32
BF16 = jnp.bfloat16

D_MODEL = 1024
DEPTH = 4
N_CTX_B = 16
CTX_S = 256
N_LAT_B = 4
LAT_S = 1024
PAST = 256
GRID_W = 64
HEADS = 8
DA_DH = 64
DA_DV = 128
MLA_Q_LORA = 384
MLA_KV_LORA = 256
MLA_NOPE = 128
MLA_ROPE = 64
MLA_DV = 128
MLA_QK_PAD = 256
ROPE_THETA = 10000.0
NORM_EPS = 1e-6
COND_ROWS = 8

LANES = 128
V_PAD = 16
TQ = 256
TM = 512
VMEM_LIMIT = 56 * 1024 * 1024

_TRANS_B = (((1,), (1,)), ((), ()))
SCORES_AHEAD = 2
CTX_PER_STEP = 2
DA_Q_SCALE = DA_DH ** -0.5 * math.log2(math.e)
MLA_Q_SCALE = (MLA_NOPE + MLA_ROPE) ** -0.5 * math.log2(math.e)


def _silu(x):
    return x * (1.0 / (1.0 + jnp.exp(-x)))


def _rms(x, g):
    r = lax.rsqrt(jnp.mean(x * x, axis=-1, keepdims=True) + NORM_EPS)
    return (x * r) * g


def _rope_slab(x, cos, sin_signed):
    lane = lax.broadcasted_iota(jnp.int32, x.shape, 1)
    first_half = (lane % 64) < 32
    partner = jnp.where(first_half, pltpu.roll(x, 96, axis=1), pltpu.roll(x, 32, axis=1))
    return x * cos + partner * sin_signed


def _ones_rows(cols):
    row = lax.broadcasted_iota(jnp.int32, (V_PAD, cols), 0)
    return jnp.where(row == 0, 1.0, 0.0).astype(BF16)


def _cond_row(mods_ref, row):
    m = mods_ref[pl.ds(row, 1), :]
    return m[:, 0:D_MODEL], m[:, D_MODEL:2 * D_MODEL], m[:, 2 * D_MODEL:3 * D_MODEL]


def _modulated(x_ref, mods_ref, g_ref, row):
    shift, scale, _ = _cond_row(mods_ref, row)
    h = _rms(x_ref[...], g_ref[...]) * (1.0 + scale) + shift
    return h.astype(BF16)


def _ada_kernel(cond_ref, w_ref, b_ref, o_ref):
    a = _silu(cond_ref[...]).astype(BF16)
    o_ref[...] = jnp.dot(a, w_ref[...].astype(BF16), preferred_element_type=F32) + b_ref[...]


def _ada_call(cond, w_ada, b_ada):
    tn = 3 * D_MODEL
    return pl.pallas_call(
        _ada_kernel,
        grid=(DEPTH, 3 * D_MODEL // tn),
        in_specs=[
            pl.BlockSpec((COND_ROWS, D_MODEL), lambda i, n: (0, 0)),
            pl.BlockSpec((None, D_MODEL, tn), lambda i, n: (i, 0, n)),
            pl.BlockSpec((None, 1, tn), lambda i, n: (i, 0, n)),
        ],
        out_specs=pl.BlockSpec((None, COND_ROWS, tn), lambda i, n: (i, 0, n)),
        out_shape=jax.ShapeDtypeStruct((DEPTH, COND_ROWS, 3 * D_MODEL), F32),
        compiler_params=pltpu.CompilerParams(
            dimension_semantics=("arbitrary", "arbitrary"), vmem_limit_bytes=VMEM_LIMIT),
        name="ada_mod",
    )(cond, w_ada, b_ada.reshape(DEPTH, 1, 3 * D_MODEL))


def _cast_job_specs(jobs):
    in_specs = [pl.BlockSpec(blk, src_map) for _, blk, src_map, _, _ in jobs]
    out_specs = [pl.BlockSpec(blk, dst_map) for _, blk, _, _, dst_map in jobs]
    out_shape = [jax.ShapeDtypeStruct(shape, BF16) for _, _, _, shape, _ in jobs]
    return in_specs, out_specs, out_shape, [a for a, _, _, _, _ in jobs]


def _run_cast_jobs(src_refs, dst_refs):
    for src, dst in zip(src_refs, dst_refs):
        dst[...] = src[...].astype(BF16)


def _write_state(ref, r, j, value):
    if j is None:
        ref[r] = value
    else:
        for jj in range(ref.shape[1]):
            ref[r, jj] = value if jj == j else jnp.zeros_like(value)


def _state_plumbing(states, shapes, j, n_inputs, n_outputs):
    def index_map(shp, layer_index):
        return lambda t: (t, layer_index) + (0,) * (len(shp) - 2)

    if states is None:
        specs = [pl.BlockSpec((CTX_PER_STEP,) + shp[1:], index_map(shp, 0)) for shp in shapes]
        return [], specs, {}, ()
    specs = [pl.BlockSpec((CTX_PER_STEP, None) + shp[2:], index_map(shp, j)) for shp in shapes]
    in_specs = [pl.BlockSpec(memory_space=pl.ANY) for _ in shapes]
    aliases = {n_inputs + i: n_outputs + i for i in range(len(shapes))}
    return in_specs, specs, aliases, tuple(states)


def _pre_diff_lat_kernel(x_ref, mods_ref, g_ref, w_ref, cos_ref, sin_ref,
                         q_ref, k_ref, vt_ref, gate_ref):
    row = 1 + pl.program_id(0) // (LAT_S // TM)
    hb = _modulated(x_ref, mods_ref, g_ref, row)
    cos = cos_ref[...]
    sin = sin_ref[...]
    n = D_MODEL
    q = jnp.dot(hb, w_ref[:, 0:n], preferred_element_type=F32)
    k = jnp.dot(hb, w_ref[:, n:2 * n], preferred_element_type=F32)
    cos_q = cos * DA_Q_SCALE
    sin_q = sin * DA_Q_SCALE
    slabs = [slice(h * LANES, (h + 1) * LANES) for h in range(HEADS)]
    q_ref[...] = jnp.concatenate(
        [_rope_slab(q[:, sl], cos_q, sin_q).astype(BF16) for sl in slabs], axis=1)
    k_ref[...] = jnp.concatenate(
        [_rope_slab(k[:, sl], cos, sin).astype(BF16) for sl in slabs], axis=1)
    vt_ref[...] = jnp.dot(hb, w_ref[:, 2 * n:3 * n], preferred_element_type=F32).T.astype(BF16)
    gate_ref[...] = _silu(jnp.dot(hb, w_ref[:, 3 * n:4 * n], preferred_element_type=F32)).astype(BF16)


def _pre_diff_lat(x, mods, g_pre, w_in, wj, cos_t, sin_t, layer, j):
    n_tok = N_LAT_B * LAT_S
    tiles_per_req = LAT_S // TM
    tok_spec = pl.BlockSpec((TM, D_MODEL), lambda t: (t, 0))
    rope_spec = pl.BlockSpec((TM, LANES), lambda t: (t % tiles_per_req, 0))
    return pl.pallas_call(
        _pre_diff_lat_kernel,
        grid=(n_tok // TM,),
        in_specs=[
            tok_spec,
            pl.BlockSpec((None, COND_ROWS, 3 * D_MODEL), lambda t: (layer, 0, 0)),
            pl.BlockSpec((None, 1, D_MODEL), lambda t: (layer, 0, 0)),
            pl.BlockSpec((None, D_MODEL, 4 * D_MODEL), lambda t: (wj, 0, 0)),
            rope_spec,
            rope_spec,
        ],
        out_specs=[
            tok_spec, tok_spec,
            pl.BlockSpec((None, D_MODEL, TM), lambda t: (t // tiles_per_req, 0, t % tiles_per_req)),
            tok_spec,
        ],
        out_shape=[
            jax.ShapeDtypeStruct((n_tok, D_MODEL), BF16),
            jax.ShapeDtypeStruct((n_tok, D_MODEL), BF16),
            jax.ShapeDtypeStruct((N_LAT_B, D_MODEL, LAT_S), BF16),
            jax.ShapeDtypeStruct((n_tok, D_MODEL), BF16),
        ],
        compiler_params=pltpu.CompilerParams(
            dimension_semantics=("arbitrary",), vmem_limit_bytes=VMEM_LIMIT),
        name="pre_diff_lat",
    )(x, mods, g_pre, w_in, cos_t, sin_t)


def _finish(heads, w_out_ref, x_ref, mods_ref, g_post_ref, row, o_ref, rows=slice(None)):
    _, _, gate = _cond_row(mods_ref, row)
    og = jnp.concatenate(heads, axis=1)
    y = jnp.dot(og, w_out_ref[...], preferred_element_type=F32)
    o_ref[rows, :] = x_ref[rows, :] + gate * _rms(y, g_post_ref[...])


def _diff_lambda(lam_ref, lam_init):
    p = lam_ref[...]
    a = jnp.sum(p[0:1, :] * p[1:2, :], axis=-1, keepdims=True)
    b = jnp.sum(p[2:3, :] * p[3:4, :], axis=-1, keepdims=True)
    return jnp.exp(a) - jnp.exp(b) + lam_init


def _stack_diff_keys(k, k2_ref, sk):
    lane = lax.broadcasted_iota(jnp.int32, (sk, LANES), 1)
    zero = jnp.zeros((sk, LANES), BF16)
    for h in range(HEADS):
        kh = k[:, h * LANES:(h + 1) * LANES]
        k2_ref[h, 0:sk, :] = jnp.where(lane < DA_DH, kh, zero)
        k2_ref[h, sk:2 * sk, :] = jnp.where(lane >= DA_DH, kh, zero)


def _stack_values_t(vt, vt_ref, c0, cols):
    ones = _ones_rows(cols)
    for h in range(HEADS):
        vt_ref[h, 0:DA_DV, c0:c0 + cols] = vt[h * DA_DV:(h + 1) * DA_DV, :]
        vt_ref[h, DA_DV:DA_DV + V_PAD, c0:c0 + cols] = ones


def _diff_heads(q_ref, k2_ref, vt_ref, gate_ref, g_sub_ref, lam, lam_init, sk, rows=slice(None)):
    g_sub = g_sub_ref[...]

    def scores(h):
        return lax.dot_general(k2_ref[h], q_ref[rows, h * LANES:(h + 1) * LANES], _TRANS_B,
                               preferred_element_type=F32)

    heads = []
    pending = [scores(h) for h in range(SCORES_AHEAD)]
    for h in range(HEADS):
        lo, hi = h * LANES, (h + 1) * LANES
        s = pending.pop(0)
        if h + SCORES_AHEAD < HEADS:
            pending.append(scores(h + SCORES_AHEAD))
        s0 = s[0:sk, :]
        s1 = s[sk:2 * sk, :]
        e0 = jnp.exp2(s0 - jnp.max(s0, axis=0, keepdims=True)).astype(BF16)
        e1 = jnp.exp2(s1 - jnp.max(s1, axis=0, keepdims=True)).astype(BF16)
        pv0 = jnp.dot(vt_ref[h], e0, preferred_element_type=F32)
        pv1 = jnp.dot(vt_ref[h], e1, preferred_element_type=F32)
        a0 = 1.0 / pv0[DA_DV:DA_DV + 1, :]
        a1 = lam / pv1[DA_DV:DA_DV + 1, :]
        ot = pv0[0:DA_DV, :] * a0 - pv1[0:DA_DV, :] * a1
        ot = ot * lax.rsqrt(jnp.mean(ot * ot, axis=0, keepdims=True) + NORM_EPS)
        o = (ot.T * g_sub) * (1.0 - lam_init)
        heads.append((o * gate_ref[rows, lo:hi]).astype(BF16))
    return heads


def _attn_diff_lat_kernel(lam_init, q_ref, k_ref, vt_ref, ckt_ref, cv_ref, gate_ref, x_ref, mods_ref,
                          g_post_ref, w_out_ref, g_sub_ref, lam_ref, o_ref, k2_ref, vb_ref):
    sk = LAT_S + PAST

    @pl.when(pl.program_id(1) == 0)
    def _():
        lane = lax.broadcasted_iota(jnp.int32, (PAST, LANES), 1)
        zero_l = jnp.zeros((LAT_S, LANES), BF16)
        zero_p = jnp.zeros((PAST, LANES), BF16)
        lane_l = lax.broadcasted_iota(jnp.int32, (LAT_S, LANES), 1)
        for h in range(HEADS):
            lo, hi = h * LANES, (h + 1) * LANES
            kh = k_ref[:, lo:hi]
            ch = ckt_ref[lo:hi, :].T.astype(BF16)
            k2_ref[h, 0:LAT_S, :] = jnp.where(lane_l < DA_DH, kh, zero_l)
            k2_ref[h, LAT_S:sk, :] = jnp.where(lane < DA_DH, ch, zero_p)
            k2_ref[h, sk:sk + LAT_S, :] = jnp.where(lane_l >= DA_DH, kh, zero_l)
            k2_ref[h, sk + LAT_S:2 * sk, :] = jnp.where(lane >= DA_DH, ch, zero_p)
            vb_ref[h, 0:DA_DV, LAT_S:sk] = cv_ref[:, h, :].T.astype(BF16)
        _stack_values_t(vt_ref[...], vb_ref, 0, LAT_S)
        ones = _ones_rows(PAST)
        for h in range(HEADS):
            vb_ref[h, DA_DV:DA_DV + V_PAD, LAT_S:sk] = ones

    lam = _diff_lambda(lam_ref, lam_init)
    heads = _diff_heads(q_ref, k2_ref, vb_ref, gate_ref, g_sub_ref, lam, lam_init, sk)
    _finish(heads, w_out_ref, x_ref, mods_ref, g_post_ref, 1 + pl.program_id(0), o_ref)


def _attn_diff_lat(q, k, vt, cache_kt, cache_v, gate, x, mods, g_post, w_out, wo, g_sub, lam_p, layer, j):
    lam_init = 0.8 - 0.6 * math.exp(-0.3 * layer)
    sk = LAT_S + PAST
    nq = LAT_S // TQ
    tok_spec = pl.BlockSpec((TQ, D_MODEL), lambda b, i: (b * nq + i, 0))
    kv_spec = pl.BlockSpec((LAT_S, D_MODEL), lambda b, i: (b, 0))
    cache_kt_spec = pl.BlockSpec((None, None, D_MODEL, PAST), lambda b, i: (b, j, 0, 0))
    cache_v_spec = pl.BlockSpec((None, None, PAST, HEADS, DA_DV), lambda b, i: (b, j, 0, 0, 0))
    return pl.pallas_call(
        functools.partial(_attn_diff_lat_kernel, lam_init),
        grid=(N_LAT_B, nq),
        in_specs=[
            tok_spec, kv_spec, pl.BlockSpec((None, D_MODEL, LAT_S), lambda b, i: (b, 0, 0)),
            cache_kt_spec, cache_v_spec, tok_spec, tok_spec,
            pl.BlockSpec((None, COND_ROWS, 3 * D_MODEL), lambda b, i: (layer, 0, 0)),
            pl.BlockSpec((None, 1, D_MODEL), lambda b, i: (layer, 0, 0)),
            pl.BlockSpec((None, D_MODEL, D_MODEL), lambda b, i: (wo, 0, 0)),
            pl.BlockSpec((None, 1, DA_DV), lambda b, i: (j, 0, 0)),
            pl.BlockSpec((None, 4, DA_DH), lambda b, i: (j, 0, 0)),
        ],
        out_specs=tok_spec,
        out_shape=jax.ShapeDtypeStruct(x.shape, F32),
        scratch_shapes=[
            pltpu.VMEM((HEADS, 2 * sk, LANES), BF16),
            pltpu.VMEM((HEADS, DA_DV + V_PAD, sk), BF16),
        ],
        compiler_params=pltpu.CompilerParams(
            dimension_semantics=("arbitrary", "arbitrary"), vmem_limit_bytes=VMEM_LIMIT),
        name="attn_diff_lat",
    )(q, k, vt, cache_kt, cache_v, gate, x, mods, g_post, w_out, g_sub, lam_p)


MLA_IN = MLA_Q_LORA + MLA_KV_LORA + MLA_ROPE + D_MODEL
_QA = slice(0, MLA_Q_LORA)
_KV = slice(MLA_Q_LORA, MLA_Q_LORA + MLA_KV_LORA)
_PE = slice(_KV.stop, _KV.stop + LANES)
_GT = slice(_KV.stop + MLA_ROPE, MLA_IN)


def _proj_t(hb, w_t_ref, rows):
    return lax.dot_general(hb, w_t_ref[rows, :], _TRANS_B, preferred_element_type=F32)


def _mla_queries(hb, w_in_ref, g_qa_ref):
    q_a = _proj_t(hb, w_in_ref, _QA)
    return _rms(q_a, g_qa_ref[...]).astype(BF16)


def _pre_mla_lat_kernel(x_ref, mods_ref, g_ref, w_in_ref, g_qa_ref, w_qb_ref, g_kva_ref, cos_ref, sin_ref,
                        q_ref, gate_ref, ckv_ref, kpe_ref):
    row = 1 + pl.program_id(0) // (LAT_S // TM)
    hb = _modulated(x_ref, mods_ref, g_ref, row)
    cos = cos_ref[...]
    sin = sin_ref[...]
    qn = _mla_queries(hb, w_in_ref, g_qa_ref)
    for h in range(HEADS):
        lo = h * MLA_QK_PAD
        q = jnp.dot(qn, w_qb_ref[:, lo:lo + MLA_QK_PAD], preferred_element_type=F32)
        q = q * MLA_Q_SCALE
        q_ref[:, lo:lo + LANES] = q[:, 0:LANES].astype(BF16)
        q_ref[:, lo + LANES:lo + 2 * LANES] = _rope_slab(q[:, LANES:2 * LANES], cos, sin).astype(BF16)
    kv_a = _proj_t(hb, w_in_ref, _KV)
    ckv_ref[...] = _rms(kv_a, g_kva_ref[...]).astype(BF16)
    gate_ref[...] = _silu(_proj_t(hb, w_in_ref, _GT)).astype(BF16)
    pe = _proj_t(hb, w_in_ref, _PE)
    kpe_ref[...] = _rope_slab(pe, cos, sin)[:, 0:MLA_ROPE].astype(BF16)


def _pre_mla_lat(x, mods, g_pre, w_in, wj, g_qa, w_qb, g_kva, cos_t, sin_t, layer, j):
    n_tok = N_LAT_B * LAT_S
    tiles_per_req = LAT_S // TM
    tok = lambda w: pl.BlockSpec((TM, w), lambda t: (t, 0))
    rope_spec = pl.BlockSpec((TM, LANES), lambda t: (t % tiles_per_req, 0))
    return pl.pallas_call(
        _pre_mla_lat_kernel,
        grid=(n_tok // TM,),
        in_specs=[
            tok(D_MODEL),
            pl.BlockSpec((None, COND_ROWS, 3 * D_MODEL), lambda t: (layer, 0, 0)),
            pl.BlockSpec((None, 1, D_MODEL), lambda t: (layer, 0, 0)),
            pl.BlockSpec((None, MLA_IN, D_MODEL), lambda t: (wj, 0, 0)),
            pl.BlockSpec((None, 1, MLA_Q_LORA), lambda t: (j, 0, 0)),
            pl.BlockSpec((None, MLA_Q_LORA, HEADS * MLA_QK_PAD), lambda t: (j, 0, 0)),
            pl.BlockSpec((None, 1, MLA_KV_LORA), lambda t: (j, 0, 0)),
            rope_spec,
            rope_spec,
        ],
        out_specs=[tok(HEADS * MLA_QK_PAD), tok(D_MODEL), tok(MLA_KV_LORA), tok(MLA_ROPE)],
        out_shape=[
            jax.ShapeDtypeStruct((n_tok, HEADS * MLA_QK_PAD), BF16),
            jax.ShapeDtypeStruct((n_tok, D_MODEL), BF16),
            jax.ShapeDtypeStruct((n_tok, MLA_KV_LORA), BF16),
            jax.ShapeDtypeStruct((n_tok, MLA_ROPE), BF16),
        ],
        compiler_params=pltpu.CompilerParams(
            dimension_semantics=("arbitrary",), vmem_limit_bytes=VMEM_LIMIT),
        name="pre_mla_lat",
    )(x, mods, g_pre, w_in, g_qa, w_qb, g_kva, cos_t, sin_t)


def _mla_expand(ckv, kpe, w_kvb_ref, kf_ref, vf_ref, r0, rows):
    zero = jnp.zeros((rows, LANES - MLA_ROPE), BF16)
    for h in range(HEADS):
        lo = h * (MLA_NOPE + MLA_DV)
        kv = jnp.dot(ckv, w_kvb_ref[:, lo:lo + MLA_NOPE + MLA_DV], preferred_element_type=F32)
        kf_ref[h, r0:r0 + rows, 0:MLA_NOPE] = kv[:, 0:MLA_NOPE].astype(BF16)
        kf_ref[h, r0:r0 + rows, MLA_NOPE:MLA_NOPE + MLA_ROPE] = kpe
        kf_ref[h, r0:r0 + rows, MLA_NOPE + MLA_ROPE:MLA_QK_PAD] = zero
        vf_ref[h, r0:r0 + rows, :] = kv[:, MLA_NOPE:MLA_NOPE + MLA_DV].astype(BF16)


def _mla_heads(q_ref, kf_ref, vf_ref, gate_ref, rows=slice(None)):
    def scores(h):
        return lax.dot_general(q_ref[rows, h * MLA_QK_PAD:(h + 1) * MLA_QK_PAD], kf_ref[h], _TRANS_B,
                               preferred_element_type=F32)

    heads = []
    pending = [scores(h) for h in range(SCORES_AHEAD)]
    for h in range(HEADS):
        s = pending.pop(0)
        if h + SCORES_AHEAD < HEADS:
            pending.append(scores(h + SCORES_AHEAD))
        e = jnp.exp2(s - jnp.max(s, axis=-1, keepdims=True))
        inv = 1.0 / jnp.sum(e, axis=-1, keepdims=True)
        o = jnp.dot(e.astype(BF16), vf_ref[h], preferred_element_type=F32) * inv
        lo, hi = h * LANES, (h + 1) * LANES
        heads.append((o * gate_ref[rows, lo:hi]).astype(BF16))
    return heads


def _attn_mla_lat_kernel(q_ref, ckv_ref, kpe_ref, cckv_ref, ckpe_ref, gate_ref, x_ref, mods_ref,
                         g_post_ref, w_out_ref, w_kvb_ref, o_ref, kf_ref, vf_ref):
    @pl.when(pl.program_id(1) == 0)
    def _():
        _mla_expand(ckv_ref[...], kpe_ref[...], w_kvb_ref, kf_ref, vf_ref, 0, LAT_S)
        _mla_expand(cckv_ref[...].astype(BF16), ckpe_ref[...].astype(BF16), w_kvb_ref, kf_ref, vf_ref,
                    LAT_S, PAST)

    heads = _mla_heads(q_ref, kf_ref, vf_ref, gate_ref)
    _finish(heads, w_out_ref, x_ref, mods_ref, g_post_ref, 1 + pl.program_id(0), o_ref)


def _attn_mla_lat(q, ckv, kpe, cache_ckv, cache_kpe, gate, x, mods, g_post, w_out, wo, w_kvb, layer, j):
    sk = LAT_S + PAST
    nq = LAT_S // TQ
    tok = lambda w: pl.BlockSpec((TQ, w), lambda b, i: (b * nq + i, 0))
    req = lambda w: pl.BlockSpec((LAT_S, w), lambda b, i: (b, 0))
    return pl.pallas_call(
        _attn_mla_lat_kernel,
        grid=(N_LAT_B, nq),
        in_specs=[
            tok(HEADS * MLA_QK_PAD), req(MLA_KV_LORA), req(MLA_ROPE),
            pl.BlockSpec((None, None, PAST, MLA_KV_LORA), lambda b, i: (b, j, 0, 0)),
            pl.BlockSpec((None, None, PAST, MLA_ROPE), lambda b, i: (b, j, 0, 0)),
            tok(D_MODEL), tok(D_MODEL),
            pl.BlockSpec((None, COND_ROWS, 3 * D_MODEL), lambda b, i: (layer, 0, 0)),
            pl.BlockSpec((None, 1, D_MODEL), lambda b, i: (layer, 0, 0)),
            pl.BlockSpec((None, D_MODEL, D_MODEL), lambda b, i: (wo, 0, 0)),
            pl.BlockSpec((None, MLA_KV_LORA, HEADS * (MLA_NOPE + MLA_DV)), lambda b, i: (j, 0, 0)),
        ],
        out_specs=tok(D_MODEL),
        out_shape=jax.ShapeDtypeStruct(x.shape, F32),
        scratch_shapes=[
            pltpu.VMEM((HEADS, sk, MLA_QK_PAD), BF16),
            pltpu.VMEM((HEADS, sk, MLA_DV), BF16),
        ],
        compiler_params=pltpu.CompilerParams(
            dimension_semantics=("arbitrary", "arbitrary"), vmem_limit_bytes=VMEM_LIMIT),
        name="attn_mla_lat",
    )(q, ckv, kpe, cache_ckv, cache_kpe, gate, x, mods, g_post, w_out, w_kvb)


def _split_layer_refs(refs, n_cast):
    scratch = refs[-2:]
    outs = refs[:-2]
    n_out = 3 + n_cast
    main = outs[len(outs) - n_out:len(outs) - n_cast]
    _run_cast_jobs(outs[len(outs) - n_out - n_cast:len(outs) - n_out], outs[len(outs) - n_cast:])
    return main, scratch


def _layer_diff_ctx_kernel(create_j, lam_init, n_cast, x_ref, mods_ref, g_pre_ref, w_in_ref, g_post_ref,
                           w_out_ref, g_sub_ref, lam_ref, *refs):
    (o_ref, skt_ref, sv_ref), (k2_ref, vb_ref) = _split_layer_refs(refs, n_cast)
    hb = _modulated(x_ref, mods_ref, g_pre_ref, 0)
    n = D_MODEL
    q = (jnp.dot(hb, w_in_ref[:, 0:n], preferred_element_type=F32) * DA_Q_SCALE).astype(BF16)
    k = jnp.dot(hb, w_in_ref[:, n:2 * n], preferred_element_type=F32)
    v = jnp.dot(hb, w_in_ref[:, 2 * n:3 * n], preferred_element_type=F32)
    gate = _silu(jnp.dot(hb, w_in_ref[:, 3 * n:4 * n], preferred_element_type=F32)).astype(BF16)
    kb = k.astype(BF16)
    lam = _diff_lambda(lam_ref, lam_init)
    for r in range(CTX_PER_STEP):
        rows = slice(r * CTX_S, (r + 1) * CTX_S)
        vt = v[rows, :].T
        _write_state(skt_ref, r, create_j, k[rows, :].T)
        _write_state(sv_ref, r, create_j, v[rows, :].reshape(CTX_S, HEADS, DA_DV))
        _stack_diff_keys(kb[rows, :], k2_ref.at[r], CTX_S)
        _stack_values_t(vt.astype(BF16), vb_ref.at[r], 0, CTX_S)
        heads = _diff_heads(q, k2_ref.at[r], vb_ref.at[r], gate, g_sub_ref, lam, lam_init, CTX_S, rows)
        _finish(heads, w_out_ref, x_ref, mods_ref, g_post_ref, 0, o_ref, rows)


def _layer_diff_ctx(x, mods, g_pre, g_post, w_in, wj, w_out, wo, g_sub, lam_p, states, layer, j, n_layers,
                    cast_jobs=()):
    lam_init = 0.8 - 0.6 * math.exp(-0.3 * layer)
    shapes = [(N_CTX_B, n_layers, D_MODEL, CTX_S), (N_CTX_B, n_layers, CTX_S, HEADS, DA_DV)]
    st_in_specs, st_out_specs, aliases, st_args = _state_plumbing(states, shapes, j, 8, 1)
    cj_in, cj_out, cj_shapes, cj_args = _cast_job_specs(cast_jobs)
    tok_spec = pl.BlockSpec((CTX_PER_STEP * CTX_S, D_MODEL), lambda t: (t, 0))
    return pl.pallas_call(
        functools.partial(_layer_diff_ctx_kernel, j if states is None else None, lam_init, len(cast_jobs)),
        grid=(N_CTX_B // CTX_PER_STEP,),
        in_specs=[
            tok_spec,
            pl.BlockSpec((None, COND_ROWS, 3 * D_MODEL), lambda t: (layer, 0, 0)),
            pl.BlockSpec((None, 1, D_MODEL), lambda t: (layer, 0, 0)),
            pl.BlockSpec((None, D_MODEL, 4 * D_MODEL), lambda t: (wj, 0, 0)),
            pl.BlockSpec((None, 1, D_MODEL), lambda t: (layer, 0, 0)),
            pl.BlockSpec((None, D_MODEL, D_MODEL), lambda t: (wo, 0, 0)),
            pl.BlockSpec((None, 1, DA_DV), lambda t: (j, 0, 0)),
            pl.BlockSpec((None, 4, DA_DH), lambda t: (j, 0, 0)),
        ] + st_in_specs + cj_in,
        out_specs=[tok_spec] + st_out_specs + cj_out,
        out_shape=[jax.ShapeDtypeStruct(x.shape, F32)] + [jax.ShapeDtypeStruct(shp, F32) for shp in shapes]
        + cj_shapes,
        input_output_aliases=aliases,
        scratch_shapes=[
            pltpu.VMEM((CTX_PER_STEP, HEADS, 2 * CTX_S, LANES), BF16),
            pltpu.VMEM((CTX_PER_STEP, HEADS, DA_DV + V_PAD, CTX_S), BF16),
        ],
        compiler_params=pltpu.CompilerParams(
            dimension_semantics=("arbitrary",), vmem_limit_bytes=VMEM_LIMIT),
        name="layer_diff_ctx",
    )(x, mods, g_pre, w_in, g_post, w_out, g_sub, lam_p, *st_args, *cj_args)


def _layer_mla_ctx_kernel(create_j, n_cast, x_ref, mods_ref, g_pre_ref, w_in_ref, g_qa_ref, w_qb_ref, g_kva_ref,
                          g_post_ref, w_out_ref, w_kvb_ref, *refs):
    (o_ref, ckv_ref, kpe_ref), (kf_ref, vf_ref) = _split_layer_refs(refs, n_cast)
    hb = _modulated(x_ref, mods_ref, g_pre_ref, 0)
    qn = _mla_queries(hb, w_in_ref, g_qa_ref)
    q = (jnp.dot(qn, w_qb_ref[...], preferred_element_type=F32) * MLA_Q_SCALE).astype(BF16)
    ckv = _rms(_proj_t(hb, w_in_ref, _KV), g_kva_ref[...])
    gate = _silu(_proj_t(hb, w_in_ref, _GT)).astype(BF16)
    pe = _proj_t(hb, w_in_ref, _PE)
    for r in range(CTX_PER_STEP):
        rows = slice(r * CTX_S, (r + 1) * CTX_S)
        kpe = pe[rows, 0:MLA_ROPE]
        _write_state(ckv_ref, r, create_j, ckv[rows, :])
        _write_state(kpe_ref, r, create_j, kpe)
        _mla_expand(ckv[rows, :].astype(BF16), kpe.astype(BF16), w_kvb_ref, kf_ref.at[r], vf_ref.at[r],
                    0, CTX_S)
        heads = _mla_heads(q, kf_ref.at[r], vf_ref.at[r], gate, rows)
        _finish(heads, w_out_ref, x_ref, mods_ref, g_post_ref, 0, o_ref, rows)


def _layer_mla_ctx(x, mods, g_pre, g_post, w_in, wj, g_qa, w_qb, g_kva, w_out, wo, w_kvb, states, layer, j,
                   n_layers, cast_jobs=()):
    shapes = [(N_CTX_B, n_layers, CTX_S, MLA_KV_LORA), (N_CTX_B, n_layers, CTX_S, MLA_ROPE)]
    st_in_specs, st_out_specs, aliases, st_args = _state_plumbing(states, shapes, j, 10, 1)
    cj_in, cj_out, cj_shapes, cj_args = _cast_job_specs(cast_jobs)
    tok_spec = pl.BlockSpec((CTX_PER_STEP * CTX_S, D_MODEL), lambda t: (t, 0))
    return pl.pallas_call(
        functools.partial(_layer_mla_ctx_kernel, j if states is None else None, len(cast_jobs)),
        grid=(N_CTX_B // CTX_PER_STEP,),
        in_specs=[
            tok_spec,
            pl.BlockSpec((None, COND_ROWS, 3 * D_MODEL), lambda t: (layer, 0, 0)),
            pl.BlockSpec((None, 1, D_MODEL), lambda t: (layer, 0, 0)),
            pl.BlockSpec((None, MLA_IN, D_MODEL), lambda t: (wj, 0, 0)),
            pl.BlockSpec((None, 1, MLA_Q_LORA), lambda t: (j, 0, 0)),
            pl.BlockSpec((None, MLA_Q_LORA, HEADS * MLA_QK_PAD), lambda t: (j, 0, 0)),
            pl.BlockSpec((None, 1, MLA_KV_LORA), lambda t: (j, 0, 0)),
            pl.BlockSpec((None, 1, D_MODEL), lambda t: (layer, 0, 0)),
            pl.BlockSpec((None, D_MODEL, D_MODEL), lambda t: (wo, 0, 0)),
            pl.BlockSpec((None, MLA_KV_LORA, HEADS * (MLA_NOPE + MLA_DV)), lambda t: (j, 0, 0)),
        ] + st_in_specs + cj_in,
        out_specs=[tok_spec] + st_out_specs + cj_out,
        out_shape=[jax.ShapeDtypeStruct(x.shape, F32)] + [jax.ShapeDtypeStruct(shp, F32) for shp in shapes]
        + cj_shapes,
        input_output_aliases=aliases,
        scratch_shapes=[
            pltpu.VMEM((CTX_PER_STEP, HEADS, CTX_S, MLA_QK_PAD), BF16),
            pltpu.VMEM((CTX_PER_STEP, HEADS, CTX_S, MLA_DV), BF16),
        ],
        compiler_params=pltpu.CompilerParams(
            dimension_semantics=("arbitrary",), vmem_limit_bytes=VMEM_LIMIT),
        name="layer_mla_ctx",
    )(x, mods, g_pre, w_in, g_qa, w_qb, g_kva, g_post, w_out, w_kvb, *st_args, *cj_args)


def _rope_tables():
    rows = LAT_S // GRID_W
    row = jnp.repeat(jnp.arange(rows), GRID_W).astype(F32)
    col = jnp.tile(jnp.arange(GRID_W), rows).astype(F32)
    n_freq = DA_DH // 4
    inv = ROPE_THETA ** (-jnp.arange(n_freq, dtype=F32) / n_freq)
    ang = jnp.concatenate([row[:, None] * inv, col[:, None] * inv], axis=-1)
    cos, sin = jnp.cos(ang), jnp.sin(ang)
    return (jnp.concatenate([cos, cos, cos, cos], axis=-1),
            jnp.concatenate([-sin, sin, -sin, sin], axis=-1))


def kernel(x_prompt, x_sample, cache_diff_k, cache_diff_v, cache_mla_ckv, cache_mla_kpe,
           c, c_ctx, w_ada, b_ada, g_pre, g_post, w_out,
           da_w_in, da_lam_q1, da_lam_k1, da_lam_q2, da_lam_k2, da_g_sub,
           mla_w_in, mla_g_qa, mla_w_qb, mla_g_kva, mla_w_kvb):
    assert DA_DH == MLA_ROPE
    n_diff = (DEPTH + 1) // 2
    n_mla = DEPTH // 2

    cond = jnp.concatenate(
        [c_ctx[None, :], c, jnp.zeros((COND_ROWS - 1 - N_LAT_B, D_MODEL), F32)], axis=0)
    mods = _ada_call(cond, w_ada, b_ada)

    cos_t, sin_t = _rope_tables()
    g_pre3 = g_pre.reshape(DEPTH, 1, D_MODEL)
    g_post3 = g_post.reshape(DEPTH, 1, D_MODEL)
    steps = N_CTX_B // CTX_PER_STEP
    w_out_first = w_out[0:2].astype(BF16)
    da_w_in_first = da_w_in[0:1].astype(BF16)
    g_sub3 = da_g_sub.reshape(n_diff, 1, DA_DV)
    lam_p = jnp.stack([da_lam_q1, da_lam_k1, da_lam_q2, da_lam_k2], axis=1)
    cache_dkt = jnp.transpose(cache_diff_k, (0, 1, 3, 4, 5, 2)).reshape(N_LAT_B, n_diff, D_MODEL, PAST)

    mla_w_in_t = jnp.swapaxes(mla_w_in, 1, 2)
    mla_w_in_first = mla_w_in_t[0:1].astype(BF16)
    w_qb4 = mla_w_qb.reshape(n_mla, MLA_Q_LORA, HEADS, MLA_NOPE + MLA_ROPE)
    w_qb_b = jnp.pad(w_qb4, ((0, 0), (0, 0), (0, 0), (0, MLA_QK_PAD - MLA_NOPE - MLA_ROPE))).reshape(
        n_mla, MLA_Q_LORA, HEADS * MLA_QK_PAD).astype(BF16)
    w_kvb_b = mla_w_kvb.astype(BF16)
    g_qa3 = mla_g_qa.reshape(n_mla, 1, MLA_Q_LORA)
    g_kva3 = mla_g_kva.reshape(n_mla, 1, MLA_KV_LORA)

    x_ctx = x_prompt.reshape(N_CTX_B * CTX_S, D_MODEL)
    x_lat = x_sample.reshape(N_LAT_B * LAT_S, D_MODEL)
    st_diff = None
    st_mla = None

    assert DEPTH == 4
    da_w = [(da_w_in_first, 0), None]
    mla_w = [(mla_w_in_first, 0), None]
    w_o = [(w_out_first, 0), (w_out_first, 1), None, None]
    wide = D_MODEL * 4 // steps
    da_cast = [(da_w_in, (None, D_MODEL, wide), lambda b: (1, 0, b),
                (1, D_MODEL, 4 * D_MODEL), lambda b: (0, 0, b))]
    cols = D_MODEL // steps
    per_layer = steps // 2
    late_casts = [(mla_w_in_t, (None, MLA_IN, cols), lambda t: (1, 0, t),
                   (1, MLA_IN, D_MODEL), lambda t: (0, 0, t)),
                  (w_out, (None, D_MODEL // per_layer, D_MODEL), lambda t: (2 + t // per_layer, t % per_layer, 0),
                   (2, D_MODEL, D_MODEL), lambda t: (t // per_layer, t % per_layer, 0))]

    for i in range(DEPTH):
        j = i // 2
        if i % 2 == 0:
            q_l, k_l, vt_l, gate_l = _pre_diff_lat(x_lat, mods, g_pre3, *da_w[j], cos_t, sin_t, i, j)
            outs = _layer_diff_ctx(x_ctx, mods, g_pre3, g_post3, *da_w[j], *w_o[i], g_sub3, lam_p, st_diff,
                                   i, j, n_diff, cast_jobs=da_cast if i == 0 else ())
            x_ctx, st_dkt, st_dv = outs[0:3]
            st_diff = (st_dkt, st_dv)
            if i == 0:
                da_w[1] = (outs[3], 0)
            x_lat = _attn_diff_lat(q_l, k_l, vt_l, cache_dkt, cache_diff_v, gate_l, x_lat, mods, g_post3,
                                   *w_o[i], g_sub3, lam_p, i, j)
        else:
            w_j = mla_w[j]
            outs = _layer_mla_ctx(x_ctx, mods, g_pre3, g_post3, *w_j, g_qa3, w_qb_b, g_kva3, *w_o[i], w_kvb_b,
                                  st_mla, i, j, n_mla, cast_jobs=late_casts if i == 1 else ())
            x_ctx, st_ckv, st_kpe = outs[0:3]
            st_mla = (st_ckv, st_kpe)
            if i == 1:
                mla_w[1] = (outs[3], 0)
                w_o[2], w_o[3] = (outs[4], 0), (outs[4], 1)
            q_l, gate_l, ckv_l, kpe_l = _pre_mla_lat(
                x_lat, mods, g_pre3, *w_j, g_qa3, w_qb_b, g_kva3, cos_t, sin_t, i, j)
            x_lat = _attn_mla_lat(q_l, ckv_l, kpe_l, cache_mla_ckv, cache_mla_kpe, gate_l, x_lat,
                                  mods, g_post3, *w_o[i], w_kvb_b, i, j)

    return (x_ctx.reshape(N_CTX_B, CTX_S, D_MODEL),
            x_lat.reshape(N_LAT_B, LAT_S, D_MODEL),
            jnp.transpose(st_dkt.reshape(N_CTX_B, n_diff, HEADS, 2, DA_DH, CTX_S), (0, 1, 5, 2, 3, 4)),
            st_dv,
            st_ckv,
            st_kpe)
```

```python
import functools
import math

import jax
import jax.numpy as jnp
from jax import lax
from jax.experimental import pallas as pl
from jax.experimental.pallas import tpu as pltpu

F32 = jnp.float32
BF16 = jnp.bfloat16

D_MODEL = 1024
DEPTH = 4
N_CTX_B = 16
CTX_S = 256
N_LAT_B = 4
LAT_S = 1024
PAST = 256
GRID_W = 64
HEADS = 8
DA_DH = 64
DA_DV = 128
MLA_Q_LORA = 384
MLA_KV_LORA = 256
MLA_NOPE = 128
MLA_ROPE = 64
MLA_DV = 128
MLA_QK_PAD = 256
ROPE_THETA = 10000.0
NORM_EPS = 1e-6
COND_ROWS = 8

LANES = 128
V_PAD = 16
TQ = 256
TM = 512
VMEM_LIMIT = 56 * 1024 * 1024

_TRANS_B = (((1,), (1,)), ((), ()))
SCORES_AHEAD = 2
CTX_PER_STEP = 2
DA_Q_SCALE = DA_DH ** -0.5 * math.log2(math.e)
MLA_Q_SCALE = (MLA_NOPE + MLA_ROPE) ** -0.5 * math.log2(math.e)


def _silu(x):
    return x * (1.0 / (1.0 + jnp.exp(-x)))


def _rms(x, g):
    r = lax.rsqrt(jnp.mean(x * x, axis=-1, keepdims=True) + NORM_EPS)
    return (x * r) * g


def _rope_slab(x, cos, sin_signed):
    lane = lax.broadcasted_iota(jnp.int32, x.shape, 1)
    first_half = (lane % 64) < 32
    partner = jnp.where(first_half, pltpu.roll(x, 96, axis=1), pltpu.roll(x, 32, axis=1))
    return x * cos + partner * sin_signed


def _ones_rows(cols):
    row = lax.broadcasted_iota(jnp.int32, (V_PAD, cols), 0)
    return jnp.where(row == 0, 1.0, 0.0).astype(BF16)


def _cond_row(mods_ref, row):
    m = mods_ref[pl.ds(row, 1), :]
    return m[:, 0:D_MODEL], m[:, D_MODEL:2 * D_MODEL], m[:, 2 * D_MODEL:3 * D_MODEL]


def _modulated(x_ref, mods_ref, g_ref, row):
    shift, scale, _ = _cond_row(mods_ref, row)
    h = _rms(x_ref[...], g_ref[...]) * (1.0 + scale) + shift
    return h.astype(BF16)


def _ada_kernel(cond_ref, w_ref, b_ref, o_ref):
    a = _silu(cond_ref[...]).astype(BF16)
    o_ref[...] = jnp.dot(a, w_ref[...].astype(BF16), preferred_element_type=F32) + b_ref[...]


def _ada_call(cond, w_ada, b_ada):
    tn = 3 * D_MODEL
    return pl.pallas_call(
        _ada_kernel,
        grid=(DEPTH, 3 * D_MODEL // tn),
        in_specs=[
            pl.BlockSpec((COND_ROWS, D_MODEL), lambda i, n: (0, 0)),
            pl.BlockSpec((None, D_MODEL, tn), lambda i, n: (i, 0, n)),
            pl.BlockSpec((None, 1, tn), lambda i, n: (i, 0, n)),
        ],
        out_specs=pl.BlockSpec((None, COND_ROWS, tn), lambda i, n: (i, 0, n)),
        out_shape=jax.ShapeDtypeStruct((DEPTH, COND_ROWS, 3 * D_MODEL), F32),
        compiler_params=pltpu.CompilerParams(
            dimension_semantics=("arbitrary", "arbitrary"), vmem_limit_bytes=VMEM_LIMIT),
        name="ada_mod",
    )(cond, w_ada, b_ada.reshape(DEPTH, 1, 3 * D_MODEL))


def _cast_job_specs(jobs):
    in_specs = [pl.BlockSpec(blk, src_map) for _, blk, src_map, _, _ in jobs]
    out_specs = [pl.BlockSpec(blk, dst_map) for _, blk, _, _, dst_map in jobs]
    out_shape = [jax.ShapeDtypeStruct(shape, BF16) for _, _, _, shape, _ in jobs]
    return in_specs, out_specs, out_shape, [a for a, _, _, _, _ in jobs]


def _run_cast_jobs(src_refs, dst_refs):
    for src, dst in zip(src_refs, dst_refs):
        dst[...] = src[...].astype(BF16)


def _write_state(ref, r, j, value):
    if j is None:
        ref[r] = value
    else:
        for jj in range(ref.shape[1]):
            ref[r, jj] = value if jj == j else jnp.zeros_like(value)


def _state_plumbing(states, shapes, j, n_inputs, n_outputs):
    def index_map(shp, layer_index):
        return lambda t: (t, layer_index) + (0,) * (len(shp) - 2)

    if states is None:
        specs = [pl.BlockSpec((CTX_PER_STEP,) + shp[1:], index_map(shp, 0)) for shp in shapes]
        return [], specs, {}, ()
    specs = [pl.BlockSpec((CTX_PER_STEP, None) + shp[2:], index_map(shp, j)) for shp in shapes]
    in_specs = [pl.BlockSpec(memory_space=pl.ANY) for _ in shapes]
    aliases = {n_inputs + i: n_outputs + i for i in range(len(shapes))}
    return in_specs, specs, aliases, tuple(states)


def _finish(heads, w_out_ref, x_ref, mods_ref, g_post_ref, row, o_ref, rows=slice(None)):
    _, _, gate = _cond_row(mods_ref, row)
    og = jnp.concatenate(heads, axis=1)
    y = jnp.dot(og, w_out_ref[...], preferred_element_type=F32)
    o_ref[rows, :] = x_ref[rows, :] + gate * _rms(y, g_post_ref[...])


def _diff_lambda(lam_ref, lam_init):
    p = lam_ref[...]
    a = jnp.sum(p[0:1, :] * p[1:2, :], axis=-1, keepdims=True)
    b = jnp.sum(p[2:3, :] * p[3:4, :], axis=-1, keepdims=True)
    return jnp.exp(a) - jnp.exp(b) + lam_init


def _stack_diff_keys(k, k2_ref, r0, sk):
    rows = k.shape[0]
    lane = lax.broadcasted_iota(jnp.int32, (rows, LANES), 1)
    zero = jnp.zeros((rows, LANES), BF16)
    for h in range(HEADS):
        kh = k[:, h * LANES:(h + 1) * LANES]
        k2_ref[h, r0:r0 + rows, :] = jnp.where(lane < DA_DH, kh, zero)
        k2_ref[h, sk + r0:sk + r0 + rows, :] = jnp.where(lane >= DA_DH, kh, zero)


def _stack_values_t(vt, vt_ref, c0, cols):
    ones = _ones_rows(cols)
    for h in range(HEADS):
        vt_ref[h, 0:DA_DV, c0:c0 + cols] = vt[h * DA_DV:(h + 1) * DA_DV, :]
        vt_ref[h, DA_DV:DA_DV + V_PAD, c0:c0 + cols] = ones


def _diff_heads(q_ref, k2_ref, vt_ref, gate_ref, g_sub_ref, lam, lam_init, sk, rows=slice(None)):
    g_sub = g_sub_ref[...]

    def scores(h):
        return lax.dot_general(k2_ref[h], q_ref[rows, h * LANES:(h + 1) * LANES], _TRANS_B,
                               preferred_element_type=F32)

    heads = []
    pending = [scores(h) for h in range(SCORES_AHEAD)]
    for h in range(HEADS):
        lo, hi = h * LANES, (h + 1) * LANES
        s = pending.pop(0)
        if h + SCORES_AHEAD < HEADS:
            pending.append(scores(h + SCORES_AHEAD))
        s0 = s[0:sk, :]
        s1 = s[sk:2 * sk, :]
        e0 = jnp.exp2(s0 - jnp.max(s0, axis=0, keepdims=True)).astype(BF16)
        e1 = jnp.exp2(s1 - jnp.max(s1, axis=0, keepdims=True)).astype(BF16)
        pv0 = jnp.dot(vt_ref[h], e0, preferred_element_type=F32)
        pv1 = jnp.dot(vt_ref[h], e1, preferred_element_type=F32)
        a0 = 1.0 / pv0[DA_DV:DA_DV + 1, :]
        a1 = lam / pv1[DA_DV:DA_DV + 1, :]
        ot = pv0[0:DA_DV, :] * a0 - pv1[0:DA_DV, :] * a1
        ot = ot * lax.rsqrt(jnp.mean(ot * ot, axis=0, keepdims=True) + NORM_EPS)
        o = (ot.T * g_sub) * (1.0 - lam_init)
        heads.append((o * gate_ref[rows, lo:hi]).astype(BF16))
    return heads


MLA_IN = MLA_Q_LORA + MLA_KV_LORA + MLA_ROPE + D_MODEL
_QA = slice(0, MLA_Q_LORA)
_KV = slice(MLA_Q_LORA, MLA_Q_LORA + MLA_KV_LORA)
_PE = slice(_KV.stop, _KV.stop + LANES)
_GT = slice(_KV.stop + MLA_ROPE, MLA_IN)


def _proj_t(hb, w_t_ref, rows):
    return lax.dot_general(hb, w_t_ref[rows, :], _TRANS_B, preferred_element_type=F32)


def _mla_queries(hb, w_in_ref, g_qa_ref):
    q_a = _proj_t(hb, w_in_ref, _QA)
    return _rms(q_a, g_qa_ref[...]).astype(BF16)


def _mla_expand(ckv, kpe, w_kvb_ref, kf_ref, vf_ref, r0, rows):
    zero = jnp.zeros((rows, LANES - MLA_ROPE), BF16)
    for h in range(HEADS):
        lo = h * (MLA_NOPE + MLA_DV)
        kv = jnp.dot(ckv, w_kvb_ref[:, lo:lo + MLA_NOPE + MLA_DV], preferred_element_type=F32)
        kf_ref[h, r0:r0 + rows, 0:MLA_NOPE] = kv[:, 0:MLA_NOPE].astype(BF16)
        kf_ref[h, r0:r0 + rows, MLA_NOPE:MLA_NOPE + MLA_ROPE] = kpe
        kf_ref[h, r0:r0 + rows, MLA_NOPE + MLA_ROPE:MLA_QK_PAD] = zero
        vf_ref[h, r0:r0 + rows, :] = kv[:, MLA_NOPE:MLA_NOPE + MLA_DV].astype(BF16)


def _mla_heads(q_ref, kf_ref, vf_ref, gate_ref, rows=slice(None)):
    def scores(h):
        return lax.dot_general(q_ref[rows, h * MLA_QK_PAD:(h + 1) * MLA_QK_PAD], kf_ref[h], _TRANS_B,
                               preferred_element_type=F32)

    heads = []
    pending = [scores(h) for h in range(SCORES_AHEAD)]
    for h in range(HEADS):
        s = pending.pop(0)
        if h + SCORES_AHEAD < HEADS:
            pending.append(scores(h + SCORES_AHEAD))
        e = jnp.exp2(s - jnp.max(s, axis=-1, keepdims=True))
        inv = 1.0 / jnp.sum(e, axis=-1, keepdims=True)
        o = jnp.dot(e.astype(BF16), vf_ref[h], preferred_element_type=F32) * inv
        lo, hi = h * LANES, (h + 1) * LANES
        heads.append((o * gate_ref[rows, lo:hi]).astype(BF16))
    return heads


def _split_layer_refs(refs, n_cast):
    scratch = refs[-2:]
    outs = refs[:-2]
    n_out = 3 + n_cast
    main = outs[len(outs) - n_out:len(outs) - n_cast]
    _run_cast_jobs(outs[len(outs) - n_out - n_cast:len(outs) - n_out], outs[len(outs) - n_cast:])
    return main, scratch


def _layer_diff_ctx_kernel(create_j, lam_init, n_cast, x_ref, mods_ref, g_pre_ref, w_in_ref, g_post_ref,
                           w_out_ref, g_sub_ref, lam_ref, *refs):
    (o_ref, skt_ref, sv_ref), (k2_ref, vb_ref) = _split_layer_refs(refs, n_cast)
    hb = _modulated(x_ref, mods_ref, g_pre_ref, 0)
    n = D_MODEL
    q = (jnp.dot(hb, w_in_ref[:, 0:n], preferred_element_type=F32) * DA_Q_SCALE).astype(BF16)
    k = jnp.dot(hb, w_in_ref[:, n:2 * n], preferred_element_type=F32)
    v = jnp.dot(hb, w_in_ref[:, 2 * n:3 * n], preferred_element_type=F32)
    gate = _silu(jnp.dot(hb, w_in_ref[:, 3 * n:4 * n], preferred_element_type=F32)).astype(BF16)
    kb = k.astype(BF16)
    lam = _diff_lambda(lam_ref, lam_init)
    for r in range(CTX_PER_STEP):
        rows = slice(r * CTX_S, (r + 1) * CTX_S)
        vt = v[rows, :].T
        _write_state(skt_ref, r, create_j, k[rows, :].T)
        _write_state(sv_ref, r, create_j, v[rows, :].reshape(CTX_S, HEADS, DA_DV))
        _stack_diff_keys(kb[rows, :], k2_ref.at[r], 0, CTX_S)
        _stack_values_t(vt.astype(BF16), vb_ref.at[r], 0, CTX_S)
        heads = _diff_heads(q, k2_ref.at[r], vb_ref.at[r], gate, g_sub_ref, lam, lam_init, CTX_S, rows)
        _finish(heads, w_out_ref, x_ref, mods_ref, g_post_ref, 0, o_ref, rows)


def _layer_diff_ctx(x, mods, g_pre, g_post, w_in, wj, w_out, wo, g_sub, lam_p, states, layer, j, n_layers,
                    cast_jobs=()):
    lam_init = 0.8 - 0.6 * math.exp(-0.3 * layer)
    shapes = [(N_CTX_B, n_layers, D_MODEL, CTX_S), (N_CTX_B, n_layers, CTX_S, HEADS, DA_DV)]
    st_in_specs, st_out_specs, aliases, st_args = _state_plumbing(states, shapes, j, 8, 1)
    cj_in, cj_out, cj_shapes, cj_args = _cast_job_specs(cast_jobs)
    tok_spec = pl.BlockSpec((CTX_PER_STEP * CTX_S, D_MODEL), lambda t: (t, 0))
    return pl.pallas_call(
        functools.partial(_layer_diff_ctx_kernel, j if states is None else None, lam_init, len(cast_jobs)),
        grid=(N_CTX_B // CTX_PER_STEP,),
        in_specs=[
            tok_spec,
            pl.BlockSpec((None, COND_ROWS, 3 * D_MODEL), lambda t: (layer, 0, 0)),
            pl.BlockSpec((None, 1, D_MODEL), lambda t: (layer, 0, 0)),
            pl.BlockSpec((None, D_MODEL, 4 * D_MODEL), lambda t: (wj, 0, 0)),
            pl.BlockSpec((None, 1, D_MODEL), lambda t: (layer, 0, 0)),
            pl.BlockSpec((None, D_MODEL, D_MODEL), lambda t: (wo, 0, 0)),
            pl.BlockSpec((None, 1, DA_DV), lambda t: (j, 0, 0)),
            pl.BlockSpec((None, 4, DA_DH), lambda t: (j, 0, 0)),
        ] + st_in_specs + cj_in,
        out_specs=[tok_spec] + st_out_specs + cj_out,
        out_shape=[jax.ShapeDtypeStruct(x.shape, F32)] + [jax.ShapeDtypeStruct(shp, F32) for shp in shapes]
        + cj_shapes,
        input_output_aliases=aliases,
        scratch_shapes=[
            pltpu.VMEM((CTX_PER_STEP, HEADS, 2 * CTX_S, LANES), BF16),
            pltpu.VMEM((CTX_PER_STEP, HEADS, DA_DV + V_PAD, CTX_S), BF16),
        ],
        compiler_params=pltpu.CompilerParams(
            dimension_semantics=("arbitrary",), vmem_limit_bytes=VMEM_LIMIT),
        name="layer_diff_ctx",
    )(x, mods, g_pre, w_in, g_post, w_out, g_sub, lam_p, *st_args, *cj_args)


def _layer_mla_ctx_kernel(create_j, n_cast, x_ref, mods_ref, g_pre_ref, w_in_ref, g_qa_ref, w_qb_ref, g_kva_ref,
                          g_post_ref, w_out_ref, w_kvb_ref, *refs):
    (o_ref, ckv_ref, kpe_ref), (kf_ref, vf_ref) = _split_layer_refs(refs, n_cast)
    hb = _modulated(x_ref, mods_ref, g_pre_ref, 0)
    qn = _mla_queries(hb, w_in_ref, g_qa_ref)
    q = (jnp.dot(qn, w_qb_ref[...], preferred_element_type=F32) * MLA_Q_SCALE).astype(BF16)
    ckv = _rms(_proj_t(hb, w_in_ref, _KV), g_kva_ref[...])
    gate = _silu(_proj_t(hb, w_in_ref, _GT)).astype(BF16)
    pe = _proj_t(hb, w_in_ref, _PE)
    for r in range(CTX_PER_STEP):
        rows = slice(r * CTX_S, (r + 1) * CTX_S)
        kpe = pe[rows, 0:MLA_ROPE]
        _write_state(ckv_ref, r, create_j, ckv[rows, :])
        _write_state(kpe_ref, r, create_j, kpe)
        _mla_expand(ckv[rows, :].astype(BF16), kpe.astype(BF16), w_kvb_ref, kf_ref.at[r], vf_ref.at[r],
                    0, CTX_S)
        heads = _mla_heads(q, kf_ref.at[r], vf_ref.at[r], gate, rows)
        _finish(heads, w_out_ref, x_ref, mods_ref, g_post_ref, 0, o_ref, rows)


def _layer_mla_ctx(x, mods, g_pre, g_post, w_in, wj, g_qa, w_qb, g_kva, w_out, wo, w_kvb, states, layer, j,
                   n_layers, cast_jobs=()):
    shapes = [(N_CTX_B, n_layers, CTX_S, MLA_KV_LORA), (N_CTX_B, n_layers, CTX_S, MLA_ROPE)]
    st_in_specs, st_out_specs, aliases, st_args = _state_plumbing(states, shapes, j, 10, 1)
    cj_in, cj_out, cj_shapes, cj_args = _cast_job_specs(cast_jobs)
    tok_spec = pl.BlockSpec((CTX_PER_STEP * CTX_S, D_MODEL), lambda t: (t, 0))
    return pl.pallas_call(
        functools.partial(_layer_mla_ctx_kernel, j if states is None else None, len(cast_jobs)),
        grid=(N_CTX_B // CTX_PER_STEP,),
        in_specs=[
            tok_spec,
            pl.BlockSpec((None, COND_ROWS, 3 * D_MODEL), lambda t: (layer, 0, 0)),
            pl.BlockSpec((None, 1, D_MODEL), lambda t: (layer, 0, 0)),
            pl.BlockSpec((None, MLA_IN, D_MODEL), lambda t: (wj, 0, 0)),
            pl.BlockSpec((None, 1, MLA_Q_LORA), lambda t: (j, 0, 0)),
            pl.BlockSpec((None, MLA_Q_LORA, HEADS * MLA_QK_PAD), lambda t: (j, 0, 0)),
            pl.BlockSpec((None, 1, MLA_KV_LORA), lambda t: (j, 0, 0)),
            pl.BlockSpec((None, 1, D_MODEL), lambda t: (layer, 0, 0)),
            pl.BlockSpec((None, D_MODEL, D_MODEL), lambda t: (wo, 0, 0)),
            pl.BlockSpec((None, MLA_KV_LORA, HEADS * (MLA_NOPE + MLA_DV)), lambda t: (j, 0, 0)),
        ] + st_in_specs + cj_in,
        out_specs=[tok_spec] + st_out_specs + cj_out,
        out_shape=[jax.ShapeDtypeStruct(x.shape, F32)] + [jax.ShapeDtypeStruct(shp, F32) for shp in shapes]
        + cj_shapes,
        input_output_aliases=aliases,
        scratch_shapes=[
            pltpu.VMEM((CTX_PER_STEP, HEADS, CTX_S, MLA_QK_PAD), BF16),
            pltpu.VMEM((CTX_PER_STEP, HEADS, CTX_S, MLA_DV), BF16),
        ],
        compiler_params=pltpu.CompilerParams(
            dimension_semantics=("arbitrary",), vmem_limit_bytes=VMEM_LIMIT),
        name="layer_mla_ctx",
    )(x, mods, g_pre, w_in, g_qa, w_qb, g_kva, g_post, w_out, w_kvb, *st_args, *cj_args)


LAT_TILES = LAT_S // TM
LAT_QBLOCKS = LAT_S // TQ


def _lat_specs():
    tile = lambda s: jnp.minimum(s, LAT_TILES - 1)
    qblk = lambda s: jnp.maximum(s - LAT_TILES, 0)
    x_proj = pl.BlockSpec((TM, D_MODEL), lambda b, s: (b * LAT_TILES + tile(s), 0))
    x_attn = pl.BlockSpec((TQ, D_MODEL), lambda b, s: (b * LAT_QBLOCKS + qblk(s), 0))
    rope = pl.BlockSpec((TM, LANES), lambda b, s: (tile(s), 0))
    return x_proj, x_attn, rope


def _query_rows():
    i = pl.program_id(1) - LAT_TILES
    return pl.ds(pl.multiple_of(i * TQ, TQ), TQ)


def _layer_diff_lat_kernel(lam_init, xp_ref, xa_ref, mods_ref, g_pre_ref, w_in_ref, cos_ref, sin_ref,
                           ckt_ref, cv_ref, g_post_ref, w_out_ref, g_sub_ref, lam_ref,
                           o_ref, k2_ref, vb_ref, q_s, gate_s):
    step = pl.program_id(1)
    row = 1 + pl.program_id(0)
    sk = LAT_S + PAST

    for t in range(LAT_TILES):
        @pl.when(step == t)
        def _(t=t):
            r0 = t * TM
            hb = _modulated(xp_ref, mods_ref, g_pre_ref, row)
            cos = cos_ref[...]
            sin = sin_ref[...]
            n = D_MODEL
            q = jnp.dot(hb, w_in_ref[:, 0:n], preferred_element_type=F32)
            k = jnp.dot(hb, w_in_ref[:, n:2 * n], preferred_element_type=F32)
            cos_q = cos * DA_Q_SCALE
            sin_q = sin * DA_Q_SCALE
            slabs = [slice(h * LANES, (h + 1) * LANES) for h in range(HEADS)]
            q_s[r0:r0 + TM, :] = jnp.concatenate(
                [_rope_slab(q[:, sl], cos_q, sin_q).astype(BF16) for sl in slabs], axis=1)
            kb = jnp.concatenate([_rope_slab(k[:, sl], cos, sin).astype(BF16) for sl in slabs], axis=1)
            _stack_diff_keys(kb, k2_ref, r0, sk)
            vt = jnp.dot(hb, w_in_ref[:, 2 * n:3 * n], preferred_element_type=F32).T.astype(BF16)
            _stack_values_t(vt, vb_ref, r0, TM)
            gate_s[r0:r0 + TM, :] = _silu(
                jnp.dot(hb, w_in_ref[:, 3 * n:4 * n], preferred_element_type=F32)).astype(BF16)

    @pl.when(step == 0)
    def _():
        lane = lax.broadcasted_iota(jnp.int32, (PAST, LANES), 1)
        zero = jnp.zeros((PAST, LANES), BF16)
        ones = _ones_rows(PAST)
        for h in range(HEADS):
            ch = ckt_ref[h * LANES:(h + 1) * LANES, :].T.astype(BF16)
            k2_ref[h, LAT_S:sk, :] = jnp.where(lane < DA_DH, ch, zero)
            k2_ref[h, sk + LAT_S:2 * sk, :] = jnp.where(lane >= DA_DH, ch, zero)
            vb_ref[h, 0:DA_DV, LAT_S:sk] = cv_ref[:, h, :].T.astype(BF16)
            vb_ref[h, DA_DV:DA_DV + V_PAD, LAT_S:sk] = ones

    @pl.when(step >= LAT_TILES)
    def _():
        lam = _diff_lambda(lam_ref, lam_init)
        heads = _diff_heads(q_s, k2_ref, vb_ref, gate_s, g_sub_ref, lam, lam_init, sk, _query_rows())
        _finish(heads, w_out_ref, xa_ref, mods_ref, g_post_ref, row, o_ref)


def _layer_diff_lat(x, mods, g_pre, g_post, w_in, wj, w_out, wo, cos_t, sin_t, cache_kt, cache_v, g_sub, lam_p,
                    layer, j):
    lam_init = 0.8 - 0.6 * math.exp(-0.3 * layer)
    sk = LAT_S + PAST
    x_proj, x_attn, rope = _lat_specs()
    return pl.pallas_call(
        functools.partial(_layer_diff_lat_kernel, lam_init),
        grid=(N_LAT_B, LAT_TILES + LAT_QBLOCKS),
        in_specs=[
            x_proj, x_attn,
            pl.BlockSpec((None, COND_ROWS, 3 * D_MODEL), lambda b, s: (layer, 0, 0)),
            pl.BlockSpec((None, 1, D_MODEL), lambda b, s: (layer, 0, 0)),
            pl.BlockSpec((None, D_MODEL, 4 * D_MODEL), lambda b, s: (wj, 0, 0)),
            rope, rope,
            pl.BlockSpec((None, None, D_MODEL, PAST), lambda b, s: (b, j, 0, 0)),
            pl.BlockSpec((None, None, PAST, HEADS, DA_DV), lambda b, s: (b, j, 0, 0, 0)),
            pl.BlockSpec((None, 1, D_MODEL), lambda b, s: (layer, 0, 0)),
            pl.BlockSpec((None, D_MODEL, D_MODEL), lambda b, s: (wo, 0, 0)),
            pl.BlockSpec((None, 1, DA_DV), lambda b, s: (j, 0, 0)),
            pl.BlockSpec((None, 4, DA_DH), lambda b, s: (j, 0, 0)),
        ],
        out_specs=x_attn,
        out_shape=jax.ShapeDtypeStruct(x.shape, F32),
        scratch_shapes=[
            pltpu.VMEM((HEADS, 2 * sk, LANES), BF16),
            pltpu.VMEM((HEADS, DA_DV + V_PAD, sk), BF16),
            pltpu.VMEM((LAT_S, D_MODEL), BF16),
            pltpu.VMEM((LAT_S, D_MODEL), BF16),
        ],
        compiler_params=pltpu.CompilerParams(
            dimension_semantics=("arbitrary", "arbitrary"), vmem_limit_bytes=VMEM_LIMIT),
        name="layer_diff_lat",
    )(x, x, mods, g_pre, w_in, cos_t, sin_t, cache_kt, cache_v, g_post, w_out, g_sub, lam_p)


def _layer_mla_lat_kernel(xp_ref, xa_ref, mods_ref, g_pre_ref, w_in_ref, g_qa_ref, w_qb_ref, g_kva_ref,
                          cos_ref, sin_ref, cckv_ref, ckpe_ref, g_post_ref, w_out_ref, w_kvb_ref,
                          o_ref, kf_ref, vf_ref, q_s, gate_s):
    step = pl.program_id(1)
    row = 1 + pl.program_id(0)

    for t in range(LAT_TILES):
        @pl.when(step == t)
        def _(t=t):
            r0 = t * TM
            hb = _modulated(xp_ref, mods_ref, g_pre_ref, row)
            cos = cos_ref[...]
            sin = sin_ref[...]
            qn = _mla_queries(hb, w_in_ref, g_qa_ref)
            for h in range(HEADS):
                lo = h * MLA_QK_PAD
                q = jnp.dot(qn, w_qb_ref[:, lo:lo + MLA_QK_PAD], preferred_element_type=F32) * MLA_Q_SCALE
                q_s[r0:r0 + TM, lo:lo + LANES] = q[:, 0:LANES].astype(BF16)
                q_s[r0:r0 + TM, lo + LANES:lo + 2 * LANES] = _rope_slab(
                    q[:, LANES:2 * LANES], cos, sin).astype(BF16)
            ckv = _rms(_proj_t(hb, w_in_ref, _KV), g_kva_ref[...]).astype(BF16)
            gate_s[r0:r0 + TM, :] = _silu(_proj_t(hb, w_in_ref, _GT)).astype(BF16)
            kpe = _rope_slab(_proj_t(hb, w_in_ref, _PE), cos, sin)[:, 0:MLA_ROPE].astype(BF16)
            _mla_expand(ckv, kpe, w_kvb_ref, kf_ref, vf_ref, r0, TM)

    @pl.when(step == 0)
    def _():
        _mla_expand(cckv_ref[...].astype(BF16), ckpe_ref[...].astype(BF16), w_kvb_ref, kf_ref, vf_ref,
                    LAT_S, PAST)

    @pl.when(step >= LAT_TILES)
    def _():
        heads = _mla_heads(q_s, kf_ref, vf_ref, gate_s, _query_rows())
        _finish(heads, w_out_ref, xa_ref, mods_ref, g_post_ref, row, o_ref)


def _layer_mla_lat(x, mods, g_pre, g_post, w_in, wj, g_qa, w_qb, g_kva, w_out, wo, w_kvb, cos_t, sin_t,
                   cache_ckv, cache_kpe, layer, j):
    sk = LAT_S + PAST
    x_proj, x_attn, rope = _lat_specs()
    return pl.pallas_call(
        _layer_mla_lat_kernel,
        grid=(N_LAT_B, LAT_TILES + LAT_QBLOCKS),
        in_specs=[
            x_proj, x_attn,
            pl.BlockSpec((None, COND_ROWS, 3 * D_MODEL), lambda b, s: (layer, 0, 0)),
            pl.BlockSpec((None, 1, D_MODEL), lambda b, s: (layer, 0, 0)),
            pl.BlockSpec((None, MLA_IN, D_MODEL), lambda b, s: (wj, 0, 0)),
            pl.BlockSpec((None, 1, MLA_Q_LORA), lambda b, s: (j, 0, 0)),
            pl.BlockSpec((None, MLA_Q_LORA, HEADS * MLA_QK_PAD), lambda b, s: (j, 0, 0)),
            pl.BlockSpec((None, 1, MLA_KV_LORA), lambda b, s: (j, 0, 0)),
            rope, rope,
            pl.BlockSpec((None, None, PAST, MLA_KV_LORA), lambda b, s: (b, j, 0, 0)),
            pl.BlockSpec((None, None, PAST, MLA_ROPE), lambda b, s: (b, j, 0, 0)),
            pl.BlockSpec((None, 1, D_MODEL), lambda b, s: (layer, 0, 0)),
            pl.BlockSpec((None, D_MODEL, D_MODEL), lambda b, s: (wo, 0, 0)),
            pl.BlockSpec((None, MLA_KV_LORA, HEADS * (MLA_NOPE + MLA_DV)), lambda b, s: (j, 0, 0)),
        ],
        out_specs=x_attn,
        out_shape=jax.ShapeDtypeStruct(x.shape, F32),
        scratch_shapes=[
            pltpu.VMEM((HEADS, sk, MLA_QK_PAD), BF16),
            pltpu.VMEM((HEADS, sk, MLA_DV), BF16),
            pltpu.VMEM((LAT_S, HEADS * MLA_QK_PAD), BF16),
            pltpu.VMEM((LAT_S, D_MODEL), BF16),
        ],
        compiler_params=pltpu.CompilerParams(
            dimension_semantics=("arbitrary", "arbitrary"), vmem_limit_bytes=VMEM_LIMIT),
        name="layer_mla_lat",
    )(x, x, mods, g_pre, w_in, g_qa, w_qb, g_kva, cos_t, sin_t, cache_ckv, cache_kpe, g_post, w_out, w_kvb)


def _rope_tables():
    rows = LAT_S // GRID_W
    row = jnp.repeat(jnp.arange(rows), GRID_W).astype(F32)
    col = jnp.tile(jnp.arange(GRID_W), rows).astype(F32)
    n_freq = DA_DH // 4
    inv = ROPE_THETA ** (-jnp.arange(n_freq, dtype=F32) / n_freq)
    ang = jnp.concatenate([row[:, None] * inv, col[:, None] * inv], axis=-1)
    cos, sin = jnp.cos(ang), jnp.sin(ang)
    return (jnp.concatenate([cos, cos, cos, cos], axis=-1),
            jnp.concatenate([-sin, sin, -sin, sin], axis=-1))


def kernel(x_prompt, x_sample, cache_diff_k, cache_diff_v, cache_mla_ckv, cache_mla_kpe,
           c, c_ctx, w_ada, b_ada, g_pre, g_post, w_out,
           da_w_in, da_lam_q1, da_lam_k1, da_lam_q2, da_lam_k2, da_g_sub,
           mla_w_in, mla_g_qa, mla_w_qb, mla_g_kva, mla_w_kvb):
    assert DA_DH == MLA_ROPE
    n_diff = (DEPTH + 1) // 2
    n_mla = DEPTH // 2

    cond = jnp.concatenate(
        [c_ctx[None, :], c, jnp.zeros((COND_ROWS - 1 - N_LAT_B, D_MODEL), F32)], axis=0)
    mods = _ada_call(cond, w_ada, b_ada)

    cos_t, sin_t = _rope_tables()
    g_pre3 = g_pre.reshape(DEPTH, 1, D_MODEL)
    g_post3 = g_post.reshape(DEPTH, 1, D_MODEL)
    steps = N_CTX_B // CTX_PER_STEP
    w_out_first = w_out[0:2].astype(BF16)
    da_w_in_first = da_w_in[0:1].astype(BF16)
    g_sub3 = da_g_sub.reshape(n_diff, 1, DA_DV)
    lam_p = jnp.stack([da_lam_q1, da_lam_k1, da_lam_q2, da_lam_k2], axis=1)
    cache_dkt = jnp.transpose(cache_diff_k, (0, 1, 3, 4, 5, 2)).reshape(N_LAT_B, n_diff, D_MODEL, PAST)

    mla_w_in_t = jnp.swapaxes(mla_w_in, 1, 2)
    mla_w_in_first = mla_w_in_t[0:1].astype(BF16)
    w_qb4 = mla_w_qb.reshape(n_mla, MLA_Q_LORA, HEADS, MLA_NOPE + MLA_ROPE)
    w_qb_b = jnp.pad(w_qb4, ((0, 0), (0, 0), (0, 0), (0, MLA_QK_PAD - MLA_NOPE - MLA_ROPE))).reshape(
        n_mla, MLA_Q_LORA, HEADS * MLA_QK_PAD).astype(BF16)
    w_kvb_b = mla_w_kvb.astype(BF16)
    g_qa3 = mla_g_qa.reshape(n_mla, 1, MLA_Q_LORA)
    g_kva3 = mla_g_kva.reshape(n_mla, 1, MLA_KV_LORA)

    x_ctx = x_prompt.reshape(N_CTX_B * CTX_S, D_MODEL)
    x_lat = x_sample.reshape(N_LAT_B * LAT_S, D_MODEL)
    st_diff = None
    st_mla = None

    assert DEPTH == 4
    da_w = [(da_w_in_first, 0), None]
    mla_w = [(mla_w_in_first, 0), None]
    w_o = [(w_out_first, 0), (w_out_first, 1), None, None]
    wide = D_MODEL * 4 // steps
    da_cast = [(da_w_in, (None, D_MODEL, wide), lambda b: (1, 0, b),
                (1, D_MODEL, 4 * D_MODEL), lambda b: (0, 0, b))]
    cols = D_MODEL // steps
    per_layer = steps // 2
    late_casts = [(mla_w_in_t, (None, MLA_IN, cols), lambda t: (1, 0, t),
                   (1, MLA_IN, D_MODEL), lambda t: (0, 0, t)),
                  (w_out, (None, D_MODEL // per_layer, D_MODEL), lambda t: (2 + t // per_layer, t % per_layer, 0),
                   (2, D_MODEL, D_MODEL), lambda t: (t // per_layer, t % per_layer, 0))]

    for i in range(DEPTH):
        j = i // 2
        if i % 2 == 0:
            w_j = da_w[j]
            outs = _layer_diff_ctx(x_ctx, mods, g_pre3, g_post3, *w_j, *w_o[i], g_sub3, lam_p, st_diff,
                                   i, j, n_diff, cast_jobs=da_cast if i == 0 else ())
            x_ctx, st_dkt, st_dv = outs[0:3]
            st_diff = (st_dkt, st_dv)
            if i == 0:
                da_w[1] = (outs[3], 0)
            x_lat = _layer_diff_lat(x_lat, mods, g_pre3, g_post3, *w_j, *w_o[i], cos_t, sin_t,
                                    cache_dkt, cache_diff_v, g_sub3, lam_p, i, j)
        else:
            w_j = mla_w[j]
            outs = _layer_mla_ctx(x_ctx, mods, g_pre3, g_post3, *w_j, g_qa3, w_qb_b, g_kva3, *w_o[i], w_kvb_b,
                                  st_mla, i, j, n_mla, cast_jobs=late_casts if i == 1 else ())
            x_ctx, st_ckv, st_kpe = outs[0:3]
            st_mla = (st_ckv, st_kpe)
            if i == 1:
                mla_w[1] = (outs[3], 0)
                w_o[2], w_o[3] = (outs[4], 0), (outs[4], 1)
            x_lat = _layer_mla_lat(x_lat, mods, g_pre3, g_post3, *w_j, g_qa3, w_qb_b, g_kva3, *w_o[i], w_kvb_b,
                                   cos_t, sin_t, cache_mla_ckv, cache_mla_kpe, i, j)

    return (x_ctx.reshape(N_CTX_B, CTX_S, D_MODEL),
            x_lat.reshape(N_LAT_B, LAT_S, D_MODEL),
            jnp.transpose(st_dkt.reshape(N_CTX_B, n_diff, HEADS, 2, DA_DH, CTX_S), (0, 1, 5, 2, 3, 4)),
            st_dv,
            st_ckv,
            st_kpe)
```

```python
import functools
import math

import jax
import jax.numpy as jnp
from jax import lax
from jax.experimental import pallas as pl
from jax.experimental.pallas import tpu as pltpu

F32 = jnp.float32
BF16 = jnp.bfloat16

D_MODEL = 1024
DEPTH = 4
N_CTX_B = 16
CTX_S = 256
N_LAT_B = 4
LAT_S = 1024
PAST = 256
GRID_W = 64
HEADS = 8
DA_DH = 64
DA_DV = 128
MLA_Q_LORA = 384
MLA_KV_LORA = 256
MLA_NOPE = 128
MLA_ROPE = 64
MLA_DV = 128
MLA_QK_PAD = 256
ROPE_THETA = 10000.0
NORM_EPS = 1e-6
COND_ROWS = 8

LANES = 128
V_PAD = 16
TQ = 256
TM = 1024
VMEM_LIMIT = 56 * 1024 * 1024

_TRANS_B = (((1,), (1,)), ((), ()))
SCORES_AHEAD = 2
CTX_PER_STEP = 2
DA_Q_SCALE = DA_DH ** -0.5 * math.log2(math.e)
MLA_Q_SCALE = (MLA_NOPE + MLA_ROPE) ** -0.5 * math.log2(math.e)


def _silu(x):
    return x * (1.0 / (1.0 + jnp.exp(-x)))


def _rms(x, g):
    r = lax.rsqrt(jnp.mean(x * x, axis=-1, keepdims=True) + NORM_EPS)
    return (x * r) * g


def _rope_slab(x, cos, sin_signed):
    lane = lax.broadcasted_iota(jnp.int32, x.shape, 1)
    first_half = (lane % 64) < 32
    partner = jnp.where(first_half, pltpu.roll(x, 96, axis=1), pltpu.roll(x, 32, axis=1))
    return x * cos + partner * sin_signed


def _ones_rows(cols):
    row = lax.broadcasted_iota(jnp.int32, (V_PAD, cols), 0)
    return jnp.where(row == 0, 1.0, 0.0).astype(BF16)


def _cond_row(mods_ref, row):
    m = mods_ref[pl.ds(row, 1), :]
    return m[:, 0:D_MODEL], m[:, D_MODEL:2 * D_MODEL], m[:, 2 * D_MODEL:3 * D_MODEL]


def _modulated(x_ref, mods_ref, g_ref, row):
    shift, scale, _ = _cond_row(mods_ref, row)
    h = _rms(x_ref[...], g_ref[...]) * (1.0 + scale) + shift
    return h.astype(BF16)


def _ada_kernel(cond_ref, w_ref, b_ref, o_ref):
    a = _silu(cond_ref[...]).astype(BF16)
    o_ref[...] = jnp.dot(a, w_ref[...].astype(BF16), preferred_element_type=F32) + b_ref[...]


def _ada_call(cond, w_ada, b_ada):
    tn = 3 * D_MODEL
    return pl.pallas_call(
        _ada_kernel,
        grid=(DEPTH, 3 * D_MODEL // tn),
        in_specs=[
            pl.BlockSpec((COND_ROWS, D_MODEL), lambda i, n: (0, 0)),
            pl.BlockSpec((None, D_MODEL, tn), lambda i, n: (i, 0, n)),
            pl.BlockSpec((None, 1, tn), lambda i, n: (i, 0, n)),
        ],
        out_specs=pl.BlockSpec((None, COND_ROWS, tn), lambda i, n: (i, 0, n)),
        out_shape=jax.ShapeDtypeStruct((DEPTH, COND_ROWS, 3 * D_MODEL), F32),
        compiler_params=pltpu.CompilerParams(
            dimension_semantics=("arbitrary", "arbitrary"), vmem_limit_bytes=VMEM_LIMIT),
        name="ada_mod",
    )(cond, w_ada, b_ada.reshape(DEPTH, 1, 3 * D_MODEL))


def _cast_job_specs(jobs):
    in_specs = [pl.BlockSpec(blk, src_map) for _, blk, src_map, _, _ in jobs]
    out_specs = [pl.BlockSpec(blk, dst_map) for _, blk, _, _, dst_map in jobs]
    out_shape = [jax.ShapeDtypeStruct(shape, BF16) for _, _, _, shape, _ in jobs]
    return in_specs, out_specs, out_shape, [a for a, _, _, _, _ in jobs]


def _run_cast_jobs(src_refs, dst_refs):
    for src, dst in zip(src_refs, dst_refs):
        dst[...] = src[...].astype(BF16)


def _write_state(ref, r, j, value):
    if j is None:
        ref[r] = value
    else:
        for jj in range(ref.shape[1]):
            ref[r, jj] = value if jj == j else jnp.zeros_like(value)


def _state_plumbing(states, shapes, j, n_inputs, n_outputs):
    def index_map(shp, layer_index):
        return lambda t: (t, layer_index) + (0,) * (len(shp) - 2)

    if states is None:
        specs = [pl.BlockSpec((CTX_PER_STEP,) + shp[1:], index_map(shp, 0)) for shp in shapes]
        return [], specs, {}, ()
    specs = [pl.BlockSpec((CTX_PER_STEP, None) + shp[2:], index_map(shp, j)) for shp in shapes]
    in_specs = [pl.BlockSpec(memory_space=pl.ANY) for _ in shapes]
    aliases = {n_inputs + i: n_outputs + i for i in range(len(shapes))}
    return in_specs, specs, aliases, tuple(states)


def _finish(heads, w_out_ref, x_ref, mods_ref, g_post_ref, row, o_ref, rows=slice(None)):
    _, _, gate = _cond_row(mods_ref, row)
    og = jnp.concatenate(heads, axis=1)
    y = jnp.dot(og, w_out_ref[...], preferred_element_type=F32)
    o_ref[rows, :] = x_ref[rows, :] + gate * _rms(y, g_post_ref[...])


def _diff_lambda(lam_ref, lam_init):
    p = lam_ref[...]
    a = jnp.sum(p[0:1, :] * p[1:2, :], axis=-1, keepdims=True)
    b = jnp.sum(p[2:3, :] * p[3:4, :], axis=-1, keepdims=True)
    return jnp.exp(a) - jnp.exp(b) + lam_init


def _stack_diff_keys(k, k2_ref, r0, sk):
    rows = k.shape[0]
    lane = lax.broadcasted_iota(jnp.int32, (rows, LANES), 1)
    zero = jnp.zeros((rows, LANES), BF16)
    for h in range(HEADS):
        kh = k[:, h * LANES:(h + 1) * LANES]
        k2_ref[h, r0:r0 + rows, :] = jnp.where(lane < DA_DH, kh, zero)
        k2_ref[h, sk + r0:sk + r0 + rows, :] = jnp.where(lane >= DA_DH, kh, zero)


def _stack_values_t(vt, vt_ref, c0, cols):
    ones = _ones_rows(cols)
    for h in range(HEADS):
        vt_ref[h, 0:DA_DV, c0:c0 + cols] = vt[h * DA_DV:(h + 1) * DA_DV, :]
        vt_ref[h, DA_DV:DA_DV + V_PAD, c0:c0 + cols] = ones


def _diff_heads(q_ref, k2_ref, vt_ref, gate_ref, g_sub_ref, lam, lam_init, sk, rows=slice(None)):
    g_sub = g_sub_ref[...]

    def scores(h):
        return lax.dot_general(k2_ref[h], q_ref[rows, h * LANES:(h + 1) * LANES], _TRANS_B,
                               preferred_element_type=F32)

    heads = []
    pending = [scores(h) for h in range(SCORES_AHEAD)]
    for h in range(HEADS):
        lo, hi = h * LANES, (h + 1) * LANES
        s = pending.pop(0)
        if h + SCORES_AHEAD < HEADS:
            pending.append(scores(h + SCORES_AHEAD))
        s0 = s[0:sk, :]
        s1 = s[sk:2 * sk, :]
        e0 = jnp.exp2(s0 - jnp.max(s0, axis=0, keepdims=True)).astype(BF16)
        e1 = jnp.exp2(s1 - jnp.max(s1, axis=0, keepdims=True)).astype(BF16)
        pv0 = jnp.dot(vt_ref[h], e0, preferred_element_type=F32)
        pv1 = jnp.dot(vt_ref[h], e1, preferred_element_type=F32)
        a0 = 1.0 / pv0[DA_DV:DA_DV + 1, :]
        a1 = lam / pv1[DA_DV:DA_DV + 1, :]
        ot = pv0[0:DA_DV, :] * a0 - pv1[0:DA_DV, :] * a1
        ot = ot * lax.rsqrt(jnp.mean(ot * ot, axis=0, keepdims=True) + NORM_EPS)
        o = (ot.T * g_sub) * (1.0 - lam_init)
        heads.append((o * gate_ref[rows, lo:hi]).astype(BF16))
    return heads


MLA_IN = MLA_Q_LORA + MLA_KV_LORA + MLA_ROPE + D_MODEL
_QA = slice(0, MLA_Q_LORA)
_KV = slice(MLA_Q_LORA, MLA_Q_LORA + MLA_KV_LORA)
_PE = slice(_KV.stop, _KV.stop + LANES)
_GT = slice(_KV.stop + MLA_ROPE, MLA_IN)


def _proj_t(hb, w_t_ref, rows):
    return lax.dot_general(hb, w_t_ref[rows, :], _TRANS_B, preferred_element_type=F32)


def _mla_queries(hb, w_in_ref, g_qa_ref):
    q_a = _proj_t(hb, w_in_ref, _QA)
    return _rms(q_a, g_qa_ref[...]).astype(BF16)


def _mla_expand(ckv, kpe, w_kvb_ref, kf_ref, vf_ref, r0, rows):
    zero = jnp.zeros((rows, LANES - MLA_ROPE), BF16)
    for h in range(HEADS):
        lo = h * (MLA_NOPE + MLA_DV)
        kv = jnp.dot(ckv, w_kvb_ref[:, lo:lo + MLA_NOPE + MLA_DV], preferred_element_type=F32)
        kf_ref[h, r0:r0 + rows, 0:MLA_NOPE] = kv[:, 0:MLA_NOPE].astype(BF16)
        kf_ref[h, r0:r0 + rows, MLA_NOPE:MLA_NOPE + MLA_ROPE] = kpe
        kf_ref[h, r0:r0 + rows, MLA_NOPE + MLA_ROPE:MLA_QK_PAD] = zero
        vf_ref[h, r0:r0 + rows, :] = kv[:, MLA_NOPE:MLA_NOPE + MLA_DV].astype(BF16)


def _mla_heads(q_ref, kf_ref, vf_ref, gate_ref, rows=slice(None)):
    def scores(h):
        return lax.dot_general(q_ref[rows, h * MLA_QK_PAD:(h + 1) * MLA_QK_PAD], kf_ref[h], _TRANS_B,
                               preferred_element_type=F32)

    heads = []
    pending = [scores(h) for h in range(SCORES_AHEAD)]
    for h in range(HEADS):
        s = pending.pop(0)
        if h + SCORES_AHEAD < HEADS:
            pending.append(scores(h + SCORES_AHEAD))
        e = jnp.exp2(s - jnp.max(s, axis=-1, keepdims=True))
        inv = 1.0 / jnp.sum(e, axis=-1, keepdims=True)
        o = jnp.dot(e.astype(BF16), vf_ref[h], preferred_element_type=F32) * inv
        lo, hi = h * LANES, (h + 1) * LANES
        heads.append((o * gate_ref[rows, lo:hi]).astype(BF16))
    return heads


def _split_layer_refs(refs, n_cast):
    scratch = refs[-2:]
    outs = refs[:-2]
    n_out = 3 + n_cast
    main = outs[len(outs) - n_out:len(outs) - n_cast]
    _run_cast_jobs(outs[len(outs) - n_out - n_cast:len(outs) - n_out], outs[len(outs) - n_cast:])
    return main, scratch


def _layer_diff_ctx_kernel(create_j, lam_init, n_cast, x_ref, mods_ref, g_pre_ref, w_in_ref, g_post_ref,
                           w_out_ref, g_sub_ref, lam_ref, *refs):
    (o_ref, skt_ref, sv_ref), (k2_ref, vb_ref) = _split_layer_refs(refs, n_cast)
    hb = _modulated(x_ref, mods_ref, g_pre_ref, 0)
    n = D_MODEL
    q = (jnp.dot(hb, w_in_ref[:, 0:n], preferred_element_type=F32) * DA_Q_SCALE).astype(BF16)
    k = jnp.dot(hb, w_in_ref[:, n:2 * n], preferred_element_type=F32)
    v = jnp.dot(hb, w_in_ref[:, 2 * n:3 * n], preferred_element_type=F32)
    gate = _silu(jnp.dot(hb, w_in_ref[:, 3 * n:4 * n], preferred_element_type=F32)).astype(BF16)
    kb = k.astype(BF16)
    lam = _diff_lambda(lam_ref, lam_init)
    for r in range(CTX_PER_STEP):
        rows = slice(r * CTX_S, (r + 1) * CTX_S)
        vt = v[rows, :].T
        _write_state(skt_ref, r, create_j, k[rows, :].T)
        _write_state(sv_ref, r, create_j, v[rows, :].reshape(CTX_S, HEADS, DA_DV))
        _stack_diff_keys(kb[rows, :], k2_ref.at[r], 0, CTX_S)
        _stack_values_t(vt.astype(BF16), vb_ref.at[r], 0, CTX_S)
        heads = _diff_heads(q, k2_ref.at[r], vb_ref.at[r], gate, g_sub_ref, lam, lam_init, CTX_S, rows)
        _finish(heads, w_out_ref, x_ref, mods_ref, g_post_ref, 0, o_ref, rows)


def _layer_diff_ctx(x, mods, g_pre, g_post, w_in, wj, w_out, wo, g_sub, lam_p, states, layer, j, n_layers,
                    cast_jobs=()):
    lam_init = 0.8 - 0.6 * math.exp(-0.3 * layer)
    shapes = [(N_CTX_B, n_layers, D_MODEL, CTX_S), (N_CTX_B, n_layers, CTX_S, HEADS, DA_DV)]
    st_in_specs, st_out_specs, aliases, st_args = _state_plumbing(states, shapes, j, 8, 1)
    cj_in, cj_out, cj_shapes, cj_args = _cast_job_specs(cast_jobs)
    tok_spec = pl.BlockSpec((CTX_PER_STEP * CTX_S, D_MODEL), lambda t: (t, 0))
    return pl.pallas_call(
        functools.partial(_layer_diff_ctx_kernel, j if states is None else None, lam_init, len(cast_jobs)),
        grid=(N_CTX_B // CTX_PER_STEP,),
        in_specs=[
            tok_spec,
            pl.BlockSpec((None, COND_ROWS, 3 * D_MODEL), lambda t: (layer, 0, 0)),
            pl.BlockSpec((None, 1, D_MODEL), lambda t: (layer, 0, 0)),
            pl.BlockSpec((None, D_MODEL, 4 * D_MODEL), lambda t: (wj, 0, 0)),
            pl.BlockSpec((None, 1, D_MODEL), lambda t: (layer, 0, 0)),
            pl.BlockSpec((None, D_MODEL, D_MODEL), lambda t: (wo, 0, 0)),
            pl.BlockSpec((None, 1, DA_DV), lambda t: (j, 0, 0)),
            pl.BlockSpec((None, 4, DA_DH), lambda t: (j, 0, 0)),
        ] + st_in_specs + cj_in,
        out_specs=[tok_spec] + st_out_specs + cj_out,
        out_shape=[jax.ShapeDtypeStruct(x.shape, F32)] + [jax.ShapeDtypeStruct(shp, F32) for shp in shapes]
        + cj_shapes,
        input_output_aliases=aliases,
        scratch_shapes=[
            pltpu.VMEM((CTX_PER_STEP, HEADS, 2 * CTX_S, LANES), BF16),
            pltpu.VMEM((CTX_PER_STEP, HEADS, DA_DV + V_PAD, CTX_S), BF16),
        ],
        compiler_params=pltpu.CompilerParams(
            dimension_semantics=("arbitrary",), vmem_limit_bytes=VMEM_LIMIT),
        name="layer_diff_ctx",
    )(x, mods, g_pre, w_in, g_post, w_out, g_sub, lam_p, *st_args, *cj_args)


def _layer_mla_ctx_kernel(create_j, n_cast, x_ref, mods_ref, g_pre_ref, w_in_ref, g_qa_ref, w_qb_ref, g_kva_ref,
                          g_post_ref, w_out_ref, w_kvb_ref, *refs):
    (o_ref, ckv_ref, kpe_ref), (kf_ref, vf_ref) = _split_layer_refs(refs, n_cast)
    hb = _modulated(x_ref, mods_ref, g_pre_ref, 0)
    qn = _mla_queries(hb, w_in_ref, g_qa_ref)
    q = (jnp.dot(qn, w_qb_ref[...], preferred_element_type=F32) * MLA_Q_SCALE).astype(BF16)
    ckv = _rms(_proj_t(hb, w_in_ref, _KV), g_kva_ref[...])
    gate = _silu(_proj_t(hb, w_in_ref, _GT)).astype(BF16)
    pe = _proj_t(hb, w_in_ref, _PE)
    for r in range(CTX_PER_STEP):
        rows = slice(r * CTX_S, (r + 1) * CTX_S)
        kpe = pe[rows, 0:MLA_ROPE]
        _write_state(ckv_ref, r, create_j, ckv[rows, :])
        _write_state(kpe_ref, r, create_j, kpe)
        _mla_expand(ckv[rows, :].astype(BF16), kpe.astype(BF16), w_kvb_ref, kf_ref.at[r], vf_ref.at[r],
                    0, CTX_S)
        heads = _mla_heads(q, kf_ref.at[r], vf_ref.at[r], gate, rows)
        _finish(heads, w_out_ref, x_ref, mods_ref, g_post_ref, 0, o_ref, rows)


def _layer_mla_ctx(x, mods, g_pre, g_post, w_in, wj, g_qa, w_qb, g_kva, w_out, wo, w_kvb, states, layer, j,
                   n_layers, cast_jobs=()):
    shapes = [(N_CTX_B, n_layers, CTX_S, MLA_KV_LORA), (N_CTX_B, n_layers, CTX_S, MLA_ROPE)]
    st_in_specs, st_out_specs, aliases, st_args = _state_plumbing(states, shapes, j, 10, 1)
    cj_in, cj_out, cj_shapes, cj_args = _cast_job_specs(cast_jobs)
    tok_spec = pl.BlockSpec((CTX_PER_STEP * CTX_S, D_MODEL), lambda t: (t, 0))
    return pl.pallas_call(
        functools.partial(_layer_mla_ctx_kernel, j if states is None else None, len(cast_jobs)),
        grid=(N_CTX_B // CTX_PER_STEP,),
        in_specs=[
            tok_spec,
            pl.BlockSpec((None, COND_ROWS, 3 * D_MODEL), lambda t: (layer, 0, 0)),
            pl.BlockSpec((None, 1, D_MODEL), lambda t: (layer, 0, 0)),
            pl.BlockSpec((None, MLA_IN, D_MODEL), lambda t: (wj, 0, 0)),
            pl.BlockSpec((None, 1, MLA_Q_LORA), lambda t: (j, 0, 0)),
            pl.BlockSpec((None, MLA_Q_LORA, HEADS * MLA_QK_PAD), lambda t: (j, 0, 0)),
            pl.BlockSpec((None, 1, MLA_KV_LORA), lambda t: (j, 0, 0)),
            pl.BlockSpec((None, 1, D_MODEL), lambda t: (layer, 0, 0)),
            pl.BlockSpec((None, D_MODEL, D_MODEL), lambda t: (wo, 0, 0)),
            pl.BlockSpec((None, MLA_KV_LORA, HEADS * (MLA_NOPE + MLA_DV)), lambda t: (j, 0, 0)),
        ] + st_in_specs + cj_in,
        out_specs=[tok_spec] + st_out_specs + cj_out,
        out_shape=[jax.ShapeDtypeStruct(x.shape, F32)] + [jax.ShapeDtypeStruct(shp, F32) for shp in shapes]
        + cj_shapes,
        input_output_aliases=aliases,
        scratch_shapes=[
            pltpu.VMEM((CTX_PER_STEP, HEADS, CTX_S, MLA_QK_PAD), BF16),
            pltpu.VMEM((CTX_PER_STEP, HEADS, CTX_S, MLA_DV), BF16),
        ],
        compiler_params=pltpu.CompilerParams(
            dimension_semantics=("arbitrary",), vmem_limit_bytes=VMEM_LIMIT),
        name="layer_mla_ctx",
    )(x, mods, g_pre, w_in, g_qa, w_qb, g_kva, g_post, w_out, w_kvb, *st_args, *cj_args)


LAT_TILES = LAT_S // TM
LAT_QBLOCKS = LAT_S // TQ


def _lat_specs():
    tile = lambda s: jnp.minimum(s, LAT_TILES - 1)
    qblk = lambda s: jnp.maximum(s - LAT_TILES, 0)
    x_proj = pl.BlockSpec((TM, D_MODEL), lambda b, s: (b * LAT_TILES + tile(s), 0))
    x_attn = pl.BlockSpec((TQ, D_MODEL), lambda b, s: (b * LAT_QBLOCKS + qblk(s), 0))
    rope = pl.BlockSpec((TM, LANES), lambda b, s: (tile(s), 0))
    return x_proj, x_attn, rope


def _query_rows():
    i = pl.program_id(1) - LAT_TILES
    return pl.ds(pl.multiple_of(i * TQ, TQ), TQ)


def _layer_diff_lat_kernel(lam_init, xp_ref, xa_ref, mods_ref, g_pre_ref, w_in_ref, cos_ref, sin_ref,
                           ckt_ref, cv_ref, g_post_ref, w_out_ref, g_sub_ref, lam_ref,
                           o_ref, k2_ref, vb_ref, q_s, gate_s):
    step = pl.program_id(1)
    row = 1 + pl.program_id(0)
    sk = LAT_S + PAST

    for t in range(LAT_TILES):
        @pl.when(step == t)
        def _(t=t):
            r0 = t * TM
            hb = _modulated(xp_ref, mods_ref, g_pre_ref, row)
            cos = cos_ref[...]
            sin = sin_ref[...]
            n = D_MODEL
            q = jnp.dot(hb, w_in_ref[:, 0:n], preferred_element_type=F32)
            k = jnp.dot(hb, w_in_ref[:, n:2 * n], preferred_element_type=F32)
            cos_q = cos * DA_Q_SCALE
            sin_q = sin * DA_Q_SCALE
            slabs = [slice(h * LANES, (h + 1) * LANES) for h in range(HEADS)]
            q_s[r0:r0 + TM, :] = jnp.concatenate(
                [_rope_slab(q[:, sl], cos_q, sin_q).astype(BF16) for sl in slabs], axis=1)
            kb = jnp.concatenate([_rope_slab(k[:, sl], cos, sin).astype(BF16) for sl in slabs], axis=1)
            _stack_diff_keys(kb, k2_ref, r0, sk)
            vt = jnp.dot(hb, w_in_ref[:, 2 * n:3 * n], preferred_element_type=F32).T.astype(BF16)
            _stack_values_t(vt, vb_ref, r0, TM)
            gate_s[r0:r0 + TM, :] = _silu(
                jnp.dot(hb, w_in_ref[:, 3 * n:4 * n], preferred_element_type=F32)).astype(BF16)

    @pl.when(step == 0)
    def _():
        lane = lax.broadcasted_iota(jnp.int32, (PAST, LANES), 1)
        zero = jnp.zeros((PAST, LANES), BF16)
        ones = _ones_rows(PAST)
        for h in range(HEADS):
            ch = ckt_ref[h * LANES:(h + 1) * LANES, :].T.astype(BF16)
            k2_ref[h, LAT_S:sk, :] = jnp.where(lane < DA_DH, ch, zero)
            k2_ref[h, sk + LAT_S:2 * sk, :] = jnp.where(lane >= DA_DH, ch, zero)
            vb_ref[h, 0:DA_DV, LAT_S:sk] = cv_ref[:, h, :].T.astype(BF16)
            vb_ref[h, DA_DV:DA_DV + V_PAD, LAT_S:sk] = ones

    @pl.when(step >= LAT_TILES)
    def _():
        lam = _diff_lambda(lam_ref, lam_init)
        heads = _diff_heads(q_s, k2_ref, vb_ref, gate_s, g_sub_ref, lam, lam_init, sk, _query_rows())
        _finish(heads, w_out_ref, xa_ref, mods_ref, g_post_ref, row, o_ref)


def _layer_diff_lat(x, mods, g_pre, g_post, w_in, wj, w_out, wo, cos_t, sin_t, cache_kt, cache_v, g_sub, lam_p,
                    layer, j):
    lam_init = 0.8 - 0.6 * math.exp(-0.3 * layer)
    sk = LAT_S + PAST
    x_proj, x_attn, rope = _lat_specs()
    return pl.pallas_call(
        functools.partial(_layer_diff_lat_kernel, lam_init),
        grid=(N_LAT_B, LAT_TILES + LAT_QBLOCKS),
        in_specs=[
            x_proj, x_attn,
            pl.BlockSpec((None, COND_ROWS, 3 * D_MODEL), lambda b, s: (layer, 0, 0)),
            pl.BlockSpec((None, 1, D_MODEL), lambda b, s: (layer, 0, 0)),
            pl.BlockSpec((None, D_MODEL, 4 * D_MODEL), lambda b, s: (wj, 0, 0)),
            rope, rope,
            pl.BlockSpec((None, None, D_MODEL, PAST), lambda b, s: (b, j, 0, 0)),
            pl.BlockSpec((None, None, PAST, HEADS, DA_DV), lambda b, s: (b, j, 0, 0, 0)),
            pl.BlockSpec((None, 1, D_MODEL), lambda b, s: (layer, 0, 0)),
            pl.BlockSpec((None, D_MODEL, D_MODEL), lambda b, s: (wo, 0, 0)),
            pl.BlockSpec((None, 1, DA_DV), lambda b, s: (j, 0, 0)),
            pl.BlockSpec((None, 4, DA_DH), lambda b, s: (j, 0, 0)),
        ],
        out_specs=x_attn,
        out_shape=jax.ShapeDtypeStruct(x.shape, F32),
        scratch_shapes=[
            pltpu.VMEM((HEADS, 2 * sk, LANES), BF16),
            pltpu.VMEM((HEADS, DA_DV + V_PAD, sk), BF16),
            pltpu.VMEM((LAT_S, D_MODEL), BF16),
            pltpu.VMEM((LAT_S, D_MODEL), BF16),
        ],
        compiler_params=pltpu.CompilerParams(
            dimension_semantics=("arbitrary", "arbitrary"), vmem_limit_bytes=VMEM_LIMIT),
        name="layer_diff_lat",
    )(x, x, mods, g_pre, w_in, cos_t, sin_t, cache_kt, cache_v, g_post, w_out, g_sub, lam_p)


def _layer_mla_lat_kernel(xp_ref, xa_ref, mods_ref, g_pre_ref, w_in_ref, g_qa_ref, w_qb_ref, g_kva_ref,
                          cos_ref, sin_ref, cckv_ref, ckpe_ref, g_post_ref, w_out_ref, w_kvb_ref,
                          o_ref, kf_ref, vf_ref, q_s, gate_s):
    step = pl.program_id(1)
    row = 1 + pl.program_id(0)

    for t in range(LAT_TILES):
        @pl.when(step == t)
        def _(t=t):
            r0 = t * TM
            hb = _modulated(xp_ref, mods_ref, g_pre_ref, row)
            cos = cos_ref[...]
            sin = sin_ref[...]
            qn = _mla_queries(hb, w_in_ref, g_qa_ref)
            for h in range(HEADS):
                lo = h * MLA_QK_PAD
                q = jnp.dot(qn, w_qb_ref[:, lo:lo + MLA_QK_PAD], preferred_element_type=F32) * MLA_Q_SCALE
                q_s[r0:r0 + TM, lo:lo + LANES] = q[:, 0:LANES].astype(BF16)
                q_s[r0:r0 + TM, lo + LANES:lo + 2 * LANES] = _rope_slab(
                    q[:, LANES:2 * LANES], cos, sin).astype(BF16)
            ckv = _rms(_proj_t(hb, w_in_ref, _KV), g_kva_ref[...]).astype(BF16)
            gate_s[r0:r0 + TM, :] = _silu(_proj_t(hb, w_in_ref, _GT)).astype(BF16)
            kpe = _rope_slab(_proj_t(hb, w_in_ref, _PE), cos, sin)[:, 0:MLA_ROPE].astype(BF16)
            _mla_expand(ckv, kpe, w_kvb_ref, kf_ref, vf_ref, r0, TM)

    @pl.when(step == 0)
    def _():
        _mla_expand(cckv_ref[...].astype(BF16), ckpe_ref[...].astype(BF16), w_kvb_ref, kf_ref, vf_ref,
                    LAT_S, PAST)

    @pl.when(step >= LAT_TILES)
    def _():
        heads = _mla_heads(q_s, kf_ref, vf_ref, gate_s, _query_rows())
        _finish(heads, w_out_ref, xa_ref, mods_ref, g_post_ref, row, o_ref)


def _layer_mla_lat(x, mods, g_pre, g_post, w_in, wj, g_qa, w_qb, g_kva, w_out, wo, w_kvb, cos_t, sin_t,
                   cache_ckv, cache_kpe, layer, j):
    sk = LAT_S + PAST
    x_proj, x_attn, rope = _lat_specs()
    return pl.pallas_call(
        _layer_mla_lat_kernel,
        grid=(N_LAT_B, LAT_TILES + LAT_QBLOCKS),
        in_specs=[
            x_proj, x_attn,
            pl.BlockSpec((None, COND_ROWS, 3 * D_MODEL), lambda b, s: (layer, 0, 0)),
            pl.BlockSpec((None, 1, D_MODEL), lambda b, s: (layer, 0, 0)),
            pl.BlockSpec((None, MLA_IN, D_MODEL), lambda b, s: (wj, 0, 0)),
            pl.BlockSpec((None, 1, MLA_Q_LORA), lambda b, s: (j, 0, 0)),
            pl.BlockSpec((None, MLA_Q_LORA, HEADS * MLA_QK_PAD), lambda b, s: (j, 0, 0)),
            pl.BlockSpec((None, 1, MLA_KV_LORA), lambda b, s: (j, 0, 0)),
            rope, rope,
            pl.BlockSpec((None, None, PAST, MLA_KV_LORA), lambda b, s: (b, j, 0, 0)),
            pl.BlockSpec((None, None, PAST, MLA_ROPE), lambda b, s: (b, j, 0, 0)),
            pl.BlockSpec((None, 1, D_MODEL), lambda b, s: (layer, 0, 0)),
            pl.BlockSpec((None, D_MODEL, D_MODEL), lambda b, s: (wo, 0, 0)),
            pl.BlockSpec((None, MLA_KV_LORA, HEADS * (MLA_NOPE + MLA_DV)), lambda b, s: (j, 0, 0)),
        ],
        out_specs=x_attn,
        out_shape=jax.ShapeDtypeStruct(x.shape, F32),
        scratch_shapes=[
            pltpu.VMEM((HEADS, sk, MLA_QK_PAD), BF16),
            pltpu.VMEM((HEADS, sk, MLA_DV), BF16),
            pltpu.VMEM((LAT_S, HEADS * MLA_QK_PAD), BF16),
            pltpu.VMEM((LAT_S, D_MODEL), BF16),
        ],
        compiler_params=pltpu.CompilerParams(
            dimension_semantics=("arbitrary", "arbitrary"), vmem_limit_bytes=VMEM_LIMIT),
        name="layer_mla_lat",
    )(x, x, mods, g_pre, w_in, g_qa, w_qb, g_kva, cos_t, sin_t, cache_ckv, cache_kpe, g_post, w_out, w_kvb)


def _rope_tables():
    rows = LAT_S // GRID_W
    row = jnp.repeat(jnp.arange(rows), GRID_W).astype(F32)
    col = jnp.tile(jnp.arange(GRID_W), rows).astype(F32)
    n_freq = DA_DH // 4
    inv = ROPE_THETA ** (-jnp.arange(n_freq, dtype=F32) / n_freq)
    ang = jnp.concatenate([row[:, None] * inv, col[:, None] * inv], axis=-1)
    cos, sin = jnp.cos(ang), jnp.sin(ang)
    return (jnp.concatenate([cos, cos, cos, cos], axis=-1),
            jnp.concatenate([-sin, sin, -sin, sin], axis=-1))


def kernel(x_prompt, x_sample, cache_diff_k, cache_diff_v, cache_mla_ckv, cache_mla_kpe,
           c, c_ctx, w_ada, b_ada, g_pre, g_post, w_out,
           da_w_in, da_lam_q1, da_lam_k1, da_lam_q2, da_lam_k2, da_g_sub,
           mla_w_in, mla_g_qa, mla_w_qb, mla_g_kva, mla_w_kvb):
    assert DA_DH == MLA_ROPE
    n_diff = (DEPTH + 1) // 2
    n_mla = DEPTH // 2

    cond = jnp.concatenate(
        [c_ctx[None, :], c, jnp.zeros((COND_ROWS - 1 - N_LAT_B, D_MODEL), F32)], axis=0)
    mods = _ada_call(cond, w_ada, b_ada)

    cos_t, sin_t = _rope_tables()
    g_pre3 = g_pre.reshape(DEPTH, 1, D_MODEL)
    g_post3 = g_post.reshape(DEPTH, 1, D_MODEL)
    steps = N_CTX_B // CTX_PER_STEP
    w_out_first = w_out[0:1].astype(BF16)
    da_w_in_first = da_w_in[0:1].astype(BF16)
    g_sub3 = da_g_sub.reshape(n_diff, 1, DA_DV)
    lam_p = jnp.stack([da_lam_q1, da_lam_k1, da_lam_q2, da_lam_k2], axis=1)
    cache_dkt = jnp.transpose(cache_diff_k, (0, 1, 3, 4, 5, 2)).reshape(N_LAT_B, n_diff, D_MODEL, PAST)

    mla_w_in_t = jnp.swapaxes(mla_w_in, 1, 2)
    w_qb4 = mla_w_qb.reshape(n_mla, MLA_Q_LORA, HEADS, MLA_NOPE + MLA_ROPE)
    w_qb_b = jnp.pad(w_qb4, ((0, 0), (0, 0), (0, 0), (0, MLA_QK_PAD - MLA_NOPE - MLA_ROPE))).reshape(
        n_mla, MLA_Q_LORA, HEADS * MLA_QK_PAD).astype(BF16)
    g_qa3 = mla_g_qa.reshape(n_mla, 1, MLA_Q_LORA)
    g_kva3 = mla_g_kva.reshape(n_mla, 1, MLA_KV_LORA)

    x_ctx = x_prompt.reshape(N_CTX_B * CTX_S, D_MODEL)
    x_lat = x_sample.reshape(N_LAT_B * LAT_S, D_MODEL)
    st_diff = None
    st_mla = None

    assert DEPTH == 4
    da_w = [(da_w_in_first, 0), None]
    mla_w = [None, None]
    w_o = [(w_out_first, 0), None, None, None]
    wide = D_MODEL * 4 // steps
    cols = D_MODEL // steps
    per_layer = steps // 2
    kv_cols = HEADS * (MLA_NOPE + MLA_DV) // steps

    def mla_in_cast(jj):
        return (mla_w_in_t, (None, MLA_IN, cols), lambda t: (jj, 0, t), (1, MLA_IN, D_MODEL), lambda t: (0, 0, t))

    first_casts = [
        (da_w_in, (None, D_MODEL, wide), lambda t: (1, 0, t), (1, D_MODEL, 4 * D_MODEL), lambda t: (0, 0, t)),
        mla_in_cast(0),
        (w_out, (None, D_MODEL // steps, D_MODEL), lambda t: (1, t, 0), (1, D_MODEL, D_MODEL), lambda t: (0, t, 0)),
        (mla_w_kvb, (n_mla, MLA_KV_LORA, kv_cols), lambda t: (0, 0, t), mla_w_kvb.shape, lambda t: (0, 0, t)),
    ]
    late_casts = [
        mla_in_cast(1),
        (w_out, (None, D_MODEL // per_layer, D_MODEL), lambda t: (2 + t // per_layer, t % per_layer, 0),
         (2, D_MODEL, D_MODEL), lambda t: (t // per_layer, t % per_layer, 0)),
    ]

    for i in range(DEPTH):
        j = i // 2
        if i % 2 == 0:
            w_j = da_w[j]
            outs = _layer_diff_ctx(x_ctx, mods, g_pre3, g_post3, *w_j, *w_o[i], g_sub3, lam_p, st_diff,
                                   i, j, n_diff, cast_jobs=first_casts if i == 0 else ())
            x_ctx, st_dkt, st_dv = outs[0:3]
            st_diff = (st_dkt, st_dv)
            if i == 0:
                da_w[1], mla_w[0], w_o[1], w_kvb_b = (outs[3], 0), (outs[4], 0), (outs[5], 0), outs[6]
            x_lat = _layer_diff_lat(x_lat, mods, g_pre3, g_post3, *w_j, *w_o[i], cos_t, sin_t,
                                    cache_dkt, cache_diff_v, g_sub3, lam_p, i, j)
        else:
            w_j = mla_w[j]
            outs = _layer_mla_ctx(x_ctx, mods, g_pre3, g_post3, *w_j, g_qa3, w_qb_b, g_kva3, *w_o[i], w_kvb_b,
                                  st_mla, i, j, n_mla, cast_jobs=late_casts if i == 1 else ())
            x_ctx, st_ckv, st_kpe = outs[0:3]
            st_mla = (st_ckv, st_kpe)
            if i == 1:
                mla_w[1] = (outs[3], 0)
                w_o[2], w_o[3] = (outs[4], 0), (outs[4], 1)
            x_lat = _layer_mla_lat(x_lat, mods, g_pre3, g_post3, *w_j, g_qa3, w_qb_b, g_kva3, *w_o[i], w_kvb_b,
                                   cos_t, sin_t, cache_mla_ckv, cache_mla_kpe, i, j)

    return (x_ctx.reshape(N_CTX_B, CTX_S, D_MODEL),
            x_lat.reshape(N_LAT_B, LAT_S, D_MODEL),
            jnp.transpose(st_dkt.reshape(N_CTX_B, n_diff, HEADS, 2, DA_DH, CTX_S), (0, 1, 5, 2, 3, 4)),
            st_dv,
            st_ckv,
            st_kpe)
```

```python
import functools
import math

import jax
import jax.numpy as jnp
from jax import lax
from jax.experimental import pallas as pl
from jax.experimental.pallas import tpu as pltpu

F32 = jnp.float32
BF16 = jnp.bfloat16

D_MODEL = 1024
DEPTH = 4
N_CTX_B = 16
CTX_S = 256
N_LAT_B = 4
LAT_S = 1024
PAST = 256
GRID_W = 64
HEADS = 8
DA_DH = 64
DA_DV = 128
MLA_Q_LORA = 384
MLA_KV_LORA = 256
MLA_NOPE = 128
MLA_ROPE = 64
MLA_DV = 128
MLA_QK_PAD = 256
ROPE_THETA = 10000.0
NORM_EPS = 1e-6
COND_ROWS = 8

LANES = 128
V7X_VMEM_MIB = 64
V_PAD = 16
TQ = 256
TM = 1024
VMEM_LIMIT = (V7X_VMEM_MIB - 8) * 1024 * 1024

_TRANS_B = (((1,), (1,)), ((), ()))
SCORES_AHEAD = 2
CTX_PER_STEP = 2
DA_Q_SCALE = DA_DH ** -0.5 * math.log2(math.e)
MLA_Q_SCALE = (MLA_NOPE + MLA_ROPE) ** -0.5 * math.log2(math.e)


def _silu(x):
    return x * (1.0 / (1.0 + jnp.exp(-x)))


def _rms(x, g):
    r = lax.rsqrt(jnp.mean(x * x, axis=-1, keepdims=True) + NORM_EPS)
    return (x * r) * g


def _rope_slab(x, cos, sin_signed):
    half = DA_DH // 2
    lane = lax.broadcasted_iota(jnp.int32, x.shape, 1)
    first_half = (lane % DA_DH) < half
    partner = jnp.where(first_half, pltpu.roll(x, LANES - half, axis=1), pltpu.roll(x, half, axis=1))
    return x * cos + partner * sin_signed


def _ones_rows(cols):
    row = lax.broadcasted_iota(jnp.int32, (V_PAD, cols), 0)
    return jnp.where(row == 0, 1.0, 0.0).astype(BF16)


def _cond_row(mods_ref, row):
    m = mods_ref[pl.ds(row, 1), :]
    return m[:, 0:D_MODEL], m[:, D_MODEL:2 * D_MODEL], m[:, 2 * D_MODEL:3 * D_MODEL]


def _modulated(x_ref, mods_ref, g_ref, row):
    shift, scale, _ = _cond_row(mods_ref, row)
    h = _rms(x_ref[...], g_ref[...]) * (1.0 + scale) + shift
    return h.astype(BF16)


def _ada_kernel(cond_ref, w_ref, b_ref, o_ref):
    a = _silu(cond_ref[...]).astype(BF16)
    o_ref[...] = jnp.dot(a, w_ref[...].astype(BF16), preferred_element_type=F32) + b_ref[...]


def _ada_call(cond, w_ada, b_ada3, n_layers):
    return pl.pallas_call(
        _ada_kernel,
        grid=(n_layers,),
        in_specs=[
            pl.BlockSpec((COND_ROWS, D_MODEL), lambda i: (0, 0)),
            pl.BlockSpec((None, D_MODEL, 3 * D_MODEL), lambda i: (i, 0, 0)),
            pl.BlockSpec((None, 1, 3 * D_MODEL), lambda i: (i, 0, 0)),
        ],
        out_specs=pl.BlockSpec((None, COND_ROWS, 3 * D_MODEL), lambda i: (i, 0, 0)),
        out_shape=jax.ShapeDtypeStruct((n_layers, COND_ROWS, 3 * D_MODEL), F32),
        compiler_params=pltpu.CompilerParams(
            dimension_semantics=("arbitrary",), vmem_limit_bytes=VMEM_LIMIT),
        name="ada_mod",
    )(cond, w_ada, b_ada3)


def _cast_block(src_ref, dst_ref):
    dst_ref[...] = src_ref[...].astype(BF16)


def _cast_job(array, blk, src_map, shape, dst_map):
    return dict(args=[array], in_specs=[pl.BlockSpec(blk, src_map)], out_spec=pl.BlockSpec(blk, dst_map),
                out_shape=jax.ShapeDtypeStruct(shape, BF16), run=_cast_block)


def _ada_job(cond, w_ada, b_ada3, layer, steps):
    tn = 3 * D_MODEL // steps
    return dict(args=[cond, w_ada, b_ada3],
                in_specs=[pl.BlockSpec((COND_ROWS, D_MODEL), lambda t: (0, 0)),
                          pl.BlockSpec((None, D_MODEL, tn), lambda t: (layer, 0, t)),
                          pl.BlockSpec((None, 1, tn), lambda t: (layer, 0, t))],
                out_spec=pl.BlockSpec((None, COND_ROWS, tn), lambda t: (0, 0, t)),
                out_shape=jax.ShapeDtypeStruct((1, COND_ROWS, 3 * D_MODEL), F32), run=_ada_kernel)


def _side_job_specs(jobs):
    in_specs = [spec for job in jobs for spec in job["in_specs"]]
    args = [a for job in jobs for a in job["args"]]
    runners = tuple((len(job["args"]), job["run"]) for job in jobs)
    return in_specs, [job["out_spec"] for job in jobs], [job["out_shape"] for job in jobs], args, runners


def _write_state(ref, r, j, value):
    if j is None:
        ref[r] = value
    else:
        for jj in range(ref.shape[1]):
            ref[r, jj] = value if jj == j else jnp.zeros_like(value)


def _state_plumbing(states, shapes, j, n_inputs, n_outputs):
    def index_map(shp, layer_index):
        return lambda t: (t, layer_index) + (0,) * (len(shp) - 2)

    if states is None:
        specs = [pl.BlockSpec((CTX_PER_STEP,) + shp[1:], index_map(shp, 0)) for shp in shapes]
        return [], specs, {}, ()
    specs = [pl.BlockSpec((CTX_PER_STEP, None) + shp[2:], index_map(shp, j)) for shp in shapes]
    in_specs = [pl.BlockSpec(memory_space=pl.ANY) for _ in shapes]
    aliases = {n_inputs + i: n_outputs + i for i in range(len(shapes))}
    return in_specs, specs, aliases, tuple(states)


def _finish(heads, w_out_ref, x_ref, mods_ref, g_post_ref, row, o_ref, rows=slice(None)):
    _, _, gate = _cond_row(mods_ref, row)
    og = jnp.concatenate(heads, axis=1)
    y = jnp.dot(og, w_out_ref[...], preferred_element_type=F32)
    o_ref[rows, :] = x_ref[rows, :] + gate * _rms(y, g_post_ref[...])


def _diff_lambda(lam_ref, lam_init):
    p = lam_ref[...]
    a = jnp.sum(p[0:1, :] * p[1:2, :], axis=-1, keepdims=True)
    b = jnp.sum(p[2:3, :] * p[3:4, :], axis=-1, keepdims=True)
    return jnp.exp(a) - jnp.exp(b) + lam_init


def _stack_diff_keys(k, k2_ref, r0, sk):
    rows = k.shape[0]
    lane = lax.broadcasted_iota(jnp.int32, (rows, LANES), 1)
    zero = jnp.zeros((rows, LANES), BF16)
    for h in range(HEADS):
        kh = k[:, h * LANES:(h + 1) * LANES]
        k2_ref[h, r0:r0 + rows, :] = jnp.where(lane < DA_DH, kh, zero)
        k2_ref[h, sk + r0:sk + r0 + rows, :] = jnp.where(lane >= DA_DH, kh, zero)


def _stack_values_t(vt, vt_ref, c0, cols):
    ones = _ones_rows(cols)
    for h in range(HEADS):
        vt_ref[h, 0:DA_DV, c0:c0 + cols] = vt[h * DA_DV:(h + 1) * DA_DV, :]
        vt_ref[h, DA_DV:DA_DV + V_PAD, c0:c0 + cols] = ones


def _diff_heads(q_ref, k2_ref, vt_ref, gate_ref, g_sub_ref, lam, lam_init, sk, rows=slice(None)):
    g_sub = g_sub_ref[...]

    def scores(h):
        return lax.dot_general(k2_ref[h], q_ref[rows, h * LANES:(h + 1) * LANES], _TRANS_B,
                               preferred_element_type=F32)

    heads = []
    pending = [scores(h) for h in range(SCORES_AHEAD)]
    for h in range(HEADS):
        lo, hi = h * LANES, (h + 1) * LANES
        s = pending.pop(0)
        if h + SCORES_AHEAD < HEADS:
            pending.append(scores(h + SCORES_AHEAD))
        s0 = s[0:sk, :]
        s1 = s[sk:2 * sk, :]
        e0 = jnp.exp2(s0 - jnp.max(s0, axis=0, keepdims=True)).astype(BF16)
        e1 = jnp.exp2(s1 - jnp.max(s1, axis=0, keepdims=True)).astype(BF16)
        pv0 = jnp.dot(vt_ref[h], e0, preferred_element_type=F32)
        pv1 = jnp.dot(vt_ref[h], e1, preferred_element_type=F32)
        a0 = 1.0 / pv0[DA_DV:DA_DV + 1, :]
        a1 = lam / pv1[DA_DV:DA_DV + 1, :]
        ot = pv0[0:DA_DV, :] * a0 - pv1[0:DA_DV, :] * a1
        ot = ot * lax.rsqrt(jnp.mean(ot * ot, axis=0, keepdims=True) + NORM_EPS)
        o = (ot.T * g_sub) * (1.0 - lam_init)
        heads.append((o * gate_ref[rows, lo:hi]).astype(BF16))
    return heads


MLA_IN = MLA_Q_LORA + MLA_KV_LORA + MLA_ROPE + D_MODEL
_QA = slice(0, MLA_Q_LORA)
_KV = slice(MLA_Q_LORA, MLA_Q_LORA + MLA_KV_LORA)
_PE = slice(_KV.stop, _KV.stop + LANES)
_GT = slice(_KV.stop + MLA_ROPE, MLA_IN)


def _proj_t(hb, w_t_ref, rows):
    return lax.dot_general(hb, w_t_ref[rows, :], _TRANS_B, preferred_element_type=F32)


def _mla_queries(hb, w_in_ref, g_qa_ref):
    q_a = _proj_t(hb, w_in_ref, _QA)
    return _rms(q_a, g_qa_ref[...]).astype(BF16)


def _mla_expand(ckv, kpe, w_kvb_ref, kf_ref, vf_ref, r0, rows):
    zero = jnp.zeros((rows, LANES - MLA_ROPE), BF16)
    for h in range(HEADS):
        lo = h * (MLA_NOPE + MLA_DV)
        kv = jnp.dot(ckv, w_kvb_ref[:, lo:lo + MLA_NOPE + MLA_DV], preferred_element_type=F32)
        kf_ref[h, r0:r0 + rows, 0:MLA_NOPE] = kv[:, 0:MLA_NOPE].astype(BF16)
        kf_ref[h, r0:r0 + rows, MLA_NOPE:MLA_NOPE + MLA_ROPE] = kpe
        kf_ref[h, r0:r0 + rows, MLA_NOPE + MLA_ROPE:MLA_QK_PAD] = zero
        vf_ref[h, r0:r0 + rows, :] = kv[:, MLA_NOPE:MLA_NOPE + MLA_DV].astype(BF16)


def _mla_heads(q_ref, kf_ref, vf_ref, gate_ref, rows=slice(None)):
    def scores(h):
        return lax.dot_general(q_ref[rows, h * MLA_QK_PAD:(h + 1) * MLA_QK_PAD], kf_ref[h], _TRANS_B,
                               preferred_element_type=F32)

    heads = []
    pending = [scores(h) for h in range(SCORES_AHEAD)]
    for h in range(HEADS):
        s = pending.pop(0)
        if h + SCORES_AHEAD < HEADS:
            pending.append(scores(h + SCORES_AHEAD))
        e = jnp.exp2(s - jnp.max(s, axis=-1, keepdims=True))
        inv = 1.0 / jnp.sum(e, axis=-1, keepdims=True)
        o = jnp.dot(e.astype(BF16), vf_ref[h], preferred_element_type=F32) * inv
        lo, hi = h * LANES, (h + 1) * LANES
        heads.append((o * gate_ref[rows, lo:hi]).astype(BF16))
    return heads


def _split_layer_refs(refs, runners):
    scratch = refs[-2:]
    refs = refs[:-2]
    n_jobs = len(runners)
    job_out = refs[len(refs) - n_jobs:]
    main = refs[len(refs) - n_jobs - 3:len(refs) - n_jobs]
    pos = len(refs) - n_jobs - 3 - sum(n for n, _ in runners)
    for (n, run), out in zip(runners, job_out):
        run(*refs[pos:pos + n], out)
        pos += n
    return main, scratch


def _layer_diff_ctx_kernel(create_j, lam_init, runners, x_ref, mods_ref, g_pre_ref, w_in_ref, g_post_ref,
                           w_out_ref, g_sub_ref, lam_ref, *refs):
    (o_ref, skt_ref, sv_ref), (k2_ref, vb_ref) = _split_layer_refs(refs, runners)
    hb = _modulated(x_ref, mods_ref, g_pre_ref, 0)
    n = D_MODEL
    q = (jnp.dot(hb, w_in_ref[:, 0:n], preferred_element_type=F32) * DA_Q_SCALE).astype(BF16)
    k = jnp.dot(hb, w_in_ref[:, n:2 * n], preferred_element_type=F32)
    v = jnp.dot(hb, w_in_ref[:, 2 * n:3 * n], preferred_element_type=F32)
    gate = _silu(jnp.dot(hb, w_in_ref[:, 3 * n:4 * n], preferred_element_type=F32)).astype(BF16)
    kb = k.astype(BF16)
    lam = _diff_lambda(lam_ref, lam_init)
    for r in range(CTX_PER_STEP):
        rows = slice(r * CTX_S, (r + 1) * CTX_S)
        vt = v[rows, :].T
        _write_state(skt_ref, r, create_j, k[rows, :].T)
        _write_state(sv_ref, r, create_j, v[rows, :].reshape(CTX_S, HEADS, DA_DV))
        _stack_diff_keys(kb[rows, :], k2_ref.at[r], 0, CTX_S)
        _stack_values_t(vt.astype(BF16), vb_ref.at[r], 0, CTX_S)
        heads = _diff_heads(q, k2_ref.at[r], vb_ref.at[r], gate, g_sub_ref, lam, lam_init, CTX_S, rows)
        _finish(heads, w_out_ref, x_ref, mods_ref, g_post_ref, 0, o_ref, rows)


def _layer_diff_ctx(x, mods, mi, g_pre, g_post, w_in, wj, w_out, wo, g_sub, lam_p, states, layer, j, n_layers,
                    side_jobs=()):
    lam_init = 0.8 - 0.6 * math.exp(-0.3 * layer)
    shapes = [(N_CTX_B, n_layers, D_MODEL, CTX_S), (N_CTX_B, n_layers, CTX_S, HEADS, DA_DV)]
    st_in_specs, st_out_specs, aliases, st_args = _state_plumbing(states, shapes, j, 8, 1)
    cj_in, cj_out, cj_shapes, cj_args, runners = _side_job_specs(side_jobs)
    tok_spec = pl.BlockSpec((CTX_PER_STEP * CTX_S, D_MODEL), lambda t: (t, 0))
    return pl.pallas_call(
        functools.partial(_layer_diff_ctx_kernel, j if states is None else None, lam_init, runners),
        grid=(N_CTX_B // CTX_PER_STEP,),
        in_specs=[
            tok_spec,
            pl.BlockSpec((None, COND_ROWS, 3 * D_MODEL), lambda t: (mi, 0, 0)),
            pl.BlockSpec((None, 1, D_MODEL), lambda t: (layer, 0, 0)),
            pl.BlockSpec((None, D_MODEL, 4 * D_MODEL), lambda t: (wj, 0, 0)),
            pl.BlockSpec((None, 1, D_MODEL), lambda t: (layer, 0, 0)),
            pl.BlockSpec((None, D_MODEL, D_MODEL), lambda t: (wo, 0, 0)),
            pl.BlockSpec((None, 1, DA_DV), lambda t: (j, 0, 0)),
            pl.BlockSpec((None, 4, DA_DH), lambda t: (j, 0, 0)),
        ] + st_in_specs + cj_in,
        out_specs=[tok_spec] + st_out_specs + cj_out,
        out_shape=[jax.ShapeDtypeStruct(x.shape, F32)] + [jax.ShapeDtypeStruct(shp, F32) for shp in shapes]
        + cj_shapes,
        input_output_aliases=aliases,
        scratch_shapes=[
            pltpu.VMEM((CTX_PER_STEP, HEADS, 2 * CTX_S, LANES), BF16),
            pltpu.VMEM((CTX_PER_STEP, HEADS, DA_DV + V_PAD, CTX_S), BF16),
        ],
        compiler_params=pltpu.CompilerParams(
            dimension_semantics=("arbitrary",), vmem_limit_bytes=VMEM_LIMIT),
        name="layer_diff_ctx",
    )(x, mods, g_pre, w_in, g_post, w_out, g_sub, lam_p, *st_args, *cj_args)


def _layer_mla_ctx_kernel(create_j, runners, x_ref, mods_ref, g_pre_ref, w_in_ref, g_qa_ref, w_qb_ref, g_kva_ref,
                          g_post_ref, w_out_ref, w_kvb_ref, *refs):
    (o_ref, ckv_ref, kpe_ref), (kf_ref, vf_ref) = _split_layer_refs(refs, runners)
    hb = _modulated(x_ref, mods_ref, g_pre_ref, 0)
    qn = _mla_queries(hb, w_in_ref, g_qa_ref)
    q = (jnp.dot(qn, w_qb_ref[...], preferred_element_type=F32) * MLA_Q_SCALE).astype(BF16)
    ckv = _rms(_proj_t(hb, w_in_ref, _KV), g_kva_ref[...])
    gate = _silu(_proj_t(hb, w_in_ref, _GT)).astype(BF16)
    pe = _proj_t(hb, w_in_ref, _PE)
    for r in range(CTX_PER_STEP):
        rows = slice(r * CTX_S, (r + 1) * CTX_S)
        kpe = pe[rows, 0:MLA_ROPE]
        _write_state(ckv_ref, r, create_j, ckv[rows, :])
        _write_state(kpe_ref, r, create_j, kpe)
        _mla_expand(ckv[rows, :].astype(BF16), kpe.astype(BF16), w_kvb_ref, kf_ref.at[r], vf_ref.at[r],
                    0, CTX_S)
        heads = _mla_heads(q, kf_ref.at[r], vf_ref.at[r], gate, rows)
        _finish(heads, w_out_ref, x_ref, mods_ref, g_post_ref, 0, o_ref, rows)


def _layer_mla_ctx(x, mods, mi, g_pre, g_post, w_in, wj, g_qa, w_qb, g_kva, w_out, wo, w_kvb, states, layer, j,
                   n_layers, side_jobs=()):
    shapes = [(N_CTX_B, n_layers, CTX_S, MLA_KV_LORA), (N_CTX_B, n_layers, CTX_S, MLA_ROPE)]
    st_in_specs, st_out_specs, aliases, st_args = _state_plumbing(states, shapes, j, 10, 1)
    cj_in, cj_out, cj_shapes, cj_args, runners = _side_job_specs(side_jobs)
    tok_spec = pl.BlockSpec((CTX_PER_STEP * CTX_S, D_MODEL), lambda t: (t, 0))
    return pl.pallas_call(
        functools.partial(_layer_mla_ctx_kernel, j if states is None else None, runners),
        grid=(N_CTX_B // CTX_PER_STEP,),
        in_specs=[
            tok_spec,
            pl.BlockSpec((None, COND_ROWS, 3 * D_MODEL), lambda t: (mi, 0, 0)),
            pl.BlockSpec((None, 1, D_MODEL), lambda t: (layer, 0, 0)),
            pl.BlockSpec((None, MLA_IN, D_MODEL), lambda t: (wj, 0, 0)),
            pl.BlockSpec((None, 1, MLA_Q_LORA), lambda t: (j, 0, 0)),
            pl.BlockSpec((None, MLA_Q_LORA, HEADS * MLA_QK_PAD), lambda t: (j, 0, 0)),
            pl.BlockSpec((None, 1, MLA_KV_LORA), lambda t: (j, 0, 0)),
            pl.BlockSpec((None, 1, D_MODEL), lambda t: (layer, 0, 0)),
            pl.BlockSpec((None, D_MODEL, D_MODEL), lambda t: (wo, 0, 0)),
            pl.BlockSpec((None, MLA_KV_LORA, HEADS * (MLA_NOPE + MLA_DV)), lambda t: (j, 0, 0)),
        ] + st_in_specs + cj_in,
        out_specs=[tok_spec] + st_out_specs + cj_out,
        out_shape=[jax.ShapeDtypeStruct(x.shape, F32)] + [jax.ShapeDtypeStruct(shp, F32) for shp in shapes]
        + cj_shapes,
        input_output_aliases=aliases,
        scratch_shapes=[
            pltpu.VMEM((CTX_PER_STEP, HEADS, CTX_S, MLA_QK_PAD), BF16),
            pltpu.VMEM((CTX_PER_STEP, HEADS, CTX_S, MLA_DV), BF16),
        ],
        compiler_params=pltpu.CompilerParams(
            dimension_semantics=("arbitrary",), vmem_limit_bytes=VMEM_LIMIT),
        name="layer_mla_ctx",
    )(x, mods, g_pre, w_in, g_qa, w_qb, g_kva, g_post, w_out, w_kvb, *st_args, *cj_args)


LAT_TILES = LAT_S // TM
LAT_QBLOCKS = LAT_S // TQ


def _lat_specs():
    tile = lambda s: jnp.minimum(s, LAT_TILES - 1)
    qblk = lambda s: jnp.maximum(s - LAT_TILES, 0)
    x_proj = pl.BlockSpec((TM, D_MODEL), lambda b, s: (b * LAT_TILES + tile(s), 0))
    x_attn = pl.BlockSpec((TQ, D_MODEL), lambda b, s: (b * LAT_QBLOCKS + qblk(s), 0))
    rope = pl.BlockSpec((TM, LANES), lambda b, s: (tile(s), 0))
    return x_proj, x_attn, rope


def _query_rows():
    i = pl.program_id(1) - LAT_TILES
    return pl.ds(pl.multiple_of(i * TQ, TQ), TQ)


def _layer_diff_lat_kernel(lam_init, xp_ref, xa_ref, mods_ref, g_pre_ref, w_in_ref, cos_ref, sin_ref,
                           ckt_ref, cv_ref, g_post_ref, w_out_ref, g_sub_ref, lam_ref,
                           o_ref, k2_ref, vb_ref, q_s, gate_s):
    step = pl.program_id(1)
    row = 1 + pl.program_id(0)
    sk = LAT_S + PAST

    for t in range(LAT_TILES):
        @pl.when(step == t)
        def _(t=t):
            r0 = t * TM
            hb = _modulated(xp_ref, mods_ref, g_pre_ref, row)
            cos = cos_ref[...]
            sin = sin_ref[...]
            n = D_MODEL
            q = jnp.dot(hb, w_in_ref[:, 0:n], preferred_element_type=F32)
            k = jnp.dot(hb, w_in_ref[:, n:2 * n], preferred_element_type=F32)
            cos_q = cos * DA_Q_SCALE
            sin_q = sin * DA_Q_SCALE
            slabs = [slice(h * LANES, (h + 1) * LANES) for h in range(HEADS)]
            q_s[r0:r0 + TM, :] = jnp.concatenate(
                [_rope_slab(q[:, sl], cos_q, sin_q).astype(BF16) for sl in slabs], axis=1)
            kb = jnp.concatenate([_rope_slab(k[:, sl], cos, sin).astype(BF16) for sl in slabs], axis=1)
            _stack_diff_keys(kb, k2_ref, r0, sk)
            vt = jnp.dot(hb, w_in_ref[:, 2 * n:3 * n], preferred_element_type=F32).T.astype(BF16)
            _stack_values_t(vt, vb_ref, r0, TM)
            gate_s[r0:r0 + TM, :] = _silu(
                jnp.dot(hb, w_in_ref[:, 3 * n:4 * n], preferred_element_type=F32)).astype(BF16)

    @pl.when(step == 0)
    def _():
        lane = lax.broadcasted_iota(jnp.int32, (PAST, LANES), 1)
        zero = jnp.zeros((PAST, LANES), BF16)
        ones = _ones_rows(PAST)
        for h in range(HEADS):
            ch = ckt_ref[h * LANES:(h + 1) * LANES, :].T.astype(BF16)
            k2_ref[h, LAT_S:sk, :] = jnp.where(lane < DA_DH, ch, zero)
            k2_ref[h, sk + LAT_S:2 * sk, :] = jnp.where(lane >= DA_DH, ch, zero)
            vb_ref[h, 0:DA_DV, LAT_S:sk] = cv_ref[:, h, :].T.astype(BF16)
            vb_ref[h, DA_DV:DA_DV + V_PAD, LAT_S:sk] = ones

    @pl.when(step >= LAT_TILES)
    def _():
        lam = _diff_lambda(lam_ref, lam_init)
        heads = _diff_heads(q_s, k2_ref, vb_ref, gate_s, g_sub_ref, lam, lam_init, sk, _query_rows())
        _finish(heads, w_out_ref, xa_ref, mods_ref, g_post_ref, row, o_ref)


def _layer_diff_lat(x, mods, mi, g_pre, g_post, w_in, wj, w_out, wo, cos_t, sin_t, cache_kt, cache_v, g_sub,
                    lam_p, layer, j):
    lam_init = 0.8 - 0.6 * math.exp(-0.3 * layer)
    sk = LAT_S + PAST
    x_proj, x_attn, rope = _lat_specs()
    return pl.pallas_call(
        functools.partial(_layer_diff_lat_kernel, lam_init),
        grid=(N_LAT_B, LAT_TILES + LAT_QBLOCKS),
        in_specs=[
            x_proj, x_attn,
            pl.BlockSpec((None, COND_ROWS, 3 * D_MODEL), lambda b, s: (mi, 0, 0)),
            pl.BlockSpec((None, 1, D_MODEL), lambda b, s: (layer, 0, 0)),
            pl.BlockSpec((None, D_MODEL, 4 * D_MODEL), lambda b, s: (wj, 0, 0)),
            rope, rope,
            pl.BlockSpec((None, None, D_MODEL, PAST), lambda b, s: (b, j, 0, 0)),
            pl.BlockSpec((None, None, PAST, HEADS, DA_DV), lambda b, s: (b, j, 0, 0, 0)),
            pl.BlockSpec((None, 1, D_MODEL), lambda b, s: (layer, 0, 0)),
            pl.BlockSpec((None, D_MODEL, D_MODEL), lambda b, s: (wo, 0, 0)),
            pl.BlockSpec((None, 1, DA_DV), lambda b, s: (j, 0, 0)),
            pl.BlockSpec((None, 4, DA_DH), lambda b, s: (j, 0, 0)),
        ],
        out_specs=x_attn,
        out_shape=jax.ShapeDtypeStruct(x.shape, F32),
        scratch_shapes=[
            pltpu.VMEM((HEADS, 2 * sk, LANES), BF16),
            pltpu.VMEM((HEADS, DA_DV + V_PAD, sk), BF16),
            pltpu.VMEM((LAT_S, D_MODEL), BF16),
            pltpu.VMEM((LAT_S, D_MODEL), BF16),
        ],
        compiler_params=pltpu.CompilerParams(
            dimension_semantics=("arbitrary", "arbitrary"), vmem_limit_bytes=VMEM_LIMIT),
        name="layer_diff_lat",
    )(x, x, mods, g_pre, w_in, cos_t, sin_t, cache_kt, cache_v, g_post, w_out, g_sub, lam_p)


def _layer_mla_lat_kernel(xp_ref, xa_ref, mods_ref, g_pre_ref, w_in_ref, g_qa_ref, w_qb_ref, g_kva_ref,
                          cos_ref, sin_ref, cckv_ref, ckpe_ref, g_post_ref, w_out_ref, w_kvb_ref,
                          o_ref, kf_ref, vf_ref, q_s, gate_s):
    step = pl.program_id(1)
    row = 1 + pl.program_id(0)

    for t in range(LAT_TILES):
        @pl.when(step == t)
        def _(t=t):
            r0 = t * TM
            hb = _modulated(xp_ref, mods_ref, g_pre_ref, row)
            cos = cos_ref[...]
            sin = sin_ref[...]
            qn = _mla_queries(hb, w_in_ref, g_qa_ref)
            for h in range(HEADS):
                lo = h * MLA_QK_PAD
                q = jnp.dot(qn, w_qb_ref[:, lo:lo + MLA_QK_PAD], preferred_element_type=F32) * MLA_Q_SCALE
                q_s[r0:r0 + TM, lo:lo + LANES] = q[:, 0:LANES].astype(BF16)
                q_s[r0:r0 + TM, lo + LANES:lo + 2 * LANES] = _rope_slab(
                    q[:, LANES:2 * LANES], cos, sin).astype(BF16)
            ckv = _rms(_proj_t(hb, w_in_ref, _KV), g_kva_ref[...]).astype(BF16)
            gate_s[r0:r0 + TM, :] = _silu(_proj_t(hb, w_in_ref, _GT)).astype(BF16)
            kpe = _rope_slab(_proj_t(hb, w_in_ref, _PE), cos, sin)[:, 0:MLA_ROPE].astype(BF16)
            _mla_expand(ckv, kpe, w_kvb_ref, kf_ref, vf_ref, r0, TM)

    @pl.when(step == 0)
    def _():
        _mla_expand(cckv_ref[...].astype(BF16), ckpe_ref[...].astype(BF16), w_kvb_ref, kf_ref, vf_ref,
                    LAT_S, PAST)

    @pl.when(step >= LAT_TILES)
    def _():
        heads = _mla_heads(q_s, kf_ref, vf_ref, gate_s, _query_rows())
        _finish(heads, w_out_ref, xa_ref, mods_ref, g_post_ref, row, o_ref)


def _layer_mla_lat(x, mods, mi, g_pre, g_post, w_in, wj, g_qa, w_qb, g_kva, w_out, wo, w_kvb, cos_t, sin_t,
                   cache_ckv, cache_kpe, layer, j):
    sk = LAT_S + PAST
    x_proj, x_attn, rope = _lat_specs()
    return pl.pallas_call(
        _layer_mla_lat_kernel,
        grid=(N_LAT_B, LAT_TILES + LAT_QBLOCKS),
        in_specs=[
            x_proj, x_attn,
            pl.BlockSpec((None, COND_ROWS, 3 * D_MODEL), lambda b, s: (mi, 0, 0)),
            pl.BlockSpec((None, 1, D_MODEL), lambda b, s: (layer, 0, 0)),
            pl.BlockSpec((None, MLA_IN, D_MODEL), lambda b, s: (wj, 0, 0)),
            pl.BlockSpec((None, 1, MLA_Q_LORA), lambda b, s: (j, 0, 0)),
            pl.BlockSpec((None, MLA_Q_LORA, HEADS * MLA_QK_PAD), lambda b, s: (j, 0, 0)),
            pl.BlockSpec((None, 1, MLA_KV_LORA), lambda b, s: (j, 0, 0)),
            rope, rope,
            pl.BlockSpec((None, None, PAST, MLA_KV_LORA), lambda b, s: (b, j, 0, 0)),
            pl.BlockSpec((None, None, PAST, MLA_ROPE), lambda b, s: (b, j, 0, 0)),
            pl.BlockSpec((None, 1, D_MODEL), lambda b, s: (layer, 0, 0)),
            pl.BlockSpec((None, D_MODEL, D_MODEL), lambda b, s: (wo, 0, 0)),
            pl.BlockSpec((None, MLA_KV_LORA, HEADS * (MLA_NOPE + MLA_DV)), lambda b, s: (j, 0, 0)),
        ],
        out_specs=x_attn,
        out_shape=jax.ShapeDtypeStruct(x.shape, F32),
        scratch_shapes=[
            pltpu.VMEM((HEADS, sk, MLA_QK_PAD), BF16),
            pltpu.VMEM((HEADS, sk, MLA_DV), BF16),
            pltpu.VMEM((LAT_S, HEADS * MLA_QK_PAD), BF16),
            pltpu.VMEM((LAT_S, D_MODEL), BF16),
        ],
        compiler_params=pltpu.CompilerParams(
            dimension_semantics=("arbitrary", "arbitrary"), vmem_limit_bytes=VMEM_LIMIT),
        name="layer_mla_lat",
    )(x, x, mods, g_pre, w_in, g_qa, w_qb, g_kva, cos_t, sin_t, cache_ckv, cache_kpe, g_post, w_out, w_kvb)


def _rope_tables():
    rows = LAT_S // GRID_W
    row = jnp.repeat(jnp.arange(rows), GRID_W).astype(F32)
    col = jnp.tile(jnp.arange(GRID_W), rows).astype(F32)
    n_freq = DA_DH // 4
    inv = ROPE_THETA ** (-jnp.arange(n_freq, dtype=F32) / n_freq)
    ang = jnp.concatenate([row[:, None] * inv, col[:, None] * inv], axis=-1)
    cos, sin = jnp.cos(ang), jnp.sin(ang)
    return (jnp.concatenate([cos, cos, cos, cos], axis=-1),
            jnp.concatenate([-sin, sin, -sin, sin], axis=-1))


def kernel(x_prompt, x_sample, cache_diff_k, cache_diff_v, cache_mla_ckv, cache_mla_kpe,
           c, c_ctx, w_ada, b_ada, g_pre, g_post, w_out,
           da_w_in, da_lam_q1, da_lam_k1, da_lam_q2, da_lam_k2, da_g_sub,
           mla_w_in, mla_g_qa, mla_w_qb, mla_g_kva, mla_w_kvb):
    assert DA_DH == MLA_ROPE
    assert x_prompt.shape == (N_CTX_B, CTX_S, D_MODEL) and x_sample.shape == (N_LAT_B, LAT_S, D_MODEL)
    assert cache_diff_k.shape == (N_LAT_B, (DEPTH + 1) // 2, PAST, HEADS, 2, DA_DH)
    assert cache_mla_ckv.shape == (N_LAT_B, DEPTH // 2, PAST, MLA_KV_LORA)
    assert w_ada.shape == (DEPTH, D_MODEL, 3 * D_MODEL) and da_w_in.shape[1:] == (D_MODEL, 4 * D_MODEL)
    n_diff = (DEPTH + 1) // 2
    n_mla = DEPTH // 2

    cond = jnp.concatenate(
        [c_ctx[None, :], c, jnp.zeros((COND_ROWS - 1 - N_LAT_B, D_MODEL), F32)], axis=0)
    b_ada3 = b_ada.reshape(DEPTH, 1, 3 * D_MODEL)
    mods_first = _ada_call(cond, w_ada, b_ada3, 2)

    cos_t, sin_t = _rope_tables()
    g_pre3 = g_pre.reshape(DEPTH, 1, D_MODEL)
    g_post3 = g_post.reshape(DEPTH, 1, D_MODEL)
    steps = N_CTX_B // CTX_PER_STEP
    w_out_first = w_out[0:1].astype(BF16)
    da_w_in_first = da_w_in[0:1].astype(BF16)
    g_sub3 = da_g_sub.reshape(n_diff, 1, DA_DV)
    lam_p = jnp.stack([da_lam_q1, da_lam_k1, da_lam_q2, da_lam_k2], axis=1)
    cache_dkt = jnp.transpose(cache_diff_k, (0, 1, 3, 4, 5, 2)).reshape(N_LAT_B, n_diff, D_MODEL, PAST)

    mla_w_in_t = jnp.swapaxes(mla_w_in, 1, 2)
    w_qb4 = mla_w_qb.reshape(n_mla, MLA_Q_LORA, HEADS, MLA_NOPE + MLA_ROPE)
    w_qb_b = jnp.pad(w_qb4, ((0, 0), (0, 0), (0, 0), (0, MLA_QK_PAD - MLA_NOPE - MLA_ROPE))).reshape(
        n_mla, MLA_Q_LORA, HEADS * MLA_QK_PAD).astype(BF16)
    g_qa3 = mla_g_qa.reshape(n_mla, 1, MLA_Q_LORA)
    g_kva3 = mla_g_kva.reshape(n_mla, 1, MLA_KV_LORA)

    x_ctx = x_prompt.reshape(N_CTX_B * CTX_S, D_MODEL)
    x_lat = x_sample.reshape(N_LAT_B * LAT_S, D_MODEL)
    st_diff = None
    st_mla = None

    assert DEPTH == 4
    da_w = [(da_w_in_first, 0), None]
    mla_w = [None, None]
    w_o = [(w_out_first, 0), None, None, None]
    wide = D_MODEL * 4 // steps
    cols = D_MODEL // steps
    per_layer = steps // 2
    kv_cols = HEADS * (MLA_NOPE + MLA_DV) // steps

    mods = [(mods_first, 0), (mods_first, 1), None, None]

    def mla_in_cast(jj):
        return _cast_job(mla_w_in_t, (None, MLA_IN, cols), lambda t: (jj, 0, t),
                         (1, MLA_IN, D_MODEL), lambda t: (0, 0, t))

    side_jobs = {
        0: [_cast_job(da_w_in, (None, D_MODEL, wide), lambda t: (1, 0, t),
                      (1, D_MODEL, 4 * D_MODEL), lambda t: (0, 0, t)),
            mla_in_cast(0),
            _cast_job(w_out, (None, D_MODEL // steps, D_MODEL), lambda t: (1, t, 0),
                      (1, D_MODEL, D_MODEL), lambda t: (0, t, 0)),
            _cast_job(mla_w_kvb, (n_mla, MLA_KV_LORA, kv_cols), lambda t: (0, 0, t),
                      mla_w_kvb.shape, lambda t: (0, 0, t))],
        1: [mla_in_cast(1),
            _cast_job(w_out, (None, D_MODEL // per_layer, D_MODEL),
                      lambda t: (2 + t // per_layer, t % per_layer, 0),
                      (2, D_MODEL, D_MODEL), lambda t: (t // per_layer, t % per_layer, 0)),
            _ada_job(cond, w_ada, b_ada3, 2, steps)],
        2: [_ada_job(cond, w_ada, b_ada3, 3, steps)],
        3: [],
    }

    for i in range(DEPTH):
        j = i // 2
        if i % 2 == 0:
            w_j = da_w[j]
            outs = _layer_diff_ctx(x_ctx, *mods[i], g_pre3, g_post3, *w_j, *w_o[i], g_sub3, lam_p, st_diff,
                                   i, j, n_diff, side_jobs=side_jobs[i])
            x_ctx, st_dkt, st_dv = outs[0:3]
            st_diff = (st_dkt, st_dv)
            if i == 0:
                da_w[1], mla_w[0], w_o[1], w_kvb_b = (outs[3], 0), (outs[4], 0), (outs[5], 0), outs[6]
            else:
                mods[3] = (outs[3], 0)
            x_lat = _layer_diff_lat(x_lat, *mods[i], g_pre3, g_post3, *w_j, *w_o[i], cos_t, sin_t,
                                    cache_dkt, cache_diff_v, g_sub3, lam_p, i, j)
        else:
            w_j = mla_w[j]
            outs = _layer_mla_ctx(x_ctx, *mods[i], g_pre3, g_post3, *w_j, g_qa3, w_qb_b, g_kva3, *w_o[i], w_kvb_b,
                                  st_mla, i, j, n_mla, side_jobs=side_jobs[i])
            x_ctx, st_ckv, st_kpe = outs[0:3]
            st_mla = (st_ckv, st_kpe)
            if i == 1:
                mla_w[1] = (outs[3], 0)
                w_o[2], w_o[3] = (outs[4], 0), (outs[4], 1)
                mods[2] = (outs[5], 0)
            x_lat = _layer_mla_lat(x_lat, *mods[i], g_pre3, g_post3, *w_j, g_qa3, w_qb_b, g_kva3, *w_o[i], w_kvb_b,
                                   cos_t, sin_t, cache_mla_ckv, cache_mla_kpe, i, j)

    return (x_ctx.reshape(N_CTX_B, CTX_S, D_MODEL),
            x_lat.reshape(N_LAT_B, LAT_S, D_MODEL),
            jnp.transpose(st_dkt.reshape(N_CTX_B, n_diff, HEADS, 2, DA_DH, CTX_S), (0, 1, 5, 2, 3, 4)),
            st_dv,
            st_ckv,
            st_kpe)
```

```python
import functools
import math

import jax
import jax.numpy as jnp
from jax import lax
from jax.experimental import pallas as pl
from jax.experimental.pallas import tpu as pltpu

F32 = jnp.float32
BF16 = jnp.bfloat16

D_MODEL = 1024
DEPTH = 4
N_CTX_B = 16
CTX_S = 256
N_LAT_B = 4
LAT_S = 1024
PAST = 256
GRID_W = 64
HEADS = 8
DA_DH = 64
DA_DV = 128
MLA_Q_LORA = 384
MLA_KV_LORA = 256
MLA_NOPE = 128
MLA_ROPE = 64
MLA_DV = 128
MLA_QK_PAD = 256
ROPE_THETA = 10000.0
NORM_EPS = 1e-6
COND_ROWS = 8

LANES = 128
V7X_VMEM_MIB = 64
V_PAD = 16
TQ = 256
TM = 1024
VMEM_LIMIT = (V7X_VMEM_MIB - 8) * 1024 * 1024

_TRANS_B = (((1,), (1,)), ((), ()))
SCORES_AHEAD = 2
CTX_PER_STEP = 2
DA_Q_SCALE = DA_DH ** -0.5 * math.log2(math.e)
MLA_Q_SCALE = (MLA_NOPE + MLA_ROPE) ** -0.5 * math.log2(math.e)


def _silu(x):
    return x * (1.0 / (1.0 + jnp.exp(-x)))


def _rms(x, g):
    r = lax.rsqrt(jnp.mean(x * x, axis=-1, keepdims=True) + NORM_EPS)
    return (x * r) * g


def _rope_slab(x, cos, sin_signed):
    half = DA_DH // 2
    lane = lax.broadcasted_iota(jnp.int32, x.shape, 1)
    first_half = (lane % DA_DH) < half
    partner = jnp.where(first_half, pltpu.roll(x, LANES - half, axis=1), pltpu.roll(x, half, axis=1))
    return x * cos + partner * sin_signed


def _ones_rows(cols):
    row = lax.broadcasted_iota(jnp.int32, (V_PAD, cols), 0)
    return jnp.where(row == 0, 1.0, 0.0).astype(BF16)


def _cond_row(mods_ref, row):
    m = mods_ref[pl.ds(row, 1), :]
    return m[:, 0:D_MODEL], m[:, D_MODEL:2 * D_MODEL], m[:, 2 * D_MODEL:3 * D_MODEL]


def _modulated(x_ref, mods_ref, g_ref, row):
    shift, scale, _ = _cond_row(mods_ref, row)
    h = _rms(x_ref[...], g_ref[...]) * (1.0 + scale) + shift
    return h.astype(BF16)


def _ada_kernel(cond_ref, w_ref, b_ref, o_ref):
    a = _silu(cond_ref[...]).astype(BF16)
    o_ref[...] = jnp.dot(a, w_ref[...].astype(BF16), preferred_element_type=F32) + b_ref[...]


def _ada_call(cond, w_ada, b_ada3, n_layers):
    return pl.pallas_call(
        _ada_kernel,
        grid=(n_layers,),
        in_specs=[
            pl.BlockSpec((COND_ROWS, D_MODEL), lambda i: (0, 0)),
            pl.BlockSpec((None, D_MODEL, 3 * D_MODEL), lambda i: (i, 0, 0)),
            pl.BlockSpec((None, 1, 3 * D_MODEL), lambda i: (i, 0, 0)),
        ],
        out_specs=pl.BlockSpec((None, COND_ROWS, 3 * D_MODEL), lambda i: (i, 0, 0)),
        out_shape=jax.ShapeDtypeStruct((n_layers, COND_ROWS, 3 * D_MODEL), F32),
        compiler_params=pltpu.CompilerParams(
            dimension_semantics=("arbitrary",), vmem_limit_bytes=VMEM_LIMIT),
        name="ada_mod",
    )(cond, w_ada, b_ada3)


def _cast_block(src_ref, dst_ref):
    dst_ref[...] = src_ref[...].astype(BF16)


def _cast_job(array, blk, src_map, shape, dst_map):
    return dict(args=[array], in_specs=[pl.BlockSpec(blk, src_map)], out_spec=pl.BlockSpec(blk, dst_map),
                out_shape=jax.ShapeDtypeStruct(shape, BF16), run=_cast_block)


def _ada_job(cond, w_ada, b_ada3, layer, steps):
    tn = 3 * D_MODEL // steps
    return dict(args=[cond, w_ada, b_ada3],
                in_specs=[pl.BlockSpec((COND_ROWS, D_MODEL), lambda t: (0, 0)),
                          pl.BlockSpec((None, D_MODEL, tn), lambda t: (layer, 0, t)),
                          pl.BlockSpec((None, 1, tn), lambda t: (layer, 0, t))],
                out_spec=pl.BlockSpec((None, COND_ROWS, tn), lambda t: (0, 0, t)),
                out_shape=jax.ShapeDtypeStruct((1, COND_ROWS, 3 * D_MODEL), F32), run=_ada_kernel)


def _side_job_specs(jobs):
    in_specs = [spec for job in jobs for spec in job["in_specs"]]
    args = [a for job in jobs for a in job["args"]]
    runners = tuple((len(job["args"]), job["run"]) for job in jobs)
    return in_specs, [job["out_spec"] for job in jobs], [job["out_shape"] for job in jobs], args, runners


def _write_state(ref, r, j, value):
    if j is None:
        ref[r] = value
    else:
        for jj in range(ref.shape[1]):
            ref[r, jj] = value if jj == j else jnp.zeros_like(value)


def _state_plumbing(states, shapes, j, n_inputs, n_outputs):
    def index_map(shp, layer_index):
        return lambda t: (t, layer_index) + (0,) * (len(shp) - 2)

    if states is None:
        specs = [pl.BlockSpec((CTX_PER_STEP,) + shp[1:], index_map(shp, 0)) for shp in shapes]
        return [], specs, {}, ()
    specs = [pl.BlockSpec((CTX_PER_STEP, None) + shp[2:], index_map(shp, j)) for shp in shapes]
    in_specs = [pl.BlockSpec(memory_space=pl.ANY) for _ in shapes]
    aliases = {n_inputs + i: n_outputs + i for i in range(len(shapes))}
    return in_specs, specs, aliases, tuple(states)


def _finish(heads, w_out_ref, x_ref, mods_ref, g_post_ref, row, o_ref, rows=slice(None)):
    _, _, gate = _cond_row(mods_ref, row)
    og = jnp.concatenate(heads, axis=1)
    y = jnp.dot(og, w_out_ref[...], preferred_element_type=F32)
    o_ref[rows, :] = x_ref[rows, :] + gate * _rms(y, g_post_ref[...])


def _diff_lambda(lam_ref, lam_init):
    p = lam_ref[...]
    a = jnp.sum(p[0:1, :] * p[1:2, :], axis=-1, keepdims=True)
    b = jnp.sum(p[2:3, :] * p[3:4, :], axis=-1, keepdims=True)
    return jnp.exp(a) - jnp.exp(b) + lam_init


def _stack_diff_keys(k, k2_ref, r0, sk):
    rows = k.shape[0]
    lane = lax.broadcasted_iota(jnp.int32, (rows, LANES), 1)
    zero = jnp.zeros((rows, LANES), BF16)
    for h in range(HEADS):
        kh = k[:, h * LANES:(h + 1) * LANES]
        k2_ref[h, r0:r0 + rows, :] = jnp.where(lane < DA_DH, kh, zero)
        k2_ref[h, sk + r0:sk + r0 + rows, :] = jnp.where(lane >= DA_DH, kh, zero)


def _stack_values_t(vt, vt_ref, c0, cols):
    ones = _ones_rows(cols)
    for h in range(HEADS):
        vt_ref[h, 0:DA_DV, c0:c0 + cols] = vt[h * DA_DV:(h + 1) * DA_DV, :]
        vt_ref[h, DA_DV:DA_DV + V_PAD, c0:c0 + cols] = ones


def _diff_heads(q_ref, k2_ref, vt_ref, gate_ref, g_sub_ref, lam, lam_init, sk, rows=slice(None), after_issue=None):
    g_sub = g_sub_ref[...]

    def scores(h):
        qh = q_ref[rows, h * LANES:(h + 1) * LANES]
        return tuple(lax.dot_general(k2_ref[h, c * sk:(c + 1) * sk, :], qh, _TRANS_B,
                                     preferred_element_type=F32) for c in range(2))

    heads = []
    pending = [scores(h) for h in range(SCORES_AHEAD)]
    if after_issue is not None:
        after_issue()
    for h in range(HEADS):
        lo, hi = h * LANES, (h + 1) * LANES
        s = pending.pop(0)
        if h + SCORES_AHEAD < HEADS:
            pending.append(scores(h + SCORES_AHEAD))
        s0, s1 = s
        e0 = jnp.exp2(s0 - jnp.max(s0, axis=0, keepdims=True)).astype(BF16)
        e1 = jnp.exp2(s1 - jnp.max(s1, axis=0, keepdims=True)).astype(BF16)
        pv0 = jnp.dot(vt_ref[h], e0, preferred_element_type=F32)
        pv1 = jnp.dot(vt_ref[h], e1, preferred_element_type=F32)
        a0 = 1.0 / pv0[DA_DV:DA_DV + 1, :]
        a1 = lam / pv1[DA_DV:DA_DV + 1, :]
        ot = pv0[0:DA_DV, :] * a0 - pv1[0:DA_DV, :] * a1
        ot = ot * lax.rsqrt(jnp.mean(ot * ot, axis=0, keepdims=True) + NORM_EPS)
        o = (ot.T * g_sub) * (1.0 - lam_init)
        heads.append((o * gate_ref[rows, lo:hi]).astype(BF16))
    return heads


MLA_IN = MLA_Q_LORA + MLA_KV_LORA + MLA_ROPE + D_MODEL
_QA = slice(0, MLA_Q_LORA)
_KV = slice(MLA_Q_LORA, MLA_Q_LORA + MLA_KV_LORA)
_PE = slice(_KV.stop, _KV.stop + LANES)
_GT = slice(_KV.stop + MLA_ROPE, MLA_IN)


def _proj_t(hb, w_t_ref, rows):
    return lax.dot_general(hb, w_t_ref[rows, :], _TRANS_B, preferred_element_type=F32)


def _mla_queries(hb, w_in_ref, g_qa_ref):
    q_a = _proj_t(hb, w_in_ref, _QA)
    return _rms(q_a, g_qa_ref[...]).astype(BF16)


def _mla_expand(ckv, kpe, w_kvb_ref, kf_ref, vf_ref, r0, rows):
    zero = jnp.zeros((rows, LANES - MLA_ROPE), BF16)
    for h in range(HEADS):
        lo = h * (MLA_NOPE + MLA_DV)
        kv = jnp.dot(ckv, w_kvb_ref[:, lo:lo + MLA_NOPE + MLA_DV], preferred_element_type=F32)
        kf_ref[h, r0:r0 + rows, 0:MLA_NOPE] = kv[:, 0:MLA_NOPE].astype(BF16)
        kf_ref[h, r0:r0 + rows, MLA_NOPE:MLA_NOPE + MLA_ROPE] = kpe
        kf_ref[h, r0:r0 + rows, MLA_NOPE + MLA_ROPE:MLA_QK_PAD] = zero
        vf_ref[h, r0:r0 + rows, :] = kv[:, MLA_NOPE:MLA_NOPE + MLA_DV].astype(BF16)


def _mla_heads(q_ref, kf_ref, vf_ref, gate_ref, rows=slice(None), after_issue=None):
    def scores(h):
        return lax.dot_general(q_ref[rows, h * MLA_QK_PAD:(h + 1) * MLA_QK_PAD], kf_ref[h], _TRANS_B,
                               preferred_element_type=F32)

    heads = []
    pending = [scores(h) for h in range(SCORES_AHEAD)]
    if after_issue is not None:
        after_issue()
    for h in range(HEADS):
        s = pending.pop(0)
        if h + SCORES_AHEAD < HEADS:
            pending.append(scores(h + SCORES_AHEAD))
        e = jnp.exp2(s - jnp.max(s, axis=-1, keepdims=True))
        inv = 1.0 / jnp.sum(e, axis=-1, keepdims=True)
        o = jnp.dot(e.astype(BF16), vf_ref[h], preferred_element_type=F32) * inv
        lo, hi = h * LANES, (h + 1) * LANES
        heads.append((o * gate_ref[rows, lo:hi]).astype(BF16))
    return heads


def _split_layer_refs(refs, runners):
    scratch = refs[-2:]
    refs = refs[:-2]
    n_jobs = len(runners)
    job_out = refs[len(refs) - n_jobs:]
    main = refs[len(refs) - n_jobs - 3:len(refs) - n_jobs]
    pos = len(refs) - n_jobs - 3 - sum(n for n, _ in runners)
    for (n, run), out in zip(runners, job_out):
        run(*refs[pos:pos + n], out)
        pos += n
    return main, scratch


def _layer_diff_ctx_kernel(create_j, lam_init, runners, x_ref, mods_ref, g_pre_ref, w_in_ref, g_post_ref,
                           w_out_ref, g_sub_ref, lam_ref, *refs):
    (o_ref, skt_ref, sv_ref), (k2_ref, vb_ref) = _split_layer_refs(refs, runners)
    hb = _modulated(x_ref, mods_ref, g_pre_ref, 0)
    n = D_MODEL
    q = (jnp.dot(hb, w_in_ref[:, 0:n], preferred_element_type=F32) * DA_Q_SCALE).astype(BF16)
    k = jnp.dot(hb, w_in_ref[:, n:2 * n], preferred_element_type=F32)
    v = jnp.dot(hb, w_in_ref[:, 2 * n:3 * n], preferred_element_type=F32)
    gate = _silu(jnp.dot(hb, w_in_ref[:, 3 * n:4 * n], preferred_element_type=F32)).astype(BF16)
    kb = k.astype(BF16)
    lam = _diff_lambda(lam_ref, lam_init)
    finish = None
    for r in range(CTX_PER_STEP):
        rows = slice(r * CTX_S, (r + 1) * CTX_S)
        vt = v[rows, :].T
        _write_state(skt_ref, r, create_j, k[rows, :].T)
        _write_state(sv_ref, r, create_j, v[rows, :].reshape(CTX_S, HEADS, DA_DV))
        _stack_diff_keys(kb[rows, :], k2_ref.at[r], 0, CTX_S)
        _stack_values_t(vt.astype(BF16), vb_ref.at[r], 0, CTX_S)
        heads = _diff_heads(q, k2_ref.at[r], vb_ref.at[r], gate, g_sub_ref, lam, lam_init, CTX_S, rows, finish)
        finish = functools.partial(_finish, heads, w_out_ref, x_ref, mods_ref, g_post_ref, 0, o_ref, rows)
    finish()


def _layer_diff_ctx(x, mods, mi, g_pre, g_post, w_in, wj, w_out, wo, g_sub, lam_p, states, layer, j, n_layers,
                    side_jobs=()):
    lam_init = 0.8 - 0.6 * math.exp(-0.3 * layer)
    shapes = [(N_CTX_B, n_layers, D_MODEL, CTX_S), (N_CTX_B, n_layers, CTX_S, HEADS, DA_DV)]
    st_in_specs, st_out_specs, aliases, st_args = _state_plumbing(states, shapes, j, 8, 1)
    cj_in, cj_out, cj_shapes, cj_args, runners = _side_job_specs(side_jobs)
    tok_spec = pl.BlockSpec((CTX_PER_STEP * CTX_S, D_MODEL), lambda t: (t, 0))
    return pl.pallas_call(
        functools.partial(_layer_diff_ctx_kernel, j if states is None else None, lam_init, runners),
        grid=(N_CTX_B // CTX_PER_STEP,),
        in_specs=[
            tok_spec,
            pl.BlockSpec((None, COND_ROWS, 3 * D_MODEL), lambda t: (mi, 0, 0)),
            pl.BlockSpec((None, 1, D_MODEL), lambda t: (layer, 0, 0)),
            pl.BlockSpec((None, D_MODEL, 4 * D_MODEL), lambda t: (wj, 0, 0)),
            pl.BlockSpec((None, 1, D_MODEL), lambda t: (layer, 0, 0)),
            pl.BlockSpec((None, D_MODEL, D_MODEL), lambda t: (wo, 0, 0)),
            pl.BlockSpec((None, 1, DA_DV), lambda t: (j, 0, 0)),
            pl.BlockSpec((None, 4, DA_DH), lambda t: (j, 0, 0)),
        ] + st_in_specs + cj_in,
        out_specs=[tok_spec] + st_out_specs + cj_out,
        out_shape=[jax.ShapeDtypeStruct(x.shape, F32)] + [jax.ShapeDtypeStruct(shp, F32) for shp in shapes]
        + cj_shapes,
        input_output_aliases=aliases,
        scratch_shapes=[
            pltpu.VMEM((CTX_PER_STEP, HEADS, 2 * CTX_S, LANES), BF16),
            pltpu.VMEM((CTX_PER_STEP, HEADS, DA_DV + V_PAD, CTX_S), BF16),
        ],
        compiler_params=pltpu.CompilerParams(
            dimension_semantics=("arbitrary",), vmem_limit_bytes=VMEM_LIMIT),
        name="layer_diff_ctx",
    )(x, mods, g_pre, w_in, g_post, w_out, g_sub, lam_p, *st_args, *cj_args)


def _layer_mla_ctx_kernel(create_j, runners, x_ref, mods_ref, g_pre_ref, w_in_ref, g_qa_ref, w_qb_ref, g_kva_ref,
                          g_post_ref, w_out_ref, w_kvb_ref, *refs):
    (o_ref, ckv_ref, kpe_ref), (kf_ref, vf_ref) = _split_layer_refs(refs, runners)
    hb = _modulated(x_ref, mods_ref, g_pre_ref, 0)
    qn = _mla_queries(hb, w_in_ref, g_qa_ref)
    q = (jnp.dot(qn, w_qb_ref[...], preferred_element_type=F32) * MLA_Q_SCALE).astype(BF16)
    ckv = _rms(_proj_t(hb, w_in_ref, _KV), g_kva_ref[...])
    gate = _silu(_proj_t(hb, w_in_ref, _GT)).astype(BF16)
    pe = _proj_t(hb, w_in_ref, _PE)
    finish = None
    for r in range(CTX_PER_STEP):
        rows = slice(r * CTX_S, (r + 1) * CTX_S)
        kpe = pe[rows, 0:MLA_ROPE]
        _write_state(ckv_ref, r, create_j, ckv[rows, :])
        _write_state(kpe_ref, r, create_j, kpe)
        _mla_expand(ckv[rows, :].astype(BF16), kpe.astype(BF16), w_kvb_ref, kf_ref.at[r], vf_ref.at[r],
                    0, CTX_S)
        heads = _mla_heads(q, kf_ref.at[r], vf_ref.at[r], gate, rows, finish)
        finish = functools.partial(_finish, heads, w_out_ref, x_ref, mods_ref, g_post_ref, 0, o_ref, rows)
    finish()


def _layer_mla_ctx(x, mods, mi, g_pre, g_post, w_in, wj, g_qa, w_qb, g_kva, w_out, wo, w_kvb, states, layer, j,
                   n_layers, side_jobs=()):
    shapes = [(N_CTX_B, n_layers, CTX_S, MLA_KV_LORA), (N_CTX_B, n_layers, CTX_S, MLA_ROPE)]
    st_in_specs, st_out_specs, aliases, st_args = _state_plumbing(states, shapes, j, 10, 1)
    cj_in, cj_out, cj_shapes, cj_args, runners = _side_job_specs(side_jobs)
    tok_spec = pl.BlockSpec((CTX_PER_STEP * CTX_S, D_MODEL), lambda t: (t, 0))
    return pl.pallas_call(
        functools.partial(_layer_mla_ctx_kernel, j if states is None else None, runners),
        grid=(N_CTX_B // CTX_PER_STEP,),
        in_specs=[
            tok_spec,
            pl.BlockSpec((None, COND_ROWS, 3 * D_MODEL), lambda t: (mi, 0, 0)),
            pl.BlockSpec((None, 1, D_MODEL), lambda t: (layer, 0, 0)),
            pl.BlockSpec((None, MLA_IN, D_MODEL), lambda t: (wj, 0, 0)),
            pl.BlockSpec((None, 1, MLA_Q_LORA), lambda t: (j, 0, 0)),
            pl.BlockSpec((None, MLA_Q_LORA, HEADS * MLA_QK_PAD), lambda t: (j, 0, 0)),
            pl.BlockSpec((None, 1, MLA_KV_LORA), lambda t: (j, 0, 0)),
            pl.BlockSpec((None, 1, D_MODEL), lambda t: (layer, 0, 0)),
            pl.BlockSpec((None, D_MODEL, D_MODEL), lambda t: (wo, 0, 0)),
            pl.BlockSpec((None, MLA_KV_LORA, HEADS * (MLA_NOPE + MLA_DV)), lambda t: (j, 0, 0)),
        ] + st_in_specs + cj_in,
        out_specs=[tok_spec] + st_out_specs + cj_out,
        out_shape=[jax.ShapeDtypeStruct(x.shape, F32)] + [jax.ShapeDtypeStruct(shp, F32) for shp in shapes]
        + cj_shapes,
        input_output_aliases=aliases,
        scratch_shapes=[
            pltpu.VMEM((CTX_PER_STEP, HEADS, CTX_S, MLA_QK_PAD), BF16),
            pltpu.VMEM((CTX_PER_STEP, HEADS, CTX_S, MLA_DV), BF16),
        ],
        compiler_params=pltpu.CompilerParams(
            dimension_semantics=("arbitrary",), vmem_limit_bytes=VMEM_LIMIT),
        name="layer_mla_ctx",
    )(x, mods, g_pre, w_in, g_qa, w_qb, g_kva, g_post, w_out, w_kvb, *st_args, *cj_args)


LAT_TILES = LAT_S // TM
Q_PER_STEP = 2
LAT_QBLOCKS = LAT_S // (TQ * Q_PER_STEP)


def _lat_specs():
    tile = lambda s: jnp.minimum(s, LAT_TILES - 1)
    qblk = lambda s: jnp.maximum(s - LAT_TILES, 0)
    x_proj = pl.BlockSpec((TM, D_MODEL), lambda b, s: (b * LAT_TILES + tile(s), 0))
    x_attn = pl.BlockSpec((TQ * Q_PER_STEP, D_MODEL), lambda b, s: (b * LAT_QBLOCKS + qblk(s), 0))
    rope = pl.BlockSpec((TM, LANES), lambda b, s: (tile(s), 0))
    return x_proj, x_attn, rope


def _query_rows(r):
    i = pl.program_id(1) - LAT_TILES
    return pl.ds(pl.multiple_of((i * Q_PER_STEP + r) * TQ, TQ), TQ), slice(r * TQ, (r + 1) * TQ)


def _layer_diff_lat_kernel(lam_init, xp_ref, xa_ref, mods_ref, g_pre_ref, w_in_ref, cos_ref, sin_ref,
                           ckt_ref, cv_ref, g_post_ref, w_out_ref, g_sub_ref, lam_ref,
                           o_ref, k2_ref, vb_ref, q_s, gate_s):
    step = pl.program_id(1)
    row = 1 + pl.program_id(0)
    sk = LAT_S + PAST

    for t in range(LAT_TILES):
        @pl.when(step == t)
        def _(t=t):
            r0 = t * TM
            hb = _modulated(xp_ref, mods_ref, g_pre_ref, row)
            cos = cos_ref[...]
            sin = sin_ref[...]
            n = D_MODEL
            q = jnp.dot(hb, w_in_ref[:, 0:n], preferred_element_type=F32)
            k = jnp.dot(hb, w_in_ref[:, n:2 * n], preferred_element_type=F32)
            cos_q = cos * DA_Q_SCALE
            sin_q = sin * DA_Q_SCALE
            slabs = [slice(h * LANES, (h + 1) * LANES) for h in range(HEADS)]
            q_s[r0:r0 + TM, :] = jnp.concatenate(
                [_rope_slab(q[:, sl], cos_q, sin_q).astype(BF16) for sl in slabs], axis=1)
            kb = jnp.concatenate([_rope_slab(k[:, sl], cos, sin).astype(BF16) for sl in slabs], axis=1)
            _stack_diff_keys(kb, k2_ref, r0, sk)
            vt = jnp.dot(hb, w_in_ref[:, 2 * n:3 * n], preferred_element_type=F32).T.astype(BF16)
            _stack_values_t(vt, vb_ref, r0, TM)
            gate_s[r0:r0 + TM, :] = _silu(
                jnp.dot(hb, w_in_ref[:, 3 * n:4 * n], preferred_element_type=F32)).astype(BF16)

    @pl.when(step == 0)
    def _():
        lane = lax.broadcasted_iota(jnp.int32, (PAST, LANES), 1)
        zero = jnp.zeros((PAST, LANES), BF16)
        ones = _ones_rows(PAST)
        for h in range(HEADS):
            ch = ckt_ref[h * LANES:(h + 1) * LANES, :].T.astype(BF16)
            k2_ref[h, LAT_S:sk, :] = jnp.where(lane < DA_DH, ch, zero)
            k2_ref[h, sk + LAT_S:2 * sk, :] = jnp.where(lane >= DA_DH, ch, zero)
            vb_ref[h, 0:DA_DV, LAT_S:sk] = cv_ref[:, h, :].T.astype(BF16)
            vb_ref[h, DA_DV:DA_DV + V_PAD, LAT_S:sk] = ones

    @pl.when(step >= LAT_TILES)
    def _():
        lam = _diff_lambda(lam_ref, lam_init)
        finish = None
        for r in range(Q_PER_STEP):
            q_rows, x_rows = _query_rows(r)
            heads = _diff_heads(q_s, k2_ref, vb_ref, gate_s, g_sub_ref, lam, lam_init, sk, q_rows, finish)
            finish = functools.partial(_finish, heads, w_out_ref, xa_ref, mods_ref, g_post_ref, row, o_ref, x_rows)
        finish()


def _layer_diff_lat(x, mods, mi, g_pre, g_post, w_in, wj, w_out, wo, cos_t, sin_t, cache_kt, cache_v, g_sub,
                    lam_p, layer, j):
    lam_init = 0.8 - 0.6 * math.exp(-0.3 * layer)
    sk = LAT_S + PAST
    x_proj, x_attn, rope = _lat_specs()
    return pl.pallas_call(
        functools.partial(_layer_diff_lat_kernel, lam_init),
        grid=(N_LAT_B, LAT_TILES + LAT_QBLOCKS),
        in_specs=[
            x_proj, x_attn,
            pl.BlockSpec((None, COND_ROWS, 3 * D_MODEL), lambda b, s: (mi, 0, 0)),
            pl.BlockSpec((None, 1, D_MODEL), lambda b, s: (layer, 0, 0)),
            pl.BlockSpec((None, D_MODEL, 4 * D_MODEL), lambda b, s: (wj, 0, 0)),
            rope, rope,
            pl.BlockSpec((None, None, D_MODEL, PAST), lambda b, s: (b, j, 0, 0)),
            pl.BlockSpec((None, None, PAST, HEADS, DA_DV), lambda b, s: (b, j, 0, 0, 0)),
            pl.BlockSpec((None, 1, D_MODEL), lambda b, s: (layer, 0, 0)),
            pl.BlockSpec((None, D_MODEL, D_MODEL), lambda b, s: (wo, 0, 0)),
            pl.BlockSpec((None, 1, DA_DV), lambda b, s: (j, 0, 0)),
            pl.BlockSpec((None, 4, DA_DH), lambda b, s: (j, 0, 0)),
        ],
        out_specs=x_attn,
        out_shape=jax.ShapeDtypeStruct(x.shape, F32),
        scratch_shapes=[
            pltpu.VMEM((HEADS, 2 * sk, LANES), BF16),
            pltpu.VMEM((HEADS, DA_DV + V_PAD, sk), BF16),
            pltpu.VMEM((LAT_S, D_MODEL), BF16),
            pltpu.VMEM((LAT_S, D_MODEL), BF16),
        ],
        compiler_params=pltpu.CompilerParams(
            dimension_semantics=("arbitrary", "arbitrary"), vmem_limit_bytes=VMEM_LIMIT),
        name="layer_diff_lat",
    )(x, x, mods, g_pre, w_in, cos_t, sin_t, cache_kt, cache_v, g_post, w_out, g_sub, lam_p)


def _layer_mla_lat_kernel(xp_ref, xa_ref, mods_ref, g_pre_ref, w_in_ref, g_qa_ref, w_qb_ref, g_kva_ref,
                          cos_ref, sin_ref, cckv_ref, ckpe_ref, g_post_ref, w_out_ref, w_kvb_ref,
                          o_ref, kf_ref, vf_ref, q_s, gate_s):
    step = pl.program_id(1)
    row = 1 + pl.program_id(0)

    for t in range(LAT_TILES):
        @pl.when(step == t)
        def _(t=t):
            r0 = t * TM
            hb = _modulated(xp_ref, mods_ref, g_pre_ref, row)
            cos = cos_ref[...]
            sin = sin_ref[...]
            qn = _mla_queries(hb, w_in_ref, g_qa_ref)
            for h in range(HEADS):
                lo = h * MLA_QK_PAD
                q = jnp.dot(qn, w_qb_ref[:, lo:lo + MLA_QK_PAD], preferred_element_type=F32) * MLA_Q_SCALE
                q_s[r0:r0 + TM, lo:lo + LANES] = q[:, 0:LANES].astype(BF16)
                q_s[r0:r0 + TM, lo + LANES:lo + 2 * LANES] = _rope_slab(
                    q[:, LANES:2 * LANES], cos, sin).astype(BF16)
            ckv = _rms(_proj_t(hb, w_in_ref, _KV), g_kva_ref[...]).astype(BF16)
            gate_s[r0:r0 + TM, :] = _silu(_proj_t(hb, w_in_ref, _GT)).astype(BF16)
            kpe = _rope_slab(_proj_t(hb, w_in_ref, _PE), cos, sin)[:, 0:MLA_ROPE].astype(BF16)
            _mla_expand(ckv, kpe, w_kvb_ref, kf_ref, vf_ref, r0, TM)

    @pl.when(step == 0)
    def _():
        _mla_expand(cckv_ref[...].astype(BF16), ckpe_ref[...].astype(BF16), w_kvb_ref, kf_ref, vf_ref,
                    LAT_S, PAST)

    @pl.when(step >= LAT_TILES)
    def _():
        finish = None
        for r in range(Q_PER_STEP):
            q_rows, x_rows = _query_rows(r)
            heads = _mla_heads(q_s, kf_ref, vf_ref, gate_s, q_rows, finish)
            finish = functools.partial(_finish, heads, w_out_ref, xa_ref, mods_ref, g_post_ref, row, o_ref, x_rows)
        finish()


def _layer_mla_lat(x, mods, mi, g_pre, g_post, w_in, wj, g_qa, w_qb, g_kva, w_out, wo, w_kvb, cos_t, sin_t,
                   cache_ckv, cache_kpe, layer, j):
    sk = LAT_S + PAST
    x_proj, x_attn, rope = _lat_specs()
    return pl.pallas_call(
        _layer_mla_lat_kernel,
        grid=(N_LAT_B, LAT_TILES + LAT_QBLOCKS),
        in_specs=[
            x_proj, x_attn,
            pl.BlockSpec((None, COND_ROWS, 3 * D_MODEL), lambda b, s: (mi, 0, 0)),
            pl.BlockSpec((None, 1, D_MODEL), lambda b, s: (layer, 0, 0)),
            pl.BlockSpec((None, MLA_IN, D_MODEL), lambda b, s: (wj, 0, 0)),
            pl.BlockSpec((None, 1, MLA_Q_LORA), lambda b, s: (j, 0, 0)),
            pl.BlockSpec((None, MLA_Q_LORA, HEADS * MLA_QK_PAD), lambda b, s: (j, 0, 0)),
            pl.BlockSpec((None, 1, MLA_KV_LORA), lambda b, s: (j, 0, 0)),
            rope, rope,
            pl.BlockSpec((None, None, PAST, MLA_KV_LORA), lambda b, s: (b, j, 0, 0)),
            pl.BlockSpec((None, None, PAST, MLA_ROPE), lambda b, s: (b, j, 0, 0)),
            pl.BlockSpec((None, 1, D_MODEL), lambda b, s: (layer, 0, 0)),
            pl.BlockSpec((None, D_MODEL, D_MODEL), lambda b, s: (wo, 0, 0)),
            pl.BlockSpec((None, MLA_KV_LORA, HEADS * (MLA_NOPE + MLA_DV)), lambda b, s: (j, 0, 0)),
        ],
        out_specs=x_attn,
        out_shape=jax.ShapeDtypeStruct(x.shape, F32),
        scratch_shapes=[
            pltpu.VMEM((HEADS, sk, MLA_QK_PAD), BF16),
            pltpu.VMEM((HEADS, sk, MLA_DV), BF16),
            pltpu.VMEM((LAT_S, HEADS * MLA_QK_PAD), BF16),
            pltpu.VMEM((LAT_S, D_MODEL), BF16),
        ],
        compiler_params=pltpu.CompilerParams(
            dimension_semantics=("arbitrary", "arbitrary"), vmem_limit_bytes=VMEM_LIMIT),
        name="layer_mla_lat",
    )(x, x, mods, g_pre, w_in, g_qa, w_qb, g_kva, cos_t, sin_t, cache_ckv, cache_kpe, g_post, w_out, w_kvb)


def _rope_tables():
    rows = LAT_S // GRID_W
    row = jnp.repeat(jnp.arange(rows), GRID_W).astype(F32)
    col = jnp.tile(jnp.arange(GRID_W), rows).astype(F32)
    n_freq = DA_DH // 4
    inv = ROPE_THETA ** (-jnp.arange(n_freq, dtype=F32) / n_freq)
    ang = jnp.concatenate([row[:, None] * inv, col[:, None] * inv], axis=-1)
    cos, sin = jnp.cos(ang), jnp.sin(ang)
    return (jnp.concatenate([cos, cos, cos, cos], axis=-1),
            jnp.concatenate([-sin, sin, -sin, sin], axis=-1))


def kernel(x_prompt, x_sample, cache_diff_k, cache_diff_v, cache_mla_ckv, cache_mla_kpe,
           c, c_ctx, w_ada, b_ada, g_pre, g_post, w_out,
           da_w_in, da_lam_q1, da_lam_k1, da_lam_q2, da_lam_k2, da_g_sub,
           mla_w_in, mla_g_qa, mla_w_qb, mla_g_kva, mla_w_kvb):
    assert DA_DH == MLA_ROPE
    assert x_prompt.shape == (N_CTX_B, CTX_S, D_MODEL) and x_sample.shape == (N_LAT_B, LAT_S, D_MODEL)
    assert cache_diff_k.shape == (N_LAT_B, (DEPTH + 1) // 2, PAST, HEADS, 2, DA_DH)
    assert cache_mla_ckv.shape == (N_LAT_B, DEPTH // 2, PAST, MLA_KV_LORA)
    assert w_ada.shape == (DEPTH, D_MODEL, 3 * D_MODEL) and da_w_in.shape[1:] == (D_MODEL, 4 * D_MODEL)
    n_diff = (DEPTH + 1) // 2
    n_mla = DEPTH // 2

    cond = jnp.concatenate(
        [c_ctx[None, :], c, jnp.zeros((COND_ROWS - 1 - N_LAT_B, D_MODEL), F32)], axis=0)
    b_ada3 = b_ada.reshape(DEPTH, 1, 3 * D_MODEL)
    mods_first = _ada_call(cond, w_ada, b_ada3, 2)

    cos_t, sin_t = _rope_tables()
    g_pre3 = g_pre.reshape(DEPTH, 1, D_MODEL)
    g_post3 = g_post.reshape(DEPTH, 1, D_MODEL)
    steps = N_CTX_B // CTX_PER_STEP
    w_out_first = w_out[0:1].astype(BF16)
    da_w_in_first = da_w_in[0:1].astype(BF16)
    g_sub3 = da_g_sub.reshape(n_diff, 1, DA_DV)
    lam_p = jnp.stack([da_lam_q1, da_lam_k1, da_lam_q2, da_lam_k2], axis=1)
    cache_dkt = jnp.transpose(cache_diff_k, (0, 1, 3, 4, 5, 2)).reshape(N_LAT_B, n_diff, D_MODEL, PAST)

    mla_w_in_t = jnp.swapaxes(mla_w_in, 1, 2)
    w_qb4 = mla_w_qb.reshape(n_mla, MLA_Q_LORA, HEADS, MLA_NOPE + MLA_ROPE)
    w_qb_b = jnp.pad(w_qb4, ((0, 0), (0, 0), (0, 0), (0, MLA_QK_PAD - MLA_NOPE - MLA_ROPE))).reshape(
        n_mla, MLA_Q_LORA, HEADS * MLA_QK_PAD).astype(BF16)
    g_qa3 = mla_g_qa.reshape(n_mla, 1, MLA_Q_LORA)
    g_kva3 = mla_g_kva.reshape(n_mla, 1, MLA_KV_LORA)

    x_ctx = x_prompt.reshape(N_CTX_B * CTX_S, D_MODEL)
    x_lat = x_sample.reshape(N_LAT_B * LAT_S, D_MODEL)
    st_diff = None
    st_mla = None

    assert DEPTH == 4
    da_w = [(da_w_in_first, 0), None]
    mla_w = [None, None]
    w_o = [(w_out_first, 0), None, None, None]
    wide = D_MODEL * 4 // steps
    cols = D_MODEL // steps
    per_layer = steps // 2
    kv_cols = HEADS * (MLA_NOPE + MLA_DV) // steps

    mods = [(mods_first, 0), (mods_first, 1), None, None]

    def mla_in_cast(jj):
        return _cast_job(mla_w_in_t, (None, MLA_IN, cols), lambda t: (jj, 0, t),
                         (1, MLA_IN, D_MODEL), lambda t: (0, 0, t))

    side_jobs = {
        0: [_cast_job(da_w_in, (None, D_MODEL, wide), lambda t: (1, 0, t),
                      (1, D_MODEL, 4 * D_MODEL), lambda t: (0, 0, t)),
            mla_in_cast(0),
            _cast_job(w_out, (None, D_MODEL // steps, D_MODEL), lambda t: (1, t, 0),
                      (1, D_MODEL, D_MODEL), lambda t: (0, t, 0)),
            _cast_job(mla_w_kvb, (n_mla, MLA_KV_LORA, kv_cols), lambda t: (0, 0, t),
                      mla_w_kvb.shape, lambda t: (0, 0, t))],
        1: [mla_in_cast(1),
            _cast_job(w_out, (None, D_MODEL // per_layer, D_MODEL),
                      lambda t: (2 + t // per_layer, t % per_layer, 0),
                      (2, D_MODEL, D_MODEL), lambda t: (t // per_layer, t % per_layer, 0)),
            _ada_job(cond, w_ada, b_ada3, 2, steps)],
        2: [_ada_job(cond, w_ada, b_ada3, 3, steps)],
        3: [],
    }

    for i in range(DEPTH):
        j = i // 2
        if i % 2 == 0:
            w_j = da_w[j]
            outs = _layer_diff_ctx(x_ctx, *mods[i], g_pre3, g_post3, *w_j, *w_o[i], g_sub3, lam_p, st_diff,
                                   i, j, n_diff, side_jobs=side_jobs[i])
            x_ctx, st_dkt, st_dv = outs[0:3]
            st_diff = (st_dkt, st_dv)
            if i == 0:
                da_w[1], mla_w[0], w_o[1], w_kvb_b = (outs[3], 0), (outs[4], 0), (outs[5], 0), outs[6]
            else:
                mods[3] = (outs[3], 0)
            x_lat = _layer_diff_lat(x_lat, *mods[i], g_pre3, g_post3, *w_j, *w_o[i], cos_t, sin_t,
                                    cache_dkt, cache_diff_v, g_sub3, lam_p, i, j)
        else:
            w_j = mla_w[j]
            outs = _layer_mla_ctx(x_ctx, *mods[i], g_pre3, g_post3, *w_j, g_qa3, w_qb_b, g_kva3, *w_o[i], w_kvb_b,
                                  st_mla, i, j, n_mla, side_jobs=side_jobs[i])
            x_ctx, st_ckv, st_kpe = outs[0:3]
            st_mla = (st_ckv, st_kpe)
            if i == 1:
                mla_w[1] = (outs[3], 0)
                w_o[2], w_o[3] = (outs[4], 0), (outs[4], 1)
                mods[2] = (outs[5], 0)
            x_lat = _layer_mla_lat(x_lat, *mods[i], g_pre3, g_post3, *w_j, g_qa3, w_qb_b, g_kva3, *w_o[i], w_kvb_b,
                                   cos_t, sin_t, cache_mla_ckv, cache_mla_kpe, i, j)

    return (x_ctx.reshape(N_CTX_B, CTX_S, D_MODEL),
            x_lat.reshape(N_LAT_B, LAT_S, D_MODEL),
            jnp.transpose(st_dkt.reshape(N_CTX_B, n_diff, HEADS, 2, DA_DH, CTX_S), (0, 1, 5, 2, 3, 4)),
            st_dv,
            st_ckv,
            st_kpe)
```

```python
import functools
import math

import jax
import jax.numpy as jnp
from jax import lax
from jax.experimental import pallas as pl
from jax.experimental.pallas import tpu as pltpu

F32 = jnp.float32
BF16 = jnp.bfloat16

D_MODEL = 1024
DEPTH = 4
N_CTX_B = 16
CTX_S = 256
N_LAT_B = 4
LAT_S = 1024
PAST = 256
GRID_W = 64
HEADS = 8
DA_DH = 64
DA_DV = 128
MLA_Q_LORA = 384
MLA_KV_LORA = 256
MLA_NOPE = 128
MLA_ROPE = 64
MLA_DV = 128
MLA_QK_PAD = 256
ROPE_THETA = 10000.0
NORM_EPS = 1e-6
COND_ROWS = 8

LANES = 128
V7X_VMEM_MIB = 64
V_PAD = 16
TQ = 256
TM = 1024
VMEM_LIMIT = (V7X_VMEM_MIB - 8) * 1024 * 1024

_TRANS_B = (((1,), (1,)), ((), ()))
SCORES_AHEAD = 2
CTX_PER_STEP = 2
DA_Q_SCALE = DA_DH ** -0.5 * math.log2(math.e)
MLA_Q_SCALE = (MLA_NOPE + MLA_ROPE) ** -0.5 * math.log2(math.e)


def _silu(x):
    return x * (1.0 / (1.0 + jnp.exp(-x)))


def _rms(x, g):
    r = lax.rsqrt(jnp.mean(x * x, axis=-1, keepdims=True) + NORM_EPS)
    return (x * r) * g


def _rope_slab(x, cos, sin_signed):
    half = DA_DH // 2
    lane = lax.broadcasted_iota(jnp.int32, x.shape, 1)
    first_half = (lane % DA_DH) < half
    partner = jnp.where(first_half, pltpu.roll(x, LANES - half, axis=1), pltpu.roll(x, half, axis=1))
    return x * cos + partner * sin_signed


def _ones_rows(cols):
    row = lax.broadcasted_iota(jnp.int32, (V_PAD, cols), 0)
    return jnp.where(row == 0, 1.0, 0.0).astype(BF16)


def _cond_row(mods_ref, row):
    m = mods_ref[pl.ds(row, 1), :]
    return m[:, 0:D_MODEL], m[:, D_MODEL:2 * D_MODEL], m[:, 2 * D_MODEL:3 * D_MODEL]


def _modulated(x_ref, mods_ref, g_ref, row):
    shift, scale, _ = _cond_row(mods_ref, row)
    h = _rms(x_ref[...], g_ref[...]) * (1.0 + scale) + shift
    return h.astype(BF16)


def _ada_kernel(cond_ref, w_ref, b_ref, o_ref):
    a = _silu(cond_ref[...]).astype(BF16)
    o_ref[...] = jnp.dot(a, w_ref[...].astype(BF16), preferred_element_type=F32) + b_ref[...]


def _ada_call(cond, w_ada, b_ada3, n_layers):
    return pl.pallas_call(
        _ada_kernel,
        grid=(n_layers,),
        in_specs=[
            pl.BlockSpec((COND_ROWS, D_MODEL), lambda i: (0, 0)),
            pl.BlockSpec((None, D_MODEL, 3 * D_MODEL), lambda i: (i, 0, 0)),
            pl.BlockSpec((None, 1, 3 * D_MODEL), lambda i: (i, 0, 0)),
        ],
        out_specs=pl.BlockSpec((None, COND_ROWS, 3 * D_MODEL), lambda i: (i, 0, 0)),
        out_shape=jax.ShapeDtypeStruct((n_layers, COND_ROWS, 3 * D_MODEL), F32),
        compiler_params=pltpu.CompilerParams(
            dimension_semantics=("arbitrary",), vmem_limit_bytes=VMEM_LIMIT),
        name="ada_mod",
    )(cond, w_ada, b_ada3)


def _cast_block(src_ref, dst_ref):
    dst_ref[...] = src_ref[...].astype(BF16)


def _cast_job(array, blk, src_map, shape, dst_map):
    return dict(args=[array], in_specs=[pl.BlockSpec(blk, src_map)], out_spec=pl.BlockSpec(blk, dst_map),
                out_shape=jax.ShapeDtypeStruct(shape, BF16), run=_cast_block)


def _ada_job(cond, w_ada, b_ada3, layer, steps):
    tn = 3 * D_MODEL // steps
    return dict(args=[cond, w_ada, b_ada3],
                in_specs=[pl.BlockSpec((COND_ROWS, D_MODEL), lambda t: (0, 0)),
                          pl.BlockSpec((None, D_MODEL, tn), lambda t: (layer, 0, t)),
                          pl.BlockSpec((None, 1, tn), lambda t: (layer, 0, t))],
                out_spec=pl.BlockSpec((None, COND_ROWS, tn), lambda t: (0, 0, t)),
                out_shape=jax.ShapeDtypeStruct((1, COND_ROWS, 3 * D_MODEL), F32), run=_ada_kernel)


def _side_job_specs(jobs):
    in_specs = [spec for job in jobs for spec in job["in_specs"]]
    args = [a for job in jobs for a in job["args"]]
    runners = tuple((len(job["args"]), job["run"]) for job in jobs)
    return in_specs, [job["out_spec"] for job in jobs], [job["out_shape"] for job in jobs], args, runners


def _write_state(ref, r, j, value):
    if j is None:
        ref[r] = value
    else:
        for jj in range(ref.shape[1]):
            ref[r, jj] = value if jj == j else jnp.zeros_like(value)


def _state_plumbing(states, shapes, j, n_inputs, n_outputs):
    def index_map(shp, layer_index):
        return lambda t: (t, layer_index) + (0,) * (len(shp) - 2)

    if states is None:
        specs = [pl.BlockSpec((CTX_PER_STEP,) + shp[1:], index_map(shp, 0)) for shp in shapes]
        return [], specs, {}, ()
    specs = [pl.BlockSpec((CTX_PER_STEP, None) + shp[2:], index_map(shp, j)) for shp in shapes]
    in_specs = [pl.BlockSpec(memory_space=pl.ANY) for _ in shapes]
    aliases = {n_inputs + i: n_outputs + i for i in range(len(shapes))}
    return in_specs, specs, aliases, tuple(states)


def _finish(heads, w_out_ref, x_ref, mods_ref, g_post_ref, row, o_ref, rows=slice(None)):
    _, _, gate = _cond_row(mods_ref, row)
    og = jnp.concatenate(heads, axis=1)
    y = jnp.dot(og, w_out_ref[...], preferred_element_type=F32)
    o_ref[rows, :] = x_ref[rows, :] + gate * _rms(y, g_post_ref[...])


def _diff_lambda(lam_ref, lam_init):
    p = lam_ref[...]
    a = jnp.sum(p[0:1, :] * p[1:2, :], axis=-1, keepdims=True)
    b = jnp.sum(p[2:3, :] * p[3:4, :], axis=-1, keepdims=True)
    return jnp.exp(a) - jnp.exp(b) + lam_init


def _stack_diff_keys(k, k2_ref, r0, sk):
    rows = k.shape[0]
    lane = lax.broadcasted_iota(jnp.int32, (rows, LANES), 1)
    zero = jnp.zeros((rows, LANES), BF16)
    for h in range(HEADS):
        kh = k[:, h * LANES:(h + 1) * LANES]
        k2_ref[h, r0:r0 + rows, :] = jnp.where(lane < DA_DH, kh, zero)
        k2_ref[h, sk + r0:sk + r0 + rows, :] = jnp.where(lane >= DA_DH, kh, zero)


def _stack_values_t(vt, vt_ref, c0, cols):
    ones = _ones_rows(cols)
    for h in range(HEADS):
        vt_ref[h, 0:DA_DV, c0:c0 + cols] = vt[h * DA_DV:(h + 1) * DA_DV, :]
        vt_ref[h, DA_DV:DA_DV + V_PAD, c0:c0 + cols] = ones


def _diff_heads(q_ref, k2_ref, vt_ref, gate_ref, g_sub_ref, lam, lam_init, sk, rows=slice(None), after_issue=None):
    g_sub = g_sub_ref[...]

    def scores(h):
        qh = q_ref[rows, h * LANES:(h + 1) * LANES]
        return tuple(lax.dot_general(k2_ref[h, c * sk:(c + 1) * sk, :], qh, _TRANS_B,
                                     preferred_element_type=F32) for c in range(2))

    heads = []
    pending = [scores(h) for h in range(SCORES_AHEAD)]
    if after_issue is not None:
        after_issue()
    for h in range(HEADS):
        lo, hi = h * LANES, (h + 1) * LANES
        s = pending.pop(0)
        if h + SCORES_AHEAD < HEADS:
            pending.append(scores(h + SCORES_AHEAD))
        s0, s1 = s
        e0 = jnp.exp2(s0 - jnp.max(s0, axis=0, keepdims=True)).astype(BF16)
        e1 = jnp.exp2(s1 - jnp.max(s1, axis=0, keepdims=True)).astype(BF16)
        pv0 = jnp.dot(vt_ref[h], e0, preferred_element_type=F32)
        pv1 = jnp.dot(vt_ref[h], e1, preferred_element_type=F32)
        a0 = 1.0 / pv0[DA_DV:DA_DV + 1, :]
        a1 = lam / pv1[DA_DV:DA_DV + 1, :]
        ot = pv0[0:DA_DV, :] * a0 - pv1[0:DA_DV, :] * a1
        ot = ot * lax.rsqrt(jnp.mean(ot * ot, axis=0, keepdims=True) + NORM_EPS)
        o = (ot.T * g_sub) * (1.0 - lam_init)
        heads.append((o * gate_ref[rows, lo:hi]).astype(BF16))
    return heads


MLA_IN = MLA_Q_LORA + MLA_KV_LORA + MLA_ROPE + D_MODEL
_QA = slice(0, MLA_Q_LORA)
_KV = slice(MLA_Q_LORA, MLA_Q_LORA + MLA_KV_LORA)
_PE = slice(_KV.stop, _KV.stop + LANES)
_GT = slice(_KV.stop + MLA_ROPE, MLA_IN)


def _proj_t(hb, w_t_ref, rows):
    return lax.dot_general(hb, w_t_ref[rows, :], _TRANS_B, preferred_element_type=F32)


def _mla_queries(hb, w_in_ref, g_qa_ref):
    q_a = _proj_t(hb, w_in_ref, _QA)
    return _rms(q_a, g_qa_ref[...]).astype(BF16)


def _mla_expand(ckv, kpe, w_kvb_ref, kf_ref, vf_ref, r0, rows):
    zero = jnp.zeros((rows, LANES - MLA_ROPE), BF16)
    for h in range(HEADS):
        lo = h * (MLA_NOPE + MLA_DV)
        kv = jnp.dot(ckv, w_kvb_ref[:, lo:lo + MLA_NOPE + MLA_DV], preferred_element_type=F32)
        kf_ref[h, r0:r0 + rows, 0:MLA_NOPE] = kv[:, 0:MLA_NOPE].astype(BF16)
        kf_ref[h, r0:r0 + rows, MLA_NOPE:MLA_NOPE + MLA_ROPE] = kpe
        kf_ref[h, r0:r0 + rows, MLA_NOPE + MLA_ROPE:MLA_QK_PAD] = zero
        vf_ref[h, r0:r0 + rows, :] = kv[:, MLA_NOPE:MLA_NOPE + MLA_DV].astype(BF16)


def _mla_heads(q_ref, kf_ref, vf_ref, gate_ref, rows=slice(None), after_issue=None):
    def scores(h):
        return lax.dot_general(q_ref[rows, h * MLA_QK_PAD:(h + 1) * MLA_QK_PAD], kf_ref[h], _TRANS_B,
                               preferred_element_type=F32)

    heads = []
    pending = [scores(h) for h in range(SCORES_AHEAD)]
    if after_issue is not None:
        after_issue()
    for h in range(HEADS):
        s = pending.pop(0)
        if h + SCORES_AHEAD < HEADS:
            pending.append(scores(h + SCORES_AHEAD))
        e = jnp.exp2(s - jnp.max(s, axis=-1, keepdims=True))
        inv = 1.0 / jnp.sum(e, axis=-1, keepdims=True)
        o = jnp.dot(e.astype(BF16), vf_ref[h], preferred_element_type=F32) * inv
        lo, hi = h * LANES, (h + 1) * LANES
        heads.append((o * gate_ref[rows, lo:hi]).astype(BF16))
    return heads


def _split_layer_refs(refs, runners):
    scratch = refs[-2:]
    refs = refs[:-2]
    n_jobs = len(runners)
    job_out = refs[len(refs) - n_jobs:]
    main = refs[len(refs) - n_jobs - 3:len(refs) - n_jobs]
    pos = len(refs) - n_jobs - 3 - sum(n for n, _ in runners)
    for (n, run), out in zip(runners, job_out):
        run(*refs[pos:pos + n], out)
        pos += n
    return main, scratch


def _layer_diff_ctx_kernel(create_j, lam_init, runners, x_ref, mods_ref, g_pre_ref, w_in_ref, g_post_ref,
                           w_out_ref, g_sub_ref, lam_ref, *refs):
    (o_ref, skt_ref, sv_ref), (k2_ref, vb_ref) = _split_layer_refs(refs, runners)
    hb = _modulated(x_ref, mods_ref, g_pre_ref, 0)
    n = D_MODEL
    q = (jnp.dot(hb, w_in_ref[:, 0:n], preferred_element_type=F32) * DA_Q_SCALE).astype(BF16)
    k = jnp.dot(hb, w_in_ref[:, n:2 * n], preferred_element_type=F32)
    v = jnp.dot(hb, w_in_ref[:, 2 * n:3 * n], preferred_element_type=F32)
    gate = _silu(jnp.dot(hb, w_in_ref[:, 3 * n:4 * n], preferred_element_type=F32)).astype(BF16)
    kb = k.astype(BF16)
    lam = _diff_lambda(lam_ref, lam_init)
    finish = None
    for r in range(CTX_PER_STEP):
        rows = slice(r * CTX_S, (r + 1) * CTX_S)
        vt = v[rows, :].T
        _write_state(skt_ref, r, create_j, k[rows, :].T)
        _write_state(sv_ref, r, create_j, v[rows, :].reshape(CTX_S, HEADS, DA_DV))
        _stack_diff_keys(kb[rows, :], k2_ref.at[r], 0, CTX_S)
        _stack_values_t(vt.astype(BF16), vb_ref.at[r], 0, CTX_S)
        heads = _diff_heads(q, k2_ref.at[r], vb_ref.at[r], gate, g_sub_ref, lam, lam_init, CTX_S, rows, finish)
        finish = functools.partial(_finish, heads, w_out_ref, x_ref, mods_ref, g_post_ref, 0, o_ref, rows)
    finish()


def _layer_diff_ctx(x, mods, mi, g_pre, g_post, w_in, wj, w_out, wo, g_sub, lam_p, states, layer, j, n_layers,
                    side_jobs=()):
    lam_init = 0.8 - 0.6 * math.exp(-0.3 * layer)
    shapes = [(N_CTX_B, n_layers, D_MODEL, CTX_S), (N_CTX_B, n_layers, CTX_S, HEADS, DA_DV)]
    st_in_specs, st_out_specs, aliases, st_args = _state_plumbing(states, shapes, j, 8, 1)
    cj_in, cj_out, cj_shapes, cj_args, runners = _side_job_specs(side_jobs)
    tok_spec = pl.BlockSpec((CTX_PER_STEP * CTX_S, D_MODEL), lambda t: (t, 0))
    return pl.pallas_call(
        functools.partial(_layer_diff_ctx_kernel, j if states is None else None, lam_init, runners),
        grid=(N_CTX_B // CTX_PER_STEP,),
        in_specs=[
            tok_spec,
            pl.BlockSpec((None, COND_ROWS, 3 * D_MODEL), lambda t: (mi, 0, 0)),
            pl.BlockSpec((None, 1, D_MODEL), lambda t: (layer, 0, 0)),
            pl.BlockSpec((None, D_MODEL, 4 * D_MODEL), lambda t: (wj, 0, 0)),
            pl.BlockSpec((None, 1, D_MODEL), lambda t: (layer, 0, 0)),
            pl.BlockSpec((None, D_MODEL, D_MODEL), lambda t: (wo, 0, 0)),
            pl.BlockSpec((None, 1, DA_DV), lambda t: (j, 0, 0)),
            pl.BlockSpec((None, 4, DA_DH), lambda t: (j, 0, 0)),
        ] + st_in_specs + cj_in,
        out_specs=[tok_spec] + st_out_specs + cj_out,
        out_shape=[jax.ShapeDtypeStruct(x.shape, F32)] + [jax.ShapeDtypeStruct(shp, F32) for shp in shapes]
        + cj_shapes,
        input_output_aliases=aliases,
        scratch_shapes=[
            pltpu.VMEM((CTX_PER_STEP, HEADS, 2 * CTX_S, LANES), BF16),
            pltpu.VMEM((CTX_PER_STEP, HEADS, DA_DV + V_PAD, CTX_S), BF16),
        ],
        compiler_params=pltpu.CompilerParams(
            dimension_semantics=("arbitrary",), vmem_limit_bytes=VMEM_LIMIT),
        name="layer_diff_ctx",
    )(x, mods, g_pre, w_in, g_post, w_out, g_sub, lam_p, *st_args, *cj_args)


def _layer_mla_ctx_kernel(create_j, runners, x_ref, mods_ref, g_pre_ref, w_in_ref, g_qa_ref, w_qb_ref, g_kva_ref,
                          g_post_ref, w_out_ref, w_kvb_ref, *refs):
    (o_ref, ckv_ref, kpe_ref), (kf_ref, vf_ref) = _split_layer_refs(refs, runners)
    hb = _modulated(x_ref, mods_ref, g_pre_ref, 0)
    qn = _mla_queries(hb, w_in_ref, g_qa_ref)
    q = (jnp.dot(qn, w_qb_ref[...], preferred_element_type=F32) * MLA_Q_SCALE).astype(BF16)
    ckv = _rms(_proj_t(hb, w_in_ref, _KV), g_kva_ref[...])
    gate = _silu(_proj_t(hb, w_in_ref, _GT)).astype(BF16)
    pe = _proj_t(hb, w_in_ref, _PE)
    finish = None
    for r in range(CTX_PER_STEP):
        rows = slice(r * CTX_S, (r + 1) * CTX_S)
        kpe = pe[rows, 0:MLA_ROPE]
        _write_state(ckv_ref, r, create_j, ckv[rows, :])
        _write_state(kpe_ref, r, create_j, kpe)
        _mla_expand(ckv[rows, :].astype(BF16), kpe.astype(BF16), w_kvb_ref, kf_ref.at[r], vf_ref.at[r],
                    0, CTX_S)
        heads = _mla_heads(q, kf_ref.at[r], vf_ref.at[r], gate, rows, finish)
        finish = functools.partial(_finish, heads, w_out_ref, x_ref, mods_ref, g_post_ref, 0, o_ref, rows)
    finish()


def _layer_mla_ctx(x, mods, mi, g_pre, g_post, w_in, wj, g_qa, w_qb, g_kva, w_out, wo, w_kvb, states, layer, j,
                   n_layers, side_jobs=()):
    shapes = [(N_CTX_B, n_layers, CTX_S, MLA_KV_LORA), (N_CTX_B, n_layers, CTX_S, MLA_ROPE)]
    st_in_specs, st_out_specs, aliases, st_args = _state_plumbing(states, shapes, j, 10, 1)
    cj_in, cj_out, cj_shapes, cj_args, runners = _side_job_specs(side_jobs)
    tok_spec = pl.BlockSpec((CTX_PER_STEP * CTX_S, D_MODEL), lambda t: (t, 0))
    return pl.pallas_call(
        functools.partial(_layer_mla_ctx_kernel, j if states is None else None, runners),
        grid=(N_CTX_B // CTX_PER_STEP,),
        in_specs=[
            tok_spec,
            pl.BlockSpec((None, COND_ROWS, 3 * D_MODEL), lambda t: (mi, 0, 0)),
            pl.BlockSpec((None, 1, D_MODEL), lambda t: (layer, 0, 0)),
            pl.BlockSpec((None, MLA_IN, D_MODEL), lambda t: (wj, 0, 0)),
            pl.BlockSpec((None, 1, MLA_Q_LORA), lambda t: (j, 0, 0)),
            pl.BlockSpec((None, MLA_Q_LORA, HEADS * MLA_QK_PAD), lambda t: (j, 0, 0)),
            pl.BlockSpec((None, 1, MLA_KV_LORA), lambda t: (j, 0, 0)),
            pl.BlockSpec((None, 1, D_MODEL), lambda t: (layer, 0, 0)),
            pl.BlockSpec((None, D_MODEL, D_MODEL), lambda t: (wo, 0, 0)),
            pl.BlockSpec((None, MLA_KV_LORA, HEADS * (MLA_NOPE + MLA_DV)), lambda t: (j, 0, 0)),
        ] + st_in_specs + cj_in,
        out_specs=[tok_spec] + st_out_specs + cj_out,
        out_shape=[jax.ShapeDtypeStruct(x.shape, F32)] + [jax.ShapeDtypeStruct(shp, F32) for shp in shapes]
        + cj_shapes,
        input_output_aliases=aliases,
        scratch_shapes=[
            pltpu.VMEM((CTX_PER_STEP, HEADS, CTX_S, MLA_QK_PAD), BF16),
            pltpu.VMEM((CTX_PER_STEP, HEADS, CTX_S, MLA_DV), BF16),
        ],
        compiler_params=pltpu.CompilerParams(
            dimension_semantics=("arbitrary",), vmem_limit_bytes=VMEM_LIMIT),
        name="layer_mla_ctx",
    )(x, mods, g_pre, w_in, g_qa, w_qb, g_kva, g_post, w_out, w_kvb, *st_args, *cj_args)


LAT_TILES = LAT_S // TM
Q_PER_STEP = 1
LAT_QBLOCKS = LAT_S // (TQ * Q_PER_STEP)


def _lat_specs():
    tile = lambda s: jnp.minimum(s, LAT_TILES - 1)
    qblk = lambda s: jnp.maximum(s - LAT_TILES, 0)
    x_proj = pl.BlockSpec((TM, D_MODEL), lambda b, s: (b * LAT_TILES + tile(s), 0))
    x_attn = pl.BlockSpec((TQ * Q_PER_STEP, D_MODEL), lambda b, s: (b * LAT_QBLOCKS + qblk(s), 0))
    rope = pl.BlockSpec((TM, LANES), lambda b, s: (tile(s), 0))
    return x_proj, x_attn, rope


def _query_rows(r):
    i = pl.program_id(1) - LAT_TILES
    return pl.ds(pl.multiple_of((i * Q_PER_STEP + r) * TQ, TQ), TQ), slice(r * TQ, (r + 1) * TQ)


def _layer_diff_lat_kernel(lam_init, xp_ref, xa_ref, mods_ref, g_pre_ref, w_in_ref, cos_ref, sin_ref,
                           ckt_ref, cv_ref, g_post_ref, w_out_ref, g_sub_ref, lam_ref,
                           o_ref, k2_ref, vb_ref, q_s, gate_s):
    step = pl.program_id(1)
    row = 1 + pl.program_id(0)
    sk = LAT_S + PAST

    for t in range(LAT_TILES):
        @pl.when(step == t)
        def _(t=t):
            r0 = t * TM
            hb = _modulated(xp_ref, mods_ref, g_pre_ref, row)
            cos = cos_ref[...]
            sin = sin_ref[...]
            n = D_MODEL
            q = jnp.dot(hb, w_in_ref[:, 0:n], preferred_element_type=F32)
            k = jnp.dot(hb, w_in_ref[:, n:2 * n], preferred_element_type=F32)
            cos_q = cos * DA_Q_SCALE
            sin_q = sin * DA_Q_SCALE
            slabs = [slice(h * LANES, (h + 1) * LANES) for h in range(HEADS)]
            q_s[r0:r0 + TM, :] = jnp.concatenate(
                [_rope_slab(q[:, sl], cos_q, sin_q).astype(BF16) for sl in slabs], axis=1)
            kb = jnp.concatenate([_rope_slab(k[:, sl], cos, sin).astype(BF16) for sl in slabs], axis=1)
            _stack_diff_keys(kb, k2_ref, r0, sk)
            vt = jnp.dot(hb, w_in_ref[:, 2 * n:3 * n], preferred_element_type=F32).T.astype(BF16)
            _stack_values_t(vt, vb_ref, r0, TM)
            gate_s[r0:r0 + TM, :] = _silu(
                jnp.dot(hb, w_in_ref[:, 3 * n:4 * n], preferred_element_type=F32)).astype(BF16)

    @pl.when(step == 0)
    def _():
        lane = lax.broadcasted_iota(jnp.int32, (PAST, LANES), 1)
        zero = jnp.zeros((PAST, LANES), BF16)
        ones = _ones_rows(PAST)
        for h in range(HEADS):
            ch = ckt_ref[h * LANES:(h + 1) * LANES, :].T.astype(BF16)
            k2_ref[h, LAT_S:sk, :] = jnp.where(lane < DA_DH, ch, zero)
            k2_ref[h, sk + LAT_S:2 * sk, :] = jnp.where(lane >= DA_DH, ch, zero)
            vb_ref[h, 0:DA_DV, LAT_S:sk] = cv_ref[:, h, :].T.astype(BF16)
            vb_ref[h, DA_DV:DA_DV + V_PAD, LAT_S:sk] = ones

    @pl.when(step >= LAT_TILES)
    def _():
        lam = _diff_lambda(lam_ref, lam_init)
        finish = None
        for r in range(Q_PER_STEP):
            q_rows, x_rows = _query_rows(r)
            heads = _diff_heads(q_s, k2_ref, vb_ref, gate_s, g_sub_ref, lam, lam_init, sk, q_rows, finish)
            finish = functools.partial(_finish, heads, w_out_ref, xa_ref, mods_ref, g_post_ref, row, o_ref, x_rows)
        finish()


def _layer_diff_lat(x, mods, mi, g_pre, g_post, w_in, wj, w_out, wo, cos_t, sin_t, cache_kt, cache_v, g_sub,
                    lam_p, layer, j):
    lam_init = 0.8 - 0.6 * math.exp(-0.3 * layer)
    sk = LAT_S + PAST
    x_proj, x_attn, rope = _lat_specs()
    return pl.pallas_call(
        functools.partial(_layer_diff_lat_kernel, lam_init),
        grid=(N_LAT_B, LAT_TILES + LAT_QBLOCKS),
        in_specs=[
            x_proj, x_attn,
            pl.BlockSpec((None, COND_ROWS, 3 * D_MODEL), lambda b, s: (mi, 0, 0)),
            pl.BlockSpec((None, 1, D_MODEL), lambda b, s: (layer, 0, 0)),
            pl.BlockSpec((None, D_MODEL, 4 * D_MODEL), lambda b, s: (wj, 0, 0)),
            rope, rope,
            pl.BlockSpec((None, None, D_MODEL, PAST), lambda b, s: (b, j, 0, 0)),
            pl.BlockSpec((None, None, PAST, HEADS, DA_DV), lambda b, s: (b, j, 0, 0, 0)),
            pl.BlockSpec((None, 1, D_MODEL), lambda b, s: (layer, 0, 0)),
            pl.BlockSpec((None, D_MODEL, D_MODEL), lambda b, s: (wo, 0, 0)),
            pl.BlockSpec((None, 1, DA_DV), lambda b, s: (j, 0, 0)),
            pl.BlockSpec((None, 4, DA_DH), lambda b, s: (j, 0, 0)),
        ],
        out_specs=x_attn,
        out_shape=jax.ShapeDtypeStruct(x.shape, F32),
        scratch_shapes=[
            pltpu.VMEM((HEADS, 2 * sk, LANES), BF16),
            pltpu.VMEM((HEADS, DA_DV + V_PAD, sk), BF16),
            pltpu.VMEM((LAT_S, D_MODEL), BF16),
            pltpu.VMEM((LAT_S, D_MODEL), BF16),
        ],
        compiler_params=pltpu.CompilerParams(
            dimension_semantics=("arbitrary", "arbitrary"), vmem_limit_bytes=VMEM_LIMIT),
        name="layer_diff_lat",
    )(x, x, mods, g_pre, w_in, cos_t, sin_t, cache_kt, cache_v, g_post, w_out, g_sub, lam_p)


def _layer_mla_lat_kernel(xp_ref, xa_ref, mods_ref, g_pre_ref, w_in_ref, g_qa_ref, w_qb_ref, g_kva_ref,
                          cos_ref, sin_ref, cckv_ref, ckpe_ref, g_post_ref, w_out_ref, w_kvb_ref,
                          o_ref, kf_ref, vf_ref, q_s, gate_s):
    step = pl.program_id(1)
    row = 1 + pl.program_id(0)

    for t in range(LAT_TILES):
        @pl.when(step == t)
        def _(t=t):
            r0 = t * TM
            hb = _modulated(xp_ref, mods_ref, g_pre_ref, row)
            cos = cos_ref[...]
            sin = sin_ref[...]
            qn = _mla_queries(hb, w_in_ref, g_qa_ref)
            for h in range(HEADS):
                lo = h * MLA_QK_PAD
                q = jnp.dot(qn, w_qb_ref[:, lo:lo + MLA_QK_PAD], preferred_element_type=F32) * MLA_Q_SCALE
                q_s[r0:r0 + TM, lo:lo + LANES] = q[:, 0:LANES].astype(BF16)
                q_s[r0:r0 + TM, lo + LANES:lo + 2 * LANES] = _rope_slab(
                    q[:, LANES:2 * LANES], cos, sin).astype(BF16)
            ckv = _rms(_proj_t(hb, w_in_ref, _KV), g_kva_ref[...]).astype(BF16)
            gate_s[r0:r0 + TM, :] = _silu(_proj_t(hb, w_in_ref, _GT)).astype(BF16)
            kpe = _rope_slab(_proj_t(hb, w_in_ref, _PE), cos, sin)[:, 0:MLA_ROPE].astype(BF16)
            _mla_expand(ckv, kpe, w_kvb_ref, kf_ref, vf_ref, r0, TM)

    @pl.when(step == 0)
    def _():
        _mla_expand(cckv_ref[...].astype(BF16), ckpe_ref[...].astype(BF16), w_kvb_ref, kf_ref, vf_ref,
                    LAT_S, PAST)

    @pl.when(step >= LAT_TILES)
    def _():
        finish = None
        for r in range(Q_PER_STEP):
            q_rows, x_rows = _query_rows(r)
            heads = _mla_heads(q_s, kf_ref, vf_ref, gate_s, q_rows, finish)
            finish = functools.partial(_finish, heads, w_out_ref, xa_ref, mods_ref, g_post_ref, row, o_ref, x_rows)
        finish()


def _layer_mla_lat(x, mods, mi, g_pre, g_post, w_in, wj, g_qa, w_qb, g_kva, w_out, wo, w_kvb, cos_t, sin_t,
                   cache_ckv, cache_kpe, layer, j):
    sk = LAT_S + PAST
    x_proj, x_attn, rope = _lat_specs()
    return pl.pallas_call(
        _layer_mla_lat_kernel,
        grid=(N_LAT_B, LAT_TILES + LAT_QBLOCKS),
        in_specs=[
            x_proj, x_attn,
            pl.BlockSpec((None, COND_ROWS, 3 * D_MODEL), lambda b, s: (mi, 0, 0)),
            pl.BlockSpec((None, 1, D_MODEL), lambda b, s: (layer, 0, 0)),
            pl.BlockSpec((None, MLA_IN, D_MODEL), lambda b, s: (wj, 0, 0)),
            pl.BlockSpec((None, 1, MLA_Q_LORA), lambda b, s: (j, 0, 0)),
            pl.BlockSpec((None, MLA_Q_LORA, HEADS * MLA_QK_PAD), lambda b, s: (j, 0, 0)),
            pl.BlockSpec((None, 1, MLA_KV_LORA), lambda b, s: (j, 0, 0)),
            rope, rope,
            pl.BlockSpec((None, None, PAST, MLA_KV_LORA), lambda b, s: (b, j, 0, 0)),
            pl.BlockSpec((None, None, PAST, MLA_ROPE), lambda b, s: (b, j, 0, 0)),
            pl.BlockSpec((None, 1, D_MODEL), lambda b, s: (layer, 0, 0)),
            pl.BlockSpec((None, D_MODEL, D_MODEL), lambda b, s: (wo, 0, 0)),
            pl.BlockSpec((None, MLA_KV_LORA, HEADS * (MLA_NOPE + MLA_DV)), lambda b, s: (j, 0, 0)),
        ],
        out_specs=x_attn,
        out_shape=jax.ShapeDtypeStruct(x.shape, F32),
        scratch_shapes=[
            pltpu.VMEM((HEADS, sk, MLA_QK_PAD), BF16),
            pltpu.VMEM((HEADS, sk, MLA_DV), BF16),
            pltpu.VMEM((LAT_S, HEADS * MLA_QK_PAD), BF16),
            pltpu.VMEM((LAT_S, D_MODEL), BF16),
        ],
        compiler_params=pltpu.CompilerParams(
            dimension_semantics=("arbitrary", "arbitrary"), vmem_limit_bytes=VMEM_LIMIT),
        name="layer_mla_lat",
    )(x, x, mods, g_pre, w_in, g_qa, w_qb, g_kva, cos_t, sin_t, cache_ckv, cache_kpe, g_post, w_out, w_kvb)


def _rope_tables():
    rows = LAT_S // GRID_W
    row = jnp.repeat(jnp.arange(rows), GRID_W).astype(F32)
    col = jnp.tile(jnp.arange(GRID_W), rows).astype(F32)
    n_freq = DA_DH // 4
    inv = ROPE_THETA ** (-jnp.arange(n_freq, dtype=F32) / n_freq)
    ang = jnp.concatenate([row[:, None] * inv, col[:, None] * inv], axis=-1)
    cos, sin = jnp.cos(ang), jnp.sin(ang)
    return (jnp.concatenate([cos, cos, cos, cos], axis=-1),
            jnp.concatenate([-sin, sin, -sin, sin], axis=-1))


def kernel(x_prompt, x_sample, cache_diff_k, cache_diff_v, cache_mla_ckv, cache_mla_kpe,
           c, c_ctx, w_ada, b_ada, g_pre, g_post, w_out,
           da_w_in, da_lam_q1, da_lam_k1, da_lam_q2, da_lam_k2, da_g_sub,
           mla_w_in, mla_g_qa, mla_w_qb, mla_g_kva, mla_w_kvb):
    assert DA_DH == MLA_ROPE
    assert x_prompt.shape == (N_CTX_B, CTX_S, D_MODEL) and x_sample.shape == (N_LAT_B, LAT_S, D_MODEL)
    assert cache_diff_k.shape == (N_LAT_B, (DEPTH + 1) // 2, PAST, HEADS, 2, DA_DH)
    assert cache_mla_ckv.shape == (N_LAT_B, DEPTH // 2, PAST, MLA_KV_LORA)
    assert w_ada.shape == (DEPTH, D_MODEL, 3 * D_MODEL) and da_w_in.shape[1:] == (D_MODEL, 4 * D_MODEL)
    n_diff = (DEPTH + 1) // 2
    n_mla = DEPTH // 2

    cond = jnp.concatenate(
        [c_ctx[None, :], c, jnp.zeros((COND_ROWS - 1 - N_LAT_B, D_MODEL), F32)], axis=0)
    b_ada3 = b_ada.reshape(DEPTH, 1, 3 * D_MODEL)
    mods_first = _ada_call(cond, w_ada, b_ada3, 2)

    cos_t, sin_t = _rope_tables()
    g_pre3 = g_pre.reshape(DEPTH, 1, D_MODEL)
    g_post3 = g_post.reshape(DEPTH, 1, D_MODEL)
    steps = N_CTX_B // CTX_PER_STEP
    w_out_first = w_out[0:1].astype(BF16)
    da_w_in_first = da_w_in[0:1].astype(BF16)
    g_sub3 = da_g_sub.reshape(n_diff, 1, DA_DV)
    lam_p = jnp.stack([da_lam_q1, da_lam_k1, da_lam_q2, da_lam_k2], axis=1)
    cache_dkt = jnp.transpose(cache_diff_k, (0, 1, 3, 4, 5, 2)).reshape(N_LAT_B, n_diff, D_MODEL, PAST)

    mla_w_in_t = jnp.swapaxes(mla_w_in, 1, 2)
    w_qb4 = mla_w_qb.reshape(n_mla, MLA_Q_LORA, HEADS, MLA_NOPE + MLA_ROPE)
    w_qb_b = jnp.pad(w_qb4, ((0, 0), (0, 0), (0, 0), (0, MLA_QK_PAD - MLA_NOPE - MLA_ROPE))).reshape(
        n_mla, MLA_Q_LORA, HEADS * MLA_QK_PAD).astype(BF16)
    g_qa3 = mla_g_qa.reshape(n_mla, 1, MLA_Q_LORA)
    g_kva3 = mla_g_kva.reshape(n_mla, 1, MLA_KV_LORA)

    x_ctx = x_prompt.reshape(N_CTX_B * CTX_S, D_MODEL)
    x_lat = x_sample.reshape(N_LAT_B * LAT_S, D_MODEL)
    st_diff = None
    st_mla = None

    assert DEPTH == 4
    da_w = [(da_w_in_first, 0), None]
    mla_w = [None, None]
    w_o = [(w_out_first, 0), None, None, None]
    wide = D_MODEL * 4 // steps
    cols = D_MODEL // steps
    per_layer = steps // 2
    kv_cols = HEADS * (MLA_NOPE + MLA_DV) // steps

    mods = [(mods_first, 0), (mods_first, 1), None, None]

    def mla_in_cast(jj):
        return _cast_job(mla_w_in_t, (None, MLA_IN, cols), lambda t: (jj, 0, t),
                         (1, MLA_IN, D_MODEL), lambda t: (0, 0, t))

    side_jobs = {
        0: [_cast_job(da_w_in, (None, D_MODEL, wide), lambda t: (1, 0, t),
                      (1, D_MODEL, 4 * D_MODEL), lambda t: (0, 0, t)),
            mla_in_cast(0),
            _cast_job(w_out, (None, D_MODEL // steps, D_MODEL), lambda t: (1, t, 0),
                      (1, D_MODEL, D_MODEL), lambda t: (0, t, 0)),
            _cast_job(mla_w_kvb, (n_mla, MLA_KV_LORA, kv_cols), lambda t: (0, 0, t),
                      mla_w_kvb.shape, lambda t: (0, 0, t))],
        1: [mla_in_cast(1),
            _cast_job(w_out, (None, D_MODEL // per_layer, D_MODEL),
                      lambda t: (2 + t // per_layer, t % per_layer, 0),
                      (2, D_MODEL, D_MODEL), lambda t: (t // per_layer, t % per_layer, 0)),
            _ada_job(cond, w_ada, b_ada3, 2, steps)],
        2: [_ada_job(cond, w_ada, b_ada3, 3, steps)],
        3: [],
    }

    for i in range(DEPTH):
        j = i // 2
        if i % 2 == 0:
            w_j = da_w[j]
            outs = _layer_diff_ctx(x_ctx, *mods[i], g_pre3, g_post3, *w_j, *w_o[i], g_sub3, lam_p, st_diff,
                                   i, j, n_diff, side_jobs=side_jobs[i])
            x_ctx, st_dkt, st_dv = outs[0:3]
            st_diff = (st_dkt, st_dv)
            if i == 0:
                da_w[1], mla_w[0], w_o[1], w_kvb_b = (outs[3], 0), (outs[4], 0), (outs[5], 0), outs[6]
            else:
                mods[3] = (outs[3], 0)
            x_lat = _layer_diff_lat(x_lat, *mods[i], g_pre3, g_post3, *w_j, *w_o[i], cos_t, sin_t,
                                    cache_dkt, cache_diff_v, g_sub3, lam_p, i, j)
        else:
            w_j = mla_w[j]
            outs = _layer_mla_ctx(x_ctx, *mods[i], g_pre3, g_post3, *w_j, g_qa3, w_qb_b, g_kva3, *w_o[i], w_kvb_b,
                                  st_mla, i, j, n_mla, side_jobs=side_jobs[i])
            x_ctx, st_ckv, st_kpe = outs[0:3]
            st_mla = (st_ckv, st_kpe)
            if i == 1:
                mla_w[1] = (outs[3], 0)
                w_o[2], w_o[3] = (outs[4], 0), (outs[4], 1)
                mods[2] = (outs[5], 0)
            x_lat = _layer_mla_lat(x_lat, *mods[i], g_pre3, g_post3, *w_j, g_qa3, w_qb_b, g_kva3, *w_o[i], w_kvb_b,
                                   cos_t, sin_t, cache_mla_ckv, cache_mla_kpe, i, j)

    return (x_ctx.reshape(N_CTX_B, CTX_S, D_MODEL),
            x_lat.reshape(N_LAT_B, LAT_S, D_MODEL),
            jnp.transpose(st_dkt.reshape(N_CTX_B, n_diff, HEADS, 2, DA_DH, CTX_S), (0, 1, 5, 2, 3, 4)),
            st_dv,
            st_ckv,
            st_kpe)
```

```python
import functools
import math

import jax
import jax.numpy as jnp
from jax import lax
from jax.experimental import pallas as pl
from jax.experimental.pallas import tpu as pltpu

F32 = jnp.float32
BF16 = jnp.bfloat16

D_MODEL = 1024
DEPTH = 4
N_CTX_B = 16
CTX_S = 256
N_LAT_B = 4
LAT_S = 1024
PAST = 256
GRID_W = 64
HEADS = 8
DA_DH = 64
DA_DV = 128
MLA_Q_LORA = 384
MLA_KV_LORA = 256
MLA_NOPE = 128
MLA_ROPE = 64
MLA_DV = 128
MLA_QK_PAD = 256
ROPE_THETA = 10000.0
NORM_EPS = 1e-6
COND_ROWS = 8

LANES = 128
V7X_VMEM_MIB = 64
V_PAD = 16
TQ = 256
TM = 1024
VMEM_LIMIT = (V7X_VMEM_MIB - 8) * 1024 * 1024

_TRANS_B = (((1,), (1,)), ((), ()))
SCORES_AHEAD = 2
CTX_PER_STEP = 2
DA_Q_SCALE = DA_DH ** -0.5 * math.log2(math.e)
MLA_Q_SCALE = (MLA_NOPE + MLA_ROPE) ** -0.5 * math.log2(math.e)


def _silu(x):
    return x * (1.0 / (1.0 + jnp.exp(-x)))


def _rms(x, g):
    r = lax.rsqrt(jnp.mean(x * x, axis=-1, keepdims=True) + NORM_EPS)
    return (x * r) * g


def _rope_slab(x, cos, sin_signed):
    half = DA_DH // 2
    lane = lax.broadcasted_iota(jnp.int32, x.shape, 1)
    first_half = (lane % DA_DH) < half
    partner = jnp.where(first_half, pltpu.roll(x, LANES - half, axis=1), pltpu.roll(x, half, axis=1))
    return x * cos + partner * sin_signed


def _ones_rows(cols):
    row = lax.broadcasted_iota(jnp.int32, (V_PAD, cols), 0)
    return jnp.where(row == 0, 1.0, 0.0).astype(BF16)


def _cond_row(mods_ref, row):
    m = mods_ref[pl.ds(row, 1), :]
    return m[:, 0:D_MODEL], m[:, D_MODEL:2 * D_MODEL], m[:, 2 * D_MODEL:3 * D_MODEL]


def _modulated(x_ref, mods_ref, g_ref, row):
    shift, scale, _ = _cond_row(mods_ref, row)
    h = _rms(x_ref[...], g_ref[...]) * (1.0 + scale) + shift
    return h.astype(BF16)


def _ada_kernel(cond_ref, w_ref, b_ref, o_ref):
    a = _silu(cond_ref[...]).astype(BF16)
    o_ref[...] = jnp.dot(a, w_ref[...].astype(BF16), preferred_element_type=F32) + b_ref[...]


def _ada_call(cond, w_ada, b_ada3, n_layers):
    return pl.pallas_call(
        _ada_kernel,
        grid=(n_layers,),
        in_specs=[
            pl.BlockSpec((COND_ROWS, D_MODEL), lambda i: (0, 0)),
            pl.BlockSpec((None, D_MODEL, 3 * D_MODEL), lambda i: (i, 0, 0)),
            pl.BlockSpec((None, 1, 3 * D_MODEL), lambda i: (i, 0, 0)),
        ],
        out_specs=pl.BlockSpec((None, COND_ROWS, 3 * D_MODEL), lambda i: (i, 0, 0)),
        out_shape=jax.ShapeDtypeStruct((n_layers, COND_ROWS, 3 * D_MODEL), F32),
        compiler_params=pltpu.CompilerParams(
            dimension_semantics=("arbitrary",), vmem_limit_bytes=VMEM_LIMIT),
        name="ada_mod",
    )(cond, w_ada, b_ada3)


def _cast_block(src_ref, dst_ref):
    dst_ref[...] = src_ref[...].astype(BF16)


def _cast_job(array, blk, src_map, shape, dst_map):
    return dict(args=[array], in_specs=[pl.BlockSpec(blk, src_map)], out_spec=pl.BlockSpec(blk, dst_map),
                out_shape=jax.ShapeDtypeStruct(shape, BF16), run=_cast_block)


def _ada_job(cond, w_ada, b_ada3, layer, steps):
    tn = 3 * D_MODEL // steps
    return dict(args=[cond, w_ada, b_ada3],
                in_specs=[pl.BlockSpec((COND_ROWS, D_MODEL), lambda t: (0, 0)),
                          pl.BlockSpec((None, D_MODEL, tn), lambda t: (layer, 0, t)),
                          pl.BlockSpec((None, 1, tn), lambda t: (layer, 0, t))],
                out_spec=pl.BlockSpec((None, COND_ROWS, tn), lambda t: (0, 0, t)),
                out_shape=jax.ShapeDtypeStruct((1, COND_ROWS, 3 * D_MODEL), F32), run=_ada_kernel)


def _side_job_specs(jobs):
    in_specs = [spec for job in jobs for spec in job["in_specs"]]
    args = [a for job in jobs for a in job["args"]]
    runners = tuple((len(job["args"]), job["run"]) for job in jobs)
    return in_specs, [job["out_spec"] for job in jobs], [job["out_shape"] for job in jobs], args, runners


def _write_state(ref, r, j, value):
    if j is None:
        ref[r] = value
    else:
        for jj in range(ref.shape[1]):
            ref[r, jj] = value if jj == j else jnp.zeros_like(value)


def _state_plumbing(states, shapes, j, n_inputs, n_outputs):
    def index_map(shp, layer_index):
        return lambda t: (t, layer_index) + (0,) * (len(shp) - 2)

    if states is None:
        specs = [pl.BlockSpec((CTX_PER_STEP,) + shp[1:], index_map(shp, 0)) for shp in shapes]
        return [], specs, {}, ()
    specs = [pl.BlockSpec((CTX_PER_STEP, None) + shp[2:], index_map(shp, j)) for shp in shapes]
    in_specs = [pl.BlockSpec(memory_space=pl.ANY) for _ in shapes]
    aliases = {n_inputs + i: n_outputs + i for i in range(len(shapes))}
    return in_specs, specs, aliases, tuple(states)


def _finish(heads, w_out_ref, x_ref, mods_ref, g_post_ref, row, o_ref, rows=slice(None)):
    _, _, gate = _cond_row(mods_ref, row)
    og = jnp.concatenate(heads, axis=1)
    y = jnp.dot(og, w_out_ref[...], preferred_element_type=F32)
    o_ref[rows, :] = x_ref[rows, :] + gate * _rms(y, g_post_ref[...])


def _diff_lambda(lam_ref, lam_init):
    p = lam_ref[...]
    a = jnp.sum(p[0:1, :] * p[1:2, :], axis=-1, keepdims=True)
    b = jnp.sum(p[2:3, :] * p[3:4, :], axis=-1, keepdims=True)
    return jnp.exp(a) - jnp.exp(b) + lam_init


def _stack_diff_keys(k, k2_ref, r0, sk):
    rows = k.shape[0]
    lane = lax.broadcasted_iota(jnp.int32, (rows, LANES), 1)
    zero = jnp.zeros((rows, LANES), BF16)
    for h in range(HEADS):
        kh = k[:, h * LANES:(h + 1) * LANES]
        k2_ref[h, r0:r0 + rows, :] = jnp.where(lane < DA_DH, kh, zero)
        k2_ref[h, sk + r0:sk + r0 + rows, :] = jnp.where(lane >= DA_DH, kh, zero)


def _stack_values_t(vt, vt_ref, c0, cols):
    ones = _ones_rows(cols)
    for h in range(HEADS):
        vt_ref[h, 0:DA_DV, c0:c0 + cols] = vt[h * DA_DV:(h + 1) * DA_DV, :]
        vt_ref[h, DA_DV:DA_DV + V_PAD, c0:c0 + cols] = ones


def _diff_heads(q_ref, k2_ref, vt_ref, gate_ref, g_sub_ref, lam, lam_init, sk, rows=slice(None), after_issue=None,
                split_scores=False):
    g_sub = g_sub_ref[...]

    def scores(h):
        qh = q_ref[rows, h * LANES:(h + 1) * LANES]
        if split_scores:
            return tuple(lax.dot_general(k2_ref[h, c * sk:(c + 1) * sk, :], qh, _TRANS_B,
                                         preferred_element_type=F32) for c in range(2))
        s = lax.dot_general(k2_ref[h], qh, _TRANS_B, preferred_element_type=F32)
        return s[0:sk, :], s[sk:2 * sk, :]

    heads = []
    pending = [scores(h) for h in range(SCORES_AHEAD)]
    if after_issue is not None:
        after_issue()
    for h in range(HEADS):
        lo, hi = h * LANES, (h + 1) * LANES
        s = pending.pop(0)
        if h + SCORES_AHEAD < HEADS:
            pending.append(scores(h + SCORES_AHEAD))
        s0, s1 = s
        e0 = jnp.exp2(s0 - jnp.max(s0, axis=0, keepdims=True)).astype(BF16)
        e1 = jnp.exp2(s1 - jnp.max(s1, axis=0, keepdims=True)).astype(BF16)
        pv0 = jnp.dot(vt_ref[h], e0, preferred_element_type=F32)
        pv1 = jnp.dot(vt_ref[h], e1, preferred_element_type=F32)
        a0 = 1.0 / pv0[DA_DV:DA_DV + 1, :]
        a1 = lam / pv1[DA_DV:DA_DV + 1, :]
        ot = pv0[0:DA_DV, :] * a0 - pv1[0:DA_DV, :] * a1
        ot = ot * lax.rsqrt(jnp.mean(ot * ot, axis=0, keepdims=True) + NORM_EPS)
        o = (ot.T * g_sub) * (1.0 - lam_init)
        heads.append((o * gate_ref[rows, lo:hi]).astype(BF16))
    return heads


MLA_IN = MLA_Q_LORA + MLA_KV_LORA + MLA_ROPE + D_MODEL
_QA = slice(0, MLA_Q_LORA)
_KV = slice(MLA_Q_LORA, MLA_Q_LORA + MLA_KV_LORA)
_PE = slice(_KV.stop, _KV.stop + LANES)
_GT = slice(_KV.stop + MLA_ROPE, MLA_IN)


def _proj_t(hb, w_t_ref, rows):
    return lax.dot_general(hb, w_t_ref[rows, :], _TRANS_B, preferred_element_type=F32)


def _mla_queries(hb, w_in_ref, g_qa_ref):
    q_a = _proj_t(hb, w_in_ref, _QA)
    return _rms(q_a, g_qa_ref[...]).astype(BF16)


def _mla_expand(ckv, kpe, w_kvb_ref, kf_ref, vf_ref, r0, rows):
    zero = jnp.zeros((rows, LANES - MLA_ROPE), BF16)
    for h in range(HEADS):
        lo = h * (MLA_NOPE + MLA_DV)
        kv = jnp.dot(ckv, w_kvb_ref[:, lo:lo + MLA_NOPE + MLA_DV], preferred_element_type=F32)
        kf_ref[h, r0:r0 + rows, 0:MLA_NOPE] = kv[:, 0:MLA_NOPE].astype(BF16)
        kf_ref[h, r0:r0 + rows, MLA_NOPE:MLA_NOPE + MLA_ROPE] = kpe
        kf_ref[h, r0:r0 + rows, MLA_NOPE + MLA_ROPE:MLA_QK_PAD] = zero
        vf_ref[h, r0:r0 + rows, :] = kv[:, MLA_NOPE:MLA_NOPE + MLA_DV].astype(BF16)


def _mla_heads(q_ref, kf_ref, vf_ref, gate_ref, rows=slice(None), after_issue=None):
    def scores(h):
        return lax.dot_general(q_ref[rows, h * MLA_QK_PAD:(h + 1) * MLA_QK_PAD], kf_ref[h], _TRANS_B,
                               preferred_element_type=F32)

    heads = []
    pending = [scores(h) for h in range(SCORES_AHEAD)]
    if after_issue is not None:
        after_issue()
    for h in range(HEADS):
        s = pending.pop(0)
        if h + SCORES_AHEAD < HEADS:
            pending.append(scores(h + SCORES_AHEAD))
        e = jnp.exp2(s - jnp.max(s, axis=-1, keepdims=True))
        inv = 1.0 / jnp.sum(e, axis=-1, keepdims=True)
        o = jnp.dot(e.astype(BF16), vf_ref[h], preferred_element_type=F32) * inv
        lo, hi = h * LANES, (h + 1) * LANES
        heads.append((o * gate_ref[rows, lo:hi]).astype(BF16))
    return heads


def _split_layer_refs(refs, runners):
    scratch = refs[-2:]
    refs = refs[:-2]
    n_jobs = len(runners)
    job_out = refs[len(refs) - n_jobs:]
    main = refs[len(refs) - n_jobs - 3:len(refs) - n_jobs]
    pos = len(refs) - n_jobs - 3 - sum(n for n, _ in runners)
    for (n, run), out in zip(runners, job_out):
        run(*refs[pos:pos + n], out)
        pos += n
    return main, scratch


def _layer_diff_ctx_kernel(create_j, lam_init, runners, x_ref, mods_ref, g_pre_ref, w_in_ref, g_post_ref,
                           w_out_ref, g_sub_ref, lam_ref, *refs):
    (o_ref, skt_ref, sv_ref), (k2_ref, vb_ref) = _split_layer_refs(refs, runners)
    hb = _modulated(x_ref, mods_ref, g_pre_ref, 0)
    n = D_MODEL
    q = (jnp.dot(hb, w_in_ref[:, 0:n], preferred_element_type=F32) * DA_Q_SCALE).astype(BF16)
    k = jnp.dot(hb, w_in_ref[:, n:2 * n], preferred_element_type=F32)
    v = jnp.dot(hb, w_in_ref[:, 2 * n:3 * n], preferred_element_type=F32)
    gate = _silu(jnp.dot(hb, w_in_ref[:, 3 * n:4 * n], preferred_element_type=F32)).astype(BF16)
    kb = k.astype(BF16)
    lam = _diff_lambda(lam_ref, lam_init)
    finish = None
    for r in range(CTX_PER_STEP):
        rows = slice(r * CTX_S, (r + 1) * CTX_S)
        vt = v[rows, :].T
        _write_state(skt_ref, r, create_j, k[rows, :].T)
        _write_state(sv_ref, r, create_j, v[rows, :].reshape(CTX_S, HEADS, DA_DV))
        _stack_diff_keys(kb[rows, :], k2_ref.at[r], 0, CTX_S)
        _stack_values_t(vt.astype(BF16), vb_ref.at[r], 0, CTX_S)
        heads = _diff_heads(q, k2_ref.at[r], vb_ref.at[r], gate, g_sub_ref, lam, lam_init, CTX_S, rows, finish,
                            split_scores=True)
        finish = functools.partial(_finish, heads, w_out_ref, x_ref, mods_ref, g_post_ref, 0, o_ref, rows)
    finish()


def _layer_diff_ctx(x, mods, mi, g_pre, g_post, w_in, wj, w_out, wo, g_sub, lam_p, states, layer, j, n_layers,
                    side_jobs=()):
    lam_init = 0.8 - 0.6 * math.exp(-0.3 * layer)
    shapes = [(N_CTX_B, n_layers, D_MODEL, CTX_S), (N_CTX_B, n_layers, CTX_S, HEADS, DA_DV)]
    st_in_specs, st_out_specs, aliases, st_args = _state_plumbing(states, shapes, j, 8, 1)
    cj_in, cj_out, cj_shapes, cj_args, runners = _side_job_specs(side_jobs)
    tok_spec = pl.BlockSpec((CTX_PER_STEP * CTX_S, D_MODEL), lambda t: (t, 0))
    return pl.pallas_call(
        functools.partial(_layer_diff_ctx_kernel, j if states is None else None, lam_init, runners),
        grid=(N_CTX_B // CTX_PER_STEP,),
        in_specs=[
            tok_spec,
            pl.BlockSpec((None, COND_ROWS, 3 * D_MODEL), lambda t: (mi, 0, 0)),
            pl.BlockSpec((None, 1, D_MODEL), lambda t: (layer, 0, 0)),
            pl.BlockSpec((None, D_MODEL, 4 * D_MODEL), lambda t: (wj, 0, 0)),
            pl.BlockSpec((None, 1, D_MODEL), lambda t: (layer, 0, 0)),
            pl.BlockSpec((None, D_MODEL, D_MODEL), lambda t: (wo, 0, 0)),
            pl.BlockSpec((None, 1, DA_DV), lambda t: (j, 0, 0)),
            pl.BlockSpec((None, 4, DA_DH), lambda t: (j, 0, 0)),
        ] + st_in_specs + cj_in,
        out_specs=[tok_spec] + st_out_specs + cj_out,
        out_shape=[jax.ShapeDtypeStruct(x.shape, F32)] + [jax.ShapeDtypeStruct(shp, F32) for shp in shapes]
        + cj_shapes,
        input_output_aliases=aliases,
        scratch_shapes=[
            pltpu.VMEM((CTX_PER_STEP, HEADS, 2 * CTX_S, LANES), BF16),
            pltpu.VMEM((CTX_PER_STEP, HEADS, DA_DV + V_PAD, CTX_S), BF16),
        ],
        compiler_params=pltpu.CompilerParams(
            dimension_semantics=("arbitrary",), vmem_limit_bytes=VMEM_LIMIT),
        name="layer_diff_ctx",
    )(x, mods, g_pre, w_in, g_post, w_out, g_sub, lam_p, *st_args, *cj_args)


def _layer_mla_ctx_kernel(create_j, runners, x_ref, mods_ref, g_pre_ref, w_in_ref, g_qa_ref, w_qb_ref, g_kva_ref,
                          g_post_ref, w_out_ref, w_kvb_ref, *refs):
    (o_ref, ckv_ref, kpe_ref), (kf_ref, vf_ref) = _split_layer_refs(refs, runners)
    hb = _modulated(x_ref, mods_ref, g_pre_ref, 0)
    qn = _mla_queries(hb, w_in_ref, g_qa_ref)
    q = (jnp.dot(qn, w_qb_ref[...], preferred_element_type=F32) * MLA_Q_SCALE).astype(BF16)
    ckv = _rms(_proj_t(hb, w_in_ref, _KV), g_kva_ref[...])
    gate = _silu(_proj_t(hb, w_in_ref, _GT)).astype(BF16)
    pe = _proj_t(hb, w_in_ref, _PE)
    finish = None
    for r in range(CTX_PER_STEP):
        rows = slice(r * CTX_S, (r + 1) * CTX_S)
        kpe = pe[rows, 0:MLA_ROPE]
        _write_state(ckv_ref, r, create_j, ckv[rows, :])
        _write_state(kpe_ref, r, create_j, kpe)
        _mla_expand(ckv[rows, :].astype(BF16), kpe.astype(BF16), w_kvb_ref, kf_ref.at[r], vf_ref.at[r],
                    0, CTX_S)
        heads = _mla_heads(q, kf_ref.at[r], vf_ref.at[r], gate, rows, finish)
        finish = functools.partial(_finish, heads, w_out_ref, x_ref, mods_ref, g_post_ref, 0, o_ref, rows)
    finish()


def _layer_mla_ctx(x, mods, mi, g_pre, g_post, w_in, wj, g_qa, w_qb, g_kva, w_out, wo, w_kvb, states, layer, j,
                   n_layers, side_jobs=()):
    shapes = [(N_CTX_B, n_layers, CTX_S, MLA_KV_LORA), (N_CTX_B, n_layers, CTX_S, MLA_ROPE)]
    st_in_specs, st_out_specs, aliases, st_args = _state_plumbing(states, shapes, j, 10, 1)
    cj_in, cj_out, cj_shapes, cj_args, runners = _side_job_specs(side_jobs)
    tok_spec = pl.BlockSpec((CTX_PER_STEP * CTX_S, D_MODEL), lambda t: (t, 0))
    return pl.pallas_call(
        functools.partial(_layer_mla_ctx_kernel, j if states is None else None, runners),
        grid=(N_CTX_B // CTX_PER_STEP,),
        in_specs=[
            tok_spec,
            pl.BlockSpec((None, COND_ROWS, 3 * D_MODEL), lambda t: (mi, 0, 0)),
            pl.BlockSpec((None, 1, D_MODEL), lambda t: (layer, 0, 0)),
            pl.BlockSpec((None, MLA_IN, D_MODEL), lambda t: (wj, 0, 0)),
            pl.BlockSpec((None, 1, MLA_Q_LORA), lambda t: (j, 0, 0)),
            pl.BlockSpec((None, MLA_Q_LORA, HEADS * MLA_QK_PAD), lambda t: (j, 0, 0)),
            pl.BlockSpec((None, 1, MLA_KV_LORA), lambda t: (j, 0, 0)),
            pl.BlockSpec((None, 1, D_MODEL), lambda t: (layer, 0, 0)),
            pl.BlockSpec((None, D_MODEL, D_MODEL), lambda t: (wo, 0, 0)),
            pl.BlockSpec((None, MLA_KV_LORA, HEADS * (MLA_NOPE + MLA_DV)), lambda t: (j, 0, 0)),
        ] + st_in_specs + cj_in,
        out_specs=[tok_spec] + st_out_specs + cj_out,
        out_shape=[jax.ShapeDtypeStruct(x.shape, F32)] + [jax.ShapeDtypeStruct(shp, F32) for shp in shapes]
        + cj_shapes,
        input_output_aliases=aliases,
        scratch_shapes=[
            pltpu.VMEM((CTX_PER_STEP, HEADS, CTX_S, MLA_QK_PAD), BF16),
            pltpu.VMEM((CTX_PER_STEP, HEADS, CTX_S, MLA_DV), BF16),
        ],
        compiler_params=pltpu.CompilerParams(
            dimension_semantics=("arbitrary",), vmem_limit_bytes=VMEM_LIMIT),
        name="layer_mla_ctx",
    )(x, mods, g_pre, w_in, g_qa, w_qb, g_kva, g_post, w_out, w_kvb, *st_args, *cj_args)


LAT_TILES = LAT_S // TM
LAT_QBLOCKS = LAT_S // TQ


def _lat_specs():
    tile = lambda s: jnp.minimum(s, LAT_TILES - 1)
    qblk = lambda s: jnp.maximum(s - LAT_TILES, 0)
    x_proj = pl.BlockSpec((TM, D_MODEL), lambda b, s: (b * LAT_TILES + tile(s), 0))
    x_attn = pl.BlockSpec((TQ, D_MODEL), lambda b, s: (b * LAT_QBLOCKS + qblk(s), 0))
    rope = pl.BlockSpec((TM, LANES), lambda b, s: (tile(s), 0))
    return x_proj, x_attn, rope


def _query_rows():
    i = pl.program_id(1) - LAT_TILES
    return pl.ds(pl.multiple_of(i * TQ, TQ), TQ)


def _layer_diff_lat_kernel(lam_init, xp_ref, xa_ref, mods_ref, g_pre_ref, w_in_ref, cos_ref, sin_ref,
                           ckt_ref, cv_ref, g_post_ref, w_out_ref, g_sub_ref, lam_ref,
                           o_ref, k2_ref, vb_ref, q_s, gate_s):
    step = pl.program_id(1)
    row = 1 + pl.program_id(0)
    sk = LAT_S + PAST

    for t in range(LAT_TILES):
        @pl.when(step == t)
        def _(t=t):
            r0 = t * TM
            hb = _modulated(xp_ref, mods_ref, g_pre_ref, row)
            cos = cos_ref[...]
            sin = sin_ref[...]
            n = D_MODEL
            q = jnp.dot(hb, w_in_ref[:, 0:n], preferred_element_type=F32)
            k = jnp.dot(hb, w_in_ref[:, n:2 * n], preferred_element_type=F32)
            cos_q = cos * DA_Q_SCALE
            sin_q = sin * DA_Q_SCALE
            slabs = [slice(h * LANES, (h + 1) * LANES) for h in range(HEADS)]
            q_s[r0:r0 + TM, :] = jnp.concatenate(
                [_rope_slab(q[:, sl], cos_q, sin_q).astype(BF16) for sl in slabs], axis=1)
            kb = jnp.concatenate([_rope_slab(k[:, sl], cos, sin).astype(BF16) for sl in slabs], axis=1)
            _stack_diff_keys(kb, k2_ref, r0, sk)
            vt = jnp.dot(hb, w_in_ref[:, 2 * n:3 * n], preferred_element_type=F32).T.astype(BF16)
            _stack_values_t(vt, vb_ref, r0, TM)
            gate_s[r0:r0 + TM, :] = _silu(
                jnp.dot(hb, w_in_ref[:, 3 * n:4 * n], preferred_element_type=F32)).astype(BF16)

    @pl.when(step == 0)
    def _():
        lane = lax.broadcasted_iota(jnp.int32, (PAST, LANES), 1)
        zero = jnp.zeros((PAST, LANES), BF16)
        ones = _ones_rows(PAST)
        for h in range(HEADS):
            ch = ckt_ref[h * LANES:(h + 1) * LANES, :].T.astype(BF16)
            k2_ref[h, LAT_S:sk, :] = jnp.where(lane < DA_DH, ch, zero)
            k2_ref[h, sk + LAT_S:2 * sk, :] = jnp.where(lane >= DA_DH, ch, zero)
            vb_ref[h, 0:DA_DV, LAT_S:sk] = cv_ref[:, h, :].T.astype(BF16)
            vb_ref[h, DA_DV:DA_DV + V_PAD, LAT_S:sk] = ones

    @pl.when(step >= LAT_TILES)
    def _():
        lam = _diff_lambda(lam_ref, lam_init)
        heads = _diff_heads(q_s, k2_ref, vb_ref, gate_s, g_sub_ref, lam, lam_init, sk, _query_rows())
        _finish(heads, w_out_ref, xa_ref, mods_ref, g_post_ref, row, o_ref)


def _layer_diff_lat(x, mods, mi, g_pre, g_post, w_in, wj, w_out, wo, cos_t, sin_t, cache_kt, cache_v, g_sub,
                    lam_p, layer, j):
    lam_init = 0.8 - 0.6 * math.exp(-0.3 * layer)
    sk = LAT_S + PAST
    x_proj, x_attn, rope = _lat_specs()
    return pl.pallas_call(
        functools.partial(_layer_diff_lat_kernel, lam_init),
        grid=(N_LAT_B, LAT_TILES + LAT_QBLOCKS),
        in_specs=[
            x_proj, x_attn,
            pl.BlockSpec((None, COND_ROWS, 3 * D_MODEL), lambda b, s: (mi, 0, 0)),
            pl.BlockSpec((None, 1, D_MODEL), lambda b, s: (layer, 0, 0)),
            pl.BlockSpec((None, D_MODEL, 4 * D_MODEL), lambda b, s: (wj, 0, 0)),
            rope, rope,
            pl.BlockSpec((None, None, D_MODEL, PAST), lambda b, s: (b, j, 0, 0)),
            pl.BlockSpec((None, None, PAST, HEADS, DA_DV), lambda b, s: (b, j, 0, 0, 0)),
            pl.BlockSpec((None, 1, D_MODEL), lambda b, s: (layer, 0, 0)),
            pl.BlockSpec((None, D_MODEL, D_MODEL), lambda b, s: (wo, 0, 0)),
            pl.BlockSpec((None, 1, DA_DV), lambda b, s: (j, 0, 0)),
            pl.BlockSpec((None, 4, DA_DH), lambda b, s: (j, 0, 0)),
        ],
        out_specs=x_attn,
        out_shape=jax.ShapeDtypeStruct(x.shape, F32),
        scratch_shapes=[
            pltpu.VMEM((HEADS, 2 * sk, LANES), BF16),
            pltpu.VMEM((HEADS, DA_DV + V_PAD, sk), BF16),
            pltpu.VMEM((LAT_S, D_MODEL), BF16),
            pltpu.VMEM((LAT_S, D_MODEL), BF16),
        ],
        compiler_params=pltpu.CompilerParams(
            dimension_semantics=("arbitrary", "arbitrary"), vmem_limit_bytes=VMEM_LIMIT),
        name="layer_diff_lat",
    )(x, x, mods, g_pre, w_in, cos_t, sin_t, cache_kt, cache_v, g_post, w_out, g_sub, lam_p)


def _layer_mla_lat_kernel(xp_ref, xa_ref, mods_ref, g_pre_ref, w_in_ref, g_qa_ref, w_qb_ref, g_kva_ref,
                          cos_ref, sin_ref, cckv_ref, ckpe_ref, g_post_ref, w_out_ref, w_kvb_ref,
                          o_ref, kf_ref, vf_ref, q_s, gate_s):
    step = pl.program_id(1)
    row = 1 + pl.program_id(0)

    for t in range(LAT_TILES):
        @pl.when(step == t)
        def _(t=t):
            r0 = t * TM
            hb = _modulated(xp_ref, mods_ref, g_pre_ref, row)
            cos = cos_ref[...]
            sin = sin_ref[...]
            qn = _mla_queries(hb, w_in_ref, g_qa_ref)
            for h in range(HEADS):
                lo = h * MLA_QK_PAD
                q = jnp.dot(qn, w_qb_ref[:, lo:lo + MLA_QK_PAD], preferred_element_type=F32) * MLA_Q_SCALE
                q_s[r0:r0 + TM, lo:lo + LANES] = q[:, 0:LANES].astype(BF16)
                q_s[r0:r0 + TM, lo + LANES:lo + 2 * LANES] = _rope_slab(
                    q[:, LANES:2 * LANES], cos, sin).astype(BF16)
            ckv = _rms(_proj_t(hb, w_in_ref, _KV), g_kva_ref[...]).astype(BF16)
            gate_s[r0:r0 + TM, :] = _silu(_proj_t(hb, w_in_ref, _GT)).astype(BF16)
            kpe = _rope_slab(_proj_t(hb, w_in_ref, _PE), cos, sin)[:, 0:MLA_ROPE].astype(BF16)
            _mla_expand(ckv, kpe, w_kvb_ref, kf_ref, vf_ref, r0, TM)

    @pl.when(step == 0)
    def _():
        _mla_expand(cckv_ref[...].astype(BF16), ckpe_ref[...].astype(BF16), w_kvb_ref, kf_ref, vf_ref,
                    LAT_S, PAST)

    @pl.when(step >= LAT_TILES)
    def _():
        heads = _mla_heads(q_s, kf_ref, vf_ref, gate_s, _query_rows())
        _finish(heads, w_out_ref, xa_ref, mods_ref, g_post_ref, row, o_ref)


def _layer_mla_lat(x, mods, mi, g_pre, g_post, w_in, wj, g_qa, w_qb, g_kva, w_out, wo, w_kvb, cos_t, sin_t,
                   cache_ckv, cache_kpe, layer, j):
    sk = LAT_S + PAST
    x_proj, x_attn, rope = _lat_specs()
    return pl.pallas_call(
        _layer_mla_lat_kernel,
        grid=(N_LAT_B, LAT_TILES + LAT_QBLOCKS),
        in_specs=[
            x_proj, x_attn,
            pl.BlockSpec((None, COND_ROWS, 3 * D_MODEL), lambda b, s: (mi, 0, 0)),
            pl.BlockSpec((None, 1, D_MODEL), lambda b, s: (layer, 0, 0)),
            pl.BlockSpec((None, MLA_IN, D_MODEL), lambda b, s: (wj, 0, 0)),
            pl.BlockSpec((None, 1, MLA_Q_LORA), lambda b, s: (j, 0, 0)),
            pl.BlockSpec((None, MLA_Q_LORA, HEADS * MLA_QK_PAD), lambda b, s: (j, 0, 0)),
            pl.BlockSpec((None, 1, MLA_KV_LORA), lambda b, s: (j, 0, 0)),
            rope, rope,
            pl.BlockSpec((None, None, PAST, MLA_KV_LORA), lambda b, s: (b, j, 0, 0)),
            pl.BlockSpec((None, None, PAST, MLA_ROPE), lambda b, s: (b, j, 0, 0)),
            pl.BlockSpec((None, 1, D_MODEL), lambda b, s: (layer, 0, 0)),
            pl.BlockSpec((None, D_MODEL, D_MODEL), lambda b, s: (wo, 0, 0)),
            pl.BlockSpec((None, MLA_KV_LORA, HEADS * (MLA_NOPE + MLA_DV)), lambda b, s: (j, 0, 0)),
        ],
        out_specs=x_attn,
        out_shape=jax.ShapeDtypeStruct(x.shape, F32),
        scratch_shapes=[
            pltpu.VMEM((HEADS, sk, MLA_QK_PAD), BF16),
            pltpu.VMEM((HEADS, sk, MLA_DV), BF16),
            pltpu.VMEM((LAT_S, HEADS * MLA_QK_PAD), BF16),
            pltpu.VMEM((LAT_S, D_MODEL), BF16),
        ],
        compiler_params=pltpu.CompilerParams(
            dimension_semantics=("arbitrary", "arbitrary"), vmem_limit_bytes=VMEM_LIMIT),
        name="layer_mla_lat",
    )(x, x, mods, g_pre, w_in, g_qa, w_qb, g_kva, cos_t, sin_t, cache_ckv, cache_kpe, g_post, w_out, w_kvb)


def _rope_tables():
    rows = LAT_S // GRID_W
    row = jnp.repeat(jnp.arange(rows), GRID_W).astype(F32)
    col = jnp.tile(jnp.arange(GRID_W), rows).astype(F32)
    n_freq = DA_DH // 4
    inv = ROPE_THETA ** (-jnp.arange(n_freq, dtype=F32) / n_freq)
    ang = jnp.concatenate([row[:, None] * inv, col[:, None] * inv], axis=-1)
    cos, sin = jnp.cos(ang), jnp.sin(ang)
    return (jnp.concatenate([cos, cos, cos, cos], axis=-1),
            jnp.concatenate([-sin, sin, -sin, sin], axis=-1))


def kernel(x_prompt, x_sample, cache_diff_k, cache_diff_v, cache_mla_ckv, cache_mla_kpe,
           c, c_ctx, w_ada, b_ada, g_pre, g_post, w_out,
           da_w_in, da_lam_q1, da_lam_k1, da_lam_q2, da_lam_k2, da_g_sub,
           mla_w_in, mla_g_qa, mla_w_qb, mla_g_kva, mla_w_kvb):
    assert DA_DH == MLA_ROPE
    assert x_prompt.shape == (N_CTX_B, CTX_S, D_MODEL) and x_sample.shape == (N_LAT_B, LAT_S, D_MODEL)
    assert cache_diff_k.shape == (N_LAT_B, (DEPTH + 1) // 2, PAST, HEADS, 2, DA_DH)
    assert cache_mla_ckv.shape == (N_LAT_B, DEPTH // 2, PAST, MLA_KV_LORA)
    assert w_ada.shape == (DEPTH, D_MODEL, 3 * D_MODEL) and da_w_in.shape[1:] == (D_MODEL, 4 * D_MODEL)
    n_diff = (DEPTH + 1) // 2
    n_mla = DEPTH // 2

    cond = jnp.concatenate(
        [c_ctx[None, :], c, jnp.zeros((COND_ROWS - 1 - N_LAT_B, D_MODEL), F32)], axis=0)
    b_ada3 = b_ada.reshape(DEPTH, 1, 3 * D_MODEL)
    mods_first = _ada_call(cond, w_ada, b_ada3, 2)

    cos_t, sin_t = _rope_tables()
    g_pre3 = g_pre.reshape(DEPTH, 1, D_MODEL)
    g_post3 = g_post.reshape(DEPTH, 1, D_MODEL)
    steps = N_CTX_B // CTX_PER_STEP
    w_out_first = w_out[0:1].astype(BF16)
    da_w_in_first = da_w_in[0:1].astype(BF16)
    g_sub3 = da_g_sub.reshape(n_diff, 1, DA_DV)
    lam_p = jnp.stack([da_lam_q1, da_lam_k1, da_lam_q2, da_lam_k2], axis=1)
    cache_dkt = jnp.transpose(cache_diff_k, (0, 1, 3, 4, 5, 2)).reshape(N_LAT_B, n_diff, D_MODEL, PAST)

    mla_w_in_t = jnp.swapaxes(mla_w_in, 1, 2)
    w_qb4 = mla_w_qb.reshape(n_mla, MLA_Q_LORA, HEADS, MLA_NOPE + MLA_ROPE)
    w_qb_b = jnp.pad(w_qb4, ((0, 0), (0, 0), (0, 0), (0, MLA_QK_PAD - MLA_NOPE - MLA_ROPE))).reshape(
        n_mla, MLA_Q_LORA, HEADS * MLA_QK_PAD).astype(BF16)
    g_qa3 = mla_g_qa.reshape(n_mla, 1, MLA_Q_LORA)
    g_kva3 = mla_g_kva.reshape(n_mla, 1, MLA_KV_LORA)

    x_ctx = x_prompt.reshape(N_CTX_B * CTX_S, D_MODEL)
    x_lat = x_sample.reshape(N_LAT_B * LAT_S, D_MODEL)
    st_diff = None
    st_mla = None

    assert DEPTH == 4
    da_w = [(da_w_in_first, 0), None]
    mla_w = [None, None]
    w_o = [(w_out_first, 0), None, None, None]
    wide = D_MODEL * 4 // steps
    cols = D_MODEL // steps
    per_layer = steps // 2
    kv_cols = HEADS * (MLA_NOPE + MLA_DV) // steps

    mods = [(mods_first, 0), (mods_first, 1), None, None]

    def mla_in_cast(jj):
        return _cast_job(mla_w_in_t, (None, MLA_IN, cols), lambda t: (jj, 0, t),
                         (1, MLA_IN, D_MODEL), lambda t: (0, 0, t))

    side_jobs = {
        0: [_cast_job(da_w_in, (None, D_MODEL, wide), lambda t: (1, 0, t),
                      (1, D_MODEL, 4 * D_MODEL), lambda t: (0, 0, t)),
            mla_in_cast(0),
            _cast_job(w_out, (None, D_MODEL // steps, D_MODEL), lambda t: (1, t, 0),
                      (1, D_MODEL, D_MODEL), lambda t: (0, t, 0)),
            _cast_job(mla_w_kvb, (n_mla, MLA_KV_LORA, kv_cols), lambda t: (0, 0, t),
                      mla_w_kvb.shape, lambda t: (0, 0, t))],
        1: [mla_in_cast(1),
            _cast_job(w_out, (None, D_MODEL // per_layer, D_MODEL),
                      lambda t: (2 + t // per_layer, t % per_layer, 0),
                      (2, D_MODEL, D_MODEL), lambda t: (t // per_layer, t % per_layer, 0)),
            _ada_job(cond, w_ada, b_ada3, 2, steps)],
        2: [_ada_job(cond, w_ada, b_ada3, 3, steps)],
        3: [],
    }

    for i in range(DEPTH):
        j = i // 2
        if i % 2 == 0:
            w_j = da_w[j]
            outs = _layer_diff_ctx(x_ctx, *mods[i], g_pre3, g_post3, *w_j, *w_o[i], g_sub3, lam_p, st_diff,
                                   i, j, n_diff, side_jobs=side_jobs[i])
            x_ctx, st_dkt, st_dv = outs[0:3]
            st_diff = (st_dkt, st_dv)
            if i == 0:
                da_w[1], mla_w[0], w_o[1], w_kvb_b = (outs[3], 0), (outs[4], 0), (outs[5], 0), outs[6]
            else:
                mods[3] = (outs[3], 0)
            x_lat = _layer_diff_lat(x_lat, *mods[i], g_pre3, g_post3, *w_j, *w_o[i], cos_t, sin_t,
                                    cache_dkt, cache_diff_v, g_sub3, lam_p, i, j)
        else:
            w_j = mla_w[j]
            outs = _layer_mla_ctx(x_ctx, *mods[i], g_pre3, g_post3, *w_j, g_qa3, w_qb_b, g_kva3, *w_o[i], w_kvb_b,
                                  st_mla, i, j, n_mla, side_jobs=side_jobs[i])
            x_ctx, st_ckv, st_kpe = outs[0:3]
            st_mla = (st_ckv, st_kpe)
            if i == 1:
                mla_w[1] = (outs[3], 0)
                w_o[2], w_o[3] = (outs[4], 0), (outs[4], 1)
                mods[2] = (outs[5], 0)
            x_lat = _layer_mla_lat(x_lat, *mods[i], g_pre3, g_post3, *w_j, g_qa3, w_qb_b, g_kva3, *w_o[i], w_kvb_b,
                                   cos_t, sin_t, cache_mla_ckv, cache_mla_kpe, i, j)

    return (x_ctx.reshape(N_CTX_B, CTX_S, D_MODEL),
            x_lat.reshape(N_LAT_B, LAT_S, D_MODEL),
            jnp.transpose(st_dkt.reshape(N_CTX_B, n_diff, HEADS, 2, DA_DH, CTX_S), (0, 1, 5, 2, 3, 4)),
            st_dv,
            st_ckv,
            st_kpe)
```

```python
import functools
import math

import jax
import jax.numpy as jnp
import numpy as np
from jax import lax
from jax.experimental import pallas as pl
from jax.experimental.pallas import tpu as pltpu

F32 = jnp.float32
BF16 = jnp.bfloat16

D_MODEL = 1024
DEPTH = 4
N_CTX_B = 16
CTX_S = 256
N_LAT_B = 4
LAT_S = 1024
PAST = 256
GRID_W = 64
HEADS = 8
DA_DH = 64
DA_DV = 128
MLA_Q_LORA = 384
MLA_KV_LORA = 256
MLA_NOPE = 128
MLA_ROPE = 64
MLA_DV = 128
MLA_QK_PAD = 256
ROPE_THETA = 10000.0
NORM_EPS = 1e-6
COND_ROWS = 8

LANES = 128
V7X_VMEM_MIB = 64
V_PAD = 16
TQ = 256
TM = 1024
VMEM_LIMIT = (V7X_VMEM_MIB - 8) * 1024 * 1024

_TRANS_B = (((1,), (1,)), ((), ()))
SCORES_AHEAD = 2
CTX_PER_STEP = 2
DA_Q_SCALE = DA_DH ** -0.5 * math.log2(math.e)
MLA_Q_SCALE = (MLA_NOPE + MLA_ROPE) ** -0.5 * math.log2(math.e)


def _silu(x):
    return x * (1.0 / (1.0 + jnp.exp(-x)))


def _rms(x, g):
    r = lax.rsqrt(jnp.mean(x * x, axis=-1, keepdims=True) + NORM_EPS)
    return (x * r) * g


def _rope_slab(x, cos, sin_signed):
    half = DA_DH // 2
    lane = lax.broadcasted_iota(jnp.int32, x.shape, 1)
    first_half = (lane % DA_DH) < half
    partner = jnp.where(first_half, pltpu.roll(x, LANES - half, axis=1), pltpu.roll(x, half, axis=1))
    return x * cos + partner * sin_signed


def _ones_rows(cols):
    row = lax.broadcasted_iota(jnp.int32, (V_PAD, cols), 0)
    return jnp.where(row == 0, 1.0, 0.0).astype(BF16)


def _cond_row(mods_ref, row):
    m = mods_ref[pl.ds(row, 1), :]
    return m[:, 0:D_MODEL], m[:, D_MODEL:2 * D_MODEL], m[:, 2 * D_MODEL:3 * D_MODEL]


def _modulated(x_ref, mods_ref, g_ref, row):
    shift, scale, _ = _cond_row(mods_ref, row)
    h = _rms(x_ref[...], g_ref[...]) * (1.0 + scale) + shift
    return h.astype(BF16)


def _ada_kernel(cond_ref, w_ref, b_ref, o_ref):
    a = _silu(cond_ref[...]).astype(BF16)
    o_ref[...] = jnp.dot(a, w_ref[...].astype(BF16), preferred_element_type=F32) + b_ref[...]


def _ada_call(cond, w_ada, b_ada3, n_layers):
    return pl.pallas_call(
        _ada_kernel,
        grid=(n_layers,),
        in_specs=[
            pl.BlockSpec((COND_ROWS, D_MODEL), lambda i: (0, 0)),
            pl.BlockSpec((None, D_MODEL, 3 * D_MODEL), lambda i: (i, 0, 0)),
            pl.BlockSpec((None, 1, 3 * D_MODEL), lambda i: (i, 0, 0)),
        ],
        out_specs=pl.BlockSpec((None, COND_ROWS, 3 * D_MODEL), lambda i: (i, 0, 0)),
        out_shape=jax.ShapeDtypeStruct((n_layers, COND_ROWS, 3 * D_MODEL), F32),
        compiler_params=pltpu.CompilerParams(
            dimension_semantics=("arbitrary",), vmem_limit_bytes=VMEM_LIMIT),
        name="ada_mod",
    )(cond, w_ada, b_ada3)


def _cast_block(src_ref, dst_ref):
    dst_ref[...] = src_ref[...].astype(BF16)


def _cast_job(array, blk, src_map, shape, dst_map):
    return dict(args=[array], in_specs=[pl.BlockSpec(blk, src_map)], out_spec=pl.BlockSpec(blk, dst_map),
                out_shape=jax.ShapeDtypeStruct(shape, BF16), run=_cast_block)


def _ada_job(cond, w_ada, b_ada3, layer, steps):
    tn = 3 * D_MODEL // steps
    return dict(args=[cond, w_ada, b_ada3],
                in_specs=[pl.BlockSpec((COND_ROWS, D_MODEL), lambda t: (0, 0)),
                          pl.BlockSpec((None, D_MODEL, tn), lambda t: (layer, 0, t)),
                          pl.BlockSpec((None, 1, tn), lambda t: (layer, 0, t))],
                out_spec=pl.BlockSpec((None, COND_ROWS, tn), lambda t: (0, 0, t)),
                out_shape=jax.ShapeDtypeStruct((1, COND_ROWS, 3 * D_MODEL), F32), run=_ada_kernel)


def _side_job_specs(jobs):
    in_specs = [spec for job in jobs for spec in job["in_specs"]]
    args = [a for job in jobs for a in job["args"]]
    runners = tuple((len(job["args"]), job["run"]) for job in jobs)
    return in_specs, [job["out_spec"] for job in jobs], [job["out_shape"] for job in jobs], args, runners


def _write_state(ref, r, j, value):
    if j is None:
        ref[r] = value
    else:
        for jj in range(ref.shape[1]):
            ref[r, jj] = value if jj == j else jnp.zeros_like(value)


def _state_plumbing(states, shapes, j, n_inputs, n_outputs):
    def index_map(shp, layer_index):
        return lambda t: (t, layer_index) + (0,) * (len(shp) - 2)

    if states is None:
        specs = [pl.BlockSpec((CTX_PER_STEP,) + shp[1:], index_map(shp, 0)) for shp in shapes]
        return [], specs, {}, ()
    specs = [pl.BlockSpec((CTX_PER_STEP, None) + shp[2:], index_map(shp, j)) for shp in shapes]
    in_specs = [pl.BlockSpec(memory_space=pl.ANY) for _ in shapes]
    aliases = {n_inputs + i: n_outputs + i for i in range(len(shapes))}
    return in_specs, specs, aliases, tuple(states)


def _finish(heads, w_out_ref, x_ref, mods_ref, g_post_ref, row, o_ref, rows=slice(None)):
    _, _, gate = _cond_row(mods_ref, row)
    og = jnp.concatenate(heads, axis=1)
    y = jnp.dot(og, w_out_ref[...], preferred_element_type=F32)
    o_ref[rows, :] = x_ref[rows, :] + gate * _rms(y, g_post_ref[...])


def _diff_lambda(lam_ref, lam_init):
    p = lam_ref[...]
    a = jnp.sum(p[0:1, :] * p[1:2, :], axis=-1, keepdims=True)
    b = jnp.sum(p[2:3, :] * p[3:4, :], axis=-1, keepdims=True)
    return jnp.exp(a) - jnp.exp(b) + lam_init


def _stack_diff_keys(k, k2_ref, r0, sk):
    rows = k.shape[0]
    lane = lax.broadcasted_iota(jnp.int32, (rows, LANES), 1)
    zero = jnp.zeros((rows, LANES), BF16)
    for h in range(HEADS):
        kh = k[:, h * LANES:(h + 1) * LANES]
        k2_ref[h, r0:r0 + rows, :] = jnp.where(lane < DA_DH, kh, zero)
        k2_ref[h, sk + r0:sk + r0 + rows, :] = jnp.where(lane >= DA_DH, kh, zero)


def _stack_values_t(vt, vt_ref, c0, cols):
    ones = _ones_rows(cols)
    for h in range(HEADS):
        vt_ref[h, 0:DA_DV, c0:c0 + cols] = vt[h * DA_DV:(h + 1) * DA_DV, :]
        vt_ref[h, DA_DV:DA_DV + V_PAD, c0:c0 + cols] = ones


def _diff_heads(q_ref, k2_ref, vt_ref, gate_ref, g_sub_ref, lam, lam_init, sk, rows=slice(None), after_issue=None,
                split_scores=False):
    g_sub = g_sub_ref[...]

    def scores(h):
        qh = q_ref[rows, h * LANES:(h + 1) * LANES]
        if split_scores:
            return tuple(lax.dot_general(k2_ref[h, c * sk:(c + 1) * sk, :], qh, _TRANS_B,
                                         preferred_element_type=F32) for c in range(2))
        s = lax.dot_general(k2_ref[h], qh, _TRANS_B, preferred_element_type=F32)
        return s[0:sk, :], s[sk:2 * sk, :]

    heads = []
    pending = [scores(h) for h in range(SCORES_AHEAD)]
    if after_issue is not None:
        after_issue()
    for h in range(HEADS):
        lo, hi = h * LANES, (h + 1) * LANES
        s = pending.pop(0)
        if h + SCORES_AHEAD < HEADS:
            pending.append(scores(h + SCORES_AHEAD))
        s0, s1 = s
        e0 = jnp.exp2(s0 - jnp.max(s0, axis=0, keepdims=True)).astype(BF16)
        e1 = jnp.exp2(s1 - jnp.max(s1, axis=0, keepdims=True)).astype(BF16)
        pv0 = jnp.dot(vt_ref[h], e0, preferred_element_type=F32)
        pv1 = jnp.dot(vt_ref[h], e1, preferred_element_type=F32)
        a0 = 1.0 / pv0[DA_DV:DA_DV + 1, :]
        a1 = lam / pv1[DA_DV:DA_DV + 1, :]
        ot = pv0[0:DA_DV, :] * a0 - pv1[0:DA_DV, :] * a1
        ot = ot * lax.rsqrt(jnp.mean(ot * ot, axis=0, keepdims=True) + NORM_EPS)
        o = (ot.T * g_sub) * (1.0 - lam_init)
        heads.append((o * gate_ref[rows, lo:hi]).astype(BF16))
    return heads


MLA_IN = MLA_Q_LORA + MLA_KV_LORA + MLA_ROPE + D_MODEL
_QA = slice(0, MLA_Q_LORA)
_KV = slice(MLA_Q_LORA, MLA_Q_LORA + MLA_KV_LORA)
_PE = slice(_KV.stop, _KV.stop + LANES)
_GT = slice(_KV.stop + MLA_ROPE, MLA_IN)


def _proj_t(hb, w_t_ref, rows):
    return lax.dot_general(hb, w_t_ref[rows, :], _TRANS_B, preferred_element_type=F32)


def _mla_queries(hb, w_in_ref, g_qa_ref):
    q_a = _proj_t(hb, w_in_ref, _QA)
    return _rms(q_a, g_qa_ref[...]).astype(BF16)


def _mla_expand(ckv, kpe, w_kvb_ref, kf_ref, vf_ref, r0, rows):
    zero = jnp.zeros((rows, LANES - MLA_ROPE), BF16)
    for h in range(HEADS):
        lo = h * (MLA_NOPE + MLA_DV)
        kv = jnp.dot(ckv, w_kvb_ref[:, lo:lo + MLA_NOPE + MLA_DV], preferred_element_type=F32)
        kf_ref[h, r0:r0 + rows, 0:MLA_NOPE] = kv[:, 0:MLA_NOPE].astype(BF16)
        kf_ref[h, r0:r0 + rows, MLA_NOPE:MLA_NOPE + MLA_ROPE] = kpe
        kf_ref[h, r0:r0 + rows, MLA_NOPE + MLA_ROPE:MLA_QK_PAD] = zero
        vf_ref[h, r0:r0 + rows, :] = kv[:, MLA_NOPE:MLA_NOPE + MLA_DV].astype(BF16)


def _mla_heads(q_ref, kf_ref, vf_ref, gate_ref, rows=slice(None), after_issue=None):
    def scores(h):
        return lax.dot_general(q_ref[rows, h * MLA_QK_PAD:(h + 1) * MLA_QK_PAD], kf_ref[h], _TRANS_B,
                               preferred_element_type=F32)

    heads = []
    pending = [scores(h) for h in range(SCORES_AHEAD)]
    if after_issue is not None:
        after_issue()
    for h in range(HEADS):
        s = pending.pop(0)
        if h + SCORES_AHEAD < HEADS:
            pending.append(scores(h + SCORES_AHEAD))
        e = jnp.exp2(s - jnp.max(s, axis=-1, keepdims=True))
        inv = 1.0 / jnp.sum(e, axis=-1, keepdims=True)
        o = jnp.dot(e.astype(BF16), vf_ref[h], preferred_element_type=F32) * inv
        lo, hi = h * LANES, (h + 1) * LANES
        heads.append((o * gate_ref[rows, lo:hi]).astype(BF16))
    return heads


def _split_layer_refs(refs, runners):
    scratch = refs[-2:]
    refs = refs[:-2]
    n_jobs = len(runners)
    job_out = refs[len(refs) - n_jobs:]
    main = refs[len(refs) - n_jobs - 3:len(refs) - n_jobs]
    pos = len(refs) - n_jobs - 3 - sum(n for n, _ in runners)
    for (n, run), out in zip(runners, job_out):
        run(*refs[pos:pos + n], out)
        pos += n
    return main, scratch


def _layer_diff_ctx_kernel(create_j, lam_init, runners, x_ref, mods_ref, g_pre_ref, w_in_ref, g_post_ref,
                           w_out_ref, g_sub_ref, lam_ref, *refs):
    (o_ref, skt_ref, sv_ref), (k2_ref, vb_ref) = _split_layer_refs(refs, runners)
    hb = _modulated(x_ref, mods_ref, g_pre_ref, 0)
    n = D_MODEL
    q = (jnp.dot(hb, w_in_ref[:, 0:n], preferred_element_type=F32) * DA_Q_SCALE).astype(BF16)
    k = jnp.dot(hb, w_in_ref[:, n:2 * n], preferred_element_type=F32)
    v = jnp.dot(hb, w_in_ref[:, 2 * n:3 * n], preferred_element_type=F32)
    gate = _silu(jnp.dot(hb, w_in_ref[:, 3 * n:4 * n], preferred_element_type=F32)).astype(BF16)
    kb = k.astype(BF16)
    lam = _diff_lambda(lam_ref, lam_init)
    finish = None
    for r in range(CTX_PER_STEP):
        rows = slice(r * CTX_S, (r + 1) * CTX_S)
        vt = v[rows, :].T
        _write_state(skt_ref, r, create_j, k[rows, :].T)
        _write_state(sv_ref, r, create_j, v[rows, :].reshape(CTX_S, HEADS, DA_DV))
        _stack_diff_keys(kb[rows, :], k2_ref.at[r], 0, CTX_S)
        _stack_values_t(vt.astype(BF16), vb_ref.at[r], 0, CTX_S)
        heads = _diff_heads(q, k2_ref.at[r], vb_ref.at[r], gate, g_sub_ref, lam, lam_init, CTX_S, rows, finish,
                            split_scores=True)
        finish = functools.partial(_finish, heads, w_out_ref, x_ref, mods_ref, g_post_ref, 0, o_ref, rows)
    finish()


def _layer_diff_ctx(x, mods, mi, g_pre, g_post, w_in, wj, w_out, wo, g_sub, lam_p, states, layer, j, n_layers,
                    side_jobs=()):
    lam_init = 0.8 - 0.6 * math.exp(-0.3 * layer)
    shapes = [(N_CTX_B, n_layers, D_MODEL, CTX_S), (N_CTX_B, n_layers, CTX_S, HEADS, DA_DV)]
    st_in_specs, st_out_specs, aliases, st_args = _state_plumbing(states, shapes, j, 8, 1)
    cj_in, cj_out, cj_shapes, cj_args, runners = _side_job_specs(side_jobs)
    tok_spec = pl.BlockSpec((CTX_PER_STEP * CTX_S, D_MODEL), lambda t: (t, 0))
    return pl.pallas_call(
        functools.partial(_layer_diff_ctx_kernel, j if states is None else None, lam_init, runners),
        grid=(N_CTX_B // CTX_PER_STEP,),
        in_specs=[
            tok_spec,
            pl.BlockSpec((None, COND_ROWS, 3 * D_MODEL), lambda t: (mi, 0, 0)),
            pl.BlockSpec((None, 1, D_MODEL), lambda t: (layer, 0, 0)),
            pl.BlockSpec((None, D_MODEL, 4 * D_MODEL), lambda t: (wj, 0, 0)),
            pl.BlockSpec((None, 1, D_MODEL), lambda t: (layer, 0, 0)),
            pl.BlockSpec((None, D_MODEL, D_MODEL), lambda t: (wo, 0, 0)),
            pl.BlockSpec((None, 1, DA_DV), lambda t: (j, 0, 0)),
            pl.BlockSpec((None, 4, DA_DH), lambda t: (j, 0, 0)),
        ] + st_in_specs + cj_in,
        out_specs=[tok_spec] + st_out_specs + cj_out,
        out_shape=[jax.ShapeDtypeStruct(x.shape, F32)] + [jax.ShapeDtypeStruct(shp, F32) for shp in shapes]
        + cj_shapes,
        input_output_aliases=aliases,
        scratch_shapes=[
            pltpu.VMEM((CTX_PER_STEP, HEADS, 2 * CTX_S, LANES), BF16),
            pltpu.VMEM((CTX_PER_STEP, HEADS, DA_DV + V_PAD, CTX_S), BF16),
        ],
        compiler_params=pltpu.CompilerParams(
            dimension_semantics=("arbitrary",), vmem_limit_bytes=VMEM_LIMIT),
        name="layer_diff_ctx",
    )(x, mods, g_pre, w_in, g_post, w_out, g_sub, lam_p, *st_args, *cj_args)


def _layer_mla_ctx_kernel(create_j, runners, x_ref, mods_ref, g_pre_ref, w_in_ref, g_qa_ref, w_qb_ref, g_kva_ref,
                          g_post_ref, w_out_ref, w_kvb_ref, *refs):
    (o_ref, ckv_ref, kpe_ref), (kf_ref, vf_ref) = _split_layer_refs(refs, runners)
    hb = _modulated(x_ref, mods_ref, g_pre_ref, 0)
    qn = _mla_queries(hb, w_in_ref, g_qa_ref)
    q = (_proj_t(qn, w_qb_ref, slice(None)) * MLA_Q_SCALE).astype(BF16)
    ckv = _rms(_proj_t(hb, w_in_ref, _KV), g_kva_ref[...])
    gate = _silu(_proj_t(hb, w_in_ref, _GT)).astype(BF16)
    pe = _proj_t(hb, w_in_ref, _PE)
    finish = None
    for r in range(CTX_PER_STEP):
        rows = slice(r * CTX_S, (r + 1) * CTX_S)
        kpe = pe[rows, 0:MLA_ROPE]
        _write_state(ckv_ref, r, create_j, ckv[rows, :])
        _write_state(kpe_ref, r, create_j, kpe.T)
        _mla_expand(ckv[rows, :].astype(BF16), kpe.astype(BF16), w_kvb_ref, kf_ref.at[r], vf_ref.at[r],
                    0, CTX_S)
        heads = _mla_heads(q, kf_ref.at[r], vf_ref.at[r], gate, rows, finish)
        finish = functools.partial(_finish, heads, w_out_ref, x_ref, mods_ref, g_post_ref, 0, o_ref, rows)
    finish()


def _layer_mla_ctx(x, mods, mi, g_pre, g_post, w_in, wj, g_qa, w_qb, g_kva, w_out, wo, w_kvb, states, layer, j,
                   n_layers, side_jobs=()):
    shapes = [(N_CTX_B, n_layers, CTX_S, MLA_KV_LORA), (N_CTX_B, n_layers, MLA_ROPE, CTX_S)]
    st_in_specs, st_out_specs, aliases, st_args = _state_plumbing(states, shapes, j, 10, 1)
    cj_in, cj_out, cj_shapes, cj_args, runners = _side_job_specs(side_jobs)
    tok_spec = pl.BlockSpec((CTX_PER_STEP * CTX_S, D_MODEL), lambda t: (t, 0))
    return pl.pallas_call(
        functools.partial(_layer_mla_ctx_kernel, j if states is None else None, runners),
        grid=(N_CTX_B // CTX_PER_STEP,),
        in_specs=[
            tok_spec,
            pl.BlockSpec((None, COND_ROWS, 3 * D_MODEL), lambda t: (mi, 0, 0)),
            pl.BlockSpec((None, 1, D_MODEL), lambda t: (layer, 0, 0)),
            pl.BlockSpec((None, MLA_IN, D_MODEL), lambda t: (wj, 0, 0)),
            pl.BlockSpec((None, 1, MLA_Q_LORA), lambda t: (j, 0, 0)),
            pl.BlockSpec((None, HEADS * MLA_QK_PAD, MLA_Q_LORA), lambda t: (j, 0, 0)),
            pl.BlockSpec((None, 1, MLA_KV_LORA), lambda t: (j, 0, 0)),
            pl.BlockSpec((None, 1, D_MODEL), lambda t: (layer, 0, 0)),
            pl.BlockSpec((None, D_MODEL, D_MODEL), lambda t: (wo, 0, 0)),
            pl.BlockSpec((None, MLA_KV_LORA, HEADS * (MLA_NOPE + MLA_DV)), lambda t: (j, 0, 0)),
        ] + st_in_specs + cj_in,
        out_specs=[tok_spec] + st_out_specs + cj_out,
        out_shape=[jax.ShapeDtypeStruct(x.shape, F32)] + [jax.ShapeDtypeStruct(shp, F32) for shp in shapes]
        + cj_shapes,
        input_output_aliases=aliases,
        scratch_shapes=[
            pltpu.VMEM((CTX_PER_STEP, HEADS, CTX_S, MLA_QK_PAD), BF16),
            pltpu.VMEM((CTX_PER_STEP, HEADS, CTX_S, MLA_DV), BF16),
        ],
        compiler_params=pltpu.CompilerParams(
            dimension_semantics=("arbitrary",), vmem_limit_bytes=VMEM_LIMIT),
        name="layer_mla_ctx",
    )(x, mods, g_pre, w_in, g_qa, w_qb, g_kva, g_post, w_out, w_kvb, *st_args, *cj_args)


LAT_TILES = LAT_S // TM
LAT_QBLOCKS = LAT_S // TQ


def _lat_specs():
    tile = lambda s: jnp.minimum(s, LAT_TILES - 1)
    qblk = lambda s: jnp.maximum(s - LAT_TILES, 0)
    x_proj = pl.BlockSpec((TM, D_MODEL), lambda b, s: (b * LAT_TILES + tile(s), 0))
    x_attn = pl.BlockSpec((TQ, D_MODEL), lambda b, s: (b * LAT_QBLOCKS + qblk(s), 0))
    rope = pl.BlockSpec((TM, LANES), lambda b, s: (tile(s), 0))
    return x_proj, x_attn, rope


def _query_rows():
    i = pl.program_id(1) - LAT_TILES
    return pl.ds(pl.multiple_of(i * TQ, TQ), TQ)


def _layer_diff_lat_kernel(lam_init, xp_ref, xa_ref, mods_ref, g_pre_ref, w_in_ref, cos_ref, sin_ref,
                           ckt_ref, cv_ref, g_post_ref, w_out_ref, g_sub_ref, lam_ref,
                           o_ref, k2_ref, vb_ref, q_s, gate_s):
    step = pl.program_id(1)
    row = 1 + pl.program_id(0)
    sk = LAT_S + PAST

    for t in range(LAT_TILES):
        @pl.when(step == t)
        def _(t=t):
            r0 = t * TM
            hb = _modulated(xp_ref, mods_ref, g_pre_ref, row)
            cos = cos_ref[...]
            sin = sin_ref[...]
            n = D_MODEL
            q = jnp.dot(hb, w_in_ref[:, 0:n], preferred_element_type=F32)
            k = jnp.dot(hb, w_in_ref[:, n:2 * n], preferred_element_type=F32)
            cos_q = cos * DA_Q_SCALE
            sin_q = sin * DA_Q_SCALE
            slabs = [slice(h * LANES, (h + 1) * LANES) for h in range(HEADS)]
            q_s[r0:r0 + TM, :] = jnp.concatenate(
                [_rope_slab(q[:, sl], cos_q, sin_q).astype(BF16) for sl in slabs], axis=1)
            kb = jnp.concatenate([_rope_slab(k[:, sl], cos, sin).astype(BF16) for sl in slabs], axis=1)
            _stack_diff_keys(kb, k2_ref, r0, sk)
            vt = jnp.dot(hb, w_in_ref[:, 2 * n:3 * n], preferred_element_type=F32).T.astype(BF16)
            _stack_values_t(vt, vb_ref, r0, TM)
            gate_s[r0:r0 + TM, :] = _silu(
                jnp.dot(hb, w_in_ref[:, 3 * n:4 * n], preferred_element_type=F32)).astype(BF16)

    @pl.when(step == 0)
    def _():
        lane = lax.broadcasted_iota(jnp.int32, (PAST, LANES), 1)
        zero = jnp.zeros((PAST, LANES), BF16)
        ones = _ones_rows(PAST)
        for h in range(HEADS):
            ch = ckt_ref[h * LANES:(h + 1) * LANES, :].T.astype(BF16)
            k2_ref[h, LAT_S:sk, :] = jnp.where(lane < DA_DH, ch, zero)
            k2_ref[h, sk + LAT_S:2 * sk, :] = jnp.where(lane >= DA_DH, ch, zero)
            vb_ref[h, 0:DA_DV, LAT_S:sk] = cv_ref[:, h, :].T.astype(BF16)
            vb_ref[h, DA_DV:DA_DV + V_PAD, LAT_S:sk] = ones

    @pl.when(step >= LAT_TILES)
    def _():
        lam = _diff_lambda(lam_ref, lam_init)
        heads = _diff_heads(q_s, k2_ref, vb_ref, gate_s, g_sub_ref, lam, lam_init, sk, _query_rows())
        _finish(heads, w_out_ref, xa_ref, mods_ref, g_post_ref, row, o_ref)


def _layer_diff_lat(x, mods, mi, g_pre, g_post, w_in, wj, w_out, wo, cos_t, sin_t, cache_kt, cache_v, g_sub,
                    lam_p, layer, j):
    lam_init = 0.8 - 0.6 * math.exp(-0.3 * layer)
    sk = LAT_S + PAST
    x_proj, x_attn, rope = _lat_specs()
    return pl.pallas_call(
        functools.partial(_layer_diff_lat_kernel, lam_init),
        grid=(N_LAT_B, LAT_TILES + LAT_QBLOCKS),
        in_specs=[
            x_proj, x_attn,
            pl.BlockSpec((None, COND_ROWS, 3 * D_MODEL), lambda b, s: (mi, 0, 0)),
            pl.BlockSpec((None, 1, D_MODEL), lambda b, s: (layer, 0, 0)),
            pl.BlockSpec((None, D_MODEL, 4 * D_MODEL), lambda b, s: (wj, 0, 0)),
            rope, rope,
            pl.BlockSpec((None, None, D_MODEL, PAST), lambda b, s: (b, j, 0, 0)),
            pl.BlockSpec((None, None, PAST, HEADS, DA_DV), lambda b, s: (b, j, 0, 0, 0)),
            pl.BlockSpec((None, 1, D_MODEL), lambda b, s: (layer, 0, 0)),
            pl.BlockSpec((None, D_MODEL, D_MODEL), lambda b, s: (wo, 0, 0)),
            pl.BlockSpec((None, 1, DA_DV), lambda b, s: (j, 0, 0)),
            pl.BlockSpec((None, 4, DA_DH), lambda b, s: (j, 0, 0)),
        ],
        out_specs=x_attn,
        out_shape=jax.ShapeDtypeStruct(x.shape, F32),
        scratch_shapes=[
            pltpu.VMEM((HEADS, 2 * sk, LANES), BF16),
            pltpu.VMEM((HEADS, DA_DV + V_PAD, sk), BF16),
            pltpu.VMEM((LAT_S, D_MODEL), BF16),
            pltpu.VMEM((LAT_S, D_MODEL), BF16),
        ],
        compiler_params=pltpu.CompilerParams(
            dimension_semantics=("arbitrary", "arbitrary"), vmem_limit_bytes=VMEM_LIMIT),
        name="layer_diff_lat",
    )(x, x, mods, g_pre, w_in, cos_t, sin_t, cache_kt, cache_v, g_post, w_out, g_sub, lam_p)


def _layer_mla_lat_kernel(xp_ref, xa_ref, mods_ref, g_pre_ref, w_in_ref, g_qa_ref, w_qb_ref, g_kva_ref,
                          cos_ref, sin_ref, cckv_ref, ckpe_ref, g_post_ref, w_out_ref, w_kvb_ref,
                          o_ref, kf_ref, vf_ref, q_s, gate_s):
    step = pl.program_id(1)
    row = 1 + pl.program_id(0)

    for t in range(LAT_TILES):
        @pl.when(step == t)
        def _(t=t):
            r0 = t * TM
            hb = _modulated(xp_ref, mods_ref, g_pre_ref, row)
            cos = cos_ref[...]
            sin = sin_ref[...]
            qn = _mla_queries(hb, w_in_ref, g_qa_ref)
            for h in range(HEADS):
                lo = h * MLA_QK_PAD
                q = _proj_t(qn, w_qb_ref, slice(lo, lo + MLA_QK_PAD)) * MLA_Q_SCALE
                q_s[r0:r0 + TM, lo:lo + LANES] = q[:, 0:LANES].astype(BF16)
                q_s[r0:r0 + TM, lo + LANES:lo + 2 * LANES] = _rope_slab(
                    q[:, LANES:2 * LANES], cos, sin).astype(BF16)
            ckv = _rms(_proj_t(hb, w_in_ref, _KV), g_kva_ref[...]).astype(BF16)
            gate_s[r0:r0 + TM, :] = _silu(_proj_t(hb, w_in_ref, _GT)).astype(BF16)
            kpe = _rope_slab(_proj_t(hb, w_in_ref, _PE), cos, sin)[:, 0:MLA_ROPE].astype(BF16)
            _mla_expand(ckv, kpe, w_kvb_ref, kf_ref, vf_ref, r0, TM)

    @pl.when(step == 0)
    def _():
        _mla_expand(cckv_ref[...].astype(BF16), ckpe_ref[...].T.astype(BF16), w_kvb_ref, kf_ref, vf_ref,
                    LAT_S, PAST)

    @pl.when(step >= LAT_TILES)
    def _():
        heads = _mla_heads(q_s, kf_ref, vf_ref, gate_s, _query_rows())
        _finish(heads, w_out_ref, xa_ref, mods_ref, g_post_ref, row, o_ref)


def _layer_mla_lat(x, mods, mi, g_pre, g_post, w_in, wj, g_qa, w_qb, g_kva, w_out, wo, w_kvb, cos_t, sin_t,
                   cache_ckv, cache_kpe, layer, j):
    sk = LAT_S + PAST
    x_proj, x_attn, rope = _lat_specs()
    return pl.pallas_call(
        _layer_mla_lat_kernel,
        grid=(N_LAT_B, LAT_TILES + LAT_QBLOCKS),
        in_specs=[
            x_proj, x_attn,
            pl.BlockSpec((None, COND_ROWS, 3 * D_MODEL), lambda b, s: (mi, 0, 0)),
            pl.BlockSpec((None, 1, D_MODEL), lambda b, s: (layer, 0, 0)),
            pl.BlockSpec((None, MLA_IN, D_MODEL), lambda b, s: (wj, 0, 0)),
            pl.BlockSpec((None, 1, MLA_Q_LORA), lambda b, s: (j, 0, 0)),
            pl.BlockSpec((None, HEADS * MLA_QK_PAD, MLA_Q_LORA), lambda b, s: (j, 0, 0)),
            pl.BlockSpec((None, 1, MLA_KV_LORA), lambda b, s: (j, 0, 0)),
            rope, rope,
            pl.BlockSpec((None, None, PAST, MLA_KV_LORA), lambda b, s: (b, j, 0, 0)),
            pl.BlockSpec((None, None, MLA_ROPE, PAST), lambda b, s: (b, j, 0, 0)),
            pl.BlockSpec((None, 1, D_MODEL), lambda b, s: (layer, 0, 0)),
            pl.BlockSpec((None, D_MODEL, D_MODEL), lambda b, s: (wo, 0, 0)),
            pl.BlockSpec((None, MLA_KV_LORA, HEADS * (MLA_NOPE + MLA_DV)), lambda b, s: (j, 0, 0)),
        ],
        out_specs=x_attn,
        out_shape=jax.ShapeDtypeStruct(x.shape, F32),
        scratch_shapes=[
            pltpu.VMEM((HEADS, sk, MLA_QK_PAD), BF16),
            pltpu.VMEM((HEADS, sk, MLA_DV), BF16),
            pltpu.VMEM((LAT_S, HEADS * MLA_QK_PAD), BF16),
            pltpu.VMEM((LAT_S, D_MODEL), BF16),
        ],
        compiler_params=pltpu.CompilerParams(
            dimension_semantics=("arbitrary", "arbitrary"), vmem_limit_bytes=VMEM_LIMIT),
        name="layer_mla_lat",
    )(x, x, mods, g_pre, w_in, g_qa, w_qb, g_kva, cos_t, sin_t, cache_ckv, cache_kpe, g_post, w_out, w_kvb)


def _rope_tables():
    rows = LAT_S // GRID_W
    row = np.repeat(np.arange(rows), GRID_W).astype(np.float64)
    col = np.tile(np.arange(GRID_W), rows).astype(np.float64)
    n_freq = DA_DH // 4
    inv = ROPE_THETA ** (-np.arange(n_freq, dtype=np.float64) / n_freq)
    ang = np.concatenate([row[:, None] * inv, col[:, None] * inv], axis=-1)
    cos, sin = np.cos(ang), np.sin(ang)
    return (jnp.asarray(np.concatenate([cos, cos, cos, cos], axis=-1), F32),
            jnp.asarray(np.concatenate([-sin, sin, -sin, sin], axis=-1), F32))


def kernel(x_prompt, x_sample, cache_diff_k, cache_diff_v, cache_mla_ckv, cache_mla_kpe,
           c, c_ctx, w_ada, b_ada, g_pre, g_post, w_out,
           da_w_in, da_lam_q1, da_lam_k1, da_lam_q2, da_lam_k2, da_g_sub,
           mla_w_in, mla_g_qa, mla_w_qb, mla_g_kva, mla_w_kvb):
    assert DA_DH == MLA_ROPE
    assert x_prompt.shape == (N_CTX_B, CTX_S, D_MODEL) and x_sample.shape == (N_LAT_B, LAT_S, D_MODEL)
    assert cache_diff_k.shape == (N_LAT_B, (DEPTH + 1) // 2, PAST, HEADS, 2, DA_DH)
    assert cache_mla_ckv.shape == (N_LAT_B, DEPTH // 2, PAST, MLA_KV_LORA)
    assert w_ada.shape == (DEPTH, D_MODEL, 3 * D_MODEL) and da_w_in.shape[1:] == (D_MODEL, 4 * D_MODEL)
    n_diff = (DEPTH + 1) // 2
    n_mla = DEPTH // 2

    cond = jnp.concatenate(
        [c_ctx[None, :], c, jnp.zeros((COND_ROWS - 1 - N_LAT_B, D_MODEL), F32)], axis=0)
    b_ada3 = b_ada.reshape(DEPTH, 1, 3 * D_MODEL)
    mods_first = _ada_call(cond, w_ada, b_ada3, 2)

    cos_t, sin_t = _rope_tables()
    g_pre3 = g_pre.reshape(DEPTH, 1, D_MODEL)
    g_post3 = g_post.reshape(DEPTH, 1, D_MODEL)
    steps = N_CTX_B // CTX_PER_STEP
    w_out_first = w_out[0:1].astype(BF16)
    da_w_in_first = da_w_in[0:1].astype(BF16)
    g_sub3 = da_g_sub.reshape(n_diff, 1, DA_DV)
    lam_p = jnp.stack([da_lam_q1, da_lam_k1, da_lam_q2, da_lam_k2], axis=1)
    cache_dkt = jnp.transpose(cache_diff_k, (0, 1, 3, 4, 5, 2)).reshape(N_LAT_B, n_diff, D_MODEL, PAST)

    mla_w_in_t = jnp.swapaxes(mla_w_in, 1, 2)
    w_qb4 = jnp.swapaxes(mla_w_qb, 1, 2).reshape(n_mla, HEADS, MLA_NOPE + MLA_ROPE, MLA_Q_LORA)
    w_qb_b = jnp.pad(w_qb4, ((0, 0), (0, 0), (0, MLA_QK_PAD - MLA_NOPE - MLA_ROPE), (0, 0))).reshape(
        n_mla, HEADS * MLA_QK_PAD, MLA_Q_LORA).astype(BF16)
    g_qa3 = mla_g_qa.reshape(n_mla, 1, MLA_Q_LORA)
    g_kva3 = mla_g_kva.reshape(n_mla, 1, MLA_KV_LORA)
    cache_kpe_t = jnp.swapaxes(cache_mla_kpe, 2, 3)

    x_ctx = x_prompt.reshape(N_CTX_B * CTX_S, D_MODEL)
    x_lat = x_sample.reshape(N_LAT_B * LAT_S, D_MODEL)
    st_diff = None
    st_mla = None

    assert DEPTH == 4
    da_w = [(da_w_in_first, 0), None]
    mla_w = [None, None]
    w_o = [(w_out_first, 0), None, None, None]
    wide = D_MODEL * 4 // steps
    cols = D_MODEL // steps
    per_layer = steps // 2
    kv_cols = HEADS * (MLA_NOPE + MLA_DV) // steps

    mods = [(mods_first, 0), (mods_first, 1), None, None]

    def mla_in_cast(jj):
        return _cast_job(mla_w_in_t, (None, MLA_IN, cols), lambda t: (jj, 0, t),
                         (1, MLA_IN, D_MODEL), lambda t: (0, 0, t))

    side_jobs = {
        0: [_cast_job(da_w_in, (None, D_MODEL, wide), lambda t: (1, 0, t),
                      (1, D_MODEL, 4 * D_MODEL), lambda t: (0, 0, t)),
            mla_in_cast(0),
            _cast_job(w_out, (None, D_MODEL // steps, D_MODEL), lambda t: (1, t, 0),
                      (1, D_MODEL, D_MODEL), lambda t: (0, t, 0)),
            _cast_job(mla_w_kvb, (n_mla, MLA_KV_LORA, kv_cols), lambda t: (0, 0, t),
                      mla_w_kvb.shape, lambda t: (0, 0, t))],
        1: [mla_in_cast(1),
            _cast_job(w_out, (None, D_MODEL // per_layer, D_MODEL),
                      lambda t: (2 + t // per_layer, t % per_layer, 0),
                      (2, D_MODEL, D_MODEL), lambda t: (t // per_layer, t % per_layer, 0)),
            _ada_job(cond, w_ada, b_ada3, 2, steps)],
        2: [_ada_job(cond, w_ada, b_ada3, 3, steps)],
        3: [],
    }

    for i in range(DEPTH):
        j = i // 2
        if i % 2 == 0:
            w_j = da_w[j]
            outs = _layer_diff_ctx(x_ctx, *mods[i], g_pre3, g_post3, *w_j, *w_o[i], g_sub3, lam_p, st_diff,
                                   i, j, n_diff, side_jobs=side_jobs[i])
            x_ctx, st_dkt, st_dv = outs[0:3]
            st_diff = (st_dkt, st_dv)
            if i == 0:
                da_w[1], mla_w[0], w_o[1], w_kvb_b = (outs[3], 0), (outs[4], 0), (outs[5], 0), outs[6]
            else:
                mods[3] = (outs[3], 0)
            x_lat = _layer_diff_lat(x_lat, *mods[i], g_pre3, g_post3, *w_j, *w_o[i], cos_t, sin_t,
                                    cache_dkt, cache_diff_v, g_sub3, lam_p, i, j)
        else:
            w_j = mla_w[j]
            outs = _layer_mla_ctx(x_ctx, *mods[i], g_pre3, g_post3, *w_j, g_qa3, w_qb_b, g_kva3, *w_o[i], w_kvb_b,
                                  st_mla, i, j, n_mla, side_jobs=side_jobs[i])
            x_ctx, st_ckv, st_kpe = outs[0:3]
            st_mla = (st_ckv, st_kpe)
            if i == 1:
                mla_w[1] = (outs[3], 0)
                w_o[2], w_o[3] = (outs[4], 0), (outs[4], 1)
                mods[2] = (outs[5], 0)
            x_lat = _layer_mla_lat(x_lat, *mods[i], g_pre3, g_post3, *w_j, g_qa3, w_qb_b, g_kva3, *w_o[i], w_kvb_b,
                                   cos_t, sin_t, cache_mla_ckv, cache_kpe_t, i, j)

    return (x_ctx.reshape(N_CTX_B, CTX_S, D_MODEL),
            x_lat.reshape(N_LAT_B, LAT_S, D_MODEL),
            jnp.transpose(st_dkt.reshape(N_CTX_B, n_diff, HEADS, 2, DA_DH, CTX_S), (0, 1, 5, 2, 3, 4)),
            st_dv,
            st_ckv,
            jnp.swapaxes(st_kpe, 2, 3))
```

```python
import functools
import math

import jax
import jax.numpy as jnp
import numpy as np
from jax import lax
from jax.experimental import pallas as pl
from jax.experimental.pallas import tpu as pltpu

F32 = jnp.float32
BF16 = jnp.bfloat16

D_MODEL = 1024
DEPTH = 4
N_CTX_B = 16
CTX_S = 256
N_LAT_B = 4
LAT_S = 1024
PAST = 256
GRID_W = 64
HEADS = 8
DA_DH = 64
DA_DV = 128
MLA_Q_LORA = 384
MLA_KV_LORA = 256
MLA_NOPE = 128
MLA_ROPE = 64
MLA_DV = 128
MLA_QK_PAD = 256
ROPE_THETA = 10000.0
NORM_EPS = 1e-6
COND_ROWS = 8

LANES = 128
V7X_VMEM_MIB = 64
V_PAD = 16
TQ = 256
TM = 1024
VMEM_LIMIT = (V7X_VMEM_MIB - 8) * 1024 * 1024

_TRANS_B = (((1,), (1,)), ((), ()))
SCORES_AHEAD = 2
CTX_PER_STEP = 2
DA_Q_SCALE = DA_DH ** -0.5 * math.log2(math.e)
MLA_Q_SCALE = (MLA_NOPE + MLA_ROPE) ** -0.5 * math.log2(math.e)


def _silu(x):
    return x * (1.0 / (1.0 + jnp.exp(-x)))


def _rms(x, g):
    r = lax.rsqrt(jnp.mean(x * x, axis=-1, keepdims=True) + NORM_EPS)
    return (x * r) * g


def _rope_slab(x, cos, sin_signed):
    half = DA_DH // 2
    lane = lax.broadcasted_iota(jnp.int32, x.shape, 1)
    first_half = (lane % DA_DH) < half
    partner = jnp.where(first_half, pltpu.roll(x, LANES - half, axis=1), pltpu.roll(x, half, axis=1))
    return x * cos + partner * sin_signed


def _ones_rows(cols):
    row = lax.broadcasted_iota(jnp.int32, (V_PAD, cols), 0)
    return jnp.where(row == 0, 1.0, 0.0).astype(BF16)


def _cond_row(mods_ref, row):
    m = mods_ref[pl.ds(row, 1), :]
    return m[:, 0:D_MODEL], m[:, D_MODEL:2 * D_MODEL], m[:, 2 * D_MODEL:3 * D_MODEL]


def _modulated(x_ref, mods_ref, g_ref, row):
    shift, scale, _ = _cond_row(mods_ref, row)
    h = _rms(x_ref[...], g_ref[...]) * (1.0 + scale) + shift
    return h.astype(BF16)


def _with_rows(kernel_fn, picks):
    def run(*refs):
        refs = list(refs)
        for n, row in picks.items():
            refs[n] = refs[n].at[pl.ds(row, 1)]
        return kernel_fn(*refs)
    return run


def _ada_kernel(layer, cond_ref, w_ref, b_ref, o_ref):
    row = pl.program_id(0) if layer is None else layer
    a = _silu(cond_ref[...]).astype(BF16)
    o_ref[...] = jnp.dot(a, w_ref[...].astype(BF16), preferred_element_type=F32) + b_ref[pl.ds(row, 1), :]


def _ada_call(cond, w_ada, b_ada, n_layers):
    return pl.pallas_call(
        functools.partial(_ada_kernel, None),
        grid=(n_layers,),
        in_specs=[
            pl.BlockSpec((COND_ROWS, D_MODEL), lambda i: (0, 0)),
            pl.BlockSpec((None, D_MODEL, 3 * D_MODEL), lambda i: (i, 0, 0)),
            pl.BlockSpec(b_ada.shape, lambda i: (0, 0)),
        ],
        out_specs=pl.BlockSpec((None, COND_ROWS, 3 * D_MODEL), lambda i: (i, 0, 0)),
        out_shape=jax.ShapeDtypeStruct((n_layers, COND_ROWS, 3 * D_MODEL), F32),
        compiler_params=pltpu.CompilerParams(
            dimension_semantics=("arbitrary",), vmem_limit_bytes=VMEM_LIMIT),
        name="ada_mod",
    )(cond, w_ada, b_ada)


def _cast_block(src_ref, dst_ref):
    dst_ref[...] = src_ref[...].astype(BF16)


def _cast_job(array, blk, src_map, shape, dst_map):
    return dict(args=[array], in_specs=[pl.BlockSpec(blk, src_map)], out_spec=pl.BlockSpec(blk, dst_map),
                out_shape=jax.ShapeDtypeStruct(shape, BF16), run=_cast_block)


def _ada_job(cond, w_ada, b_ada, layer, steps):
    tn = 3 * D_MODEL // steps
    return dict(args=[cond, w_ada, b_ada],
                in_specs=[pl.BlockSpec((COND_ROWS, D_MODEL), lambda t: (0, 0)),
                          pl.BlockSpec((None, D_MODEL, tn), lambda t: (layer, 0, t)),
                          pl.BlockSpec((b_ada.shape[0], tn), lambda t: (0, t))],
                out_spec=pl.BlockSpec((None, COND_ROWS, tn), lambda t: (0, 0, t)),
                out_shape=jax.ShapeDtypeStruct((1, COND_ROWS, 3 * D_MODEL), F32),
                run=functools.partial(_ada_kernel, layer))


def _side_job_specs(jobs):
    in_specs = [spec for job in jobs for spec in job["in_specs"]]
    args = [a for job in jobs for a in job["args"]]
    runners = tuple((len(job["args"]), job["run"]) for job in jobs)
    return in_specs, [job["out_spec"] for job in jobs], [job["out_shape"] for job in jobs], args, runners


def _write_state(ref, r, j, value):
    if j is None:
        ref[r] = value
    else:
        for jj in range(ref.shape[1]):
            ref[r, jj] = value if jj == j else jnp.zeros_like(value)


def _state_plumbing(states, shapes, j, n_inputs, n_outputs):
    def index_map(shp, layer_index):
        return lambda t: (t, layer_index) + (0,) * (len(shp) - 2)

    if states is None:
        specs = [pl.BlockSpec((CTX_PER_STEP,) + shp[1:], index_map(shp, 0)) for shp in shapes]
        return [], specs, {}, ()
    specs = [pl.BlockSpec((CTX_PER_STEP, None) + shp[2:], index_map(shp, j)) for shp in shapes]
    in_specs = [pl.BlockSpec(memory_space=pl.ANY) for _ in shapes]
    aliases = {n_inputs + i: n_outputs + i for i in range(len(shapes))}
    return in_specs, specs, aliases, tuple(states)


def _finish(heads, w_out_ref, x_ref, mods_ref, g_post_ref, row, o_ref, rows=slice(None)):
    _, _, gate = _cond_row(mods_ref, row)
    og = jnp.concatenate(heads, axis=1)
    y = jnp.dot(og, w_out_ref[...], preferred_element_type=F32)
    o_ref[rows, :] = x_ref[rows, :] + gate * _rms(y, g_post_ref[...])


def _diff_lambda(lam_ref, lam_init):
    p = lam_ref[...]
    a = jnp.sum(p[0:1, :] * p[1:2, :], axis=-1, keepdims=True)
    b = jnp.sum(p[2:3, :] * p[3:4, :], axis=-1, keepdims=True)
    return jnp.exp(a) - jnp.exp(b) + lam_init


def _stack_diff_keys(k, k2_ref, r0, sk):
    rows = k.shape[0]
    lane = lax.broadcasted_iota(jnp.int32, (rows, LANES), 1)
    zero = jnp.zeros((rows, LANES), BF16)
    for h in range(HEADS):
        kh = k[:, h * LANES:(h + 1) * LANES]
        k2_ref[h, r0:r0 + rows, :] = jnp.where(lane < DA_DH, kh, zero)
        k2_ref[h, sk + r0:sk + r0 + rows, :] = jnp.where(lane >= DA_DH, kh, zero)


def _stack_values_t(vt, vt_ref, c0, cols):
    ones = _ones_rows(cols)
    for h in range(HEADS):
        vt_ref[h, 0:DA_DV, c0:c0 + cols] = vt[h * DA_DV:(h + 1) * DA_DV, :]
        vt_ref[h, DA_DV:DA_DV + V_PAD, c0:c0 + cols] = ones


def _diff_heads(q_ref, k2_ref, vt_ref, gate_ref, g_sub_ref, lam, lam_init, sk, rows=slice(None), after_issue=None,
                split_scores=False):
    g_sub = g_sub_ref[...]

    def scores(h):
        qh = q_ref[rows, h * LANES:(h + 1) * LANES]
        if split_scores:
            return tuple(lax.dot_general(k2_ref[h, c * sk:(c + 1) * sk, :], qh, _TRANS_B,
                                         preferred_element_type=F32) for c in range(2))
        s = lax.dot_general(k2_ref[h], qh, _TRANS_B, preferred_element_type=F32)
        return s[0:sk, :], s[sk:2 * sk, :]

    heads = []
    pending = [scores(h) for h in range(SCORES_AHEAD)]
    if after_issue is not None:
        after_issue()
    for h in range(HEADS):
        lo, hi = h * LANES, (h + 1) * LANES
        s = pending.pop(0)
        if h + SCORES_AHEAD < HEADS:
            pending.append(scores(h + SCORES_AHEAD))
        s0, s1 = s
        e0 = jnp.exp2(s0 - jnp.max(s0, axis=0, keepdims=True)).astype(BF16)
        e1 = jnp.exp2(s1 - jnp.max(s1, axis=0, keepdims=True)).astype(BF16)
        pv0 = jnp.dot(vt_ref[h], e0, preferred_element_type=F32)
        pv1 = jnp.dot(vt_ref[h], e1, preferred_element_type=F32)
        a0 = 1.0 / pv0[DA_DV:DA_DV + 1, :]
        a1 = lam / pv1[DA_DV:DA_DV + 1, :]
        ot = pv0[0:DA_DV, :] * a0 - pv1[0:DA_DV, :] * a1
        ot = ot * lax.rsqrt(jnp.mean(ot * ot, axis=0, keepdims=True) + NORM_EPS)
        o = (ot.T * g_sub) * (1.0 - lam_init)
        heads.append((o * gate_ref[rows, lo:hi]).astype(BF16))
    return heads


MLA_IN = MLA_Q_LORA + MLA_KV_LORA + MLA_ROPE + D_MODEL
_QA = slice(0, MLA_Q_LORA)
_KV = slice(MLA_Q_LORA, MLA_Q_LORA + MLA_KV_LORA)
_PE = slice(_KV.stop, _KV.stop + LANES)
_GT = slice(_KV.stop + MLA_ROPE, MLA_IN)


def _proj_t(hb, w_t_ref, rows):
    return lax.dot_general(hb, w_t_ref[rows, :], _TRANS_B, preferred_element_type=F32)


def _mla_queries(hb, w_in_ref, g_qa_ref):
    q_a = _proj_t(hb, w_in_ref, _QA)
    return _rms(q_a, g_qa_ref[...]).astype(BF16)


def _mla_expand(ckv, kpe, w_kvb_ref, kf_ref, vf_ref, r0, rows):
    zero = jnp.zeros((rows, LANES - MLA_ROPE), BF16)
    for h in range(HEADS):
        lo = h * (MLA_NOPE + MLA_DV)
        kv = jnp.dot(ckv, w_kvb_ref[:, lo:lo + MLA_NOPE + MLA_DV], preferred_element_type=F32)
        kf_ref[h, r0:r0 + rows, 0:MLA_NOPE] = kv[:, 0:MLA_NOPE].astype(BF16)
        kf_ref[h, r0:r0 + rows, MLA_NOPE:MLA_NOPE + MLA_ROPE] = kpe
        kf_ref[h, r0:r0 + rows, MLA_NOPE + MLA_ROPE:MLA_QK_PAD] = zero
        vf_ref[h, r0:r0 + rows, :] = kv[:, MLA_NOPE:MLA_NOPE + MLA_DV].astype(BF16)


def _mla_heads(q_ref, kf_ref, vf_ref, gate_ref, rows=slice(None), after_issue=None):
    def scores(h):
        return lax.dot_general(q_ref[rows, h * MLA_QK_PAD:(h + 1) * MLA_QK_PAD], kf_ref[h], _TRANS_B,
                               preferred_element_type=F32)

    heads = []
    pending = [scores(h) for h in range(SCORES_AHEAD)]
    if after_issue is not None:
        after_issue()
    for h in range(HEADS):
        s = pending.pop(0)
        if h + SCORES_AHEAD < HEADS:
            pending.append(scores(h + SCORES_AHEAD))
        e = jnp.exp2(s - jnp.max(s, axis=-1, keepdims=True))
        inv = 1.0 / jnp.sum(e, axis=-1, keepdims=True)
        o = jnp.dot(e.astype(BF16), vf_ref[h], preferred_element_type=F32) * inv
        lo, hi = h * LANES, (h + 1) * LANES
        heads.append((o * gate_ref[rows, lo:hi]).astype(BF16))
    return heads


def _split_layer_refs(refs, runners):
    scratch = refs[-2:]
    refs = refs[:-2]
    n_jobs = len(runners)
    job_out = refs[len(refs) - n_jobs:]
    main = refs[len(refs) - n_jobs - 3:len(refs) - n_jobs]
    pos = len(refs) - n_jobs - 3 - sum(n for n, _ in runners)
    for (n, run), out in zip(runners, job_out):
        run(*refs[pos:pos + n], out)
        pos += n
    return main, scratch


def _layer_diff_ctx_kernel(create_j, lam_init, runners, x_ref, mods_ref, g_pre_ref, w_in_ref, g_post_ref,
                           w_out_ref, g_sub_ref, lam_ref, *refs):
    (o_ref, skt_ref, sv_ref), (k2_ref, vb_ref) = _split_layer_refs(refs, runners)
    hb = _modulated(x_ref, mods_ref, g_pre_ref, 0)
    n = D_MODEL
    q = (jnp.dot(hb, w_in_ref[:, 0:n], preferred_element_type=F32) * DA_Q_SCALE).astype(BF16)
    k = jnp.dot(hb, w_in_ref[:, n:2 * n], preferred_element_type=F32)
    v = jnp.dot(hb, w_in_ref[:, 2 * n:3 * n], preferred_element_type=F32)
    gate = _silu(jnp.dot(hb, w_in_ref[:, 3 * n:4 * n], preferred_element_type=F32)).astype(BF16)
    kb = k.astype(BF16)
    lam = _diff_lambda(lam_ref, lam_init)
    finish = None
    for r in range(CTX_PER_STEP):
        rows = slice(r * CTX_S, (r + 1) * CTX_S)
        vt = v[rows, :].T
        _write_state(skt_ref, r, create_j, k[rows, :].T)
        _write_state(sv_ref, r, create_j, v[rows, :].reshape(CTX_S, HEADS, DA_DV))
        _stack_diff_keys(kb[rows, :], k2_ref.at[r], 0, CTX_S)
        _stack_values_t(vt.astype(BF16), vb_ref.at[r], 0, CTX_S)
        heads = _diff_heads(q, k2_ref.at[r], vb_ref.at[r], gate, g_sub_ref, lam, lam_init, CTX_S, rows, finish,
                            split_scores=True)
        finish = functools.partial(_finish, heads, w_out_ref, x_ref, mods_ref, g_post_ref, 0, o_ref, rows)
    finish()


def _layer_diff_ctx(x, mods, mi, g_pre, g_post, w_in, wj, w_out, wo, g_sub, lam_p, states, layer, j, n_layers,
                    side_jobs=()):
    lam_init = 0.8 - 0.6 * math.exp(-0.3 * layer)
    shapes = [(N_CTX_B, n_layers, D_MODEL, CTX_S), (N_CTX_B, n_layers, CTX_S, HEADS, DA_DV)]
    st_in_specs, st_out_specs, aliases, st_args = _state_plumbing(states, shapes, j, 8, 1)
    cj_in, cj_out, cj_shapes, cj_args, runners = _side_job_specs(side_jobs)
    tok_spec = pl.BlockSpec((CTX_PER_STEP * CTX_S, D_MODEL), lambda t: (t, 0))
    return pl.pallas_call(
        _with_rows(functools.partial(_layer_diff_ctx_kernel, j if states is None else None, lam_init, runners),
                   {2: layer, 4: layer}),
        grid=(N_CTX_B // CTX_PER_STEP,),
        in_specs=[
            tok_spec,
            pl.BlockSpec((None, COND_ROWS, 3 * D_MODEL), lambda t: (mi, 0, 0)),
            pl.BlockSpec(g_pre.shape, lambda t: (0, 0)),
            pl.BlockSpec((None, D_MODEL, 4 * D_MODEL), lambda t: (wj, 0, 0)),
            pl.BlockSpec(g_post.shape, lambda t: (0, 0)),
            pl.BlockSpec((None, D_MODEL, D_MODEL), lambda t: (wo, 0, 0)),
            pl.BlockSpec((None, 1, DA_DV), lambda t: (j, 0, 0)),
            pl.BlockSpec((None, 4, DA_DH), lambda t: (j, 0, 0)),
        ] + st_in_specs + cj_in,
        out_specs=[tok_spec] + st_out_specs + cj_out,
        out_shape=[jax.ShapeDtypeStruct(x.shape, F32)] + [jax.ShapeDtypeStruct(shp, F32) for shp in shapes]
        + cj_shapes,
        input_output_aliases=aliases,
        scratch_shapes=[
            pltpu.VMEM((CTX_PER_STEP, HEADS, 2 * CTX_S, LANES), BF16),
            pltpu.VMEM((CTX_PER_STEP, HEADS, DA_DV + V_PAD, CTX_S), BF16),
        ],
        compiler_params=pltpu.CompilerParams(
            dimension_semantics=("arbitrary",), vmem_limit_bytes=VMEM_LIMIT),
        name="layer_diff_ctx",
    )(x, mods, g_pre, w_in, g_post, w_out, g_sub, lam_p, *st_args, *cj_args)


def _layer_mla_ctx_kernel(create_j, runners, x_ref, mods_ref, g_pre_ref, w_in_ref, g_qa_ref, w_qb_ref, g_kva_ref,
                          g_post_ref, w_out_ref, w_kvb_ref, *refs):
    (o_ref, ckv_ref, kpe_ref), (kf_ref, vf_ref) = _split_layer_refs(refs, runners)
    hb = _modulated(x_ref, mods_ref, g_pre_ref, 0)
    qn = _mla_queries(hb, w_in_ref, g_qa_ref)
    q = (_proj_t(qn, w_qb_ref, slice(None)) * MLA_Q_SCALE).astype(BF16)
    ckv = _rms(_proj_t(hb, w_in_ref, _KV), g_kva_ref[...])
    gate = _silu(_proj_t(hb, w_in_ref, _GT)).astype(BF16)
    pe = _proj_t(hb, w_in_ref, _PE)
    finish = None
    for r in range(CTX_PER_STEP):
        rows = slice(r * CTX_S, (r + 1) * CTX_S)
        kpe = pe[rows, 0:MLA_ROPE]
        _write_state(ckv_ref, r, create_j, ckv[rows, :])
        _write_state(kpe_ref, r, create_j, kpe.T)
        _mla_expand(ckv[rows, :].astype(BF16), kpe.astype(BF16), w_kvb_ref, kf_ref.at[r], vf_ref.at[r],
                    0, CTX_S)
        heads = _mla_heads(q, kf_ref.at[r], vf_ref.at[r], gate, rows, finish)
        finish = functools.partial(_finish, heads, w_out_ref, x_ref, mods_ref, g_post_ref, 0, o_ref, rows)
    finish()


def _layer_mla_ctx(x, mods, mi, g_pre, g_post, w_in, wj, g_qa, w_qb, g_kva, w_out, wo, w_kvb, states, layer, j,
                   n_layers, side_jobs=()):
    shapes = [(N_CTX_B, n_layers, CTX_S, MLA_KV_LORA), (N_CTX_B, n_layers, MLA_ROPE, CTX_S)]
    st_in_specs, st_out_specs, aliases, st_args = _state_plumbing(states, shapes, j, 10, 1)
    cj_in, cj_out, cj_shapes, cj_args, runners = _side_job_specs(side_jobs)
    tok_spec = pl.BlockSpec((CTX_PER_STEP * CTX_S, D_MODEL), lambda t: (t, 0))
    return pl.pallas_call(
        _with_rows(functools.partial(_layer_mla_ctx_kernel, j if states is None else None, runners),
                   {2: layer, 4: j, 6: j, 7: layer}),
        grid=(N_CTX_B // CTX_PER_STEP,),
        in_specs=[
            tok_spec,
            pl.BlockSpec((None, COND_ROWS, 3 * D_MODEL), lambda t: (mi, 0, 0)),
            pl.BlockSpec(g_pre.shape, lambda t: (0, 0)),
            pl.BlockSpec((None, MLA_IN, D_MODEL), lambda t: (wj, 0, 0)),
            pl.BlockSpec(g_qa.shape, lambda t: (0, 0)),
            pl.BlockSpec((None, HEADS * MLA_QK_PAD, MLA_Q_LORA), lambda t: (j, 0, 0)),
            pl.BlockSpec(g_kva.shape, lambda t: (0, 0)),
            pl.BlockSpec(g_post.shape, lambda t: (0, 0)),
            pl.BlockSpec((None, D_MODEL, D_MODEL), lambda t: (wo, 0, 0)),
            pl.BlockSpec((None, MLA_KV_LORA, HEADS * (MLA_NOPE + MLA_DV)), lambda t: (j, 0, 0)),
        ] + st_in_specs + cj_in,
        out_specs=[tok_spec] + st_out_specs + cj_out,
        out_shape=[jax.ShapeDtypeStruct(x.shape, F32)] + [jax.ShapeDtypeStruct(shp, F32) for shp in shapes]
        + cj_shapes,
        input_output_aliases=aliases,
        scratch_shapes=[
            pltpu.VMEM((CTX_PER_STEP, HEADS, CTX_S, MLA_QK_PAD), BF16),
            pltpu.VMEM((CTX_PER_STEP, HEADS, CTX_S, MLA_DV), BF16),
        ],
        compiler_params=pltpu.CompilerParams(
            dimension_semantics=("arbitrary",), vmem_limit_bytes=VMEM_LIMIT),
        name="layer_mla_ctx",
    )(x, mods, g_pre, w_in, g_qa, w_qb, g_kva, g_post, w_out, w_kvb, *st_args, *cj_args)


LAT_TILES = LAT_S // TM
LAT_QBLOCKS = LAT_S // TQ


def _lat_specs():
    tile = lambda s: jnp.minimum(s, LAT_TILES - 1)
    qblk = lambda s: jnp.maximum(s - LAT_TILES, 0)
    x_proj = pl.BlockSpec((TM, D_MODEL), lambda b, s: (b * LAT_TILES + tile(s), 0))
    x_attn = pl.BlockSpec((TQ, D_MODEL), lambda b, s: (b * LAT_QBLOCKS + qblk(s), 0))
    rope = pl.BlockSpec((TM, LANES), lambda b, s: (tile(s), 0))
    return x_proj, x_attn, rope


def _query_rows():
    i = pl.program_id(1) - LAT_TILES
    return pl.ds(pl.multiple_of(i * TQ, TQ), TQ)


def _layer_diff_lat_kernel(lam_init, xp_ref, xa_ref, mods_ref, g_pre_ref, w_in_ref, cos_ref, sin_ref,
                           ckt_ref, cv_ref, g_post_ref, w_out_ref, g_sub_ref, lam_ref,
                           o_ref, k2_ref, vb_ref, q_s, gate_s):
    step = pl.program_id(1)
    row = 1 + pl.program_id(0)
    sk = LAT_S + PAST

    for t in range(LAT_TILES):
        @pl.when(step == t)
        def _(t=t):
            r0 = t * TM
            hb = _modulated(xp_ref, mods_ref, g_pre_ref, row)
            cos = cos_ref[...]
            sin = sin_ref[...]
            n = D_MODEL
            q = jnp.dot(hb, w_in_ref[:, 0:n], preferred_element_type=F32)
            k = jnp.dot(hb, w_in_ref[:, n:2 * n], preferred_element_type=F32)
            cos_q = cos * DA_Q_SCALE
            sin_q = sin * DA_Q_SCALE
            slabs = [slice(h * LANES, (h + 1) * LANES) for h in range(HEADS)]
            q_s[r0:r0 + TM, :] = jnp.concatenate(
                [_rope_slab(q[:, sl], cos_q, sin_q).astype(BF16) for sl in slabs], axis=1)
            kb = jnp.concatenate([_rope_slab(k[:, sl], cos, sin).astype(BF16) for sl in slabs], axis=1)
            _stack_diff_keys(kb, k2_ref, r0, sk)
            vt = jnp.dot(hb, w_in_ref[:, 2 * n:3 * n], preferred_element_type=F32).T.astype(BF16)
            _stack_values_t(vt, vb_ref, r0, TM)
            gate_s[r0:r0 + TM, :] = _silu(
                jnp.dot(hb, w_in_ref[:, 3 * n:4 * n], preferred_element_type=F32)).astype(BF16)

    @pl.when(step == 0)
    def _():
        lane = lax.broadcasted_iota(jnp.int32, (PAST, LANES), 1)
        zero = jnp.zeros((PAST, LANES), BF16)
        ones = _ones_rows(PAST)
        for h in range(HEADS):
            ch = ckt_ref[h * LANES:(h + 1) * LANES, :].T.astype(BF16)
            k2_ref[h, LAT_S:sk, :] = jnp.where(lane < DA_DH, ch, zero)
            k2_ref[h, sk + LAT_S:2 * sk, :] = jnp.where(lane >= DA_DH, ch, zero)
            vb_ref[h, 0:DA_DV, LAT_S:sk] = cv_ref[:, h, :].T.astype(BF16)
            vb_ref[h, DA_DV:DA_DV + V_PAD, LAT_S:sk] = ones

    @pl.when(step >= LAT_TILES)
    def _():
        lam = _diff_lambda(lam_ref, lam_init)
        heads = _diff_heads(q_s, k2_ref, vb_ref, gate_s, g_sub_ref, lam, lam_init, sk, _query_rows())
        _finish(heads, w_out_ref, xa_ref, mods_ref, g_post_ref, row, o_ref)


def _layer_diff_lat(x, mods, mi, g_pre, g_post, w_in, wj, w_out, wo, cos_t, sin_t, cache_kt, cache_v, g_sub,
                    lam_p, layer, j):
    lam_init = 0.8 - 0.6 * math.exp(-0.3 * layer)
    sk = LAT_S + PAST
    x_proj, x_attn, rope = _lat_specs()
    return pl.pallas_call(
        _with_rows(functools.partial(_layer_diff_lat_kernel, lam_init), {3: layer, 9: layer}),
        grid=(N_LAT_B, LAT_TILES + LAT_QBLOCKS),
        in_specs=[
            x_proj, x_attn,
            pl.BlockSpec((None, COND_ROWS, 3 * D_MODEL), lambda b, s: (mi, 0, 0)),
            pl.BlockSpec(g_pre.shape, lambda b, s: (0, 0)),
            pl.BlockSpec((None, D_MODEL, 4 * D_MODEL), lambda b, s: (wj, 0, 0)),
            rope, rope,
            pl.BlockSpec((None, None, D_MODEL, PAST), lambda b, s: (b, j, 0, 0)),
            pl.BlockSpec((None, None, PAST, HEADS, DA_DV), lambda b, s: (b, j, 0, 0, 0)),
            pl.BlockSpec(g_post.shape, lambda b, s: (0, 0)),
            pl.BlockSpec((None, D_MODEL, D_MODEL), lambda b, s: (wo, 0, 0)),
            pl.BlockSpec((None, 1, DA_DV), lambda b, s: (j, 0, 0)),
            pl.BlockSpec((None, 4, DA_DH), lambda b, s: (j, 0, 0)),
        ],
        out_specs=x_attn,
        out_shape=jax.ShapeDtypeStruct(x.shape, F32),
        scratch_shapes=[
            pltpu.VMEM((HEADS, 2 * sk, LANES), BF16),
            pltpu.VMEM((HEADS, DA_DV + V_PAD, sk), BF16),
            pltpu.VMEM((LAT_S, D_MODEL), BF16),
            pltpu.VMEM((LAT_S, D_MODEL), BF16),
        ],
        compiler_params=pltpu.CompilerParams(
            dimension_semantics=("arbitrary", "arbitrary"), vmem_limit_bytes=VMEM_LIMIT),
        name="layer_diff_lat",
    )(x, x, mods, g_pre, w_in, cos_t, sin_t, cache_kt, cache_v, g_post, w_out, g_sub, lam_p)


def _layer_mla_lat_kernel(xp_ref, xa_ref, mods_ref, g_pre_ref, w_in_ref, g_qa_ref, w_qb_ref, g_kva_ref,
                          cos_ref, sin_ref, cckv_ref, ckpe_ref, g_post_ref, w_out_ref, w_kvb_ref,
                          o_ref, kf_ref, vf_ref, q_s, gate_s):
    step = pl.program_id(1)
    row = 1 + pl.program_id(0)

    for t in range(LAT_TILES):
        @pl.when(step == t)
        def _(t=t):
            r0 = t * TM
            hb = _modulated(xp_ref, mods_ref, g_pre_ref, row)
            cos = cos_ref[...]
            sin = sin_ref[...]
            qn = _mla_queries(hb, w_in_ref, g_qa_ref)
            for h in range(HEADS):
                lo = h * MLA_QK_PAD
                q = _proj_t(qn, w_qb_ref, slice(lo, lo + MLA_QK_PAD)) * MLA_Q_SCALE
                q_s[r0:r0 + TM, lo:lo + LANES] = q[:, 0:LANES].astype(BF16)
                q_s[r0:r0 + TM, lo + LANES:lo + 2 * LANES] = _rope_slab(
                    q[:, LANES:2 * LANES], cos, sin).astype(BF16)
            ckv = _rms(_proj_t(hb, w_in_ref, _KV), g_kva_ref[...]).astype(BF16)
            gate_s[r0:r0 + TM, :] = _silu(_proj_t(hb, w_in_ref, _GT)).astype(BF16)
            kpe = _rope_slab(_proj_t(hb, w_in_ref, _PE), cos, sin)[:, 0:MLA_ROPE].astype(BF16)
            _mla_expand(ckv, kpe, w_kvb_ref, kf_ref, vf_ref, r0, TM)

    @pl.when(step == 0)
    def _():
        _mla_expand(cckv_ref[...].astype(BF16), ckpe_ref[...].T.astype(BF16), w_kvb_ref, kf_ref, vf_ref,
                    LAT_S, PAST)

    @pl.when(step >= LAT_TILES)
    def _():
        heads = _mla_heads(q_s, kf_ref, vf_ref, gate_s, _query_rows())
        _finish(heads, w_out_ref, xa_ref, mods_ref, g_post_ref, row, o_ref)


def _layer_mla_lat(x, mods, mi, g_pre, g_post, w_in, wj, g_qa, w_qb, g_kva, w_out, wo, w_kvb, cos_t, sin_t,
                   cache_ckv, cache_kpe, layer, j):
    sk = LAT_S + PAST
    x_proj, x_attn, rope = _lat_specs()
    return pl.pallas_call(
        _with_rows(_layer_mla_lat_kernel, {3: layer, 5: j, 7: j, 12: layer}),
        grid=(N_LAT_B, LAT_TILES + LAT_QBLOCKS),
        in_specs=[
            x_proj, x_attn,
            pl.BlockSpec((None, COND_ROWS, 3 * D_MODEL), lambda b, s: (mi, 0, 0)),
            pl.BlockSpec(g_pre.shape, lambda b, s: (0, 0)),
            pl.BlockSpec((None, MLA_IN, D_MODEL), lambda b, s: (wj, 0, 0)),
            pl.BlockSpec(g_qa.shape, lambda b, s: (0, 0)),
            pl.BlockSpec((None, HEADS * MLA_QK_PAD, MLA_Q_LORA), lambda b, s: (j, 0, 0)),
            pl.BlockSpec(g_kva.shape, lambda b, s: (0, 0)),
            rope, rope,
            pl.BlockSpec((None, None, PAST, MLA_KV_LORA), lambda b, s: (b, j, 0, 0)),
            pl.BlockSpec((None, None, MLA_ROPE, PAST), lambda b, s: (b, j, 0, 0)),
            pl.BlockSpec(g_post.shape, lambda b, s: (0, 0)),
            pl.BlockSpec((None, D_MODEL, D_MODEL), lambda b, s: (wo, 0, 0)),
            pl.BlockSpec((None, MLA_KV_LORA, HEADS * (MLA_NOPE + MLA_DV)), lambda b, s: (j, 0, 0)),
        ],
        out_specs=x_attn,
        out_shape=jax.ShapeDtypeStruct(x.shape, F32),
        scratch_shapes=[
            pltpu.VMEM((HEADS, sk, MLA_QK_PAD), BF16),
            pltpu.VMEM((HEADS, sk, MLA_DV), BF16),
            pltpu.VMEM((LAT_S, HEADS * MLA_QK_PAD), BF16),
            pltpu.VMEM((LAT_S, D_MODEL), BF16),
        ],
        compiler_params=pltpu.CompilerParams(
            dimension_semantics=("arbitrary", "arbitrary"), vmem_limit_bytes=VMEM_LIMIT),
        name="layer_mla_lat",
    )(x, x, mods, g_pre, w_in, g_qa, w_qb, g_kva, cos_t, sin_t, cache_ckv, cache_kpe, g_post, w_out, w_kvb)


def _rope_tables():
    rows = LAT_S // GRID_W
    row = np.repeat(np.arange(rows), GRID_W).astype(np.float64)
    col = np.tile(np.arange(GRID_W), rows).astype(np.float64)
    n_freq = DA_DH // 4
    inv = ROPE_THETA ** (-np.arange(n_freq, dtype=np.float64) / n_freq)
    ang = np.concatenate([row[:, None] * inv, col[:, None] * inv], axis=-1)
    cos, sin = np.cos(ang), np.sin(ang)
    return (jnp.asarray(np.concatenate([cos, cos, cos, cos], axis=-1), F32),
            jnp.asarray(np.concatenate([-sin, sin, -sin, sin], axis=-1), F32))


def kernel(x_prompt, x_sample, cache_diff_k, cache_diff_v, cache_mla_ckv, cache_mla_kpe,
           c, c_ctx, w_ada, b_ada, g_pre, g_post, w_out,
           da_w_in, da_lam_q1, da_lam_k1, da_lam_q2, da_lam_k2, da_g_sub,
           mla_w_in, mla_g_qa, mla_w_qb, mla_g_kva, mla_w_kvb):
    assert DA_DH == MLA_ROPE
    assert x_prompt.shape == (N_CTX_B, CTX_S, D_MODEL) and x_sample.shape == (N_LAT_B, LAT_S, D_MODEL)
    assert cache_diff_k.shape == (N_LAT_B, (DEPTH + 1) // 2, PAST, HEADS, 2, DA_DH)
    assert cache_mla_ckv.shape == (N_LAT_B, DEPTH // 2, PAST, MLA_KV_LORA)
    assert w_ada.shape == (DEPTH, D_MODEL, 3 * D_MODEL) and da_w_in.shape[1:] == (D_MODEL, 4 * D_MODEL)
    n_diff = (DEPTH + 1) // 2
    n_mla = DEPTH // 2

    cond = jnp.concatenate(
        [c_ctx[None, :], c, jnp.zeros((COND_ROWS - 1 - N_LAT_B, D_MODEL), F32)], axis=0)
    mods_first = _ada_call(cond, w_ada, b_ada, 2)

    cos_t, sin_t = _rope_tables()
    steps = N_CTX_B // CTX_PER_STEP
    w_out_first = w_out[0:1].astype(BF16)
    da_w_in_first = da_w_in[0:1].astype(BF16)
    g_sub3 = da_g_sub.reshape(n_diff, 1, DA_DV)
    lam_p = jnp.stack([da_lam_q1, da_lam_k1, da_lam_q2, da_lam_k2], axis=1)
    cache_dkt = jnp.transpose(cache_diff_k, (0, 1, 3, 4, 5, 2)).reshape(N_LAT_B, n_diff, D_MODEL, PAST)

    mla_w_in_t = jnp.swapaxes(mla_w_in, 1, 2)
    w_qb4 = jnp.swapaxes(mla_w_qb, 1, 2).reshape(n_mla, HEADS, MLA_NOPE + MLA_ROPE, MLA_Q_LORA)
    w_qb_b = jnp.pad(w_qb4, ((0, 0), (0, 0), (0, MLA_QK_PAD - MLA_NOPE - MLA_ROPE), (0, 0))).reshape(
        n_mla, HEADS * MLA_QK_PAD, MLA_Q_LORA).astype(BF16)
    cache_kpe_t = jnp.swapaxes(cache_mla_kpe, 2, 3)

    x_ctx = x_prompt.reshape(N_CTX_B * CTX_S, D_MODEL)
    x_lat = x_sample.reshape(N_LAT_B * LAT_S, D_MODEL)
    st_diff = None
    st_mla = None

    assert DEPTH == 4
    da_w = [(da_w_in_first, 0), None]
    mla_w = [None, None]
    w_o = [(w_out_first, 0), None, None, None]
    wide = D_MODEL * 4 // steps
    cols = D_MODEL // steps
    per_layer = steps // 2
    kv_cols = HEADS * (MLA_NOPE + MLA_DV) // steps

    mods = [(mods_first, 0), (mods_first, 1), None, None]

    def mla_in_cast(jj):
        return _cast_job(mla_w_in_t, (None, MLA_IN, cols), lambda t: (jj, 0, t),
                         (1, MLA_IN, D_MODEL), lambda t: (0, 0, t))

    side_jobs = {
        0: [_cast_job(da_w_in, (None, D_MODEL, wide), lambda t: (1, 0, t),
                      (1, D_MODEL, 4 * D_MODEL), lambda t: (0, 0, t)),
            mla_in_cast(0),
            _cast_job(w_out, (None, D_MODEL // steps, D_MODEL), lambda t: (1, t, 0),
                      (1, D_MODEL, D_MODEL), lambda t: (0, t, 0)),
            _cast_job(mla_w_kvb, (n_mla, MLA_KV_LORA, kv_cols), lambda t: (0, 0, t),
                      mla_w_kvb.shape, lambda t: (0, 0, t))],
        1: [mla_in_cast(1),
            _cast_job(w_out, (None, D_MODEL // per_layer, D_MODEL),
                      lambda t: (2 + t // per_layer, t % per_layer, 0),
                      (2, D_MODEL, D_MODEL), lambda t: (t // per_layer, t % per_layer, 0)),
            _ada_job(cond, w_ada, b_ada, 2, steps)],
        2: [_ada_job(cond, w_ada, b_ada, 3, steps)],
        3: [],
    }

    for i in range(DEPTH):
        j = i // 2
        if i % 2 == 0:
            w_j = da_w[j]
            outs = _layer_diff_ctx(x_ctx, *mods[i], g_pre, g_post, *w_j, *w_o[i], g_sub3, lam_p, st_diff,
                                   i, j, n_diff, side_jobs=side_jobs[i])
            x_ctx, st_dkt, st_dv = outs[0:3]
            st_diff = (st_dkt, st_dv)
            if i == 0:
                da_w[1], mla_w[0], w_o[1], w_kvb_b = (outs[3], 0), (outs[4], 0), (outs[5], 0), outs[6]
            else:
                mods[3] = (outs[3], 0)
            x_lat = _layer_diff_lat(x_lat, *mods[i], g_pre, g_post, *w_j, *w_o[i], cos_t, sin_t,
                                    cache_dkt, cache_diff_v, g_sub3, lam_p, i, j)
        else:
            w_j = mla_w[j]
            outs = _layer_mla_ctx(x_ctx, *mods[i], g_pre, g_post, *w_j, mla_g_qa, w_qb_b, mla_g_kva, *w_o[i], w_kvb_b,
                                  st_mla, i, j, n_mla, side_jobs=side_jobs[i])
            x_ctx, st_ckv, st_kpe = outs[0:3]
            st_mla = (st_ckv, st_kpe)
            if i == 1:
                mla_w[1] = (outs[3], 0)
                w_o[2], w_o[3] = (outs[4], 0), (outs[4], 1)
                mods[2] = (outs[5], 0)
            x_lat = _layer_mla_lat(x_lat, *mods[i], g_pre, g_post, *w_j, mla_g_qa, w_qb_b, mla_g_kva, *w_o[i], w_kvb_b,
                                   cos_t, sin_t, cache_mla_ckv, cache_kpe_t, i, j)

    return (x_ctx.reshape(N_CTX_B, CTX_S, D_MODEL),
            x_lat.reshape(N_LAT_B, LAT_S, D_MODEL),
            jnp.transpose(st_dkt.reshape(N_CTX_B, n_diff, HEADS, 2, DA_DH, CTX_S), (0, 1, 5, 2, 3, 4)),
            st_dv,
            st_ckv,
            jnp.swapaxes(st_kpe, 2, 3))
```

```python
import functools
import math

import jax
import jax.numpy as jnp
import numpy as np
from jax import lax
from jax.experimental import pallas as pl
from jax.experimental.pallas import tpu as pltpu

F32 = jnp.float32
BF16 = jnp.bfloat16

D_MODEL = 1024
DEPTH = 4
N_CTX_B = 16
CTX_S = 256
N_LAT_B = 4
LAT_S = 1024
PAST = 256
GRID_W = 64
HEADS = 8
DA_DH = 64
DA_DV = 128
MLA_Q_LORA = 384
MLA_KV_LORA = 256
MLA_NOPE = 128
MLA_ROPE = 64
MLA_DV = 128
MLA_QK_PAD = 256
ROPE_THETA = 10000.0
NORM_EPS = 1e-6
COND_ROWS = 8

LANES = 128
V7X_VMEM_MIB = 64
V_PAD = 16
TQ = 256
TM = 1024
VMEM_LIMIT = (V7X_VMEM_MIB - 8) * 1024 * 1024

_TRANS_B = (((1,), (1,)), ((), ()))
SCORES_AHEAD = 2
CTX_PER_STEP = 2
DA_Q_SCALE = DA_DH ** -0.5 * math.log2(math.e)
MLA_Q_SCALE = (MLA_NOPE + MLA_ROPE) ** -0.5 * math.log2(math.e)


def _silu(x):
    return x * (1.0 / (1.0 + jnp.exp(-x)))


def _rms(x, g):
    r = lax.rsqrt(jnp.mean(x * x, axis=-1, keepdims=True) + NORM_EPS)
    return (x * r) * g


def _rope_slab(x, cos, sin_signed):
    half = DA_DH // 2
    lane = lax.broadcasted_iota(jnp.int32, x.shape, 1)
    first_half = (lane % DA_DH) < half
    partner = jnp.where(first_half, pltpu.roll(x, LANES - half, axis=1), pltpu.roll(x, half, axis=1))
    return x * cos + partner * sin_signed


def _ones_rows(cols):
    row = lax.broadcasted_iota(jnp.int32, (V_PAD, cols), 0)
    return jnp.where(row == 0, 1.0, 0.0).astype(BF16)


def _cond_row(mods_ref, row):
    m = mods_ref[pl.ds(row, 1), :]
    return m[:, 0:D_MODEL], m[:, D_MODEL:2 * D_MODEL], m[:, 2 * D_MODEL:3 * D_MODEL]


def _modulated(x_ref, mods_ref, g_ref, row):
    shift, scale, _ = _cond_row(mods_ref, row)
    h = _rms(x_ref[...], g_ref[...]) * (1.0 + scale) + shift
    return h.astype(BF16)


def _with_rows(kernel_fn, picks):
    def run(*refs):
        refs = list(refs)
        for n, row in picks.items():
            refs[n] = refs[n].at[pl.ds(row, 1)]
        return kernel_fn(*refs)
    return run


def _ada_kernel(layer, cond_ref, w_ref, b_ref, o_ref):
    row = pl.program_id(0) if layer is None else layer
    a = _silu(cond_ref[...]).astype(BF16)
    o_ref[...] = jnp.dot(a, w_ref[...].astype(BF16), preferred_element_type=F32) + b_ref[pl.ds(row, 1), :]


def _ada_call(cond, w_ada, b_ada, n_layers):
    return pl.pallas_call(
        functools.partial(_ada_kernel, None),
        grid=(n_layers,),
        in_specs=[
            pl.BlockSpec((COND_ROWS, D_MODEL), lambda i: (0, 0)),
            pl.BlockSpec((None, D_MODEL, 3 * D_MODEL), lambda i: (i, 0, 0)),
            pl.BlockSpec(b_ada.shape, lambda i: (0, 0)),
        ],
        out_specs=pl.BlockSpec((None, COND_ROWS, 3 * D_MODEL), lambda i: (i, 0, 0)),
        out_shape=jax.ShapeDtypeStruct((n_layers, COND_ROWS, 3 * D_MODEL), F32),
        compiler_params=pltpu.CompilerParams(
            dimension_semantics=("arbitrary",), vmem_limit_bytes=VMEM_LIMIT),
        name="ada_mod",
    )(cond, w_ada, b_ada)


def _cast_block(src_ref, dst_ref):
    dst_ref[...] = src_ref[...].astype(BF16)


def _cast_job(array, blk, src_map, shape, dst_map):
    return dict(args=[array], in_specs=[pl.BlockSpec(blk, src_map)], out_spec=pl.BlockSpec(blk, dst_map),
                out_shape=jax.ShapeDtypeStruct(shape, BF16), run=_cast_block)


def _pad_heads_block(src_ref, dst_ref):
    qk = MLA_NOPE + MLA_ROPE
    zero = jnp.zeros(src_ref.shape[:2] + (MLA_QK_PAD - qk,), BF16)
    for h in range(src_ref.shape[2] // qk):
        lo = h * MLA_QK_PAD
        dst_ref[:, :, lo:lo + qk] = src_ref[:, :, h * qk:(h + 1) * qk].astype(BF16)
        dst_ref[:, :, lo + qk:lo + MLA_QK_PAD] = zero


def _pad_heads_job(w_qb, steps):
    n, rows, _ = w_qb.shape
    span = steps // (HEADS // 2)
    return dict(args=[w_qb],
                in_specs=[pl.BlockSpec((n, rows, 2 * (MLA_NOPE + MLA_ROPE)), lambda t: (0, 0, t // span))],
                out_spec=pl.BlockSpec((n, rows, 2 * MLA_QK_PAD), lambda t: (0, 0, t // span)),
                out_shape=jax.ShapeDtypeStruct((n, rows, HEADS * MLA_QK_PAD), BF16), run=_pad_heads_block)


def _ada_job(cond, w_ada, b_ada, layer, steps):
    tn = 3 * D_MODEL // steps
    return dict(args=[cond, w_ada, b_ada],
                in_specs=[pl.BlockSpec((COND_ROWS, D_MODEL), lambda t: (0, 0)),
                          pl.BlockSpec((None, D_MODEL, tn), lambda t: (layer, 0, t)),
                          pl.BlockSpec((b_ada.shape[0], tn), lambda t: (0, t))],
                out_spec=pl.BlockSpec((None, COND_ROWS, tn), lambda t: (0, 0, t)),
                out_shape=jax.ShapeDtypeStruct((1, COND_ROWS, 3 * D_MODEL), F32),
                run=functools.partial(_ada_kernel, layer))


def _side_job_specs(jobs):
    in_specs = [spec for job in jobs for spec in job["in_specs"]]
    args = [a for job in jobs for a in job["args"]]
    runners = tuple((len(job["args"]), job["run"]) for job in jobs)
    return in_specs, [job["out_spec"] for job in jobs], [job["out_shape"] for job in jobs], args, runners


def _write_state(ref, r, j, value):
    if j is None:
        ref[r] = value
    else:
        for jj in range(ref.shape[1]):
            ref[r, jj] = value if jj == j else jnp.zeros_like(value)


def _state_plumbing(states, shapes, j, n_inputs, n_outputs):
    def index_map(shp, layer_index):
        return lambda t: (t, layer_index) + (0,) * (len(shp) - 2)

    if states is None:
        specs = [pl.BlockSpec((CTX_PER_STEP,) + shp[1:], index_map(shp, 0)) for shp in shapes]
        return [], specs, {}, ()
    specs = [pl.BlockSpec((CTX_PER_STEP, None) + shp[2:], index_map(shp, j)) for shp in shapes]
    in_specs = [pl.BlockSpec(memory_space=pl.ANY) for _ in shapes]
    aliases = {n_inputs + i: n_outputs + i for i in range(len(shapes))}
    return in_specs, specs, aliases, tuple(states)


def _finish(heads, w_out_ref, x_ref, mods_ref, g_post_ref, row, o_ref, rows=slice(None)):
    _, _, gate = _cond_row(mods_ref, row)
    og = jnp.concatenate(heads, axis=1)
    y = jnp.dot(og, w_out_ref[...], preferred_element_type=F32)
    o_ref[rows, :] = x_ref[rows, :] + gate * _rms(y, g_post_ref[...])


def _diff_lambda(lam_ref, lam_init):
    p = lam_ref[...]
    a = jnp.sum(p[0:1, :] * p[1:2, :], axis=-1, keepdims=True)
    b = jnp.sum(p[2:3, :] * p[3:4, :], axis=-1, keepdims=True)
    return jnp.exp(a) - jnp.exp(b) + lam_init


def _stack_diff_keys(k, k2_ref, r0, sk):
    rows = k.shape[0]
    lane = lax.broadcasted_iota(jnp.int32, (rows, LANES), 1)
    zero = jnp.zeros((rows, LANES), BF16)
    for h in range(HEADS):
        kh = k[:, h * LANES:(h + 1) * LANES]
        k2_ref[h, r0:r0 + rows, :] = jnp.where(lane < DA_DH, kh, zero)
        k2_ref[h, sk + r0:sk + r0 + rows, :] = jnp.where(lane >= DA_DH, kh, zero)


def _stack_values_t(vt, vt_ref, c0, cols):
    ones = _ones_rows(cols)
    for h in range(HEADS):
        vt_ref[h, 0:DA_DV, c0:c0 + cols] = vt[h * DA_DV:(h + 1) * DA_DV, :]
        vt_ref[h, DA_DV:DA_DV + V_PAD, c0:c0 + cols] = ones


def _diff_heads(q_ref, k2_ref, vt_ref, gate_ref, g_sub_ref, lam, lam_init, sk, rows=slice(None), after_issue=None,
                split_scores=False):
    g_sub = g_sub_ref[...]

    def scores(h):
        qh = q_ref[rows, h * LANES:(h + 1) * LANES]
        if split_scores:
            return tuple(lax.dot_general(k2_ref[h, c * sk:(c + 1) * sk, :], qh, _TRANS_B,
                                         preferred_element_type=F32) for c in range(2))
        s = lax.dot_general(k2_ref[h], qh, _TRANS_B, preferred_element_type=F32)
        return s[0:sk, :], s[sk:2 * sk, :]

    heads = []
    pending = [scores(h) for h in range(SCORES_AHEAD)]
    if after_issue is not None:
        after_issue()
    for h in range(HEADS):
        lo, hi = h * LANES, (h + 1) * LANES
        s = pending.pop(0)
        if h + SCORES_AHEAD < HEADS:
            pending.append(scores(h + SCORES_AHEAD))
        s0, s1 = s
        e0 = jnp.exp2(s0 - jnp.max(s0, axis=0, keepdims=True)).astype(BF16)
        e1 = jnp.exp2(s1 - jnp.max(s1, axis=0, keepdims=True)).astype(BF16)
        pv0 = jnp.dot(vt_ref[h], e0, preferred_element_type=F32)
        pv1 = jnp.dot(vt_ref[h], e1, preferred_element_type=F32)
        a0 = 1.0 / pv0[DA_DV:DA_DV + 1, :]
        a1 = lam / pv1[DA_DV:DA_DV + 1, :]
        ot = pv0[0:DA_DV, :] * a0 - pv1[0:DA_DV, :] * a1
        ot = ot * lax.rsqrt(jnp.mean(ot * ot, axis=0, keepdims=True) + NORM_EPS)
        o = (ot.T * g_sub) * (1.0 - lam_init)
        heads.append((o * gate_ref[rows, lo:hi]).astype(BF16))
    return heads


MLA_IN = MLA_Q_LORA + MLA_KV_LORA + MLA_ROPE + D_MODEL
_QA = slice(0, MLA_Q_LORA)
_KV = slice(MLA_Q_LORA, MLA_Q_LORA + MLA_KV_LORA)
_PE = slice(_KV.stop, _KV.stop + LANES)
_GT = slice(_KV.stop + MLA_ROPE, MLA_IN)


def _proj_t(hb, w_t_ref, rows):
    return lax.dot_general(hb, w_t_ref[rows, :], _TRANS_B, preferred_element_type=F32)


def _mla_queries(hb, w_in_ref, g_qa_ref):
    q_a = _proj_t(hb, w_in_ref, _QA)
    return _rms(q_a, g_qa_ref[...]).astype(BF16)


def _mla_expand(ckv, kpe, w_kvb_ref, kf_ref, vf_ref, r0, rows):
    zero = jnp.zeros((rows, LANES - MLA_ROPE), BF16)
    for h in range(HEADS):
        lo = h * (MLA_NOPE + MLA_DV)
        kv = jnp.dot(ckv, w_kvb_ref[:, lo:lo + MLA_NOPE + MLA_DV], preferred_element_type=F32)
        kf_ref[h, r0:r0 + rows, 0:MLA_NOPE] = kv[:, 0:MLA_NOPE].astype(BF16)
        kf_ref[h, r0:r0 + rows, MLA_NOPE:MLA_NOPE + MLA_ROPE] = kpe
        kf_ref[h, r0:r0 + rows, MLA_NOPE + MLA_ROPE:MLA_QK_PAD] = zero
        vf_ref[h, r0:r0 + rows, :] = kv[:, MLA_NOPE:MLA_NOPE + MLA_DV].astype(BF16)


def _mla_heads(q_ref, kf_ref, vf_ref, gate_ref, rows=slice(None), after_issue=None):
    def scores(h):
        return lax.dot_general(q_ref[rows, h * MLA_QK_PAD:(h + 1) * MLA_QK_PAD], kf_ref[h], _TRANS_B,
                               preferred_element_type=F32)

    heads = []
    pending = [scores(h) for h in range(SCORES_AHEAD)]
    if after_issue is not None:
        after_issue()
    for h in range(HEADS):
        s = pending.pop(0)
        if h + SCORES_AHEAD < HEADS:
            pending.append(scores(h + SCORES_AHEAD))
        e = jnp.exp2(s - jnp.max(s, axis=-1, keepdims=True))
        inv = 1.0 / jnp.sum(e, axis=-1, keepdims=True)
        o = jnp.dot(e.astype(BF16), vf_ref[h], preferred_element_type=F32) * inv
        lo, hi = h * LANES, (h + 1) * LANES
        heads.append((o * gate_ref[rows, lo:hi]).astype(BF16))
    return heads


def _split_layer_refs(refs, runners):
    scratch = refs[-2:]
    refs = refs[:-2]
    n_jobs = len(runners)
    job_out = refs[len(refs) - n_jobs:]
    main = refs[len(refs) - n_jobs - 3:len(refs) - n_jobs]
    pos = len(refs) - n_jobs - 3 - sum(n for n, _ in runners)
    for (n, run), out in zip(runners, job_out):
        run(*refs[pos:pos + n], out)
        pos += n
    return main, scratch


def _layer_diff_ctx_kernel(create_j, lam_init, runners, x_ref, mods_ref, g_pre_ref, w_in_ref, g_post_ref,
                           w_out_ref, g_sub_ref, lam_ref, *refs):
    (o_ref, skt_ref, sv_ref), (k2_ref, vb_ref) = _split_layer_refs(refs, runners)
    hb = _modulated(x_ref, mods_ref, g_pre_ref, 0)
    n = D_MODEL
    q = (jnp.dot(hb, w_in_ref[:, 0:n], preferred_element_type=F32) * DA_Q_SCALE).astype(BF16)
    k = jnp.dot(hb, w_in_ref[:, n:2 * n], preferred_element_type=F32)
    v = jnp.dot(hb, w_in_ref[:, 2 * n:3 * n], preferred_element_type=F32)
    gate = _silu(jnp.dot(hb, w_in_ref[:, 3 * n:4 * n], preferred_element_type=F32)).astype(BF16)
    kb = k.astype(BF16)
    lam = _diff_lambda(lam_ref, lam_init)
    finish = None
    for r in range(CTX_PER_STEP):
        rows = slice(r * CTX_S, (r + 1) * CTX_S)
        vt = v[rows, :].T
        _write_state(skt_ref, r, create_j, k[rows, :].T)
        _write_state(sv_ref, r, create_j, v[rows, :].reshape(CTX_S, HEADS, DA_DV))
        _stack_diff_keys(kb[rows, :], k2_ref.at[r], 0, CTX_S)
        _stack_values_t(vt.astype(BF16), vb_ref.at[r], 0, CTX_S)
        heads = _diff_heads(q, k2_ref.at[r], vb_ref.at[r], gate, g_sub_ref, lam, lam_init, CTX_S, rows, finish,
                            split_scores=True)
        finish = functools.partial(_finish, heads, w_out_ref, x_ref, mods_ref, g_post_ref, 0, o_ref, rows)
    finish()


def _layer_diff_ctx(x, mods, mi, g_pre, g_post, w_in, wj, w_out, wo, g_sub, lam_p, states, layer, j, n_layers,
                    side_jobs=()):
    lam_init = 0.8 - 0.6 * math.exp(-0.3 * layer)
    shapes = [(N_CTX_B, n_layers, D_MODEL, CTX_S), (N_CTX_B, n_layers, CTX_S, HEADS, DA_DV)]
    st_in_specs, st_out_specs, aliases, st_args = _state_plumbing(states, shapes, j, 8, 1)
    cj_in, cj_out, cj_shapes, cj_args, runners = _side_job_specs(side_jobs)
    tok_spec = pl.BlockSpec((CTX_PER_STEP * CTX_S, D_MODEL), lambda t: (t, 0))
    return pl.pallas_call(
        _with_rows(functools.partial(_layer_diff_ctx_kernel, j if states is None else None, lam_init, runners),
                   {2: layer, 4: layer}),
        grid=(N_CTX_B // CTX_PER_STEP,),
        in_specs=[
            tok_spec,
            pl.BlockSpec((None, COND_ROWS, 3 * D_MODEL), lambda t: (mi, 0, 0)),
            pl.BlockSpec(g_pre.shape, lambda t: (0, 0)),
            pl.BlockSpec((None, D_MODEL, 4 * D_MODEL), lambda t: (wj, 0, 0)),
            pl.BlockSpec(g_post.shape, lambda t: (0, 0)),
            pl.BlockSpec((None, D_MODEL, D_MODEL), lambda t: (wo, 0, 0)),
            pl.BlockSpec((None, 1, DA_DV), lambda t: (j, 0, 0)),
            pl.BlockSpec((None, 4, DA_DH), lambda t: (j, 0, 0)),
        ] + st_in_specs + cj_in,
        out_specs=[tok_spec] + st_out_specs + cj_out,
        out_shape=[jax.ShapeDtypeStruct(x.shape, F32)] + [jax.ShapeDtypeStruct(shp, F32) for shp in shapes]
        + cj_shapes,
        input_output_aliases=aliases,
        scratch_shapes=[
            pltpu.VMEM((CTX_PER_STEP, HEADS, 2 * CTX_S, LANES), BF16),
            pltpu.VMEM((CTX_PER_STEP, HEADS, DA_DV + V_PAD, CTX_S), BF16),
        ],
        compiler_params=pltpu.CompilerParams(
            dimension_semantics=("arbitrary",), vmem_limit_bytes=VMEM_LIMIT),
        name="layer_diff_ctx",
    )(x, mods, g_pre, w_in, g_post, w_out, g_sub, lam_p, *st_args, *cj_args)


def _layer_mla_ctx_kernel(create_j, runners, x_ref, mods_ref, g_pre_ref, w_in_ref, g_qa_ref, w_qb_ref, g_kva_ref,
                          g_post_ref, w_out_ref, w_kvb_ref, *refs):
    (o_ref, ckv_ref, kpe_ref), (kf_ref, vf_ref) = _split_layer_refs(refs, runners)
    hb = _modulated(x_ref, mods_ref, g_pre_ref, 0)
    qn = _mla_queries(hb, w_in_ref, g_qa_ref)
    q = (jnp.dot(qn, w_qb_ref[...], preferred_element_type=F32) * MLA_Q_SCALE).astype(BF16)
    ckv = _rms(_proj_t(hb, w_in_ref, _KV), g_kva_ref[...])
    gate = _silu(_proj_t(hb, w_in_ref, _GT)).astype(BF16)
    pe = _proj_t(hb, w_in_ref, _PE)
    finish = None
    for r in range(CTX_PER_STEP):
        rows = slice(r * CTX_S, (r + 1) * CTX_S)
        kpe = pe[rows, 0:MLA_ROPE]
        _write_state(ckv_ref, r, create_j, ckv[rows, :])
        _write_state(kpe_ref, r, create_j, kpe.T)
        _mla_expand(ckv[rows, :].astype(BF16), kpe.astype(BF16), w_kvb_ref, kf_ref.at[r], vf_ref.at[r],
                    0, CTX_S)
        heads = _mla_heads(q, kf_ref.at[r], vf_ref.at[r], gate, rows, finish)
        finish = functools.partial(_finish, heads, w_out_ref, x_ref, mods_ref, g_post_ref, 0, o_ref, rows)
    finish()


def _layer_mla_ctx(x, mods, mi, g_pre, g_post, w_in, wj, g_qa, w_qb, g_kva, w_out, wo, w_kvb, states, layer, j,
                   n_layers, side_jobs=()):
    shapes = [(N_CTX_B, n_layers, CTX_S, MLA_KV_LORA), (N_CTX_B, n_layers, MLA_ROPE, CTX_S)]
    st_in_specs, st_out_specs, aliases, st_args = _state_plumbing(states, shapes, j, 10, 1)
    cj_in, cj_out, cj_shapes, cj_args, runners = _side_job_specs(side_jobs)
    tok_spec = pl.BlockSpec((CTX_PER_STEP * CTX_S, D_MODEL), lambda t: (t, 0))
    return pl.pallas_call(
        _with_rows(functools.partial(_layer_mla_ctx_kernel, j if states is None else None, runners),
                   {2: layer, 4: j, 6: j, 7: layer}),
        grid=(N_CTX_B // CTX_PER_STEP,),
        in_specs=[
            tok_spec,
            pl.BlockSpec((None, COND_ROWS, 3 * D_MODEL), lambda t: (mi, 0, 0)),
            pl.BlockSpec(g_pre.shape, lambda t: (0, 0)),
            pl.BlockSpec((None, MLA_IN, D_MODEL), lambda t: (wj, 0, 0)),
            pl.BlockSpec(g_qa.shape, lambda t: (0, 0)),
            pl.BlockSpec((None, MLA_Q_LORA, HEADS * MLA_QK_PAD), lambda t: (j, 0, 0)),
            pl.BlockSpec(g_kva.shape, lambda t: (0, 0)),
            pl.BlockSpec(g_post.shape, lambda t: (0, 0)),
            pl.BlockSpec((None, D_MODEL, D_MODEL), lambda t: (wo, 0, 0)),
            pl.BlockSpec((None, MLA_KV_LORA, HEADS * (MLA_NOPE + MLA_DV)), lambda t: (j, 0, 0)),
        ] + st_in_specs + cj_in,
        out_specs=[tok_spec] + st_out_specs + cj_out,
        out_shape=[jax.ShapeDtypeStruct(x.shape, F32)] + [jax.ShapeDtypeStruct(shp, F32) for shp in shapes]
        + cj_shapes,
        input_output_aliases=aliases,
        scratch_shapes=[
            pltpu.VMEM((CTX_PER_STEP, HEADS, CTX_S, MLA_QK_PAD), BF16),
            pltpu.VMEM((CTX_PER_STEP, HEADS, CTX_S, MLA_DV), BF16),
        ],
        compiler_params=pltpu.CompilerParams(
            dimension_semantics=("arbitrary",), vmem_limit_bytes=VMEM_LIMIT),
        name="layer_mla_ctx",
    )(x, mods, g_pre, w_in, g_qa, w_qb, g_kva, g_post, w_out, w_kvb, *st_args, *cj_args)


LAT_TILES = LAT_S // TM
LAT_QBLOCKS = LAT_S // TQ


def _lat_specs():
    tile = lambda s: jnp.minimum(s, LAT_TILES - 1)
    qblk = lambda s: jnp.maximum(s - LAT_TILES, 0)
    x_proj = pl.BlockSpec((TM, D_MODEL), lambda b, s: (b * LAT_TILES + tile(s), 0))
    x_attn = pl.BlockSpec((TQ, D_MODEL), lambda b, s: (b * LAT_QBLOCKS + qblk(s), 0))
    rope = pl.BlockSpec((TM, LANES), lambda b, s: (tile(s), 0))
    return x_proj, x_attn, rope


def _query_rows():
    i = pl.program_id(1) - LAT_TILES
    return pl.ds(pl.multiple_of(i * TQ, TQ), TQ)


def _layer_diff_lat_kernel(lam_init, xp_ref, xa_ref, mods_ref, g_pre_ref, w_in_ref, cos_ref, sin_ref,
                           ckt_ref, cv_ref, g_post_ref, w_out_ref, g_sub_ref, lam_ref,
                           o_ref, k2_ref, vb_ref, q_s, gate_s):
    step = pl.program_id(1)
    row = 1 + pl.program_id(0)
    sk = LAT_S + PAST

    for t in range(LAT_TILES):
        @pl.when(step == t)
        def _(t=t):
            r0 = t * TM
            hb = _modulated(xp_ref, mods_ref, g_pre_ref, row)
            cos = cos_ref[...]
            sin = sin_ref[...]
            n = D_MODEL
            q = jnp.dot(hb, w_in_ref[:, 0:n], preferred_element_type=F32)
            k = jnp.dot(hb, w_in_ref[:, n:2 * n], preferred_element_type=F32)
            cos_q = cos * DA_Q_SCALE
            sin_q = sin * DA_Q_SCALE
            slabs = [slice(h * LANES, (h + 1) * LANES) for h in range(HEADS)]
            q_s[r0:r0 + TM, :] = jnp.concatenate(
                [_rope_slab(q[:, sl], cos_q, sin_q).astype(BF16) for sl in slabs], axis=1)
            kb = jnp.concatenate([_rope_slab(k[:, sl], cos, sin).astype(BF16) for sl in slabs], axis=1)
            _stack_diff_keys(kb, k2_ref, r0, sk)
            vt = jnp.dot(hb, w_in_ref[:, 2 * n:3 * n], preferred_element_type=F32).T.astype(BF16)
            _stack_values_t(vt, vb_ref, r0, TM)
            gate_s[r0:r0 + TM, :] = _silu(
                jnp.dot(hb, w_in_ref[:, 3 * n:4 * n], preferred_element_type=F32)).astype(BF16)

    @pl.when(step == 0)
    def _():
        lane = lax.broadcasted_iota(jnp.int32, (PAST, LANES), 1)
        zero = jnp.zeros((PAST, LANES), BF16)
        ones = _ones_rows(PAST)
        for h in range(HEADS):
            ch = ckt_ref[h * LANES:(h + 1) * LANES, :].T.astype(BF16)
            k2_ref[h, LAT_S:sk, :] = jnp.where(lane < DA_DH, ch, zero)
            k2_ref[h, sk + LAT_S:2 * sk, :] = jnp.where(lane >= DA_DH, ch, zero)
            vb_ref[h, 0:DA_DV, LAT_S:sk] = cv_ref[:, h, :].T.astype(BF16)
            vb_ref[h, DA_DV:DA_DV + V_PAD, LAT_S:sk] = ones

    @pl.when(step >= LAT_TILES)
    def _():
        lam = _diff_lambda(lam_ref, lam_init)
        heads = _diff_heads(q_s, k2_ref, vb_ref, gate_s, g_sub_ref, lam, lam_init, sk, _query_rows())
        _finish(heads, w_out_ref, xa_ref, mods_ref, g_post_ref, row, o_ref)


def _layer_diff_lat(x, mods, mi, g_pre, g_post, w_in, wj, w_out, wo, cos_t, sin_t, cache_kt, cache_v, g_sub,
                    lam_p, layer, j):
    lam_init = 0.8 - 0.6 * math.exp(-0.3 * layer)
    sk = LAT_S + PAST
    x_proj, x_attn, rope = _lat_specs()
    return pl.pallas_call(
        _with_rows(functools.partial(_layer_diff_lat_kernel, lam_init), {3: layer, 9: layer}),
        grid=(N_LAT_B, LAT_TILES + LAT_QBLOCKS),
        in_specs=[
            x_proj, x_attn,
            pl.BlockSpec((None, COND_ROWS, 3 * D_MODEL), lambda b, s: (mi, 0, 0)),
            pl.BlockSpec(g_pre.shape, lambda b, s: (0, 0)),
            pl.BlockSpec((None, D_MODEL, 4 * D_MODEL), lambda b, s: (wj, 0, 0)),
            rope, rope,
            pl.BlockSpec((None, None, D_MODEL, PAST), lambda b, s: (b, j, 0, 0)),
            pl.BlockSpec((None, None, PAST, HEADS, DA_DV), lambda b, s: (b, j, 0, 0, 0)),
            pl.BlockSpec(g_post.shape, lambda b, s: (0, 0)),
            pl.BlockSpec((None, D_MODEL, D_MODEL), lambda b, s: (wo, 0, 0)),
            pl.BlockSpec((None, 1, DA_DV), lambda b, s: (j, 0, 0)),
            pl.BlockSpec((None, 4, DA_DH), lambda b, s: (j, 0, 0)),
        ],
        out_specs=x_attn,
        out_shape=jax.ShapeDtypeStruct(x.shape, F32),
        scratch_shapes=[
            pltpu.VMEM((HEADS, 2 * sk, LANES), BF16),
            pltpu.VMEM((HEADS, DA_DV + V_PAD, sk), BF16),
            pltpu.VMEM((LAT_S, D_MODEL), BF16),
            pltpu.VMEM((LAT_S, D_MODEL), BF16),
        ],
        compiler_params=pltpu.CompilerParams(
            dimension_semantics=("arbitrary", "arbitrary"), vmem_limit_bytes=VMEM_LIMIT),
        name="layer_diff_lat",
    )(x, x, mods, g_pre, w_in, cos_t, sin_t, cache_kt, cache_v, g_post, w_out, g_sub, lam_p)


def _layer_mla_lat_kernel(xp_ref, xa_ref, mods_ref, g_pre_ref, w_in_ref, g_qa_ref, w_qb_ref, g_kva_ref,
                          cos_ref, sin_ref, cckv_ref, ckpe_ref, g_post_ref, w_out_ref, w_kvb_ref,
                          o_ref, kf_ref, vf_ref, q_s, gate_s):
    step = pl.program_id(1)
    row = 1 + pl.program_id(0)

    for t in range(LAT_TILES):
        @pl.when(step == t)
        def _(t=t):
            r0 = t * TM
            hb = _modulated(xp_ref, mods_ref, g_pre_ref, row)
            cos = cos_ref[...]
            sin = sin_ref[...]
            qn = _mla_queries(hb, w_in_ref, g_qa_ref)
            for h in range(HEADS):
                lo = h * MLA_QK_PAD
                q = jnp.dot(qn, w_qb_ref[:, lo:lo + MLA_QK_PAD], preferred_element_type=F32) * MLA_Q_SCALE
                q_s[r0:r0 + TM, lo:lo + LANES] = q[:, 0:LANES].astype(BF16)
                q_s[r0:r0 + TM, lo + LANES:lo + 2 * LANES] = _rope_slab(
                    q[:, LANES:2 * LANES], cos, sin).astype(BF16)
            ckv = _rms(_proj_t(hb, w_in_ref, _KV), g_kva_ref[...]).astype(BF16)
            gate_s[r0:r0 + TM, :] = _silu(_proj_t(hb, w_in_ref, _GT)).astype(BF16)
            kpe = _rope_slab(_proj_t(hb, w_in_ref, _PE), cos, sin)[:, 0:MLA_ROPE].astype(BF16)
            _mla_expand(ckv, kpe, w_kvb_ref, kf_ref, vf_ref, r0, TM)

    @pl.when(step == 0)
    def _():
        _mla_expand(cckv_ref[...].astype(BF16), ckpe_ref[...].T.astype(BF16), w_kvb_ref, kf_ref, vf_ref,
                    LAT_S, PAST)

    @pl.when(step >= LAT_TILES)
    def _():
        heads = _mla_heads(q_s, kf_ref, vf_ref, gate_s, _query_rows())
        _finish(heads, w_out_ref, xa_ref, mods_ref, g_post_ref, row, o_ref)


def _layer_mla_lat(x, mods, mi, g_pre, g_post, w_in, wj, g_qa, w_qb, g_kva, w_out, wo, w_kvb, cos_t, sin_t,
                   cache_ckv, cache_kpe, layer, j):
    sk = LAT_S + PAST
    x_proj, x_attn, rope = _lat_specs()
    return pl.pallas_call(
        _with_rows(_layer_mla_lat_kernel, {3: layer, 5: j, 7: j, 12: layer}),
        grid=(N_LAT_B, LAT_TILES + LAT_QBLOCKS),
        in_specs=[
            x_proj, x_attn,
            pl.BlockSpec((None, COND_ROWS, 3 * D_MODEL), lambda b, s: (mi, 0, 0)),
            pl.BlockSpec(g_pre.shape, lambda b, s: (0, 0)),
            pl.BlockSpec((None, MLA_IN, D_MODEL), lambda b, s: (wj, 0, 0)),
            pl.BlockSpec(g_qa.shape, lambda b, s: (0, 0)),
            pl.BlockSpec((None, MLA_Q_LORA, HEADS * MLA_QK_PAD), lambda b, s: (j, 0, 0)),
            pl.BlockSpec(g_kva.shape, lambda b, s: (0, 0)),
            rope, rope,
            pl.BlockSpec((None, None, PAST, MLA_KV_LORA), lambda b, s: (b, j, 0, 0)),
            pl.BlockSpec((None, None, MLA_ROPE, PAST), lambda b, s: (b, j, 0, 0)),
            pl.BlockSpec(g_post.shape, lambda b, s: (0, 0)),
            pl.BlockSpec((None, D_MODEL, D_MODEL), lambda b, s: (wo, 0, 0)),
            pl.BlockSpec((None, MLA_KV_LORA, HEADS * (MLA_NOPE + MLA_DV)), lambda b, s: (j, 0, 0)),
        ],
        out_specs=x_attn,
        out_shape=jax.ShapeDtypeStruct(x.shape, F32),
        scratch_shapes=[
            pltpu.VMEM((HEADS, sk, MLA_QK_PAD), BF16),
            pltpu.VMEM((HEADS, sk, MLA_DV), BF16),
            pltpu.VMEM((LAT_S, HEADS * MLA_QK_PAD), BF16),
            pltpu.VMEM((LAT_S, D_MODEL), BF16),
        ],
        compiler_params=pltpu.CompilerParams(
            dimension_semantics=("arbitrary", "arbitrary"), vmem_limit_bytes=VMEM_LIMIT),
        name="layer_mla_lat",
    )(x, x, mods, g_pre, w_in, g_qa, w_qb, g_kva, cos_t, sin_t, cache_ckv, cache_kpe, g_post, w_out, w_kvb)


def _rope_tables():
    rows = LAT_S // GRID_W
    row = np.repeat(np.arange(rows), GRID_W).astype(np.float64)
    col = np.tile(np.arange(GRID_W), rows).astype(np.float64)
    n_freq = DA_DH // 4
    inv = ROPE_THETA ** (-np.arange(n_freq, dtype=np.float64) / n_freq)
    ang = np.concatenate([row[:, None] * inv, col[:, None] * inv], axis=-1)
    cos, sin = np.cos(ang), np.sin(ang)
    return (jnp.asarray(np.concatenate([cos, cos, cos, cos], axis=-1), F32),
            jnp.asarray(np.concatenate([-sin, sin, -sin, sin], axis=-1), F32))


def kernel(x_prompt, x_sample, cache_diff_k, cache_diff_v, cache_mla_ckv, cache_mla_kpe,
           c, c_ctx, w_ada, b_ada, g_pre, g_post, w_out,
           da_w_in, da_lam_q1, da_lam_k1, da_lam_q2, da_lam_k2, da_g_sub,
           mla_w_in, mla_g_qa, mla_w_qb, mla_g_kva, mla_w_kvb):
    assert DA_DH == MLA_ROPE
    assert x_prompt.shape == (N_CTX_B, CTX_S, D_MODEL) and x_sample.shape == (N_LAT_B, LAT_S, D_MODEL)
    assert cache_diff_k.shape == (N_LAT_B, (DEPTH + 1) // 2, PAST, HEADS, 2, DA_DH)
    assert cache_mla_ckv.shape == (N_LAT_B, DEPTH // 2, PAST, MLA_KV_LORA)
    assert w_ada.shape == (DEPTH, D_MODEL, 3 * D_MODEL) and da_w_in.shape[1:] == (D_MODEL, 4 * D_MODEL)
    n_diff = (DEPTH + 1) // 2
    n_mla = DEPTH // 2

    cond = jnp.concatenate(
        [c_ctx[None, :], c, jnp.zeros((COND_ROWS - 1 - N_LAT_B, D_MODEL), F32)], axis=0)
    mods_first = _ada_call(cond, w_ada, b_ada, 1)

    cos_t, sin_t = _rope_tables()
    steps = N_CTX_B // CTX_PER_STEP
    w_out_first = w_out[0:1].astype(BF16)
    da_w_in_first = da_w_in[0:1].astype(BF16)
    g_sub3 = da_g_sub.reshape(n_diff, 1, DA_DV)
    lam_p = jnp.stack([da_lam_q1, da_lam_k1, da_lam_q2, da_lam_k2], axis=1)
    cache_dkt = jnp.transpose(cache_diff_k, (0, 1, 3, 4, 5, 2)).reshape(N_LAT_B, n_diff, D_MODEL, PAST)

    mla_w_in_t = jnp.swapaxes(mla_w_in, 1, 2)
    cache_kpe_t = jnp.swapaxes(cache_mla_kpe, 2, 3)

    x_ctx = x_prompt.reshape(N_CTX_B * CTX_S, D_MODEL)
    x_lat = x_sample.reshape(N_LAT_B * LAT_S, D_MODEL)
    st_diff = None
    st_mla = None

    assert DEPTH == 4
    da_w = [(da_w_in_first, 0), None]
    mla_w = [None, None]
    w_o = [(w_out_first, 0), None, None, None]
    wide = D_MODEL * 4 // steps
    cols = D_MODEL // steps
    per_layer = steps // 2
    kv_cols = HEADS * (MLA_NOPE + MLA_DV) // steps

    mods = [(mods_first, 0), None, None, None]

    def mla_in_cast(jj):
        return _cast_job(mla_w_in_t, (None, MLA_IN, cols), lambda t: (jj, 0, t),
                         (1, MLA_IN, D_MODEL), lambda t: (0, 0, t))

    side_jobs = {
        0: [mla_in_cast(0),
            _cast_job(w_out, (None, D_MODEL // steps, D_MODEL), lambda t: (1, t, 0),
                      (1, D_MODEL, D_MODEL), lambda t: (0, t, 0)),
            _cast_job(mla_w_kvb, (n_mla, MLA_KV_LORA, kv_cols), lambda t: (0, 0, t),
                      mla_w_kvb.shape, lambda t: (0, 0, t)),
            _pad_heads_job(mla_w_qb, steps),
            _ada_job(cond, w_ada, b_ada, 1, steps)],
        1: [mla_in_cast(1),
            _cast_job(w_out, (None, D_MODEL // per_layer, D_MODEL),
                      lambda t: (2 + t // per_layer, t % per_layer, 0),
                      (2, D_MODEL, D_MODEL), lambda t: (t // per_layer, t % per_layer, 0)),
            _ada_job(cond, w_ada, b_ada, 2, steps),
            _cast_job(da_w_in, (None, D_MODEL, wide), lambda t: (1, 0, t),
                      (1, D_MODEL, 4 * D_MODEL), lambda t: (0, 0, t))],
        2: [_ada_job(cond, w_ada, b_ada, 3, steps)],
        3: [],
    }

    for i in range(DEPTH):
        j = i // 2
        if i % 2 == 0:
            w_j = da_w[j]
            outs = _layer_diff_ctx(x_ctx, *mods[i], g_pre, g_post, *w_j, *w_o[i], g_sub3, lam_p, st_diff,
                                   i, j, n_diff, side_jobs=side_jobs[i])
            x_ctx, st_dkt, st_dv = outs[0:3]
            st_diff = (st_dkt, st_dv)
            if i == 0:
                mla_w[0], w_o[1], w_kvb_b, w_qb_b = (outs[3], 0), (outs[4], 0), outs[5], outs[6]
                mods[1] = (outs[7], 0)
            else:
                mods[3] = (outs[3], 0)
            x_lat = _layer_diff_lat(x_lat, *mods[i], g_pre, g_post, *w_j, *w_o[i], cos_t, sin_t,
                                    cache_dkt, cache_diff_v, g_sub3, lam_p, i, j)
        else:
            w_j = mla_w[j]
            outs = _layer_mla_ctx(x_ctx, *mods[i], g_pre, g_post, *w_j, mla_g_qa, w_qb_b, mla_g_kva, *w_o[i], w_kvb_b,
                                  st_mla, i, j, n_mla, side_jobs=side_jobs[i])
            x_ctx, st_ckv, st_kpe = outs[0:3]
            st_mla = (st_ckv, st_kpe)
            if i == 1:
                mla_w[1] = (outs[3], 0)
                w_o[2], w_o[3] = (outs[4], 0), (outs[4], 1)
                mods[2] = (outs[5], 0)
                da_w[1] = (outs[6], 0)
            x_lat = _layer_mla_lat(x_lat, *mods[i], g_pre, g_post, *w_j, mla_g_qa, w_qb_b, mla_g_kva, *w_o[i], w_kvb_b,
                                   cos_t, sin_t, cache_mla_ckv, cache_kpe_t, i, j)

    return (x_ctx.reshape(N_CTX_B, CTX_S, D_MODEL),
            x_lat.reshape(N_LAT_B, LAT_S, D_MODEL),
            jnp.transpose(st_dkt.reshape(N_CTX_B, n_diff, HEADS, 2, DA_DH, CTX_S), (0, 1, 5, 2, 3, 4)),
            st_dv,
            st_ckv,
            jnp.swapaxes(st_kpe, 2, 3))
```

```python
import functools
import math

import jax
import jax.numpy as jnp
import numpy as np
from jax import lax
from jax.experimental import pallas as pl
from jax.experimental.pallas import tpu as pltpu

F32 = jnp.float32
BF16 = jnp.bfloat16

D_MODEL = 1024
DEPTH = 4
N_CTX_B = 16
CTX_S = 256
N_LAT_B = 4
LAT_S = 1024
PAST = 256
GRID_W = 64
HEADS = 8
DA_DH = 64
DA_DV = 128
MLA_Q_LORA = 384
MLA_KV_LORA = 256
MLA_NOPE = 128
MLA_ROPE = 64
MLA_DV = 128
MLA_QK_PAD = 256
ROPE_THETA = 10000.0
NORM_EPS = 1e-6
COND_ROWS = 8

LANES = 128
V7X_VMEM_MIB = 64
V_PAD = 16
TQ = 256
TM = 1024
VMEM_LIMIT = (V7X_VMEM_MIB - 8) * 1024 * 1024

_TRANS_B = (((1,), (1,)), ((), ()))
SCORES_AHEAD = 2
CTX_PER_STEP = 2
DA_Q_SCALE = DA_DH ** -0.5 * math.log2(math.e)
MLA_Q_SCALE = (MLA_NOPE + MLA_ROPE) ** -0.5 * math.log2(math.e)


def _silu(x):
    return x * (1.0 / (1.0 + jnp.exp(-x)))


def _rms(x, g):
    r = lax.rsqrt(jnp.mean(x * x, axis=-1, keepdims=True) + NORM_EPS)
    return (x * r) * g


def _rope_slab(x, cos, sin_signed):
    half = DA_DH // 2
    lane = lax.broadcasted_iota(jnp.int32, x.shape, 1)
    first_half = (lane % DA_DH) < half
    partner = jnp.where(first_half, pltpu.roll(x, LANES - half, axis=1), pltpu.roll(x, half, axis=1))
    return x * cos + partner * sin_signed


def _ones_rows(cols):
    row = lax.broadcasted_iota(jnp.int32, (V_PAD, cols), 0)
    return jnp.where(row == 0, 1.0, 0.0).astype(BF16)


def _cond_row(mods_ref, row):
    m = mods_ref[pl.ds(row, 1), :]
    return m[:, 0:D_MODEL], m[:, D_MODEL:2 * D_MODEL], m[:, 2 * D_MODEL:3 * D_MODEL]


def _modulated(x_ref, mods_ref, g_ref, row):
    shift, scale, _ = _cond_row(mods_ref, row)
    h = _rms(x_ref[...], g_ref[...]) * (1.0 + scale) + shift
    return h.astype(BF16)


def _with_rows(kernel_fn, picks):
    def run(*refs):
        refs = list(refs)
        for n, row in picks.items():
            refs[n] = refs[n].at[pl.ds(row, 1)]
        return kernel_fn(*refs)
    return run


def _ada_kernel(layer, cond_ref, w_ref, b_ref, o_ref):
    a = _silu(cond_ref[...]).astype(BF16)
    o_ref[...] = jnp.dot(a, w_ref[...].astype(BF16), preferred_element_type=F32) + b_ref[pl.ds(layer, 1), :]


def _cast_block(src_ref, dst_ref):
    dst_ref[...] = src_ref[...].astype(BF16)


def _cast_job(array, blk, src_map, shape, dst_map):
    return dict(args=[array], in_specs=[pl.BlockSpec(blk, src_map)], out_spec=pl.BlockSpec(blk, dst_map),
                out_shape=jax.ShapeDtypeStruct(shape, BF16), run=_cast_block)


def _pad_heads_block(src_ref, dst_ref):
    qk = MLA_NOPE + MLA_ROPE
    zero = jnp.zeros(src_ref.shape[:2] + (MLA_QK_PAD - qk,), BF16)
    for h in range(src_ref.shape[2] // qk):
        lo = h * MLA_QK_PAD
        dst_ref[:, :, lo:lo + qk] = src_ref[:, :, h * qk:(h + 1) * qk].astype(BF16)
        dst_ref[:, :, lo + qk:lo + MLA_QK_PAD] = zero


def _pad_heads_job(w_qb, steps):
    n, rows, _ = w_qb.shape
    span = steps // (HEADS // 2)
    return dict(args=[w_qb],
                in_specs=[pl.BlockSpec((n, rows, 2 * (MLA_NOPE + MLA_ROPE)), lambda t: (0, 0, t // span))],
                out_spec=pl.BlockSpec((n, rows, 2 * MLA_QK_PAD), lambda t: (0, 0, t // span)),
                out_shape=jax.ShapeDtypeStruct((n, rows, HEADS * MLA_QK_PAD), BF16), run=_pad_heads_block)


def _ada_job(cond, w_ada, b_ada, layer, steps):
    tn = 3 * D_MODEL // steps
    return dict(args=[cond, w_ada, b_ada],
                in_specs=[pl.BlockSpec((COND_ROWS, D_MODEL), lambda t: (0, 0)),
                          pl.BlockSpec((None, D_MODEL, tn), lambda t: (layer, 0, t)),
                          pl.BlockSpec((b_ada.shape[0], tn), lambda t: (0, t))],
                out_spec=pl.BlockSpec((None, COND_ROWS, tn), lambda t: (0, 0, t)),
                out_shape=jax.ShapeDtypeStruct((1, COND_ROWS, 3 * D_MODEL), F32),
                run=functools.partial(_ada_kernel, layer))


def _side_job_specs(jobs):
    in_specs = [spec for job in jobs for spec in job["in_specs"]]
    args = [a for job in jobs for a in job["args"]]
    runners = tuple((len(job["args"]), job["run"]) for job in jobs)
    return in_specs, [job["out_spec"] for job in jobs], [job["out_shape"] for job in jobs], args, runners


def _jobs_kernel(runners, *refs):
    outs = refs[len(refs) - len(runners):]
    pos = 0
    for (n, run), out in zip(runners, outs):
        run(*refs[pos:pos + n], out)
        pos += n


def _jobs_call(jobs, steps):
    in_specs, out_specs, out_shapes, args, runners = _side_job_specs(jobs)
    return pl.pallas_call(
        functools.partial(_jobs_kernel, runners),
        grid=(steps,),
        in_specs=in_specs,
        out_specs=out_specs,
        out_shape=out_shapes,
        compiler_params=pltpu.CompilerParams(
            dimension_semantics=("arbitrary",), vmem_limit_bytes=VMEM_LIMIT),
        name="first_layer_prep",
    )(*args)


def _write_state(ref, r, j, value):
    if j is None:
        ref[r] = value
    else:
        for jj in range(ref.shape[1]):
            ref[r, jj] = value if jj == j else jnp.zeros_like(value)


def _state_plumbing(states, shapes, j, n_inputs, n_outputs):
    def index_map(shp, layer_index):
        return lambda t: (t, layer_index) + (0,) * (len(shp) - 2)

    if states is None:
        specs = [pl.BlockSpec((CTX_PER_STEP,) + shp[1:], index_map(shp, 0)) for shp in shapes]
        return [], specs, {}, ()
    specs = [pl.BlockSpec((CTX_PER_STEP, None) + shp[2:], index_map(shp, j)) for shp in shapes]
    in_specs = [pl.BlockSpec(memory_space=pl.ANY) for _ in shapes]
    aliases = {n_inputs + i: n_outputs + i for i in range(len(shapes))}
    return in_specs, specs, aliases, tuple(states)


def _finish(heads, w_out_ref, x_ref, mods_ref, g_post_ref, row, o_ref, rows=slice(None)):
    _, _, gate = _cond_row(mods_ref, row)
    og = jnp.concatenate(heads, axis=1)
    y = jnp.dot(og, w_out_ref[...], preferred_element_type=F32)
    o_ref[rows, :] = x_ref[rows, :] + gate * _rms(y, g_post_ref[...])


def _diff_lambda(lam_ref, lam_init):
    p = lam_ref[...]
    a = jnp.sum(p[0:1, :] * p[1:2, :], axis=-1, keepdims=True)
    b = jnp.sum(p[2:3, :] * p[3:4, :], axis=-1, keepdims=True)
    return jnp.exp(a) - jnp.exp(b) + lam_init


def _stack_diff_keys(k, k2_ref, r0, sk):
    rows = k.shape[0]
    lane = lax.broadcasted_iota(jnp.int32, (rows, LANES), 1)
    zero = jnp.zeros((rows, LANES), BF16)
    for h in range(HEADS):
        kh = k[:, h * LANES:(h + 1) * LANES]
        k2_ref[h, r0:r0 + rows, :] = jnp.where(lane < DA_DH, kh, zero)
        k2_ref[h, sk + r0:sk + r0 + rows, :] = jnp.where(lane >= DA_DH, kh, zero)


def _stack_values_t(vt, vt_ref, c0, cols):
    ones = _ones_rows(cols)
    for h in range(HEADS):
        vt_ref[h, 0:DA_DV, c0:c0 + cols] = vt[h * DA_DV:(h + 1) * DA_DV, :]
        vt_ref[h, DA_DV:DA_DV + V_PAD, c0:c0 + cols] = ones


def _diff_heads(q_ref, k2_ref, vt_ref, gate_ref, g_sub_ref, lam, lam_init, sk, rows=slice(None), after_issue=None,
                split_scores=False):
    g_sub = g_sub_ref[...]

    def scores(h):
        qh = q_ref[rows, h * LANES:(h + 1) * LANES]
        if split_scores:
            return tuple(lax.dot_general(k2_ref[h, c * sk:(c + 1) * sk, :], qh, _TRANS_B,
                                         preferred_element_type=F32) for c in range(2))
        s = lax.dot_general(k2_ref[h], qh, _TRANS_B, preferred_element_type=F32)
        return s[0:sk, :], s[sk:2 * sk, :]

    heads = []
    pending = [scores(h) for h in range(SCORES_AHEAD)]
    if after_issue is not None:
        after_issue()
    for h in range(HEADS):
        lo, hi = h * LANES, (h + 1) * LANES
        s = pending.pop(0)
        if h + SCORES_AHEAD < HEADS:
            pending.append(scores(h + SCORES_AHEAD))
        s0, s1 = s
        e0 = jnp.exp2(s0 - jnp.max(s0, axis=0, keepdims=True)).astype(BF16)
        e1 = jnp.exp2(s1 - jnp.max(s1, axis=0, keepdims=True)).astype(BF16)
        pv0 = jnp.dot(vt_ref[h], e0, preferred_element_type=F32)
        pv1 = jnp.dot(vt_ref[h], e1, preferred_element_type=F32)
        a0 = 1.0 / pv0[DA_DV:DA_DV + 1, :]
        a1 = lam / pv1[DA_DV:DA_DV + 1, :]
        ot = pv0[0:DA_DV, :] * a0 - pv1[0:DA_DV, :] * a1
        ot = ot * lax.rsqrt(jnp.mean(ot * ot, axis=0, keepdims=True) + NORM_EPS)
        o = (ot.T * g_sub) * (1.0 - lam_init)
        heads.append((o * gate_ref[rows, lo:hi]).astype(BF16))
    return heads


MLA_IN = MLA_Q_LORA + MLA_KV_LORA + MLA_ROPE + D_MODEL
_QA = slice(0, MLA_Q_LORA)
_KV = slice(MLA_Q_LORA, MLA_Q_LORA + MLA_KV_LORA)
_PE = slice(_KV.stop, _KV.stop + LANES)
_GT = slice(_KV.stop + MLA_ROPE, MLA_IN)


def _proj_t(hb, w_t_ref, rows):
    return lax.dot_general(hb, w_t_ref[rows, :], _TRANS_B, preferred_element_type=F32)


def _mla_queries(hb, w_in_ref, g_qa_ref):
    q_a = _proj_t(hb, w_in_ref, _QA)
    return _rms(q_a, g_qa_ref[...]).astype(BF16)


def _mla_expand(ckv, kpe, w_kvb_ref, kf_ref, vf_ref, r0, rows):
    zero = jnp.zeros((rows, LANES - MLA_ROPE), BF16)
    for h in range(HEADS):
        lo = h * (MLA_NOPE + MLA_DV)
        kv = jnp.dot(ckv, w_kvb_ref[:, lo:lo + MLA_NOPE + MLA_DV], preferred_element_type=F32)
        kf_ref[h, r0:r0 + rows, 0:MLA_NOPE] = kv[:, 0:MLA_NOPE].astype(BF16)
        kf_ref[h, r0:r0 + rows, MLA_NOPE:MLA_NOPE + MLA_ROPE] = kpe
        kf_ref[h, r0:r0 + rows, MLA_NOPE + MLA_ROPE:MLA_QK_PAD] = zero
        vf_ref[h, r0:r0 + rows, :] = kv[:, MLA_NOPE:MLA_NOPE + MLA_DV].astype(BF16)


def _mla_heads(q_ref, kf_ref, vf_ref, gate_ref, rows=slice(None), after_issue=None):
    def scores(h):
        return lax.dot_general(q_ref[rows, h * MLA_QK_PAD:(h + 1) * MLA_QK_PAD], kf_ref[h], _TRANS_B,
                               preferred_element_type=F32)

    heads = []
    pending = [scores(h) for h in range(SCORES_AHEAD)]
    if after_issue is not None:
        after_issue()
    for h in range(HEADS):
        s = pending.pop(0)
        if h + SCORES_AHEAD < HEADS:
            pending.append(scores(h + SCORES_AHEAD))
        e = jnp.exp2(s - jnp.max(s, axis=-1, keepdims=True))
        inv = 1.0 / jnp.sum(e, axis=-1, keepdims=True)
        o = jnp.dot(e.astype(BF16), vf_ref[h], preferred_element_type=F32) * inv
        lo, hi = h * LANES, (h + 1) * LANES
        heads.append((o * gate_ref[rows, lo:hi]).astype(BF16))
    return heads


def _split_layer_refs(refs, runners):
    scratch = refs[-2:]
    refs = refs[:-2]
    n_jobs = len(runners)
    job_out = refs[len(refs) - n_jobs:]
    main = refs[len(refs) - n_jobs - 3:len(refs) - n_jobs]
    pos = len(refs) - n_jobs - 3 - sum(n for n, _ in runners)
    for (n, run), out in zip(runners, job_out):
        run(*refs[pos:pos + n], out)
        pos += n
    return main, scratch


def _layer_diff_ctx_kernel(create_j, lam_init, runners, x_ref, mods_ref, g_pre_ref, w_in_ref, g_post_ref,
                           w_out_ref, g_sub_ref, lam_ref, *refs):
    (o_ref, skt_ref, sv_ref), (k2_ref, vb_ref) = _split_layer_refs(refs, runners)
    hb = _modulated(x_ref, mods_ref, g_pre_ref, 0)
    n = D_MODEL
    q = (jnp.dot(hb, w_in_ref[:, 0:n], preferred_element_type=F32) * DA_Q_SCALE).astype(BF16)
    k = jnp.dot(hb, w_in_ref[:, n:2 * n], preferred_element_type=F32)
    v = jnp.dot(hb, w_in_ref[:, 2 * n:3 * n], preferred_element_type=F32)
    gate = _silu(jnp.dot(hb, w_in_ref[:, 3 * n:4 * n], preferred_element_type=F32)).astype(BF16)
    kb = k.astype(BF16)
    lam = _diff_lambda(lam_ref, lam_init)
    finish = None
    for r in range(CTX_PER_STEP):
        rows = slice(r * CTX_S, (r + 1) * CTX_S)
        vt = v[rows, :].T
        _write_state(skt_ref, r, create_j, k[rows, :].T)
        _write_state(sv_ref, r, create_j, v[rows, :].reshape(CTX_S, HEADS, DA_DV))
        _stack_diff_keys(kb[rows, :], k2_ref.at[r], 0, CTX_S)
        _stack_values_t(vt.astype(BF16), vb_ref.at[r], 0, CTX_S)
        heads = _diff_heads(q, k2_ref.at[r], vb_ref.at[r], gate, g_sub_ref, lam, lam_init, CTX_S, rows, finish,
                            split_scores=True)
        finish = functools.partial(_finish, heads, w_out_ref, x_ref, mods_ref, g_post_ref, 0, o_ref, rows)
    finish()


def _layer_diff_ctx(x, mods, mi, g_pre, g_post, w_in, wj, w_out, wo, g_sub, lam_p, states, layer, j, n_layers,
                    side_jobs=()):
    lam_init = 0.8 - 0.6 * math.exp(-0.3 * layer)
    shapes = [(N_CTX_B, n_layers, D_MODEL, CTX_S), (N_CTX_B, n_layers, CTX_S, HEADS, DA_DV)]
    st_in_specs, st_out_specs, aliases, st_args = _state_plumbing(states, shapes, j, 8, 1)
    cj_in, cj_out, cj_shapes, cj_args, runners = _side_job_specs(side_jobs)
    tok_spec = pl.BlockSpec((CTX_PER_STEP * CTX_S, D_MODEL), lambda t: (t, 0))
    return pl.pallas_call(
        _with_rows(functools.partial(_layer_diff_ctx_kernel, j if states is None else None, lam_init, runners),
                   {2: layer, 4: layer}),
        grid=(N_CTX_B // CTX_PER_STEP,),
        in_specs=[
            tok_spec,
            pl.BlockSpec((None, COND_ROWS, 3 * D_MODEL), lambda t: (mi, 0, 0)),
            pl.BlockSpec(g_pre.shape, lambda t: (0, 0)),
            pl.BlockSpec((None, D_MODEL, 4 * D_MODEL), lambda t: (wj, 0, 0)),
            pl.BlockSpec(g_post.shape, lambda t: (0, 0)),
            pl.BlockSpec((None, D_MODEL, D_MODEL), lambda t: (wo, 0, 0)),
            pl.BlockSpec((None, 1, DA_DV), lambda t: (j, 0, 0)),
            pl.BlockSpec((None, 4, DA_DH), lambda t: (j, 0, 0)),
        ] + st_in_specs + cj_in,
        out_specs=[tok_spec] + st_out_specs + cj_out,
        out_shape=[jax.ShapeDtypeStruct(x.shape, F32)] + [jax.ShapeDtypeStruct(shp, F32) for shp in shapes]
        + cj_shapes,
        input_output_aliases=aliases,
        scratch_shapes=[
            pltpu.VMEM((CTX_PER_STEP, HEADS, 2 * CTX_S, LANES), BF16),
            pltpu.VMEM((CTX_PER_STEP, HEADS, DA_DV + V_PAD, CTX_S), BF16),
        ],
        compiler_params=pltpu.CompilerParams(
            dimension_semantics=("arbitrary",), vmem_limit_bytes=VMEM_LIMIT),
        name="layer_diff_ctx",
    )(x, mods, g_pre, w_in, g_post, w_out, g_sub, lam_p, *st_args, *cj_args)


def _layer_mla_ctx_kernel(create_j, runners, x_ref, mods_ref, g_pre_ref, w_in_ref, g_qa_ref, w_qb_ref, g_kva_ref,
                          g_post_ref, w_out_ref, w_kvb_ref, *refs):
    (o_ref, ckv_ref, kpe_ref), (kf_ref, vf_ref) = _split_layer_refs(refs, runners)
    hb = _modulated(x_ref, mods_ref, g_pre_ref, 0)
    qn = _mla_queries(hb, w_in_ref, g_qa_ref)
    q = (jnp.dot(qn, w_qb_ref[...], preferred_element_type=F32) * MLA_Q_SCALE).astype(BF16)
    ckv = _rms(_proj_t(hb, w_in_ref, _KV), g_kva_ref[...])
    gate = _silu(_proj_t(hb, w_in_ref, _GT)).astype(BF16)
    pe = _proj_t(hb, w_in_ref, _PE)
    finish = None
    for r in range(CTX_PER_STEP):
        rows = slice(r * CTX_S, (r + 1) * CTX_S)
        kpe = pe[rows, 0:MLA_ROPE]
        _write_state(ckv_ref, r, create_j, ckv[rows, :])
        _write_state(kpe_ref, r, create_j, kpe.T)
        _mla_expand(ckv[rows, :].astype(BF16), kpe.astype(BF16), w_kvb_ref, kf_ref.at[r], vf_ref.at[r],
                    0, CTX_S)
        heads = _mla_heads(q, kf_ref.at[r], vf_ref.at[r], gate, rows, finish)
        finish = functools.partial(_finish, heads, w_out_ref, x_ref, mods_ref, g_post_ref, 0, o_ref, rows)
    finish()


def _layer_mla_ctx(x, mods, mi, g_pre, g_post, w_in, wj, g_qa, w_qb, g_kva, w_out, wo, w_kvb, states, layer, j,
                   n_layers, side_jobs=()):
    shapes = [(N_CTX_B, n_layers, CTX_S, MLA_KV_LORA), (N_CTX_B, n_layers, MLA_ROPE, CTX_S)]
    st_in_specs, st_out_specs, aliases, st_args = _state_plumbing(states, shapes, j, 10, 1)
    cj_in, cj_out, cj_shapes, cj_args, runners = _side_job_specs(side_jobs)
    tok_spec = pl.BlockSpec((CTX_PER_STEP * CTX_S, D_MODEL), lambda t: (t, 0))
    return pl.pallas_call(
        _with_rows(functools.partial(_layer_mla_ctx_kernel, j if states is None else None, runners),
                   {2: layer, 4: j, 6: j, 7: layer}),
        grid=(N_CTX_B // CTX_PER_STEP,),
        in_specs=[
            tok_spec,
            pl.BlockSpec((None, COND_ROWS, 3 * D_MODEL), lambda t: (mi, 0, 0)),
            pl.BlockSpec(g_pre.shape, lambda t: (0, 0)),
            pl.BlockSpec((None, MLA_IN, D_MODEL), lambda t: (wj, 0, 0)),
            pl.BlockSpec(g_qa.shape, lambda t: (0, 0)),
            pl.BlockSpec((None, MLA_Q_LORA, HEADS * MLA_QK_PAD), lambda t: (j, 0, 0)),
            pl.BlockSpec(g_kva.shape, lambda t: (0, 0)),
            pl.BlockSpec(g_post.shape, lambda t: (0, 0)),
            pl.BlockSpec((None, D_MODEL, D_MODEL), lambda t: (wo, 0, 0)),
            pl.BlockSpec((None, MLA_KV_LORA, HEADS * (MLA_NOPE + MLA_DV)), lambda t: (j, 0, 0)),
        ] + st_in_specs + cj_in,
        out_specs=[tok_spec] + st_out_specs + cj_out,
        out_shape=[jax.ShapeDtypeStruct(x.shape, F32)] + [jax.ShapeDtypeStruct(shp, F32) for shp in shapes]
        + cj_shapes,
        input_output_aliases=aliases,
        scratch_shapes=[
            pltpu.VMEM((CTX_PER_STEP, HEADS, CTX_S, MLA_QK_PAD), BF16),
            pltpu.VMEM((CTX_PER_STEP, HEADS, CTX_S, MLA_DV), BF16),
        ],
        compiler_params=pltpu.CompilerParams(
            dimension_semantics=("arbitrary",), vmem_limit_bytes=VMEM_LIMIT),
        name="layer_mla_ctx",
    )(x, mods, g_pre, w_in, g_qa, w_qb, g_kva, g_post, w_out, w_kvb, *st_args, *cj_args)


LAT_TILES = LAT_S // TM
LAT_QBLOCKS = LAT_S // TQ


def _lat_specs():
    tile = lambda s: jnp.minimum(s, LAT_TILES - 1)
    qblk = lambda s: jnp.maximum(s - LAT_TILES, 0)
    x_proj = pl.BlockSpec((TM, D_MODEL), lambda b, s: (b * LAT_TILES + tile(s), 0))
    x_attn = pl.BlockSpec((TQ, D_MODEL), lambda b, s: (b * LAT_QBLOCKS + qblk(s), 0))
    rope = pl.BlockSpec((TM, LANES), lambda b, s: (tile(s), 0))
    return x_proj, x_attn, rope


def _query_rows():
    i = pl.program_id(1) - LAT_TILES
    return pl.ds(pl.multiple_of(i * TQ, TQ), TQ)


def _layer_diff_lat_kernel(lam_init, xp_ref, xa_ref, mods_ref, g_pre_ref, w_in_ref, cos_ref, sin_ref,
                           ckt_ref, cv_ref, g_post_ref, w_out_ref, g_sub_ref, lam_ref,
                           o_ref, k2_ref, vb_ref, q_s, gate_s):
    step = pl.program_id(1)
    row = 1 + pl.program_id(0)
    sk = LAT_S + PAST

    for t in range(LAT_TILES):
        @pl.when(step == t)
        def _(t=t):
            r0 = t * TM
            hb = _modulated(xp_ref, mods_ref, g_pre_ref, row)
            cos = cos_ref[...]
            sin = sin_ref[...]
            n = D_MODEL
            q = jnp.dot(hb, w_in_ref[:, 0:n], preferred_element_type=F32)
            k = jnp.dot(hb, w_in_ref[:, n:2 * n], preferred_element_type=F32)
            cos_q = cos * DA_Q_SCALE
            sin_q = sin * DA_Q_SCALE
            slabs = [slice(h * LANES, (h + 1) * LANES) for h in range(HEADS)]
            q_s[r0:r0 + TM, :] = jnp.concatenate(
                [_rope_slab(q[:, sl], cos_q, sin_q).astype(BF16) for sl in slabs], axis=1)
            kb = jnp.concatenate([_rope_slab(k[:, sl], cos, sin).astype(BF16) for sl in slabs], axis=1)
            _stack_diff_keys(kb, k2_ref, r0, sk)
            vt = jnp.dot(hb, w_in_ref[:, 2 * n:3 * n], preferred_element_type=F32).T.astype(BF16)
            _stack_values_t(vt, vb_ref, r0, TM)
            gate_s[r0:r0 + TM, :] = _silu(
                jnp.dot(hb, w_in_ref[:, 3 * n:4 * n], preferred_element_type=F32)).astype(BF16)

    @pl.when(step == 0)
    def _():
        lane = lax.broadcasted_iota(jnp.int32, (PAST, LANES), 1)
        zero = jnp.zeros((PAST, LANES), BF16)
        ones = _ones_rows(PAST)
        for h in range(HEADS):
            ch = ckt_ref[h * LANES:(h + 1) * LANES, :].T.astype(BF16)
            k2_ref[h, LAT_S:sk, :] = jnp.where(lane < DA_DH, ch, zero)
            k2_ref[h, sk + LAT_S:2 * sk, :] = jnp.where(lane >= DA_DH, ch, zero)
            vb_ref[h, 0:DA_DV, LAT_S:sk] = cv_ref[:, h, :].T.astype(BF16)
            vb_ref[h, DA_DV:DA_DV + V_PAD, LAT_S:sk] = ones

    @pl.when(step >= LAT_TILES)
    def _():
        lam = _diff_lambda(lam_ref, lam_init)
        heads = _diff_heads(q_s, k2_ref, vb_ref, gate_s, g_sub_ref, lam, lam_init, sk, _query_rows())
        _finish(heads, w_out_ref, xa_ref, mods_ref, g_post_ref, row, o_ref)


def _layer_diff_lat(x, mods, mi, g_pre, g_post, w_in, wj, w_out, wo, cos_t, sin_t, cache_kt, cache_v, g_sub,
                    lam_p, layer, j):
    lam_init = 0.8 - 0.6 * math.exp(-0.3 * layer)
    sk = LAT_S + PAST
    x_proj, x_attn, rope = _lat_specs()
    return pl.pallas_call(
        _with_rows(functools.partial(_layer_diff_lat_kernel, lam_init), {3: layer, 9: layer}),
        grid=(N_LAT_B, LAT_TILES + LAT_QBLOCKS),
        in_specs=[
            x_proj, x_attn,
            pl.BlockSpec((None, COND_ROWS, 3 * D_MODEL), lambda b, s: (mi, 0, 0)),
            pl.BlockSpec(g_pre.shape, lambda b, s: (0, 0)),
            pl.BlockSpec((None, D_MODEL, 4 * D_MODEL), lambda b, s: (wj, 0, 0)),
            rope, rope,
            pl.BlockSpec((None, None, D_MODEL, PAST), lambda b, s: (b, j, 0, 0)),
            pl.BlockSpec((None, None, PAST, HEADS, DA_DV), lambda b, s: (b, j, 0, 0, 0)),
            pl.BlockSpec(g_post.shape, lambda b, s: (0, 0)),
            pl.BlockSpec((None, D_MODEL, D_MODEL), lambda b, s: (wo, 0, 0)),
            pl.BlockSpec((None, 1, DA_DV), lambda b, s: (j, 0, 0)),
            pl.BlockSpec((None, 4, DA_DH), lambda b, s: (j, 0, 0)),
        ],
        out_specs=x_attn,
        out_shape=jax.ShapeDtypeStruct(x.shape, F32),
        scratch_shapes=[
            pltpu.VMEM((HEADS, 2 * sk, LANES), BF16),
            pltpu.VMEM((HEADS, DA_DV + V_PAD, sk), BF16),
            pltpu.VMEM((LAT_S, D_MODEL), BF16),
            pltpu.VMEM((LAT_S, D_MODEL), BF16),
        ],
        compiler_params=pltpu.CompilerParams(
            dimension_semantics=("arbitrary", "arbitrary"), vmem_limit_bytes=VMEM_LIMIT),
        name="layer_diff_lat",
    )(x, x, mods, g_pre, w_in, cos_t, sin_t, cache_kt, cache_v, g_post, w_out, g_sub, lam_p)


def _layer_mla_lat_kernel(xp_ref, xa_ref, mods_ref, g_pre_ref, w_in_ref, g_qa_ref, w_qb_ref, g_kva_ref,
                          cos_ref, sin_ref, cckv_ref, ckpe_ref, g_post_ref, w_out_ref, w_kvb_ref,
                          o_ref, kf_ref, vf_ref, q_s, gate_s):
    step = pl.program_id(1)
    row = 1 + pl.program_id(0)

    for t in range(LAT_TILES):
        @pl.when(step == t)
        def _(t=t):
            r0 = t * TM
            hb = _modulated(xp_ref, mods_ref, g_pre_ref, row)
            cos = cos_ref[...]
            sin = sin_ref[...]
            qn = _mla_queries(hb, w_in_ref, g_qa_ref)
            for h in range(HEADS):
                lo = h * MLA_QK_PAD
                q = jnp.dot(qn, w_qb_ref[:, lo:lo + MLA_QK_PAD], preferred_element_type=F32) * MLA_Q_SCALE
                q_s[r0:r0 + TM, lo:lo + LANES] = q[:, 0:LANES].astype(BF16)
                q_s[r0:r0 + TM, lo + LANES:lo + 2 * LANES] = _rope_slab(
                    q[:, LANES:2 * LANES], cos, sin).astype(BF16)
            ckv = _rms(_proj_t(hb, w_in_ref, _KV), g_kva_ref[...]).astype(BF16)
            gate_s[r0:r0 + TM, :] = _silu(_proj_t(hb, w_in_ref, _GT)).astype(BF16)
            kpe = _rope_slab(_proj_t(hb, w_in_ref, _PE), cos, sin)[:, 0:MLA_ROPE].astype(BF16)
            _mla_expand(ckv, kpe, w_kvb_ref, kf_ref, vf_ref, r0, TM)

    @pl.when(step == 0)
    def _():
        _mla_expand(cckv_ref[...].astype(BF16), ckpe_ref[...].T.astype(BF16), w_kvb_ref, kf_ref, vf_ref,
                    LAT_S, PAST)

    @pl.when(step >= LAT_TILES)
    def _():
        heads = _mla_heads(q_s, kf_ref, vf_ref, gate_s, _query_rows())
        _finish(heads, w_out_ref, xa_ref, mods_ref, g_post_ref, row, o_ref)


def _layer_mla_lat(x, mods, mi, g_pre, g_post, w_in, wj, g_qa, w_qb, g_kva, w_out, wo, w_kvb, cos_t, sin_t,
                   cache_ckv, cache_kpe, layer, j):
    sk = LAT_S + PAST
    x_proj, x_attn, rope = _lat_specs()
    return pl.pallas_call(
        _with_rows(_layer_mla_lat_kernel, {3: layer, 5: j, 7: j, 12: layer}),
        grid=(N_LAT_B, LAT_TILES + LAT_QBLOCKS),
        in_specs=[
            x_proj, x_attn,
            pl.BlockSpec((None, COND_ROWS, 3 * D_MODEL), lambda b, s: (mi, 0, 0)),
            pl.BlockSpec(g_pre.shape, lambda b, s: (0, 0)),
            pl.BlockSpec((None, MLA_IN, D_MODEL), lambda b, s: (wj, 0, 0)),
            pl.BlockSpec(g_qa.shape, lambda b, s: (0, 0)),
            pl.BlockSpec((None, MLA_Q_LORA, HEADS * MLA_QK_PAD), lambda b, s: (j, 0, 0)),
            pl.BlockSpec(g_kva.shape, lambda b, s: (0, 0)),
            rope, rope,
            pl.BlockSpec((None, None, PAST, MLA_KV_LORA), lambda b, s: (b, j, 0, 0)),
            pl.BlockSpec((None, None, MLA_ROPE, PAST), lambda b, s: (b, j, 0, 0)),
            pl.BlockSpec(g_post.shape, lambda b, s: (0, 0)),
            pl.BlockSpec((None, D_MODEL, D_MODEL), lambda b, s: (wo, 0, 0)),
            pl.BlockSpec((None, MLA_KV_LORA, HEADS * (MLA_NOPE + MLA_DV)), lambda b, s: (j, 0, 0)),
        ],
        out_specs=x_attn,
        out_shape=jax.ShapeDtypeStruct(x.shape, F32),
        scratch_shapes=[
            pltpu.VMEM((HEADS, sk, MLA_QK_PAD), BF16),
            pltpu.VMEM((HEADS, sk, MLA_DV), BF16),
            pltpu.VMEM((LAT_S, HEADS * MLA_QK_PAD), BF16),
            pltpu.VMEM((LAT_S, D_MODEL), BF16),
        ],
        compiler_params=pltpu.CompilerParams(
            dimension_semantics=("arbitrary", "arbitrary"), vmem_limit_bytes=VMEM_LIMIT),
        name="layer_mla_lat",
    )(x, x, mods, g_pre, w_in, g_qa, w_qb, g_kva, cos_t, sin_t, cache_ckv, cache_kpe, g_post, w_out, w_kvb)


def _rope_tables():
    rows = LAT_S // GRID_W
    row = np.repeat(np.arange(rows), GRID_W).astype(np.float64)
    col = np.tile(np.arange(GRID_W), rows).astype(np.float64)
    n_freq = DA_DH // 4
    inv = ROPE_THETA ** (-np.arange(n_freq, dtype=np.float64) / n_freq)
    ang = np.concatenate([row[:, None] * inv, col[:, None] * inv], axis=-1)
    cos, sin = np.cos(ang), np.sin(ang)
    return (jnp.asarray(np.concatenate([cos, cos, cos, cos], axis=-1), F32),
            jnp.asarray(np.concatenate([-sin, sin, -sin, sin], axis=-1), F32))


def kernel(x_prompt, x_sample, cache_diff_k, cache_diff_v, cache_mla_ckv, cache_mla_kpe,
           c, c_ctx, w_ada, b_ada, g_pre, g_post, w_out,
           da_w_in, da_lam_q1, da_lam_k1, da_lam_q2, da_lam_k2, da_g_sub,
           mla_w_in, mla_g_qa, mla_w_qb, mla_g_kva, mla_w_kvb):
    assert DA_DH == MLA_ROPE
    assert x_prompt.shape == (N_CTX_B, CTX_S, D_MODEL) and x_sample.shape == (N_LAT_B, LAT_S, D_MODEL)
    assert cache_diff_k.shape == (N_LAT_B, (DEPTH + 1) // 2, PAST, HEADS, 2, DA_DH)
    assert cache_mla_ckv.shape == (N_LAT_B, DEPTH // 2, PAST, MLA_KV_LORA)
    assert w_ada.shape == (DEPTH, D_MODEL, 3 * D_MODEL) and da_w_in.shape[1:] == (D_MODEL, 4 * D_MODEL)
    n_diff = (DEPTH + 1) // 2
    n_mla = DEPTH // 2

    cond = jnp.concatenate(
        [c_ctx[None, :], c, jnp.zeros((COND_ROWS - 1 - N_LAT_B, D_MODEL), F32)], axis=0)
    steps = N_CTX_B // CTX_PER_STEP
    wide = D_MODEL * 4 // steps

    def da_in_cast(jj):
        return _cast_job(da_w_in, (None, D_MODEL, wide), lambda t: (jj, 0, t),
                         (1, D_MODEL, 4 * D_MODEL), lambda t: (0, 0, t))

    mods_first, da_w_in_first, w_out_first = _jobs_call(
        [_ada_job(cond, w_ada, b_ada, 0, steps),
         da_in_cast(0),
         _cast_job(w_out, (None, D_MODEL // steps, D_MODEL), lambda t: (0, t, 0),
                   (1, D_MODEL, D_MODEL), lambda t: (0, t, 0))], steps)

    cos_t, sin_t = _rope_tables()
    g_sub3 = da_g_sub.reshape(n_diff, 1, DA_DV)
    lam_p = jnp.stack([da_lam_q1, da_lam_k1, da_lam_q2, da_lam_k2], axis=1)
    cache_dkt = jnp.transpose(cache_diff_k, (0, 1, 3, 4, 5, 2)).reshape(N_LAT_B, n_diff, D_MODEL, PAST)

    mla_w_in_t = jnp.swapaxes(mla_w_in, 1, 2)
    cache_kpe_t = jnp.swapaxes(cache_mla_kpe, 2, 3)

    x_ctx = x_prompt.reshape(N_CTX_B * CTX_S, D_MODEL)
    x_lat = x_sample.reshape(N_LAT_B * LAT_S, D_MODEL)
    st_diff = None
    st_mla = None

    assert DEPTH == 4
    da_w = [(da_w_in_first, 0), None]
    mla_w = [None, None]
    w_o = [(w_out_first, 0), None, None, None]
    cols = D_MODEL // steps
    per_layer = steps // 2
    kv_cols = HEADS * (MLA_NOPE + MLA_DV) // steps

    mods = [(mods_first, 0), None, None, None]

    def mla_in_cast(jj):
        return _cast_job(mla_w_in_t, (None, MLA_IN, cols), lambda t: (jj, 0, t),
                         (1, MLA_IN, D_MODEL), lambda t: (0, 0, t))

    side_jobs = {
        0: [mla_in_cast(0),
            _cast_job(w_out, (None, D_MODEL // steps, D_MODEL), lambda t: (1, t, 0),
                      (1, D_MODEL, D_MODEL), lambda t: (0, t, 0)),
            _cast_job(mla_w_kvb, (n_mla, MLA_KV_LORA, kv_cols), lambda t: (0, 0, t),
                      mla_w_kvb.shape, lambda t: (0, 0, t)),
            _pad_heads_job(mla_w_qb, steps),
            _ada_job(cond, w_ada, b_ada, 1, steps)],
        1: [mla_in_cast(1),
            _cast_job(w_out, (None, D_MODEL // per_layer, D_MODEL),
                      lambda t: (2 + t // per_layer, t % per_layer, 0),
                      (2, D_MODEL, D_MODEL), lambda t: (t // per_layer, t % per_layer, 0)),
            _ada_job(cond, w_ada, b_ada, 2, steps),
            da_in_cast(1)],
        2: [_ada_job(cond, w_ada, b_ada, 3, steps)],
        3: [],
    }

    for i in range(DEPTH):
        j = i // 2
        if i % 2 == 0:
            w_j = da_w[j]
            outs = _layer_diff_ctx(x_ctx, *mods[i], g_pre, g_post, *w_j, *w_o[i], g_sub3, lam_p, st_diff,
                                   i, j, n_diff, side_jobs=side_jobs[i])
            x_ctx, st_dkt, st_dv = outs[0:3]
            st_diff = (st_dkt, st_dv)
            if i == 0:
                mla_w[0], w_o[1], w_kvb_b, w_qb_b = (outs[3], 0), (outs[4], 0), outs[5], outs[6]
                mods[1] = (outs[7], 0)
            else:
                mods[3] = (outs[3], 0)
            x_lat = _layer_diff_lat(x_lat, *mods[i], g_pre, g_post, *w_j, *w_o[i], cos_t, sin_t,
                                    cache_dkt, cache_diff_v, g_sub3, lam_p, i, j)
        else:
            w_j = mla_w[j]
            outs = _layer_mla_ctx(x_ctx, *mods[i], g_pre, g_post, *w_j, mla_g_qa, w_qb_b, mla_g_kva, *w_o[i], w_kvb_b,
                                  st_mla, i, j, n_mla, side_jobs=side_jobs[i])
            x_ctx, st_ckv, st_kpe = outs[0:3]
            st_mla = (st_ckv, st_kpe)
            if i == 1:
                mla_w[1] = (outs[3], 0)
                w_o[2], w_o[3] = (outs[4], 0), (outs[4], 1)
                mods[2] = (outs[5], 0)
                da_w[1] = (outs[6], 0)
            x_lat = _layer_mla_lat(x_lat, *mods[i], g_pre, g_post, *w_j, mla_g_qa, w_qb_b, mla_g_kva, *w_o[i], w_kvb_b,
                                   cos_t, sin_t, cache_mla_ckv, cache_kpe_t, i, j)

    return (x_ctx.reshape(N_CTX_B, CTX_S, D_MODEL),
            x_lat.reshape(N_LAT_B, LAT_S, D_MODEL),
            jnp.transpose(st_dkt.reshape(N_CTX_B, n_diff, HEADS, 2, DA_DH, CTX_S), (0, 1, 5, 2, 3, 4)),
            st_dv,
            st_ckv,
            jnp.swapaxes(st_kpe, 2, 3))
```

```python
import functools
import math

import jax
import jax.numpy as jnp
import numpy as np
from jax import lax
from jax.experimental import pallas as pl
from jax.experimental.pallas import tpu as pltpu

F32 = jnp.float32
BF16 = jnp.bfloat16

D_MODEL = 1024
DEPTH = 4
N_CTX_B = 16
CTX_S = 256
N_LAT_B = 4
LAT_S = 1024
PAST = 256
GRID_W = 64
HEADS = 8
DA_DH = 64
DA_DV = 128
MLA_Q_LORA = 384
MLA_KV_LORA = 256
MLA_NOPE = 128
MLA_ROPE = 64
MLA_DV = 128
MLA_QK_PAD = 256
ROPE_THETA = 10000.0
NORM_EPS = 1e-6
COND_ROWS = 8

LANES = 128
V7X_VMEM_MIB = 64
V_PAD = 16
TQ = 256
TM = 1024
VMEM_LIMIT = (V7X_VMEM_MIB - 8) * 1024 * 1024

_TRANS_B = (((1,), (1,)), ((), ()))
SCORES_AHEAD = 2
CTX_PER_STEP = 2
PREP_STEPS = 8
PREP_BUFFERS = 3
DA_Q_SCALE = DA_DH ** -0.5 * math.log2(math.e)
MLA_Q_SCALE = (MLA_NOPE + MLA_ROPE) ** -0.5 * math.log2(math.e)


def _silu(x):
    return x * (1.0 / (1.0 + jnp.exp(-x)))


def _rms(x, g):
    r = lax.rsqrt(jnp.mean(x * x, axis=-1, keepdims=True) + NORM_EPS)
    return (x * r) * g


def _rope_slab(x, cos, sin_signed):
    half = DA_DH // 2
    lane = lax.broadcasted_iota(jnp.int32, x.shape, 1)
    first_half = (lane % DA_DH) < half
    partner = jnp.where(first_half, pltpu.roll(x, LANES - half, axis=1), pltpu.roll(x, half, axis=1))
    return x * cos + partner * sin_signed


def _ones_rows(cols):
    row = lax.broadcasted_iota(jnp.int32, (V_PAD, cols), 0)
    return jnp.where(row == 0, 1.0, 0.0).astype(BF16)


def _cond_row(mods_ref, row):
    m = mods_ref[pl.ds(row, 1), :]
    return m[:, 0:D_MODEL], m[:, D_MODEL:2 * D_MODEL], m[:, 2 * D_MODEL:3 * D_MODEL]


def _modulated(x_ref, mods_ref, g_ref, row):
    shift, scale, _ = _cond_row(mods_ref, row)
    h = _rms(x_ref[...], g_ref[...]) * (1.0 + scale) + shift
    return h.astype(BF16)


def _with_rows(kernel_fn, picks):
    def run(*refs):
        refs = list(refs)
        for n, row in picks.items():
            refs[n] = refs[n].at[pl.ds(row, 1)]
        return kernel_fn(*refs)
    return run


def _ada_kernel(layer, cond_ref, w_ref, b_ref, o_ref):
    a = _silu(cond_ref[...]).astype(BF16)
    o_ref[...] = jnp.dot(a, w_ref[...].astype(BF16), preferred_element_type=F32) + b_ref[pl.ds(layer, 1), :]


def _cast_block(src_ref, dst_ref):
    dst_ref[...] = src_ref[...].astype(BF16)


def _cast_job(array, blk, src_map, shape, dst_map):
    return dict(args=[array], in_specs=[pl.BlockSpec(blk, src_map)], out_spec=pl.BlockSpec(blk, dst_map),
                out_shape=jax.ShapeDtypeStruct(shape, BF16), run=_cast_block)


def _pad_heads_block(src_ref, dst_ref):
    qk = MLA_NOPE + MLA_ROPE
    zero = jnp.zeros(src_ref.shape[:2] + (MLA_QK_PAD - qk,), BF16)
    for h in range(src_ref.shape[2] // qk):
        lo = h * MLA_QK_PAD
        dst_ref[:, :, lo:lo + qk] = src_ref[:, :, h * qk:(h + 1) * qk].astype(BF16)
        dst_ref[:, :, lo + qk:lo + MLA_QK_PAD] = zero


def _pad_heads_job(w_qb, steps):
    n, rows, _ = w_qb.shape
    span = steps // (HEADS // 2)
    return dict(args=[w_qb],
                in_specs=[pl.BlockSpec((n, rows, 2 * (MLA_NOPE + MLA_ROPE)), lambda t: (0, 0, t // span))],
                out_spec=pl.BlockSpec((n, rows, 2 * MLA_QK_PAD), lambda t: (0, 0, t // span)),
                out_shape=jax.ShapeDtypeStruct((n, rows, HEADS * MLA_QK_PAD), BF16), run=_pad_heads_block)


def _ada_job(cond, w_ada, b_ada, layer, steps):
    tn = 3 * D_MODEL // steps
    return dict(args=[cond, w_ada, b_ada],
                in_specs=[pl.BlockSpec((COND_ROWS, D_MODEL), lambda t: (0, 0)),
                          pl.BlockSpec((None, D_MODEL, tn), lambda t: (layer, 0, t)),
                          pl.BlockSpec((b_ada.shape[0], tn), lambda t: (0, t))],
                out_spec=pl.BlockSpec((None, COND_ROWS, tn), lambda t: (0, 0, t)),
                out_shape=jax.ShapeDtypeStruct((1, COND_ROWS, 3 * D_MODEL), F32),
                run=functools.partial(_ada_kernel, layer))


def _side_job_specs(jobs):
    in_specs = [spec for job in jobs for spec in job["in_specs"]]
    args = [a for job in jobs for a in job["args"]]
    runners = tuple((len(job["args"]), job["run"]) for job in jobs)
    return in_specs, [job["out_spec"] for job in jobs], [job["out_shape"] for job in jobs], args, runners


def _first_layer_prep_kernel(cond_ref, b_ref, w_ada_hbm, w_in_hbm, w_out_hbm, mods_ref, w_in_o_hbm, w_out_o_hbm,
                            ada_buf, in_buf, out_buf, in_cast, out_cast, read_sem, write_sem):
    tn, wide, rows = ada_buf.shape[2], in_buf.shape[2], out_buf.shape[1]

    def reads(t):
        slot = t % PREP_BUFFERS
        return (pltpu.make_async_copy(w_ada_hbm.at[0, :, pl.ds(t * tn, tn)], ada_buf.at[slot], read_sem.at[0, slot]),
                pltpu.make_async_copy(w_in_hbm.at[0, :, pl.ds(t * wide, wide)], in_buf.at[slot], read_sem.at[1, slot]),
                pltpu.make_async_copy(w_out_hbm.at[0, pl.ds(t * rows, rows), :], out_buf.at[slot], read_sem.at[2, slot]))

    def writes(t):
        slot = t % 2
        return (pltpu.make_async_copy(in_cast.at[slot], w_in_o_hbm.at[0, :, pl.ds(t * wide, wide)], write_sem.at[0, slot]),
                pltpu.make_async_copy(out_cast.at[slot], w_out_o_hbm.at[0, pl.ds(t * rows, rows), :], write_sem.at[1, slot]))

    for t in range(min(PREP_BUFFERS, PREP_STEPS)):
        for copy in reads(t):
            copy.start()
    a = _silu(cond_ref[...]).astype(BF16)
    for t in range(PREP_STEPS):
        slot = t % PREP_BUFFERS
        for copy in reads(t):
            copy.wait()
        if t >= 2:
            for copy in writes(t - 2):
                copy.wait()
        mods_ref[0, :, t * tn:(t + 1) * tn] = (
            jnp.dot(a, ada_buf[slot].astype(BF16), preferred_element_type=F32) + b_ref[0:1, t * tn:(t + 1) * tn])
        in_cast[t % 2] = in_buf[slot].astype(BF16)
        out_cast[t % 2] = out_buf[slot].astype(BF16)
        for copy in writes(t):
            copy.start()
        if t + PREP_BUFFERS < PREP_STEPS:
            for copy in reads(t + PREP_BUFFERS):
                copy.start()
    for t in range(max(PREP_STEPS - 2, 0), PREP_STEPS):
        for copy in writes(t):
            copy.wait()


def _first_layer_prep(cond, w_ada, b_ada, da_w_in, w_out):
    tn, wide, rows = 3 * D_MODEL // PREP_STEPS, 4 * D_MODEL // PREP_STEPS, D_MODEL // PREP_STEPS
    hbm = pl.BlockSpec(memory_space=pl.ANY)
    return pl.pallas_call(
        _first_layer_prep_kernel,
        in_specs=[pl.BlockSpec(memory_space=pltpu.VMEM), pl.BlockSpec(memory_space=pltpu.VMEM), hbm, hbm, hbm],
        out_specs=[pl.BlockSpec(memory_space=pltpu.VMEM), hbm, hbm],
        out_shape=[jax.ShapeDtypeStruct((1, COND_ROWS, 3 * D_MODEL), F32),
                   jax.ShapeDtypeStruct((1, D_MODEL, 4 * D_MODEL), BF16),
                   jax.ShapeDtypeStruct((1, D_MODEL, D_MODEL), BF16)],
        scratch_shapes=[
            pltpu.VMEM((PREP_BUFFERS, D_MODEL, tn), F32),
            pltpu.VMEM((PREP_BUFFERS, D_MODEL, wide), F32),
            pltpu.VMEM((PREP_BUFFERS, rows, D_MODEL), F32),
            pltpu.VMEM((2, D_MODEL, wide), BF16),
            pltpu.VMEM((2, rows, D_MODEL), BF16),
            pltpu.SemaphoreType.DMA((3, PREP_BUFFERS)),
            pltpu.SemaphoreType.DMA((2, 2)),
        ],
        compiler_params=pltpu.CompilerParams(vmem_limit_bytes=VMEM_LIMIT),
        name="first_layer_prep",
    )(cond, b_ada, w_ada, da_w_in, w_out)


def _write_state(ref, r, j, value):
    if j is None:
        ref[r] = value
    else:
        for jj in range(ref.shape[1]):
            ref[r, jj] = value if jj == j else jnp.zeros_like(value)


def _state_plumbing(states, shapes, j, n_inputs, n_outputs):
    def index_map(shp, layer_index):
        return lambda t: (t, layer_index) + (0,) * (len(shp) - 2)

    if states is None:
        specs = [pl.BlockSpec((CTX_PER_STEP,) + shp[1:], index_map(shp, 0)) for shp in shapes]
        return [], specs, {}, ()
    specs = [pl.BlockSpec((CTX_PER_STEP, None) + shp[2:], index_map(shp, j)) for shp in shapes]
    in_specs = [pl.BlockSpec(memory_space=pl.ANY) for _ in shapes]
    aliases = {n_inputs + i: n_outputs + i for i in range(len(shapes))}
    return in_specs, specs, aliases, tuple(states)


def _finish(heads, w_out_ref, x_ref, mods_ref, g_post_ref, row, o_ref, rows=slice(None)):
    _, _, gate = _cond_row(mods_ref, row)
    og = jnp.concatenate(heads, axis=1)
    y = jnp.dot(og, w_out_ref[...], preferred_element_type=F32)
    o_ref[rows, :] = x_ref[rows, :] + gate * _rms(y, g_post_ref[...])


def _diff_lambda(lam_ref, lam_init):
    p = lam_ref[...]
    a = jnp.sum(p[0:1, :] * p[1:2, :], axis=-1, keepdims=True)
    b = jnp.sum(p[2:3, :] * p[3:4, :], axis=-1, keepdims=True)
    return jnp.exp(a) - jnp.exp(b) + lam_init


def _stack_diff_keys(k, k2_ref, r0, sk):
    rows = k.shape[0]
    lane = lax.broadcasted_iota(jnp.int32, (rows, LANES), 1)
    zero = jnp.zeros((rows, LANES), BF16)
    for h in range(HEADS):
        kh = k[:, h * LANES:(h + 1) * LANES]
        k2_ref[h, r0:r0 + rows, :] = jnp.where(lane < DA_DH, kh, zero)
        k2_ref[h, sk + r0:sk + r0 + rows, :] = jnp.where(lane >= DA_DH, kh, zero)


def _stack_values_t(vt, vt_ref, c0, cols):
    ones = _ones_rows(cols)
    for h in range(HEADS):
        vt_ref[h, 0:DA_DV, c0:c0 + cols] = vt[h * DA_DV:(h + 1) * DA_DV, :]
        vt_ref[h, DA_DV:DA_DV + V_PAD, c0:c0 + cols] = ones


def _diff_heads(q_ref, k2_ref, vt_ref, gate_ref, g_sub_ref, lam, lam_init, sk, rows=slice(None), after_issue=None,
                split_scores=False):
    g_sub = g_sub_ref[...]

    def scores(h):
        qh = q_ref[rows, h * LANES:(h + 1) * LANES]
        if split_scores:
            return tuple(lax.dot_general(k2_ref[h, c * sk:(c + 1) * sk, :], qh, _TRANS_B,
                                         preferred_element_type=F32) for c in range(2))
        s = lax.dot_general(k2_ref[h], qh, _TRANS_B, preferred_element_type=F32)
        return s[0:sk, :], s[sk:2 * sk, :]

    heads = []
    pending = [scores(h) for h in range(SCORES_AHEAD)]
    if after_issue is not None:
        after_issue()
    for h in range(HEADS):
        lo, hi = h * LANES, (h + 1) * LANES
        s = pending.pop(0)
        if h + SCORES_AHEAD < HEADS:
            pending.append(scores(h + SCORES_AHEAD))
        s0, s1 = s
        e0 = jnp.exp2(s0 - jnp.max(s0, axis=0, keepdims=True)).astype(BF16)
        e1 = jnp.exp2(s1 - jnp.max(s1, axis=0, keepdims=True)).astype(BF16)
        pv0 = jnp.dot(vt_ref[h], e0, preferred_element_type=F32)
        pv1 = jnp.dot(vt_ref[h], e1, preferred_element_type=F32)
        a0 = 1.0 / pv0[DA_DV:DA_DV + 1, :]
        a1 = lam / pv1[DA_DV:DA_DV + 1, :]
        ot = pv0[0:DA_DV, :] * a0 - pv1[0:DA_DV, :] * a1
        ot = ot * lax.rsqrt(jnp.mean(ot * ot, axis=0, keepdims=True) + NORM_EPS)
        o = (ot.T * g_sub) * (1.0 - lam_init)
        heads.append((o * gate_ref[rows, lo:hi]).astype(BF16))
    return heads


MLA_IN = MLA_Q_LORA + MLA_KV_LORA + MLA_ROPE + D_MODEL
_QA = slice(0, MLA_Q_LORA)
_KV = slice(MLA_Q_LORA, MLA_Q_LORA + MLA_KV_LORA)
_PE = slice(_KV.stop, _KV.stop + LANES)
_GT = slice(_KV.stop + MLA_ROPE, MLA_IN)


def _proj_t(hb, w_t_ref, rows):
    return lax.dot_general(hb, w_t_ref[rows, :], _TRANS_B, preferred_element_type=F32)


def _mla_queries(hb, w_in_ref, g_qa_ref):
    q_a = _proj_t(hb, w_in_ref, _QA)
    return _rms(q_a, g_qa_ref[...]).astype(BF16)


def _mla_expand(ckv, kpe, w_kvb_ref, kf_ref, vf_ref, r0, rows):
    zero = jnp.zeros((rows, LANES - MLA_ROPE), BF16)
    for h in range(HEADS):
        lo = h * (MLA_NOPE + MLA_DV)
        kv = jnp.dot(ckv, w_kvb_ref[:, lo:lo + MLA_NOPE + MLA_DV], preferred_element_type=F32)
        kf_ref[h, r0:r0 + rows, 0:MLA_NOPE] = kv[:, 0:MLA_NOPE].astype(BF16)
        kf_ref[h, r0:r0 + rows, MLA_NOPE:MLA_NOPE + MLA_ROPE] = kpe
        kf_ref[h, r0:r0 + rows, MLA_NOPE + MLA_ROPE:MLA_QK_PAD] = zero
        vf_ref[h, r0:r0 + rows, :] = kv[:, MLA_NOPE:MLA_NOPE + MLA_DV].astype(BF16)


def _mla_heads(q_ref, kf_ref, vf_ref, gate_ref, rows=slice(None), after_issue=None):
    def scores(h):
        return lax.dot_general(q_ref[rows, h * MLA_QK_PAD:(h + 1) * MLA_QK_PAD], kf_ref[h], _TRANS_B,
                               preferred_element_type=F32)

    heads = []
    pending = [scores(h) for h in range(SCORES_AHEAD)]
    if after_issue is not None:
        after_issue()
    for h in range(HEADS):
        s = pending.pop(0)
        if h + SCORES_AHEAD < HEADS:
            pending.append(scores(h + SCORES_AHEAD))
        e = jnp.exp2(s - jnp.max(s, axis=-1, keepdims=True))
        inv = 1.0 / jnp.sum(e, axis=-1, keepdims=True)
        o = jnp.dot(e.astype(BF16), vf_ref[h], preferred_element_type=F32) * inv
        lo, hi = h * LANES, (h + 1) * LANES
        heads.append((o * gate_ref[rows, lo:hi]).astype(BF16))
    return heads


def _split_layer_refs(refs, runners):
    scratch = refs[-2:]
    refs = refs[:-2]
    n_jobs = len(runners)
    job_out = refs[len(refs) - n_jobs:]
    main = refs[len(refs) - n_jobs - 3:len(refs) - n_jobs]
    pos = len(refs) - n_jobs - 3 - sum(n for n, _ in runners)
    for (n, run), out in zip(runners, job_out):
        run(*refs[pos:pos + n], out)
        pos += n
    return main, scratch


def _layer_diff_ctx_kernel(create_j, lam_init, runners, x_ref, mods_ref, g_pre_ref, w_in_ref, g_post_ref,
                           w_out_ref, g_sub_ref, lam_ref, *refs):
    (o_ref, skt_ref, sv_ref), (k2_ref, vb_ref) = _split_layer_refs(refs, runners)
    hb = _modulated(x_ref, mods_ref, g_pre_ref, 0)
    n = D_MODEL
    q = (jnp.dot(hb, w_in_ref[:, 0:n], preferred_element_type=F32) * DA_Q_SCALE).astype(BF16)
    k = jnp.dot(hb, w_in_ref[:, n:2 * n], preferred_element_type=F32)
    v = jnp.dot(hb, w_in_ref[:, 2 * n:3 * n], preferred_element_type=F32)
    gate = _silu(jnp.dot(hb, w_in_ref[:, 3 * n:4 * n], preferred_element_type=F32)).astype(BF16)
    kb = k.astype(BF16)
    lam = _diff_lambda(lam_ref, lam_init)
    finish = None
    for r in range(CTX_PER_STEP):
        rows = slice(r * CTX_S, (r + 1) * CTX_S)
        vt = v[rows, :].T
        _write_state(skt_ref, r, create_j, k[rows, :].T)
        _write_state(sv_ref, r, create_j, v[rows, :].reshape(CTX_S, HEADS, DA_DV))
        _stack_diff_keys(kb[rows, :], k2_ref.at[r], 0, CTX_S)
        _stack_values_t(vt.astype(BF16), vb_ref.at[r], 0, CTX_S)
        heads = _diff_heads(q, k2_ref.at[r], vb_ref.at[r], gate, g_sub_ref, lam, lam_init, CTX_S, rows, finish,
                            split_scores=True)
        finish = functools.partial(_finish, heads, w_out_ref, x_ref, mods_ref, g_post_ref, 0, o_ref, rows)
    finish()


def _layer_diff_ctx(x, mods, mi, g_pre, g_post, w_in, wj, w_out, wo, g_sub, lam_p, states, layer, j, n_layers,
                    side_jobs=()):
    lam_init = 0.8 - 0.6 * math.exp(-0.3 * layer)
    shapes = [(N_CTX_B, n_layers, D_MODEL, CTX_S), (N_CTX_B, n_layers, CTX_S, HEADS, DA_DV)]
    st_in_specs, st_out_specs, aliases, st_args = _state_plumbing(states, shapes, j, 8, 1)
    cj_in, cj_out, cj_shapes, cj_args, runners = _side_job_specs(side_jobs)
    tok_spec = pl.BlockSpec((CTX_PER_STEP * CTX_S, D_MODEL), lambda t: (t, 0))
    return pl.pallas_call(
        _with_rows(functools.partial(_layer_diff_ctx_kernel, j if states is None else None, lam_init, runners),
                   {2: layer, 4: layer}),
        grid=(N_CTX_B // CTX_PER_STEP,),
        in_specs=[
            tok_spec,
            pl.BlockSpec((None, COND_ROWS, 3 * D_MODEL), lambda t: (mi, 0, 0)),
            pl.BlockSpec(g_pre.shape, lambda t: (0, 0)),
            pl.BlockSpec((None, D_MODEL, 4 * D_MODEL), lambda t: (wj, 0, 0)),
            pl.BlockSpec(g_post.shape, lambda t: (0, 0)),
            pl.BlockSpec((None, D_MODEL, D_MODEL), lambda t: (wo, 0, 0)),
            pl.BlockSpec((None, 1, DA_DV), lambda t: (j, 0, 0)),
            pl.BlockSpec((None, 4, DA_DH), lambda t: (j, 0, 0)),
        ] + st_in_specs + cj_in,
        out_specs=[tok_spec] + st_out_specs + cj_out,
        out_shape=[jax.ShapeDtypeStruct(x.shape, F32)] + [jax.ShapeDtypeStruct(shp, F32) for shp in shapes]
        + cj_shapes,
        input_output_aliases=aliases,
        scratch_shapes=[
            pltpu.VMEM((CTX_PER_STEP, HEADS, 2 * CTX_S, LANES), BF16),
            pltpu.VMEM((CTX_PER_STEP, HEADS, DA_DV + V_PAD, CTX_S), BF16),
        ],
        compiler_params=pltpu.CompilerParams(
            dimension_semantics=("arbitrary",), vmem_limit_bytes=VMEM_LIMIT),
        name="layer_diff_ctx",
    )(x, mods, g_pre, w_in, g_post, w_out, g_sub, lam_p, *st_args, *cj_args)


def _layer_mla_ctx_kernel(create_j, runners, x_ref, mods_ref, g_pre_ref, w_in_ref, g_qa_ref, w_qb_ref, g_kva_ref,
                          g_post_ref, w_out_ref, w_kvb_ref, *refs):
    (o_ref, ckv_ref, kpe_ref), (kf_ref, vf_ref) = _split_layer_refs(refs, runners)
    hb = _modulated(x_ref, mods_ref, g_pre_ref, 0)
    qn = _mla_queries(hb, w_in_ref, g_qa_ref)
    q = (jnp.dot(qn, w_qb_ref[...], preferred_element_type=F32) * MLA_Q_SCALE).astype(BF16)
    ckv = _rms(_proj_t(hb, w_in_ref, _KV), g_kva_ref[...])
    gate = _silu(_proj_t(hb, w_in_ref, _GT)).astype(BF16)
    pe = _proj_t(hb, w_in_ref, _PE)
    finish = None
    for r in range(CTX_PER_STEP):
        rows = slice(r * CTX_S, (r + 1) * CTX_S)
        kpe = pe[rows, 0:MLA_ROPE]
        _write_state(ckv_ref, r, create_j, ckv[rows, :])
        _write_state(kpe_ref, r, create_j, kpe.T)
        _mla_expand(ckv[rows, :].astype(BF16), kpe.astype(BF16), w_kvb_ref, kf_ref.at[r], vf_ref.at[r],
                    0, CTX_S)
        heads = _mla_heads(q, kf_ref.at[r], vf_ref.at[r], gate, rows, finish)
        finish = functools.partial(_finish, heads, w_out_ref, x_ref, mods_ref, g_post_ref, 0, o_ref, rows)
    finish()


def _layer_mla_ctx(x, mods, mi, g_pre, g_post, w_in, wj, g_qa, w_qb, g_kva, w_out, wo, w_kvb, states, layer, j,
                   n_layers, side_jobs=()):
    shapes = [(N_CTX_B, n_layers, CTX_S, MLA_KV_LORA), (N_CTX_B, n_layers, MLA_ROPE, CTX_S)]
    st_in_specs, st_out_specs, aliases, st_args = _state_plumbing(states, shapes, j, 10, 1)
    cj_in, cj_out, cj_shapes, cj_args, runners = _side_job_specs(side_jobs)
    tok_spec = pl.BlockSpec((CTX_PER_STEP * CTX_S, D_MODEL), lambda t: (t, 0))
    return pl.pallas_call(
        _with_rows(functools.partial(_layer_mla_ctx_kernel, j if states is None else None, runners),
                   {2: layer, 4: j, 6: j, 7: layer}),
        grid=(N_CTX_B // CTX_PER_STEP,),
        in_specs=[
            tok_spec,
            pl.BlockSpec((None, COND_ROWS, 3 * D_MODEL), lambda t: (mi, 0, 0)),
            pl.BlockSpec(g_pre.shape, lambda t: (0, 0)),
            pl.BlockSpec((None, MLA_IN, D_MODEL), lambda t: (wj, 0, 0)),
            pl.BlockSpec(g_qa.shape, lambda t: (0, 0)),
            pl.BlockSpec((None, MLA_Q_LORA, HEADS * MLA_QK_PAD), lambda t: (j, 0, 0)),
            pl.BlockSpec(g_kva.shape, lambda t: (0, 0)),
            pl.BlockSpec(g_post.shape, lambda t: (0, 0)),
            pl.BlockSpec((None, D_MODEL, D_MODEL), lambda t: (wo, 0, 0)),
            pl.BlockSpec((None, MLA_KV_LORA, HEADS * (MLA_NOPE + MLA_DV)), lambda t: (j, 0, 0)),
        ] + st_in_specs + cj_in,
        out_specs=[tok_spec] + st_out_specs + cj_out,
        out_shape=[jax.ShapeDtypeStruct(x.shape, F32)] + [jax.ShapeDtypeStruct(shp, F32) for shp in shapes]
        + cj_shapes,
        input_output_aliases=aliases,
        scratch_shapes=[
            pltpu.VMEM((CTX_PER_STEP, HEADS, CTX_S, MLA_QK_PAD), BF16),
            pltpu.VMEM((CTX_PER_STEP, HEADS, CTX_S, MLA_DV), BF16),
        ],
        compiler_params=pltpu.CompilerParams(
            dimension_semantics=("arbitrary",), vmem_limit_bytes=VMEM_LIMIT),
        name="layer_mla_ctx",
    )(x, mods, g_pre, w_in, g_qa, w_qb, g_kva, g_post, w_out, w_kvb, *st_args, *cj_args)


LAT_TILES = LAT_S // TM
LAT_QBLOCKS = LAT_S // TQ


def _lat_specs():
    tile = lambda s: jnp.minimum(s, LAT_TILES - 1)
    qblk = lambda s: jnp.maximum(s - LAT_TILES, 0)
    x_proj = pl.BlockSpec((TM, D_MODEL), lambda b, s: (b * LAT_TILES + tile(s), 0))
    x_attn = pl.BlockSpec((TQ, D_MODEL), lambda b, s: (b * LAT_QBLOCKS + qblk(s), 0))
    rope = pl.BlockSpec((TM, LANES), lambda b, s: (tile(s), 0))
    return x_proj, x_attn, rope


def _query_rows():
    i = pl.program_id(1) - LAT_TILES
    return pl.ds(pl.multiple_of(i * TQ, TQ), TQ)


def _layer_diff_lat_kernel(lam_init, xp_ref, xa_ref, mods_ref, g_pre_ref, w_in_ref, cos_ref, sin_ref,
                           ckt_ref, cv_ref, g_post_ref, w_out_ref, g_sub_ref, lam_ref,
                           o_ref, k2_ref, vb_ref, q_s, gate_s):
    step = pl.program_id(1)
    row = 1 + pl.program_id(0)
    sk = LAT_S + PAST

    for t in range(LAT_TILES):
        @pl.when(step == t)
        def _(t=t):
            r0 = t * TM
            hb = _modulated(xp_ref, mods_ref, g_pre_ref, row)
            cos = cos_ref[...]
            sin = sin_ref[...]
            n = D_MODEL
            q = jnp.dot(hb, w_in_ref[:, 0:n], preferred_element_type=F32)
            k = jnp.dot(hb, w_in_ref[:, n:2 * n], preferred_element_type=F32)
            cos_q = cos * DA_Q_SCALE
            sin_q = sin * DA_Q_SCALE
            slabs = [slice(h * LANES, (h + 1) * LANES) for h in range(HEADS)]
            q_s[r0:r0 + TM, :] = jnp.concatenate(
                [_rope_slab(q[:, sl], cos_q, sin_q).astype(BF16) for sl in slabs], axis=1)
            kb = jnp.concatenate([_rope_slab(k[:, sl], cos, sin).astype(BF16) for sl in slabs], axis=1)
            _stack_diff_keys(kb, k2_ref, r0, sk)
            vt = jnp.dot(hb, w_in_ref[:, 2 * n:3 * n], preferred_element_type=F32).T.astype(BF16)
            _stack_values_t(vt, vb_ref, r0, TM)
            gate_s[r0:r0 + TM, :] = _silu(
                jnp.dot(hb, w_in_ref[:, 3 * n:4 * n], preferred_element_type=F32)).astype(BF16)

    @pl.when(step == 0)
    def _():
        lane = lax.broadcasted_iota(jnp.int32, (PAST, LANES), 1)
        zero = jnp.zeros((PAST, LANES), BF16)
        ones = _ones_rows(PAST)
        for h in range(HEADS):
            ch = ckt_ref[h * LANES:(h + 1) * LANES, :].T.astype(BF16)
            k2_ref[h, LAT_S:sk, :] = jnp.where(lane < DA_DH, ch, zero)
            k2_ref[h, sk + LAT_S:2 * sk, :] = jnp.where(lane >= DA_DH, ch, zero)
            vb_ref[h, 0:DA_DV, LAT_S:sk] = cv_ref[:, h, :].T.astype(BF16)
            vb_ref[h, DA_DV:DA_DV + V_PAD, LAT_S:sk] = ones

    @pl.when(step >= LAT_TILES)
    def _():
        lam = _diff_lambda(lam_ref, lam_init)
        heads = _diff_heads(q_s, k2_ref, vb_ref, gate_s, g_sub_ref, lam, lam_init, sk, _query_rows())
        _finish(heads, w_out_ref, xa_ref, mods_ref, g_post_ref, row, o_ref)


def _layer_diff_lat(x, mods, mi, g_pre, g_post, w_in, wj, w_out, wo, cos_t, sin_t, cache_kt, cache_v, g_sub,
                    lam_p, layer, j):
    lam_init = 0.8 - 0.6 * math.exp(-0.3 * layer)
    sk = LAT_S + PAST
    x_proj, x_attn, rope = _lat_specs()
    return pl.pallas_call(
        _with_rows(functools.partial(_layer_diff_lat_kernel, lam_init), {3: layer, 9: layer}),
        grid=(N_LAT_B, LAT_TILES + LAT_QBLOCKS),
        in_specs=[
            x_proj, x_attn,
            pl.BlockSpec((None, COND_ROWS, 3 * D_MODEL), lambda b, s: (mi, 0, 0)),
            pl.BlockSpec(g_pre.shape, lambda b, s: (0, 0)),
            pl.BlockSpec((None, D_MODEL, 4 * D_MODEL), lambda b, s: (wj, 0, 0)),
            rope, rope,
            pl.BlockSpec((None, None, D_MODEL, PAST), lambda b, s: (b, j, 0, 0)),
            pl.BlockSpec((None, None, PAST, HEADS, DA_DV), lambda b, s: (b, j, 0, 0, 0)),
            pl.BlockSpec(g_post.shape, lambda b, s: (0, 0)),
            pl.BlockSpec((None, D_MODEL, D_MODEL), lambda b, s: (wo, 0, 0)),
            pl.BlockSpec((None, 1, DA_DV), lambda b, s: (j, 0, 0)),
            pl.BlockSpec((None, 4, DA_DH), lambda b, s: (j, 0, 0)),
        ],
        out_specs=x_attn,
        out_shape=jax.ShapeDtypeStruct(x.shape, F32),
        scratch_shapes=[
            pltpu.VMEM((HEADS, 2 * sk, LANES), BF16),
            pltpu.VMEM((HEADS, DA_DV + V_PAD, sk), BF16),
            pltpu.VMEM((LAT_S, D_MODEL), BF16),
            pltpu.VMEM((LAT_S, D_MODEL), BF16),
        ],
        compiler_params=pltpu.CompilerParams(
            dimension_semantics=("arbitrary", "arbitrary"), vmem_limit_bytes=VMEM_LIMIT),
        name="layer_diff_lat",
    )(x, x, mods, g_pre, w_in, cos_t, sin_t, cache_kt, cache_v, g_post, w_out, g_sub, lam_p)


def _layer_mla_lat_kernel(xp_ref, xa_ref, mods_ref, g_pre_ref, w_in_ref, g_qa_ref, w_qb_ref, g_kva_ref,
                          cos_ref, sin_ref, cckv_ref, ckpe_ref, g_post_ref, w_out_ref, w_kvb_ref,
                          o_ref, kf_ref, vf_ref, q_s, gate_s):
    step = pl.program_id(1)
    row = 1 + pl.program_id(0)

    for t in range(LAT_TILES):
        @pl.when(step == t)
        def _(t=t):
            r0 = t * TM
            hb = _modulated(xp_ref, mods_ref, g_pre_ref, row)
            cos = cos_ref[...]
            sin = sin_ref[...]
            qn = _mla_queries(hb, w_in_ref, g_qa_ref)
            for h in range(HEADS):
                lo = h * MLA_QK_PAD
                q = jnp.dot(qn, w_qb_ref[:, lo:lo + MLA_QK_PAD], preferred_element_type=F32) * MLA_Q_SCALE
                q_s[r0:r0 + TM, lo:lo + LANES] = q[:, 0:LANES].astype(BF16)
                q_s[r0:r0 + TM, lo + LANES:lo + 2 * LANES] = _rope_slab(
                    q[:, LANES:2 * LANES], cos, sin).astype(BF16)
            ckv = _rms(_proj_t(hb, w_in_ref, _KV), g_kva_ref[...]).astype(BF16)
            gate_s[r0:r0 + TM, :] = _silu(_proj_t(hb, w_in_ref, _GT)).astype(BF16)
            kpe = _rope_slab(_proj_t(hb, w_in_ref, _PE), cos, sin)[:, 0:MLA_ROPE].astype(BF16)
            _mla_expand(ckv, kpe, w_kvb_ref, kf_ref, vf_ref, r0, TM)

    @pl.when(step == 0)
    def _():
        _mla_expand(cckv_ref[...].astype(BF16), ckpe_ref[...].T.astype(BF16), w_kvb_ref, kf_ref, vf_ref,
                    LAT_S, PAST)

    @pl.when(step >= LAT_TILES)
    def _():
        heads = _mla_heads(q_s, kf_ref, vf_ref, gate_s, _query_rows())
        _finish(heads, w_out_ref, xa_ref, mods_ref, g_post_ref, row, o_ref)


def _layer_mla_lat(x, mods, mi, g_pre, g_post, w_in, wj, g_qa, w_qb, g_kva, w_out, wo, w_kvb, cos_t, sin_t,
                   cache_ckv, cache_kpe, layer, j):
    sk = LAT_S + PAST
    x_proj, x_attn, rope = _lat_specs()
    return pl.pallas_call(
        _with_rows(_layer_mla_lat_kernel, {3: layer, 5: j, 7: j, 12: layer}),
        grid=(N_LAT_B, LAT_TILES + LAT_QBLOCKS),
        in_specs=[
            x_proj, x_attn,
            pl.BlockSpec((None, COND_ROWS, 3 * D_MODEL), lambda b, s: (mi, 0, 0)),
            pl.BlockSpec(g_pre.shape, lambda b, s: (0, 0)),
            pl.BlockSpec((None, MLA_IN, D_MODEL), lambda b, s: (wj, 0, 0)),
            pl.BlockSpec(g_qa.shape, lambda b, s: (0, 0)),
            pl.BlockSpec((None, MLA_Q_LORA, HEADS * MLA_QK_PAD), lambda b, s: (j, 0, 0)),
            pl.BlockSpec(g_kva.shape, lambda b, s: (0, 0)),
            rope, rope,
            pl.BlockSpec((None, None, PAST, MLA_KV_LORA), lambda b, s: (b, j, 0, 0)),
            pl.BlockSpec((None, None, MLA_ROPE, PAST), lambda b, s: (b, j, 0, 0)),
            pl.BlockSpec(g_post.shape, lambda b, s: (0, 0)),
            pl.BlockSpec((None, D_MODEL, D_MODEL), lambda b, s: (wo, 0, 0)),
            pl.BlockSpec((None, MLA_KV_LORA, HEADS * (MLA_NOPE + MLA_DV)), lambda b, s: (j, 0, 0)),
        ],
        out_specs=x_attn,
        out_shape=jax.ShapeDtypeStruct(x.shape, F32),
        scratch_shapes=[
            pltpu.VMEM((HEADS, sk, MLA_QK_PAD), BF16),
            pltpu.VMEM((HEADS, sk, MLA_DV), BF16),
            pltpu.VMEM((LAT_S, HEADS * MLA_QK_PAD), BF16),
            pltpu.VMEM((LAT_S, D_MODEL), BF16),
        ],
        compiler_params=pltpu.CompilerParams(
            dimension_semantics=("arbitrary", "arbitrary"), vmem_limit_bytes=VMEM_LIMIT),
        name="layer_mla_lat",
    )(x, x, mods, g_pre, w_in, g_qa, w_qb, g_kva, cos_t, sin_t, cache_ckv, cache_kpe, g_post, w_out, w_kvb)


def _rope_tables():
    rows = LAT_S // GRID_W
    row = np.repeat(np.arange(rows), GRID_W).astype(np.float64)
    col = np.tile(np.arange(GRID_W), rows).astype(np.float64)
    n_freq = DA_DH // 4
    inv = ROPE_THETA ** (-np.arange(n_freq, dtype=np.float64) / n_freq)
    ang = np.concatenate([row[:, None] * inv, col[:, None] * inv], axis=-1)
    cos, sin = np.cos(ang), np.sin(ang)
    return (jnp.asarray(np.concatenate([cos, cos, cos, cos], axis=-1), F32),
            jnp.asarray(np.concatenate([-sin, sin, -sin, sin], axis=-1), F32))


def kernel(x_prompt, x_sample, cache_diff_k, cache_diff_v, cache_mla_ckv, cache_mla_kpe,
           c, c_ctx, w_ada, b_ada, g_pre, g_post, w_out,
           da_w_in, da_lam_q1, da_lam_k1, da_lam_q2, da_lam_k2, da_g_sub,
           mla_w_in, mla_g_qa, mla_w_qb, mla_g_kva, mla_w_kvb):
    assert DA_DH == MLA_ROPE
    assert x_prompt.shape == (N_CTX_B, CTX_S, D_MODEL) and x_sample.shape == (N_LAT_B, LAT_S, D_MODEL)
    assert cache_diff_k.shape == (N_LAT_B, (DEPTH + 1) // 2, PAST, HEADS, 2, DA_DH)
    assert cache_mla_ckv.shape == (N_LAT_B, DEPTH // 2, PAST, MLA_KV_LORA)
    assert w_ada.shape == (DEPTH, D_MODEL, 3 * D_MODEL) and da_w_in.shape[1:] == (D_MODEL, 4 * D_MODEL)
    n_diff = (DEPTH + 1) // 2
    n_mla = DEPTH // 2

    cond = jnp.concatenate(
        [c_ctx[None, :], c, jnp.zeros((COND_ROWS - 1 - N_LAT_B, D_MODEL), F32)], axis=0)
    steps = N_CTX_B // CTX_PER_STEP
    wide = D_MODEL * 4 // steps

    def da_in_cast(jj):
        return _cast_job(da_w_in, (None, D_MODEL, wide), lambda t: (jj, 0, t),
                         (1, D_MODEL, 4 * D_MODEL), lambda t: (0, 0, t))

    mods_first, da_w_in_first, w_out_first = _first_layer_prep(cond, w_ada, b_ada, da_w_in, w_out)

    cos_t, sin_t = _rope_tables()
    g_sub3 = da_g_sub.reshape(n_diff, 1, DA_DV)
    lam_p = jnp.stack([da_lam_q1, da_lam_k1, da_lam_q2, da_lam_k2], axis=1)
    cache_dkt = jnp.transpose(cache_diff_k, (0, 1, 3, 4, 5, 2)).reshape(N_LAT_B, n_diff, D_MODEL, PAST)

    mla_w_in_t = jnp.swapaxes(mla_w_in, 1, 2)
    cache_kpe_t = jnp.swapaxes(cache_mla_kpe, 2, 3)

    x_ctx = x_prompt.reshape(N_CTX_B * CTX_S, D_MODEL)
    x_lat = x_sample.reshape(N_LAT_B * LAT_S, D_MODEL)
    st_diff = None
    st_mla = None

    assert DEPTH == 4
    da_w = [(da_w_in_first, 0), None]
    mla_w = [None, None]
    w_o = [(w_out_first, 0), None, None, None]
    cols = D_MODEL // steps
    per_layer = steps // 2
    kv_cols = HEADS * (MLA_NOPE + MLA_DV) // steps

    mods = [(mods_first, 0), None, None, None]

    def mla_in_cast(jj):
        return _cast_job(mla_w_in_t, (None, MLA_IN, cols), lambda t: (jj, 0, t),
                         (1, MLA_IN, D_MODEL), lambda t: (0, 0, t))

    side_jobs = {
        0: [mla_in_cast(0),
            _cast_job(w_out, (None, D_MODEL // steps, D_MODEL), lambda t: (1, t, 0),
                      (1, D_MODEL, D_MODEL), lambda t: (0, t, 0)),
            _cast_job(mla_w_kvb, (n_mla, MLA_KV_LORA, kv_cols), lambda t: (0, 0, t),
                      mla_w_kvb.shape, lambda t: (0, 0, t)),
            _pad_heads_job(mla_w_qb, steps),
            _ada_job(cond, w_ada, b_ada, 1, steps)],
        1: [mla_in_cast(1),
            _cast_job(w_out, (None, D_MODEL // per_layer, D_MODEL),
                      lambda t: (2 + t // per_layer, t % per_layer, 0),
                      (2, D_MODEL, D_MODEL), lambda t: (t // per_layer, t % per_layer, 0)),
            _ada_job(cond, w_ada, b_ada, 2, steps),
            da_in_cast(1)],
        2: [_ada_job(cond, w_ada, b_ada, 3, steps)],
        3: [],
    }

    for i in range(DEPTH):
        j = i // 2
        if i % 2 == 0:
            w_j = da_w[j]
            outs = _layer_diff_ctx(x_ctx, *mods[i], g_pre, g_post, *w_j, *w_o[i], g_sub3, lam_p, st_diff,
                                   i, j, n_diff, side_jobs=side_jobs[i])
            x_ctx, st_dkt, st_dv = outs[0:3]
            st_diff = (st_dkt, st_dv)
            if i == 0:
                mla_w[0], w_o[1], w_kvb_b, w_qb_b = (outs[3], 0), (outs[4], 0), outs[5], outs[6]
                mods[1] = (outs[7], 0)
            else:
                mods[3] = (outs[3], 0)
            x_lat = _layer_diff_lat(x_lat, *mods[i], g_pre, g_post, *w_j, *w_o[i], cos_t, sin_t,
                                    cache_dkt, cache_diff_v, g_sub3, lam_p, i, j)
        else:
            w_j = mla_w[j]
            outs = _layer_mla_ctx(x_ctx, *mods[i], g_pre, g_post, *w_j, mla_g_qa, w_qb_b, mla_g_kva, *w_o[i], w_kvb_b,
                                  st_mla, i, j, n_mla, side_jobs=side_jobs[i])
            x_ctx, st_ckv, st_kpe = outs[0:3]
            st_mla = (st_ckv, st_kpe)
            if i == 1:
                mla_w[1] = (outs[3], 0)
                w_o[2], w_o[3] = (outs[4], 0), (outs[4], 1)
                mods[2] = (outs[5], 0)
                da_w[1] = (outs[6], 0)
            x_lat = _layer_mla_lat(x_lat, *mods[i], g_pre, g_post, *w_j, mla_g_qa, w_qb_b, mla_g_kva, *w_o[i], w_kvb_b,
                                   cos_t, sin_t, cache_mla_ckv, cache_kpe_t, i, j)

    return (x_ctx.reshape(N_CTX_B, CTX_S, D_MODEL),
            x_lat.reshape(N_LAT_B, LAT_S, D_MODEL),
            jnp.transpose(st_dkt.reshape(N_CTX_B, n_diff, HEADS, 2, DA_DH, CTX_S), (0, 1, 5, 2, 3, 4)),
            st_dv,
            st_ckv,
            jnp.swapaxes(st_kpe, 2, 3))
```
